```python
import jax, jax.numpy as jnp
from jax import lax
import numpy as np

D_MODEL = 2048
BATCH = 2
SEQ = 16384
DEPTH = 2

HEAD_DIM = 64
MOBA_HEADS = 8
NSA_HEADS = 8
NSA_KV_HEADS = 2
FOX_HEADS = 12
MEM_HEADS = 4
N_MEM = 256
MIX_WIDTH = (MOBA_HEADS + NSA_HEADS + FOX_HEADS + MEM_HEADS) * HEAD_DIM

Q_BLOCK = 128
MOBA_BLOCK = 256
MOBA_TOPK = 3
NSA_CMP_LEN = 32
NSA_CMP_STRIDE = 16
NSA_CMP_HIDDEN = 128
NSA_SEL_BLOCK = 64
NSA_TOPK = 16
NSA_WINDOW = 512
NSA_FORCE_SCORE = 1.0e4
D_FF = 5632
CONV_WIDTH = 3
LN_EPS = 1e-5
DEEPNORM_ALPHA = (2 * DEPTH) ** 0.25
DEEPNORM_BETA = (8 * DEPTH) ** -0.25

N_MOBA_QKV = 3 * MOBA_HEADS * HEAD_DIM
N_NSA_Q = NSA_HEADS * HEAD_DIM
N_NSA_KV = 6 * NSA_KV_HEADS * HEAD_DIM
N_NSA_GATE = 3 * NSA_HEADS
N_FOX_QKV = 3 * FOX_HEADS * HEAD_DIM
N_FOX_F = FOX_HEADS
N_MEM_Q = MEM_HEADS * HEAD_DIM
IN_SIZES = (N_MOBA_QKV, N_NSA_Q, N_NSA_KV, N_NSA_GATE, N_FOX_QKV, N_FOX_F, N_MEM_Q)
N_IN = N_MOBA_QKV + N_NSA_Q + N_NSA_KV + N_NSA_GATE + N_FOX_QKV + N_FOX_F + N_MEM_Q

kernel_name = "hybrid_moba_nsa_fox_deepnorm"

F32 = jnp.float32


def layer_norm(x, g, b):
    xf = x.astype(F32)
    mu = jnp.mean(xf, axis=-1, keepdims=True)
    var = jnp.mean(jnp.square(xf - mu), axis=-1, keepdims=True)
    y = (xf - mu) * lax.rsqrt(var + LN_EPS)
    return (y * g.astype(F32) + b.astype(F32)).astype(x.dtype)


def alibi_slopes(n):
    return jnp.exp2(-8.0 * jnp.arange(1, n + 1, dtype=F32) / n)


def masked_softmax(s, mask):
    s = jnp.where(mask, s, -jnp.inf)
    m = jnp.max(s, axis=-1, keepdims=True)
    m = jnp.where(jnp.isfinite(m), m, 0.0)
    e = jnp.where(mask, jnp.exp(s - m), 0.0)
    return e / jnp.maximum(jnp.sum(e, axis=-1, keepdims=True), 1e-30)


def moba_attention(q, k, v, slopes):
    B, H, S, Dh = q.shape
    scale = Dh ** -0.5
    s_pad = -(-S // MOBA_BLOCK) * MOBA_BLOCK
    nb = s_pad // MOBA_BLOCK
    n_sel = min(MOBA_TOPK, nb)
    pad = ((0, 0), (0, 0), (0, s_pad - S), (0, 0))
    k_p = jnp.pad(k, pad)
    v_p = jnp.pad(v, pad)
    k_blk = k_p.reshape(B, H, nb, MOBA_BLOCK, Dh)
    v_blk = v_p.reshape(B, H, nb, MOBA_BLOCK, Dh)
    k_mean = jnp.mean(k_blk.astype(F32), axis=3)
    bi = jnp.arange(B)[:, None, None, None]
    hi = jnp.arange(H)[None, :, None, None]
    blk_ids = jnp.arange(nb)
    off = jnp.arange(MOBA_BLOCK)

    def chunk(ci):
        c0 = ci * Q_BLOCK
        t = c0 + jnp.arange(Q_BLOCK)
        own = c0 // MOBA_BLOCK
        qc = lax.dynamic_slice_in_dim(q, c0, Q_BLOCK, axis=2)
        gate = jnp.einsum('bhqd,bhnd->bhqn', qc, k_mean, preferred_element_type=F32)
        gate = jnp.where(blk_ids < own, gate, -jnp.inf)
        _, sel = lax.top_k(gate, n_sel)
        sel_valid = sel < own
        k_sel = k_blk[bi, hi, sel]
        v_sel = v_blk[bi, hi, sel]
        pos_sel = sel[..., None] * MOBA_BLOCK + off
        s_sel = jnp.einsum('bhqd,bhqnld->bhqnl', qc, k_sel, preferred_element_type=F32) * scale
        s_sel = s_sel - slopes[:, None, None, None] * (t[:, None, None] - pos_sel).astype(F32)
        m_sel = jnp.broadcast_to(sel_valid[..., None], s_sel.shape)
        k_own = lax.dynamic_slice_in_dim(k_p, own * MOBA_BLOCK, MOBA_BLOCK, axis=2)
        v_own = lax.dynamic_slice_in_dim(v_p, own * MOBA_BLOCK, MOBA_BLOCK, axis=2)
        pos_own = own * MOBA_BLOCK + off
        d_own = t[:, None] - pos_own[None, :]
        s_own = jnp.einsum('bhqd,bhld->bhql', qc, k_own, preferred_element_type=F32) * scale
        s_own = s_own - slopes[:, None, None] * d_own.astype(F32)
        m_own = jnp.broadcast_to(d_own >= 0, s_own.shape)
        s_all = jnp.concatenate([s_own, s_sel.reshape(B, H, Q_BLOCK, -1)], axis=-1)
        m_all = jnp.concatenate([m_own, m_sel.reshape(B, H, Q_BLOCK, -1)], axis=-1)
        p = masked_softmax(s_all, m_all)
        p_own = p[..., :MOBA_BLOCK]
        p_sel = p[..., MOBA_BLOCK:].reshape(B, H, Q_BLOCK, n_sel, MOBA_BLOCK)
        o = (jnp.einsum('bhql,bhld->bhqd', p_own, v_own)
             + jnp.einsum('bhqnl,bhqnld->bhqd', p_sel, v_sel))
        return o.astype(q.dtype)

    out = lax.map(chunk, jnp.arange(S // Q_BLOCK))
    return out.transpose(1, 2, 0, 3, 4).reshape(B, H, S, Dh)


def nsa_compress(x, pe, w1, w2):
    B, S, G, Dh = x.shape
    xs = x.reshape(B, S // NSA_CMP_STRIDE, NSA_CMP_STRIDE, G, Dh)
    blk = jnp.concatenate([xs[:, :-1], xs[:, 1:]], axis=2)
    h = jax.nn.gelu(jnp.einsum('bnlgd,lde->bnge', blk + pe[None, None, :, None, :], w1))
    return jnp.einsum('bnge,ed->bgnd', h, w2)


def nsa_attention(q, k_cmp, v_cmp, k_slc, v_slc, k_win, v_win, gates, slopes):
    B, Hq, S, Dh = q.shape
    G = k_slc.shape[1]
    Hg = Hq // G
    nc = k_cmp.shape[2]
    nsb = S // NSA_SEL_BLOCK
    n_sel = min(NSA_TOPK, nsb)
    scale = Dh ** -0.5
    slopes_g = slopes.reshape(G, Hg)
    cmp_end = jnp.arange(nc) * NSA_CMP_STRIDE + NSA_CMP_LEN - 1
    ratio = NSA_SEL_BLOCK // NSA_CMP_STRIDE
    front = NSA_CMP_LEN // NSA_CMP_STRIDE - 1
    n_int = ratio + front
    back = ratio * (nsb - 1) + n_int - front - nc
    blk = jnp.arange(nsb)
    off = jnp.arange(NSA_SEL_BLOCK)
    k_sb = k_slc.reshape(B, G, nsb, NSA_SEL_BLOCK, Dh)
    v_sb = v_slc.reshape(B, G, nsb, NSA_SEL_BLOCK, Dh)
    wpad = ((0, 0), (0, 0), (NSA_WINDOW, 0), (0, 0))
    kw = jnp.pad(k_win, wpad)
    vw = jnp.pad(v_win, wpad)
    bi = jnp.arange(B)[:, None, None, None]
    gi = jnp.arange(G)[None, :, None, None]
    q_all = q.reshape(B, G, Hg, S, Dh)
    g_all = gates.reshape(B, G, Hg, S, 3)

    def chunk(ci):
        c0 = ci * Q_BLOCK
        t = c0 + jnp.arange(Q_BLOCK)
        qg = lax.dynamic_slice_in_dim(q_all, c0, Q_BLOCK, axis=3)
        gc = lax.dynamic_slice_in_dim(g_all, c0, Q_BLOCK, axis=3)
        d_c = t[:, None] - cmp_end[None, :]
        s_c = jnp.einsum('bghqd,bgnd->bghqn', qg, k_cmp, preferred_element_type=F32) * scale
        s_c = s_c - slopes_g[:, :, None, None] * d_c.astype(F32)
        p_c = masked_softmax(s_c, d_c >= 0)
        o_c = jnp.einsum('bghqn,bgnd->bghqd', p_c, v_cmp)
        imp_c = jnp.pad(jnp.sum(p_c, axis=2), ((0, 0), (0, 0), (0, 0), (front, back)))
        imp = imp_c[..., 0:ratio * (nsb - 1) + 1:ratio]
        for o in range(1, n_int):
            imp = imp + imp_c[..., o:o + ratio * (nsb - 1) + 1:ratio]
        jt = (t // NSA_SEL_BLOCK)[:, None]
        forced = (blk == 0) | (blk == jt) | (blk == jt - 1)
        imp = jnp.where(forced, NSA_FORCE_SCORE, imp)
        imp = jnp.where(blk * NSA_SEL_BLOCK <= t[:, None], imp, -jnp.inf)
        _, sel = lax.top_k(imp, n_sel)
        k_g = k_sb[bi, gi, sel]
        v_g = v_sb[bi, gi, sel]
        d_s = t[:, None, None] - (sel[..., None] * NSA_SEL_BLOCK + off)
        s_s = jnp.einsum('bghqd,bgqnld->bghqnl', qg, k_g, preferred_element_type=F32) * scale
        s_s = s_s - slopes_g[:, :, None, None, None] * d_s[:, :, None].astype(F32)
        shp = s_s.shape
        m_s = jnp.broadcast_to((d_s >= 0)[:, :, None], shp)
        p_s = masked_softmax(s_s.reshape(shp[:4] + (-1,)), m_s.reshape(shp[:4] + (-1,))).reshape(shp)
        o_s = jnp.einsum('bghqnl,bgqnld->bghqd', p_s, v_g)
        kwc = lax.dynamic_slice_in_dim(kw, c0, Q_BLOCK + NSA_WINDOW, axis=2)
        vwc = lax.dynamic_slice_in_dim(vw, c0, Q_BLOCK + NSA_WINDOW, axis=2)
        pos_w = c0 - NSA_WINDOW + jnp.arange(Q_BLOCK + NSA_WINDOW)
        d_w = t[:, None] - pos_w[None, :]
        m_w = (d_w >= 0) & (d_w < NSA_WINDOW) & (pos_w[None, :] >= 0)
        s_w = jnp.einsum('bghqd,bgkd->bghqk', qg, kwc, preferred_element_type=F32) * scale
        s_w = s_w - slopes_g[:, :, None, None] * d_w.astype(F32)
        p_w = masked_softmax(s_w, m_w)
        o_w = jnp.einsum('bghqk,bgkd->bghqd', p_w, vwc)
        o = gc[..., 0:1] * o_c + gc[..., 1:2] * o_s + gc[..., 2:3] * o_w
        return o.reshape(B, Hq, Q_BLOCK, Dh).astype(q.dtype)

    out = lax.map(chunk, jnp.arange(S // Q_BLOCK))
    return out.transpose(1, 2, 0, 3, 4).reshape(B, Hq, S, Dh)


def forgetting_attention(q, k, v, log_f):
    B, H, S, Dh = q.shape
    scale = Dh ** -0.5
    c = lax.cumsum(log_f, axis=2)
    s_pos = jnp.arange(S)

    def chunk(ci):
        c0 = ci * Q_BLOCK
        t = c0 + jnp.arange(Q_BLOCK)
        qc = lax.dynamic_slice_in_dim(q, c0, Q_BLOCK, axis=2)
        cq = lax.dynamic_slice_in_dim(c, c0, Q_BLOCK, axis=2)
        s = jnp.einsum('bhqd,bhkd->bhqk', qc, k, preferred_element_type=F32) * scale
        s = s + (cq[..., :, None] - c[:, :, None, :])
        p = masked_softmax(s, s_pos[None, :] <= t[:, None])
        return jnp.einsum('bhqk,bhkd->bhqd', p, v).astype(q.dtype)

    out = lax.map(chunk, jnp.arange(S // Q_BLOCK))
    return out.transpose(1, 2, 0, 3, 4).reshape(B, H, S, Dh)


def memory_attention(q, mem_k, mem_v):
    s = jnp.einsum('bhqd,bhmd->bhqm', q, mem_k, preferred_element_type=F32) * (q.shape[-1] ** -0.5)
    p = jax.nn.softmax(s, axis=-1)
    return jnp.einsum('bhqm,bhmd->bhqd', p, mem_v).astype(q.dtype)


def hybrid_mixer(x, mem, w_in, b_forget, w_mem_kv, cmp_pe, cmp_w1, cmp_w2, w_out):
    B, S, _ = x.shape
    Dh = HEAD_DIM
    proj = x @ w_in
    split_at = []
    acc = 0
    for n in IN_SIZES[:-1]:
        acc += n
        split_at.append(acc)
    moba_qkv, nsa_q, nsa_kv, nsa_g, fox_qkv, fox_f, mem_q = jnp.split(proj, split_at, axis=-1)

    mq, mk, mv = moba_qkv.reshape(B, S, 3, MOBA_HEADS, Dh).transpose(2, 0, 3, 1, 4)
    o_moba = moba_attention(mq, mk, mv, alibi_slopes(MOBA_HEADS))

    nq = nsa_q.reshape(B, S, NSA_HEADS, Dh).transpose(0, 2, 1, 3)
    kv = nsa_kv.reshape(B, S, 6, NSA_KV_HEADS, Dh)
    k_cmp = nsa_compress(kv[:, :, 0], cmp_pe[0], cmp_w1[0], cmp_w2[0])
    v_cmp = nsa_compress(kv[:, :, 1], cmp_pe[1], cmp_w1[1], cmp_w2[1])
    k_slc, v_slc, k_win, v_win = [kv[:, :, i].transpose(0, 2, 1, 3) for i in range(2, 6)]
    gates = jax.nn.sigmoid(nsa_g.reshape(B, S, NSA_HEADS, 3)).transpose(0, 2, 1, 3)
    o_nsa = nsa_attention(nq, k_cmp, v_cmp, k_slc, v_slc, k_win, v_win, gates,
                          alibi_slopes(NSA_HEADS))

    fq, fk, fv = fox_qkv.reshape(B, S, 3, FOX_HEADS, Dh).transpose(2, 0, 3, 1, 4)
    log_f = jax.nn.log_sigmoid((fox_f + b_forget).astype(F32)).transpose(0, 2, 1)
    o_fox = forgetting_attention(fq, fk, fv, log_f)

    eq = mem_q.reshape(B, S, MEM_HEADS, Dh).transpose(0, 2, 1, 3)
    ek, ev = (mem @ w_mem_kv).reshape(B, mem.shape[1], 2, MEM_HEADS, Dh).transpose(2, 0, 3, 1, 4)
    o_mem = memory_attention(eq, ek, ev)

    o = jnp.concatenate([o_moba, o_nsa, o_fox, o_mem], axis=1)
    o = o.transpose(0, 2, 1, 3).reshape(B, S, MIX_WIDTH)
    return o @ w_out


def conv_ffn(x, w_up, conv_w, conv_b, w_down):
    S = x.shape[1]
    u, g = jnp.split(x @ w_up, 2, axis=-1)
    gp = jnp.pad(g, ((0, 0), (CONV_WIDTH - 1, 0), (0, 0)))
    gc = conv_b
    for tap in range(CONV_WIDTH):
        gc = gc + conv_w[tap] * gp[:, tap:tap + S]
    return (jax.nn.gelu(gc) * u) @ w_down


def setup_inputs(seed: int = 0) -> dict:
    key = jax.random.key(seed)
    ks = jax.random.split(key, 20)

    def nrm(k, shape, scale):
        return jax.random.normal(k, shape, F32) * scale

    return {
        "x": nrm(ks[0], (BATCH, SEQ, D_MODEL), 1.0),
        "mem": nrm(ks[1], (BATCH, N_MEM, D_MODEL), 1.0),
        "emb_ln_g": 1.0 + nrm(ks[2], (D_MODEL,), 0.02),
        "emb_ln_b": nrm(ks[3], (D_MODEL,), 0.02),
        "w_in": nrm(ks[4], (DEPTH, D_MODEL, N_IN), D_MODEL ** -0.5),
        "b_forget": 3.0 + nrm(ks[5], (DEPTH, FOX_HEADS), 1.5),
        "w_mem_kv": nrm(ks[6], (DEPTH, D_MODEL, 2 * MEM_HEADS * HEAD_DIM), D_MODEL ** -0.5),
        "nsa_cmp_pe": nrm(ks[7], (DEPTH, 2, NSA_CMP_LEN, HEAD_DIM), 0.02),
        "nsa_cmp_w1": nrm(ks[8], (DEPTH, 2, NSA_CMP_LEN, HEAD_DIM, NSA_CMP_HIDDEN),
                          (NSA_CMP_LEN * HEAD_DIM) ** -0.5),
        "nsa_cmp_w2": nrm(ks[9], (DEPTH, 2, NSA_CMP_HIDDEN, HEAD_DIM), NSA_CMP_HIDDEN ** -0.5),
        "w_out": nrm(ks[10], (DEPTH, MIX_WIDTH, D_MODEL), MIX_WIDTH ** -0.5 * DEEPNORM_BETA),
        "ln1_g": 1.0 + nrm(ks[11], (DEPTH, D_MODEL), 0.02),
        "ln1_b": nrm(ks[12], (DEPTH, D_MODEL), 0.02),
        "ffn_w_up": nrm(ks[13], (DEPTH, D_MODEL, 2 * D_FF), D_MODEL ** -0.5),
        "ffn_conv_w": nrm(ks[14], (DEPTH, CONV_WIDTH, D_FF), CONV_WIDTH ** -0.5),
        "ffn_conv_b": nrm(ks[15], (DEPTH, D_FF), 0.02),
        "ffn_w_down": nrm(ks[16], (DEPTH, D_FF, D_MODEL), D_FF ** -0.5 * DEEPNORM_BETA),
        "ln2_g": 1.0 + nrm(ks[17], (DEPTH, D_MODEL), 0.02),
        "ln2_b": nrm(ks[18], (DEPTH, D_MODEL), 0.02),
    }


def reference(x, mem, emb_ln_g, emb_ln_b, w_in, b_forget, w_mem_kv, nsa_cmp_pe, nsa_cmp_w1,
              nsa_cmp_w2, w_out, ln1_g, ln1_b, ffn_w_up, ffn_conv_w, ffn_conv_b, ffn_w_down,
              ln2_g, ln2_b):
    h = layer_norm(x, emb_ln_g, emb_ln_b)
    for l in range(DEPTH):
        mix = hybrid_mixer(h, mem, w_in[l], b_forget[l], w_mem_kv[l], nsa_cmp_pe[l],
                           nsa_cmp_w1[l], nsa_cmp_w2[l], w_out[l])
        h = layer_norm(DEEPNORM_ALPHA * h + mix, ln1_g[l], ln1_b[l])
        ffn = conv_ffn(h, ffn_w_up[l], ffn_conv_w[l], ffn_conv_b[l], ffn_w_down[l])
        h = layer_norm(DEEPNORM_ALPHA * h + ffn, ln2_g[l], ln2_b[l])
    return h
```

```python
import functools
import math

import jax
import jax.numpy as jnp
import numpy as np
from jax import lax
from jax.experimental import pallas as pl
from jax.experimental.pallas import tpu as pltpu

F32 = jnp.float32
BF16 = jnp.bfloat16

HEAD_DIM = 64
MOBA_HEADS = 8
NSA_HEADS = 8
NSA_KV_HEADS = 2
NSA_GROUP = NSA_HEADS // NSA_KV_HEADS
FOX_HEADS = 12
MEM_HEADS = 4
MOBA_BLOCK = 256
MOBA_TOPK = 3
NSA_CMP_LEN = 32
NSA_CMP_STRIDE = 16
NSA_SEL_BLOCK = 64
NSA_TOPK = 16
NSA_WINDOW = 512
NSA_FORCE_SCORE = 1.0e4
CONV_WIDTH = 3
LN_EPS = 1e-5
DEPTH = 2
DEEPNORM_ALPHA = (2 * DEPTH) ** 0.25

LOG2E = math.log2(math.e)
Q_SCALE = HEAD_DIM ** -0.5 * LOG2E
NEG = -1.0e30
VMEM_LIMIT = 56 * 1024 * 1024
LANES = 128

SLOT_MOBA_Q, SLOT_MOBA_K, SLOT_MOBA_V = 0, 8, 16
SLOT_NSA_Q = 24
SLOT_NSA_KV = 32
SLOT_FOX_Q, SLOT_FOX_K, SLOT_FOX_V = 44, 56, 68
SLOT_MEM_Q = 80
N_SLOTS = 84


def _cparams(sem):
    return pltpu.CompilerParams(dimension_semantics=sem, vmem_limit_bytes=VMEM_LIMIT)


def _split3(x):
    hi = x.astype(BF16)
    r1 = x - hi.astype(F32)
    mid = r1.astype(BF16)
    lo = (r1 - mid.astype(F32)).astype(BF16)
    return hi, mid, lo


def _dot_nt(a, b):
    return lax.dot_general(a, b, (((1,), (1,)), ((), ())), preferred_element_type=F32)


def _dot(a, b):
    return jnp.dot(a, b, preferred_element_type=F32)


def _layer_norm_rows(x, g, b):
    mu = jnp.mean(x, axis=-1, keepdims=True)
    xc = x - mu
    var = jnp.mean(xc * xc, axis=-1, keepdims=True)
    return xc * lax.rsqrt(var + LN_EPS) * g + b


def _ln_kernel(x_ref, g_ref, b_ref, o32_ref, o16_ref):
    y = _layer_norm_rows(x_ref[...], g_ref[...], b_ref[...])
    o32_ref[...] = y
    o16_ref[...] = y.astype(BF16)


def layer_norm_pallas(x, g, b, tm=512):
    m, d = x.shape
    return pl.pallas_call(
        _ln_kernel,
        grid=(m // tm,),
        in_specs=[pl.BlockSpec((tm, d), lambda i: (i, 0)),
                  pl.BlockSpec((1, d), lambda i: (0, 0)),
                  pl.BlockSpec((1, d), lambda i: (0, 0))],
        out_specs=[pl.BlockSpec((tm, d), lambda i: (i, 0)),
                   pl.BlockSpec((tm, d), lambda i: (i, 0))],
        out_shape=[jax.ShapeDtypeStruct((m, d), F32), jax.ShapeDtypeStruct((m, d), BF16)],
        compiler_params=_cparams(("parallel",)),
        name="layer_norm",
    )(x, g.reshape(1, d), b.reshape(1, d))


def _gates_kernel(h_ref, wg_ref, wft_ref, bf_ref, tri_ref, g_ref, c_ref, carry_ref):
    si = pl.program_id(1)

    @pl.when(si == 0)
    def _():
        carry_ref[...] = jnp.zeros_like(carry_ref)

    h = h_ref[0]
    h_hi = h.astype(BF16)
    h_lo = (h - h_hi.astype(F32)).astype(BF16)
    wg = wg_ref[...]
    wg_hi = wg.astype(BF16)
    wg_lo = (wg - wg_hi.astype(F32)).astype(BF16)
    g = _dot(h_hi, wg_hi) + _dot(h_lo, wg_hi) + _dot(h_hi, wg_lo)
    g_ref[0] = 1.0 / (1.0 + jnp.exp(-g))
    wf = wft_ref[...]
    wf_hi = wf.astype(BF16)
    wf_lo = (wf - wf_hi.astype(F32)).astype(BF16)
    ft = _dot_nt(wf_hi, h_hi) + _dot_nt(wf_hi, h_lo) + _dot_nt(wf_lo, h_hi)
    x = ft + bf_ref[...]
    logf = jnp.minimum(x, 0.0) - jnp.log(1.0 + jnp.exp(-jnp.abs(x)))
    tri = tri_ref[...]
    l_hi, l_mid, l_lo = _split3(logf)
    c = _dot(l_hi, tri) + _dot(l_mid, tri) + _dot(l_lo, tri) + carry_ref[:, :1]
    c_ref[0] = c * LOG2E
    carry_ref[...] = jnp.broadcast_to(c[:, -1:], carry_ref.shape)


def gates_pallas(h3, w_gate, w_forget, b_forget, t=512):
    b, s, d = h3.shape
    gw = 3 * NSA_GROUP
    ng = NSA_KV_HEADS * LANES
    wg = jnp.zeros((d, ng), F32)
    for g in range(NSA_KV_HEADS):
        wg = wg.at[:, g * LANES:g * LANES + gw].set(w_gate[:, g * gw:(g + 1) * gw])
    wft = jnp.zeros((16, d), F32).at[:FOX_HEADS].set(w_forget.T)
    bf = jnp.zeros((16, 1), F32).at[:FOX_HEADS, 0].set(b_forget)
    tri = (np.arange(t)[:, None] <= np.arange(t)[None, :]).astype(np.float32)
    return pl.pallas_call(
        _gates_kernel,
        grid=(b, s // t),
        in_specs=[pl.BlockSpec((1, t, d), lambda i, j: (i, j, 0)),
                  pl.BlockSpec((d, ng), lambda i, j: (0, 0)),
                  pl.BlockSpec((16, d), lambda i, j: (0, 0)),
                  pl.BlockSpec((16, 1), lambda i, j: (0, 0)),
                  pl.BlockSpec((t, t), lambda i, j: (0, 0))],
        out_specs=[pl.BlockSpec((1, t, ng), lambda i, j: (i, j, 0)),
                   pl.BlockSpec((1, 16, t), lambda i, j: (i, 0, j))],
        out_shape=[jax.ShapeDtypeStruct((b, s, ng), F32),
                   jax.ShapeDtypeStruct((b, 16, s), F32)],
        scratch_shapes=[pltpu.VMEM((16, LANES), F32)],
        compiler_params=_cparams(("parallel", "arbitrary")),
        name="gates_cumsum",
    )(h3, wg, wft, bf, jnp.asarray(tri, BF16))


def _proj_heads_kernel(x_ref, w_ref, sc_ref, o_ref, *, heads_per_step):
    acc = _dot(x_ref[0], w_ref[...]) * sc_ref[...]
    for j in range(heads_per_step):
        o_ref[0, j] = acc[:, j * HEAD_DIM:(j + 1) * HEAD_DIM].astype(o_ref.dtype)


def proj_heads_pallas(x3, w, colscale, tm, heads_per_step, out_dtype=BF16):
    b, s, d = x3.shape
    n = w.shape[1]
    n_slots = n // HEAD_DIM
    tn = heads_per_step * HEAD_DIM
    return pl.pallas_call(
        functools.partial(_proj_heads_kernel, heads_per_step=heads_per_step),
        grid=(b, s // tm, n // tn),
        in_specs=[pl.BlockSpec((1, tm, d), lambda i, j, k: (i, j, 0)),
                  pl.BlockSpec((d, tn), lambda i, j, k: (0, k)),
                  pl.BlockSpec((1, tn), lambda i, j, k: (0, k))],
        out_specs=pl.BlockSpec((1, heads_per_step, tm, HEAD_DIM), lambda i, j, k: (i, k, j, 0)),
        out_shape=jax.ShapeDtypeStruct((b, n_slots, s, HEAD_DIM), out_dtype),
        compiler_params=_cparams(("parallel", "parallel", "arbitrary")),
        name="proj_heads",
    )(x3, w, colscale.reshape(1, n))


def _matmul_ln_kernel(x_ref, w_ref, r_ref, g_ref, b_ref, o32_ref, o16_ref, acc_ref):
    k = pl.program_id(1)

    @pl.when(k == 0)
    def _():
        acc_ref[...] = jnp.zeros_like(acc_ref)

    acc_ref[...] += _dot(x_ref[...], w_ref[...])

    @pl.when(k == pl.num_programs(1) - 1)
    def _():
        y = _layer_norm_rows(DEEPNORM_ALPHA * r_ref[...] + acc_ref[...], g_ref[...], b_ref[...])
        o32_ref[...] = y
        o16_ref[...] = y.astype(BF16)


def matmul_ln_pallas(x, w, res, g, b, tm, tk):
    m, kk = x.shape
    d = w.shape[1]
    return pl.pallas_call(
        _matmul_ln_kernel,
        grid=(m // tm, kk // tk),
        in_specs=[pl.BlockSpec((tm, tk), lambda i, k: (i, k)),
                  pl.BlockSpec((tk, d), lambda i, k: (k, 0)),
                  pl.BlockSpec((tm, d), lambda i, k: (i, 0)),
                  pl.BlockSpec((1, d), lambda i, k: (0, 0)),
                  pl.BlockSpec((1, d), lambda i, k: (0, 0))],
        out_specs=[pl.BlockSpec((tm, d), lambda i, k: (i, 0)),
                   pl.BlockSpec((tm, d), lambda i, k: (i, 0))],
        out_shape=[jax.ShapeDtypeStruct((m, d), F32), jax.ShapeDtypeStruct((m, d), BF16)],
        scratch_shapes=[pltpu.VMEM((tm, d), F32)],
        compiler_params=_cparams(("parallel", "arbitrary")),
        name="matmul_ln",
    )(x, w, res, g.reshape(1, d), b.reshape(1, d))


HALO = 16


def _gelu_tanh(x):
    return 0.5 * x * (1.0 + jnp.tanh(math.sqrt(2.0 / math.pi) * (x + 0.044715 * x * x * x)))


def _ffn_up_kernel(x_ref, xh_ref, wu_ref, wg_ref, cw_ref, cb_ref, o_ref):
    j = pl.program_id(1)
    x = x_ref[0]
    u = _dot(x, wu_ref[...])
    g = _dot(x, wg_ref[...])
    gh = _dot(xh_ref[0], wg_ref[...]) * jnp.where(j > 0, 1.0, 0.0)
    row = lax.broadcasted_iota(jnp.int32, g.shape, 0)
    prev1 = gh[HALO - 1:HALO, :]
    prev2 = gh[HALO - 2:HALO - 1, :]
    g_m1 = jnp.where(row == 0, prev1, pltpu.roll(g, 1, axis=0))
    g_m2 = jnp.where(row == 0, prev2, jnp.where(row == 1, prev1, pltpu.roll(g, 2, axis=0)))
    cw = cw_ref[...]
    gc = cb_ref[...] + cw[0:1] * g_m2 + cw[1:2] * g_m1 + cw[2:3] * g
    o_ref[0] = (_gelu_tanh(gc) * u).astype(o_ref.dtype)


def ffn_up_pallas(x3, w_up, conv_w, conv_b, tm, tn):
    b, s, d = x3.shape
    dff = w_up.shape[1] // 2
    nt = dff // tn
    hb = tm // HALO
    cw = jnp.zeros((8, dff), F32).at[:CONV_WIDTH].set(conv_w)
    return pl.pallas_call(
        _ffn_up_kernel,
        grid=(b, s // tm, nt),
        in_specs=[pl.BlockSpec((1, tm, d), lambda i, j, k: (i, j, 0)),
                  pl.BlockSpec((1, HALO, d), lambda i, j, k: (i, jnp.maximum(j * hb - 1, 0), 0)),
                  pl.BlockSpec((d, tn), lambda i, j, k: (0, k)),
                  pl.BlockSpec((d, tn), lambda i, j, k: (0, k + nt)),
                  pl.BlockSpec((8, tn), lambda i, j, k: (0, k)),
                  pl.BlockSpec((1, tn), lambda i, j, k: (0, k))],
        out_specs=pl.BlockSpec((1, tm, tn), lambda i, j, k: (i, j, k)),
        out_shape=jax.ShapeDtypeStruct((b, s, dff), BF16),
        compiler_params=_cparams(("parallel", "parallel", "arbitrary")),
        name="ffn_up",
    )(x3, x3, w_up, w_up, cw, conv_b.reshape(1, dff))


def _online_step(s, delta, v, m_ref, l_ref, acc_ref):
    m_prev = m_ref[:, :1]
    m_cur = jnp.max(s, axis=-1, keepdims=True) + delta
    m_new = jnp.maximum(m_prev, m_cur)
    alpha = jnp.exp2(m_prev - m_new)
    p = jnp.exp2(s - (m_new - delta))
    l_ref[...] = alpha * l_ref[...] + jnp.sum(p, axis=-1, keepdims=True)
    acc_ref[...] = alpha * acc_ref[...] + _dot(p.astype(BF16), v)
    m_ref[...] = jnp.broadcast_to(m_new, m_ref.shape)


def _init_state(m_ref, l_ref, acc_ref):
    m_ref[...] = jnp.full(m_ref.shape, NEG, F32)
    l_ref[...] = jnp.zeros(l_ref.shape, F32)
    acc_ref[...] = jnp.zeros(acc_ref.shape, F32)


def _tile(ref, idx, size):
    return ref[0, 0, pl.ds(pl.multiple_of(idx * size, size), size), :]


def _moba_kernel(slope_ref, q_ref, k_ref, v_ref, ind_ref, o_ref, m_ref, l_ref, acc_ref, km_ref, sel_ref):
    t = MOBA_BLOCK
    h = pl.program_id(1)
    qi = pl.program_id(2)
    slope2 = slope_ref[h]

    @pl.when(qi == 0)
    def _():
        km_ref[...] = _dot(ind_ref[...], k_ref[0, 0]) * (1.0 / MOBA_BLOCK)

    q = q_ref[0, 0]
    km = km_ref[...]
    km_hi = km.astype(BF16)
    km_lo = (km - km_hi.astype(F32)).astype(BF16)
    gate = _dot_nt(q, km_hi) + _dot_nt(q, km_lo)
    blk = lax.broadcasted_iota(jnp.int32, gate.shape, 1)
    valid = blk < qi
    work = jnp.where(valid, gate, -jnp.inf)
    sel = jnp.zeros(gate.shape, F32)
    big = jnp.int32(2 ** 30)
    for _ in range(MOBA_TOPK):
        mx = jnp.max(work, axis=-1, keepdims=True)
        first = jnp.min(jnp.where(work == mx, blk, big), axis=-1, keepdims=True)
        hit = blk == first
        sel = jnp.where(hit, 1.0, sel)
        work = jnp.where(hit, -jnp.inf, work)
    sel_ref[...] = jnp.where(valid, sel, 0.0)

    col = lax.broadcasted_iota(jnp.int32, (t, t), 1)
    row = lax.broadcasted_iota(jnp.int32, (t, t), 0)
    colbias = slope2 * lax.broadcasted_iota(jnp.int32, (1, t), 1).astype(F32)

    _init_state(m_ref, l_ref, acc_ref)
    s = _dot_nt(q, _tile(k_ref, qi, t)) + colbias
    s = jnp.where(col <= row, s, NEG)
    _online_step(s, 0.0, _tile(v_ref, qi, t), m_ref, l_ref, acc_ref)

    def body(kj, carry):
        chosen = jnp.sum(jnp.where(blk == kj, sel_ref[...], 0.0), axis=-1, keepdims=True)
        delta = jnp.where(chosen > 0.0, slope2 * ((kj - qi) * t).astype(F32), NEG)
        s = _dot_nt(q, _tile(k_ref, kj, t)) + colbias
        _online_step(s, delta, _tile(v_ref, kj, t), m_ref, l_ref, acc_ref)
        return carry

    lax.fori_loop(0, qi, body, 0)
    o_ref[0, 0] = (acc_ref[...] / l_ref[:, :1]).astype(o_ref.dtype)


def _alibi_slopes2(n):
    return (np.exp2(-8.0 * np.arange(1, n + 1, dtype=np.float64) / n) * LOG2E).astype(np.float32)


def moba_pallas(qkv):
    b, _, s, dh = qkv.shape
    t = MOBA_BLOCK
    nb = s // t
    nbp = max(LANES, nb)
    ind = np.zeros((nbp, s), np.float32)
    ind[np.arange(s) // t, np.arange(s)] = 1.0
    grid_spec = pltpu.PrefetchScalarGridSpec(
        num_scalar_prefetch=1,
        grid=(b, MOBA_HEADS, nb),
        in_specs=[pl.BlockSpec((1, 1, t, dh), lambda i, h, j, sl: (i, SLOT_MOBA_Q + h, j, 0)),
                  pl.BlockSpec((1, 1, s, dh), lambda i, h, j, sl: (i, SLOT_MOBA_K + h, 0, 0)),
                  pl.BlockSpec((1, 1, s, dh), lambda i, h, j, sl: (i, SLOT_MOBA_V + h, 0, 0)),
                  pl.BlockSpec((nbp, s), lambda i, h, j, sl: (0, 0))],
        out_specs=pl.BlockSpec((1, 1, t, dh), lambda i, h, j, sl: (i, h, j, 0)),
        scratch_shapes=[pltpu.VMEM((t, LANES), F32), pltpu.VMEM((t, LANES), F32),
                        pltpu.VMEM((t, dh), F32), pltpu.VMEM((nbp, dh), F32),
                        pltpu.VMEM((t, nbp), F32)])
    return pl.pallas_call(
        _moba_kernel,
        grid_spec=grid_spec,
        out_shape=jax.ShapeDtypeStruct((b, MOBA_HEADS, s, dh), BF16),
        compiler_params=_cparams(("parallel", "parallel", "arbitrary")),
        name="moba_attention",
    )(jnp.asarray(_alibi_slopes2(MOBA_HEADS)), qkv, qkv, qkv, jnp.asarray(ind, BF16))


FOX_TILE = 512


def _fox_kernel(cbase_ref, q_ref, k_ref, v_ref, c_ref, o_ref, m_ref, l_ref, acc_ref):
    t = FOX_TILE
    u = pl.program_id(0) * FOX_HEADS + pl.program_id(1)
    qi = pl.program_id(2)
    nt = pl.num_programs(2)
    q = q_ref[0, 0]
    cb_q = cbase_ref[u * nt + qi]
    col = lax.broadcasted_iota(jnp.int32, (t, t), 1)
    row = lax.broadcasted_iota(jnp.int32, (t, t), 0)

    def scores(kj):
        cb_k = cbase_ref[u * nt + kj]
        ck = c_ref[0, 0, :, pl.ds(pl.multiple_of(kj * t, t), t)]
        return _dot_nt(q, _tile(k_ref, kj, t)) + (cb_k - ck), cb_q - cb_k

    _init_state(m_ref, l_ref, acc_ref)
    s, delta = scores(qi)
    s = jnp.where(col <= row, s, NEG)
    _online_step(s, delta, _tile(v_ref, qi, t), m_ref, l_ref, acc_ref)

    def body(kj, carry):
        s, delta = scores(kj)
        _online_step(s, delta, _tile(v_ref, kj, t), m_ref, l_ref, acc_ref)
        return carry

    lax.fori_loop(0, qi, body, 0)
    o_ref[0, 0] = (acc_ref[...] / l_ref[:, :1]).astype(o_ref.dtype)


def fox_pallas(qkv, c2):
    b, _, s, dh = qkv.shape
    t = FOX_TILE
    nt = s // t
    cbase = c2[:, :FOX_HEADS, ::t].reshape(-1)
    c4 = c2.reshape(b, 16, 1, s)
    grid_spec = pltpu.PrefetchScalarGridSpec(
        num_scalar_prefetch=1,
        grid=(b, FOX_HEADS, nt),
        in_specs=[pl.BlockSpec((1, 1, t, dh), lambda i, h, j, cb: (i, SLOT_FOX_Q + h, j, 0)),
                  pl.BlockSpec((1, 1, s, dh), lambda i, h, j, cb: (i, SLOT_FOX_K + h, 0, 0)),
                  pl.BlockSpec((1, 1, s, dh), lambda i, h, j, cb: (i, SLOT_FOX_V + h, 0, 0)),
                  pl.BlockSpec((1, 1, 1, s), lambda i, h, j, cb: (i, h, 0, 0))],
        out_specs=pl.BlockSpec((1, 1, t, dh), lambda i, h, j, cb: (i, h, j, 0)),
        scratch_shapes=[pltpu.VMEM((t, LANES), F32), pltpu.VMEM((t, LANES), F32),
                        pltpu.VMEM((t, dh), F32)])
    return pl.pallas_call(
        _fox_kernel,
        grid_spec=grid_spec,
        out_shape=jax.ShapeDtypeStruct((b, FOX_HEADS, s, dh), BF16),
        compiler_params=_cparams(("parallel", "parallel", "arbitrary")),
        name="fox_attention",
    )(cbase, qkv, qkv, qkv, c4)


def _nsa_compress_kernel(x_ref, w1a_ref, w1b_ref, pe_ref, w1_ref, w2_ref, o_ref):
    nr = x_ref.shape[2]
    x = x_ref[0, 0]

    def near_f32(xb, w):
        w_hi = w.astype(BF16)
        w_lo = (w - w_hi.astype(F32)).astype(BF16)
        return _dot(xb, w_hi) + _dot(xb, w_lo)

    a = near_f32(x, w1a_ref[0])
    bm = near_f32(x, w1b_ref[0])
    pe_hi, pe_mid, pe_lo = _split3(pe_ref[0])
    w1 = w1_ref[0]
    w1_hi = w1.astype(BF16)
    w1_lo = (w1 - w1_hi.astype(F32)).astype(BF16)
    pe_term = (_dot(pe_hi, w1_hi) + _dot(pe_mid, w1_hi) + _dot(pe_lo, w1_hi)
               + _dot(pe_hi, w1_lo) + _dot(pe_mid, w1_lo))[0:1]
    pre = a + pltpu.roll(bm, nr - 1, axis=0) + pe_term
    hid = _gelu_tanh(pre)
    h_hi, h_mid, h_lo = _split3(hid)
    w2 = w2_ref[0]
    w2_hi = w2.astype(BF16)
    w2_lo = (w2 - w2_hi.astype(F32)).astype(BF16)
    o_ref[0, 0] = (_dot(h_hi, w2_hi) + _dot(h_mid, w2_hi) + _dot(h_lo, w2_hi)
                   + _dot(h_hi, w2_lo) + _dot(h_mid, w2_lo))


def nsa_compress_pallas(qkv, pe, w1, w2):
    b, _, s, dh = qkv.shape
    nr = s // NSA_CMP_STRIDE
    half = NSA_CMP_STRIDE * dh
    hid = w1.shape[-1]
    x = qkv[:, SLOT_NSA_KV:SLOT_NSA_KV + 4].reshape(b, 4, nr, half)
    w1f = w1.reshape(2, NSA_CMP_LEN * dh, hid)
    pef = jnp.zeros((2, 8, NSA_CMP_LEN * dh), F32).at[:, 0].set(pe.reshape(2, NSA_CMP_LEN * dh))
    return pl.pallas_call(
        _nsa_compress_kernel,
        grid=(b, 4),
        in_specs=[pl.BlockSpec((1, 1, nr, half), lambda i, j: (i, j, 0, 0)),
                  pl.BlockSpec((1, half, hid), lambda i, j: (j // 2, 0, 0)),
                  pl.BlockSpec((1, half, hid), lambda i, j: (j // 2, 1, 0)),
                  pl.BlockSpec((1, 8, 2 * half), lambda i, j: (j // 2, 0, 0)),
                  pl.BlockSpec((1, 2 * half, hid), lambda i, j: (j // 2, 0, 0)),
                  pl.BlockSpec((1, hid, dh), lambda i, j: (j // 2, 0, 0))],
        out_specs=pl.BlockSpec((1, 1, nr, dh), lambda i, j: (i, j, 0, 0)),
        out_shape=jax.ShapeDtypeStruct((b, 4, nr, dh), F32),
        compiler_params=_cparams(("parallel", "parallel")),
        name="nsa_compress",
    )(x, w1f, w1f, pef, w1f, w2)


NSA_TQ = 128


def _nsa_cmp_kernel(slope_ref, q_ref, kc_ref, vc_ref, mimp_ref, oc_ref, sel_ref):
    tq = NSA_TQ
    g = pl.program_id(1)
    q0 = pl.program_id(2) * tq
    nr = kc_ref.shape[2]
    nsb = sel_ref.shape[3]
    kc = kc_ref[0, 0]
    kc_hi = kc.astype(BF16)
    kc_lo = (kc - kc_hi.astype(F32)).astype(BF16)
    vc = vc_ref[0, 0].astype(BF16)
    t_rows = q0 + lax.broadcasted_iota(jnp.int32, (tq, 1), 0)
    cmp_end = NSA_CMP_STRIDE * lax.broadcasted_iota(jnp.int32, (1, nr), 1) + (NSA_CMP_LEN - 1)
    mask = cmp_end <= t_rows
    colrel = (cmp_end - q0).astype(F32)
    psum = jnp.zeros((tq, nr), F32)
    for hh in range(NSA_GROUP):
        q = q_ref[0, hh]
        s = _dot_nt(q, kc_hi) + _dot_nt(q, kc_lo) + slope_ref[g * NSA_GROUP + hh] * colrel
        s = jnp.where(mask, s, -jnp.inf)
        m = jnp.max(s, axis=-1, keepdims=True)
        m = jnp.where(m > -jnp.inf, m, 0.0)
        e = jnp.where(mask, jnp.exp2(s - m), 0.0)
        p = e / jnp.maximum(jnp.sum(e, axis=-1, keepdims=True), 1e-30)
        oc_ref[0, hh] = _dot(p.astype(BF16), vc)
        psum = psum + p
    p_hi, p_mid, p_lo = _split3(psum)
    mimp = mimp_ref[...]
    imp = _dot(p_hi, mimp) + _dot(p_mid, mimp) + _dot(p_lo, mimp)
    blk = lax.broadcasted_iota(jnp.int32, (tq, nsb), 1)
    jt = t_rows // NSA_SEL_BLOCK
    forced = (blk == 0) | (blk == jt) | (blk == jt - 1)
    imp = jnp.where(forced, NSA_FORCE_SCORE, imp)
    valid = blk * NSA_SEL_BLOCK <= t_rows
    work0 = jnp.where(valid, imp, -jnp.inf)
    big = jnp.int32(2 ** 30)

    def pick(_, carry):
        work, sel = carry
        mx = jnp.max(work, axis=-1, keepdims=True)
        first = jnp.min(jnp.where(work == mx, blk, big), axis=-1, keepdims=True)
        hit = blk == first
        return jnp.where(hit, -jnp.inf, work), jnp.where(hit, 1.0, sel)

    _, sel = lax.fori_loop(0, min(NSA_TOPK, nsb), pick, (work0, jnp.zeros((tq, nsb), F32)))
    sel_ref[0, 0] = jnp.where(valid, sel, 0.0).astype(sel_ref.dtype)


def nsa_cmp_pallas(qkv, cmp_kv):
    b, _, s, dh = qkv.shape
    tq = NSA_TQ
    nr = cmp_kv.shape[2]
    nsb = s // NSA_SEL_BLOCK
    ratio = NSA_SEL_BLOCK // NSA_CMP_STRIDE
    front = NSA_CMP_LEN // NSA_CMP_STRIDE - 1
    n_int = ratio + front
    n_idx = np.arange(nr)[:, None]
    j_idx = np.arange(nsb)[None, :]
    mimp = ((n_idx >= ratio * j_idx - front) & (n_idx <= ratio * j_idx + n_int - 1 - front)
            & (n_idx < nr - 1)).astype(np.float32)
    qslot = SLOT_NSA_Q // NSA_GROUP
    grid_spec = pltpu.PrefetchScalarGridSpec(
        num_scalar_prefetch=1,
        grid=(b, NSA_KV_HEADS, s // tq),
        in_specs=[pl.BlockSpec((1, NSA_GROUP, tq, dh), lambda i, g, j, sl: (i, qslot + g, j, 0)),
                  pl.BlockSpec((1, 1, nr, dh), lambda i, g, j, sl: (i, g, 0, 0)),
                  pl.BlockSpec((1, 1, nr, dh), lambda i, g, j, sl: (i, 2 + g, 0, 0)),
                  pl.BlockSpec((nr, nsb), lambda i, g, j, sl: (0, 0))],
        out_specs=[pl.BlockSpec((1, NSA_GROUP, tq, dh), lambda i, g, j, sl: (i, g, j, 0)),
                   pl.BlockSpec((1, 1, tq, nsb), lambda i, g, j, sl: (i, g, j, 0))])
    return pl.pallas_call(
        _nsa_cmp_kernel,
        grid_spec=grid_spec,
        out_shape=[jax.ShapeDtypeStruct((b, NSA_HEADS, s, dh), F32),
                   jax.ShapeDtypeStruct((b, NSA_KV_HEADS, s, nsb), BF16)],
        compiler_params=_cparams(("parallel", "parallel", "parallel")),
        name="nsa_compressed_select",
    )(jnp.asarray(_alibi_slopes2(NSA_HEADS)), qkv, cmp_kv, cmp_kv, jnp.asarray(mimp, BF16))


NSA_TK = 256


def _nsa_sel_kernel(q_ref, k_ref, v_ref, sel_ref, exp_ref, btab_ref, srow_ref, o_ref,
                    m_ref, l_ref, acc_ref):
    tq, tk = NSA_TQ, NSA_TK
    rows = NSA_GROUP * tq
    dh = q_ref.shape[3]
    q0 = pl.program_id(2) * tq
    q = q_ref[0].reshape(rows, dh)
    sel = sel_ref[0, 0]
    btab = btab_ref[...]
    srow = srow_ref[:, :1]
    diag = q0 // tk

    def scores(kj):
        s = _dot_nt(q, _tile(k_ref, kj, tk)) + btab
        chosen = _dot(sel, exp_ref[kj])
        s = s.reshape(NSA_GROUP, tq, tk) + ((chosen - 1.0) * (-NEG))[None]
        return s.reshape(rows, tk), srow * (kj * tk - q0).astype(F32)

    _init_state(m_ref, l_ref, acc_ref)

    def body(kj, carry):
        s, delta = scores(kj)
        _online_step(s, delta, _tile(v_ref, kj, tk), m_ref, l_ref, acc_ref)
        return carry

    lax.fori_loop(0, diag, body, 0)
    s, delta = scores(diag)
    key = diag * tk + lax.broadcasted_iota(jnp.int32, (tq, tk), 1)
    tpos = q0 + lax.broadcasted_iota(jnp.int32, (tq, tk), 0)
    s = jnp.where((key <= tpos)[None], s.reshape(NSA_GROUP, tq, tk), NEG).reshape(rows, tk)
    _online_step(s, delta, _tile(v_ref, diag, tk), m_ref, l_ref, acc_ref)
    o_ref[0] = (acc_ref[...] / l_ref[:, :1]).reshape(NSA_GROUP, tq, dh)


def _nsa_row_tables(tk):
    slopes = _alibi_slopes2(NSA_HEADS).reshape(NSA_KV_HEADS, NSA_GROUP)
    srow = np.repeat(slopes, NSA_TQ, axis=1)
    btab = srow[:, :, None] * np.arange(tk, dtype=np.float32)[None, None, :]
    srow128 = np.repeat(srow[:, :, None], LANES, axis=2)
    return jnp.asarray(btab, F32), jnp.asarray(srow128, F32)


def nsa_sel_pallas(qkv, sel):
    b, _, s, dh = qkv.shape
    tq, tk = NSA_TQ, NSA_TK
    nsb = s // NSA_SEL_BLOCK
    nkv = s // tk
    rows = NSA_GROUP * tq
    key_blk = (np.arange(nkv)[:, None, None] * tk + np.arange(tk)[None, None, :]) // NSA_SEL_BLOCK
    expand = (np.arange(nsb)[None, :, None] == key_blk).astype(np.float32)
    btab, srow = _nsa_row_tables(tk)
    qslot = SLOT_NSA_Q // NSA_GROUP
    return pl.pallas_call(
        _nsa_sel_kernel,
        grid=(b, NSA_KV_HEADS, s // tq),
        in_specs=[pl.BlockSpec((1, NSA_GROUP, tq, dh), lambda i, g, j: (i, qslot + g, j, 0)),
                  pl.BlockSpec((1, 1, s, dh), lambda i, g, j: (i, SLOT_NSA_KV + 4 + g, 0, 0)),
                  pl.BlockSpec((1, 1, s, dh), lambda i, g, j: (i, SLOT_NSA_KV + 6 + g, 0, 0)),
                  pl.BlockSpec((1, 1, tq, nsb), lambda i, g, j: (i, g, j, 0)),
                  pl.BlockSpec((nkv, nsb, tk), lambda i, g, j: (0, 0, 0)),
                  pl.BlockSpec((None, rows, tk), lambda i, g, j: (g, 0, 0)),
                  pl.BlockSpec((None, rows, LANES), lambda i, g, j: (g, 0, 0))],
        out_specs=pl.BlockSpec((1, NSA_GROUP, tq, dh), lambda i, g, j: (i, g, j, 0)),
        out_shape=jax.ShapeDtypeStruct((b, NSA_HEADS, s, dh), F32),
        scratch_shapes=[pltpu.VMEM((rows, LANES), F32), pltpu.VMEM((rows, LANES), F32),
                        pltpu.VMEM((rows, dh), F32)],
        compiler_params=_cparams(("parallel", "parallel", "arbitrary")),
        name="nsa_selected",
    )(qkv, qkv, qkv, sel, jnp.asarray(expand, BF16), btab, srow)


NSA_WT = 128


def _nsa_win_kernel(q_ref, k_ref, v_ref, btab_ref, srow_ref, oc_ref, os_ref, gate_ref, o_ref,
                    m_ref, l_ref, acc_ref):
    tq = NSA_TQ
    wt = NSA_WT
    rows = NSA_GROUP * tq
    dh = q_ref.shape[3]
    g = pl.program_id(1)
    qi = pl.program_id(2)
    q = q_ref[0].reshape(rows, dh)
    btab = btab_ref[...]
    srow = srow_ref[:, :1]
    col = lax.broadcasted_iota(jnp.int32, (tq, wt), 1)
    row = lax.broadcasted_iota(jnp.int32, (tq, wt), 0)
    span = NSA_WINDOW // wt

    def step(kj, keep):
        s = _dot_nt(q, _tile(k_ref, kj, wt)) + btab
        if keep is not None:
            s = jnp.where(keep[None], s.reshape(NSA_GROUP, tq, wt), NEG).reshape(rows, wt)
        delta = srow * ((kj - qi) * wt).astype(F32)
        _online_step(s, delta, _tile(v_ref, kj, wt), m_ref, l_ref, acc_ref)

    _init_state(m_ref, l_ref, acc_ref)
    step(qi, col <= row)

    def body(d, carry):
        @pl.when(qi - d >= 0)
        def _():
            step(qi - d, None)
        return carry

    lax.fori_loop(1, span, body, 0)

    @pl.when(qi - span >= 0)
    def _():
        step(qi - span, col > row)

    o_w = (acc_ref[...] / l_ref[:, :1]).reshape(NSA_GROUP, tq, dh)
    gates = gate_ref[0]
    for hh in range(NSA_GROUP):
        c0 = 3 * hh
        mix =(gates[:, c0:c0 + 1] * oc_ref[0, hh] + gates[:, c0 + 1:c0 + 2] * os_ref[0, hh]
               + gates[:, c0 + 2:c0 + 3] * o_w[hh])
        o_ref[0, hh] = mix.astype(o_ref.dtype)


def nsa_win_pallas(qkv, o_c, o_s, gates):
    b, _, s, dh = qkv.shape
    tq, wt = NSA_TQ, NSA_WT
    assert tq == wt
    rows = NSA_GROUP * tq
    btab, srow = _nsa_row_tables(wt)
    qslot = SLOT_NSA_Q // NSA_GROUP
    head_blk = pl.BlockSpec((1, NSA_GROUP, tq, dh), lambda i, g, j: (i, g, j, 0))
    gate_spec = [pl.BlockSpec((1, tq, LANES), lambda i, g, j: (i, j, g))]
    return pl.pallas_call(
        _nsa_win_kernel,
        grid=(b, NSA_KV_HEADS, s // tq),
        in_specs=[pl.BlockSpec((1, NSA_GROUP, tq, dh), lambda i, g, j: (i, qslot + g, j, 0)),
                  pl.BlockSpec((1, 1, s, dh), lambda i, g, j: (i, SLOT_NSA_KV + 8 + g, 0, 0)),
                  pl.BlockSpec((1, 1, s, dh), lambda i, g, j: (i, SLOT_NSA_KV + 10 + g, 0, 0)),
                  pl.BlockSpec((None, rows, wt), lambda i, g, j: (g, 0, 0)),
                  pl.BlockSpec((None, rows, LANES), lambda i, g, j: (g, 0, 0)),
                  head_blk, head_blk] + gate_spec,
        out_specs=head_blk,
        out_shape=jax.ShapeDtypeStruct((b, NSA_HEADS, s, dh), BF16),
        scratch_shapes=[pltpu.VMEM((rows, LANES), F32), pltpu.VMEM((rows, LANES), F32),
                        pltpu.VMEM((rows, dh), F32)],
        compiler_params=_cparams(("parallel", "parallel", "arbitrary")),
        name="nsa_window_mix",
    )(qkv, qkv, qkv, btab, srow, o_c, o_s, gates)


def _mem_attn_kernel(q_ref, k_ref, v_ref, o_ref):
    for hh in range(MEM_HEADS):
        s = _dot_nt(q_ref[0, hh], k_ref[0, hh])
        m = jnp.max(s, axis=-1, keepdims=True)
        e = jnp.exp2(s - m)
        p = e / jnp.sum(e, axis=-1, keepdims=True)
        o_ref[0, hh] = _dot(p.astype(BF16), v_ref[0, hh]).astype(o_ref.dtype)


def mem_attn_pallas(qkv, mem_kv, tq=512):
    b, _, s, dh = qkv.shape
    n_mem = mem_kv.shape[2]
    return pl.pallas_call(
        _mem_attn_kernel,
        grid=(b, s // tq),
        in_specs=[pl.BlockSpec((1, MEM_HEADS, tq, dh), lambda i, j: (i, SLOT_MEM_Q // MEM_HEADS, j, 0)),
                  pl.BlockSpec((1, MEM_HEADS, n_mem, dh), lambda i, j: (i, 0, 0, 0)),
                  pl.BlockSpec((1, MEM_HEADS, n_mem, dh), lambda i, j: (i, 1, 0, 0))],
        out_specs=pl.BlockSpec((1, MEM_HEADS, tq, dh), lambda i, j: (i, 0, j, 0)),
        out_shape=jax.ShapeDtypeStruct((b, MEM_HEADS, s, dh), BF16),
        compiler_params=_cparams(("parallel", "parallel")),
        name="memory_attention",
    )(qkv, mem_kv, mem_kv)


def _in_proj_weights(w_in):
    d = w_in.shape[0]
    sizes = (3 * MOBA_HEADS * HEAD_DIM, NSA_HEADS * HEAD_DIM, 6 * NSA_KV_HEADS * HEAD_DIM, 3 * NSA_HEADS,
             3 * FOX_HEADS * HEAD_DIM, FOX_HEADS, MEM_HEADS * HEAD_DIM)
    offs = np.concatenate([[0], np.cumsum(sizes)])
    moba, nsa_q, nsa_kv, nsa_g, fox, fox_f, mem_q = (w_in[:, offs[i]:offs[i + 1]] for i in range(7))
    w_heads = jnp.concatenate([moba, nsa_q, nsa_kv, fox, mem_q], axis=1).astype(BF16)
    assert w_heads.shape == (d, N_SLOTS * HEAD_DIM)
    scale = np.ones((N_SLOTS, HEAD_DIM), np.float32)
    for q_slot, n in ((SLOT_MOBA_Q, MOBA_HEADS), (SLOT_NSA_Q, NSA_HEADS), (SLOT_FOX_Q, FOX_HEADS),
                      (SLOT_MEM_Q, MEM_HEADS)):
        scale[q_slot:q_slot + n] = Q_SCALE
    return w_heads, jnp.asarray(scale.reshape(-1)), nsa_g, fox_f


def _mixer(h32, h16, mem16, w_in, b_forget, w_mem_kv, cmp_pe, cmp_w1, cmp_w2):
    b, s, d = h16.shape
    w_heads, colscale, w_gate, w_forget = _in_proj_weights(w_in)
    gates, c2 = gates_pallas(h32, w_gate, w_forget, b_forget)
    qkv = proj_heads_pallas(h16, w_heads, colscale, tm=min(1024, s), heads_per_step=12)
    mem_kv = proj_heads_pallas(mem16, w_mem_kv.astype(BF16), jnp.ones((w_mem_kv.shape[1],), F32),
                               tm=mem16.shape[1], heads_per_step=2 * MEM_HEADS)
    o_moba = moba_pallas(qkv)
    o_fox = fox_pallas(qkv, c2)
    cmp_kv = nsa_compress_pallas(qkv, cmp_pe, cmp_w1, cmp_w2)
    o_c, sel = nsa_cmp_pallas(qkv, cmp_kv)
    o_s = nsa_sel_pallas(qkv, sel)
    o_nsa = nsa_win_pallas(qkv, o_c, o_s, gates)
    o_mem = mem_attn_pallas(qkv, mem_kv)
    o = jnp.concatenate([o_moba, o_nsa, o_fox, o_mem], axis=1)
    return o.transpose(0, 2, 1, 3).reshape(b * s, -1)


def kernel(x, mem, emb_ln_g, emb_ln_b, w_in, b_forget, w_mem_kv, nsa_cmp_pe, nsa_cmp_w1, nsa_cmp_w2,
           w_out, ln1_g, ln1_b, ffn_w_up, ffn_conv_w, ffn_conv_b, ffn_w_down, ln2_g, ln2_b):
    b, s, d = x.shape
    depth = w_in.shape[0]
    dff = ffn_w_down.shape[1]
    mem16 = mem.astype(BF16)
    h32, h16 = layer_norm_pallas(x.reshape(b * s, d), emb_ln_g, emb_ln_b)
    for l in range(depth):
        o = _mixer(h32.reshape(b, s, d), h16.reshape(b, s, d), mem16, w_in[l], b_forget[l], w_mem_kv[l],
                   nsa_cmp_pe[l], nsa_cmp_w1[l], nsa_cmp_w2[l])
        h32, h16 = matmul_ln_pallas(o, w_out[l].astype(BF16), h32, ln1_g[l], ln1_b[l], tm=512, tk=d)
        a = ffn_up_pallas(h16.reshape(b, s, d), ffn_w_up[l].astype(BF16), ffn_conv_w[l], ffn_conv_b[l],
                          tm=min(1024, s), tn=512)
        h32, h16 = matmul_ln_pallas(a.reshape(b * s, dff), ffn_w_down[l].astype(BF16), h32,
                                    ln2_g[l], ln2_b[l], tm=512, tk=dff // 4)
    return h32.reshape(b, s, d)
```

```python
import functools
import math

import jax
import jax.numpy as jnp
import ml_dtypes
import numpy as np
from jax import lax
from jax.experimental import pallas as pl
from jax.experimental.pallas import tpu as pltpu

F32 = jnp.float32
BF16 = jnp.bfloat16

HEAD_DIM = 64
MOBA_HEADS = 8
NSA_HEADS = 8
NSA_KV_HEADS = 2
NSA_GROUP = NSA_HEADS // NSA_KV_HEADS
FOX_HEADS = 12
MEM_HEADS = 4
MOBA_BLOCK = 256
MOBA_TOPK = 3
NSA_CMP_LEN = 32
NSA_CMP_STRIDE = 16
NSA_SEL_BLOCK = 64
NSA_TOPK = 16
NSA_WINDOW = 512
NSA_FORCE_SCORE = 1.0e4
CONV_WIDTH = 3
LN_EPS = 1e-5
DEPTH = 2
DEEPNORM_ALPHA = (2 * DEPTH) ** 0.25

LOG2E = math.log2(math.e)
Q_SCALE = HEAD_DIM ** -0.5 * LOG2E
NEG = -1.0e30
SKIP = -3.0e38
VMEM_LIMIT = 56 * 1024 * 1024
LANES = 128
KAUG = 2 * HEAD_DIM
VROWS = HEAD_DIM + 16
POS_PERIOD = 256

TQ_MOBA, TQ_NSA, TQ_FOX = 0, 8, 16
TV_MOBA, TV_NSA, TV_FOX = 28, 36, 40
T_SLOTS = 52
KA_MOBA, KA_NSA = 0, 8
KA_SLOTS = 12
N_CMP, N_MEMQ, N_NSAQ = 0, 4, 8
N_SLOTS = 16


def _cparams(sem):
    return pltpu.CompilerParams(dimension_semantics=sem, vmem_limit_bytes=VMEM_LIMIT)


def _split2(x):
    hi = x.astype(BF16)
    return hi, (x - hi.astype(F32)).astype(BF16)


def _split3(x):
    hi = x.astype(BF16)
    r1 = x - hi.astype(F32)
    mid = r1.astype(BF16)
    lo = (r1 - mid.astype(F32)).astype(BF16)
    return hi, mid, lo


def _np_split3(x):
    x = np.asarray(x, np.float32)
    hi = x.astype(ml_dtypes.bfloat16).astype(np.float32)
    r1 = x - hi
    mid = r1.astype(ml_dtypes.bfloat16).astype(np.float32)
    lo = (r1 - mid).astype(ml_dtypes.bfloat16).astype(np.float32)
    return hi, mid, lo


def _dot_nt(a, b):
    return lax.dot_general(a, b, (((1,), (1,)), ((), ())), preferred_element_type=F32)


def _dot_tn(a, b):
    return lax.dot_general(a, b, (((0,), (0,)), ((), ())), preferred_element_type=F32)


def _dot(a, b):
    return jnp.dot(a, b, preferred_element_type=F32)


def _layer_norm_rows(x, g, b):
    mu = jnp.mean(x, axis=-1, keepdims=True)
    xc = x - mu
    var = jnp.mean(xc * xc, axis=-1, keepdims=True)
    return xc * lax.rsqrt(var + LN_EPS) * g + b


def _alibi_slopes2(n):
    return (np.exp2(-8.0 * np.arange(1, n + 1, dtype=np.float64) / n) * LOG2E).astype(np.float32)


def _slope_rows(slopes2, lanes):
    pieces = np.stack(_np_split3(slopes2), axis=1)
    rows = np.zeros((len(slopes2), HEAD_DIM, lanes), np.float32)
    rows[:, :3, :] = pieces[:, :, None]
    return jnp.asarray(rows, BF16)


def _ln_kernel(x_ref, g_ref, b_ref, o32_ref, o16_ref):
    y = _layer_norm_rows(x_ref[...], g_ref[...], b_ref[...])
    o32_ref[...] = y
    o16_ref[...] = y.astype(BF16)


def layer_norm_pallas(x, g, b, tm=512):
    m, d = x.shape
    return pl.pallas_call(
        _ln_kernel,
        grid=(m // tm,),
        in_specs=[pl.BlockSpec((tm, d), lambda i: (i, 0)),
                  pl.BlockSpec((1, d), lambda i: (0, 0)),
                  pl.BlockSpec((1, d), lambda i: (0, 0))],
        out_specs=[pl.BlockSpec((tm, d), lambda i: (i, 0)),
                   pl.BlockSpec((tm, d), lambda i: (i, 0))],
        out_shape=[jax.ShapeDtypeStruct((m, d), F32), jax.ShapeDtypeStruct((m, d), BF16)],
        compiler_params=_cparams(("parallel",)),
        name="layer_norm",
    )(x, g.reshape(1, d), b.reshape(1, d))


def _gates_kernel(h_ref, wg_ref, wf_ref, bf_ref, tri_ref, place_ref, g_ref, caug_ref, carry_ref):
    si = pl.program_id(1)

    @pl.when(si == 0)
    def _():
        carry_ref[...] = jnp.zeros_like(carry_ref)

    h_hi, h_lo = _split2(h_ref[0])
    wg_hi, wg_lo = _split2(wg_ref[...])
    g = _dot(h_hi, wg_hi) + _dot(h_lo, wg_hi) + _dot(h_hi, wg_lo)
    g_ref[0] = 1.0 / (1.0 + jnp.exp(-g))
    wf_hi, wf_lo = _split2(wf_ref[...])
    x = _dot(h_hi, wf_hi) + _dot(h_lo, wf_hi) + _dot(h_hi, wf_lo) + bf_ref[...]
    logf = jnp.minimum(x, 0.0) - jnp.log(1.0 + jnp.exp(-jnp.abs(x)))
    tri = tri_ref[...]
    l_hi, l_mid, l_lo = _split3(logf)
    c = _dot(tri, l_hi) + _dot(tri, l_mid) + _dot(tri, l_lo) + carry_ref[0:1, :]
    carry_ref[...] = jnp.broadcast_to(c[-1:, :], carry_ref.shape)
    n_hi, n_mid, n_lo = _split3(-LOG2E * c)
    caug = _dot(n_hi, place_ref[0]) + _dot(n_mid, place_ref[1]) + _dot(n_lo, place_ref[2])
    caug_ref[0] = caug.astype(BF16)


def gates_pallas(h3, w_gate, w_forget, b_forget, t=512):
    b, s, d = h3.shape
    gw = 3 * NSA_GROUP
    ng = NSA_KV_HEADS * LANES
    wg = jnp.zeros((d, ng), F32)
    for g in range(NSA_KV_HEADS):
        wg = wg.at[:, g * LANES:g * LANES + gw].set(w_gate[:, g * gw:(g + 1) * gw])
    wf = jnp.zeros((d, LANES), F32).at[:, :FOX_HEADS].set(w_forget)
    bf = jnp.zeros((1, LANES), F32).at[0, :FOX_HEADS].set(b_forget)
    tri = (np.arange(t)[None, :] <= np.arange(t)[:, None]).astype(np.float32)
    place = np.zeros((3, LANES, FOX_HEADS * KAUG), np.float32)
    for piece in range(3):
        for hh in range(FOX_HEADS):
            place[piece, hh, hh * KAUG + HEAD_DIM + piece] = 1.0
    nc = FOX_HEADS * KAUG
    return pl.pallas_call(
        _gates_kernel,
        grid=(b, s // t),
        in_specs=[pl.BlockSpec((1, t, d), lambda i, j: (i, j, 0)),
                  pl.BlockSpec((d, ng), lambda i, j: (0, 0)),
                  pl.BlockSpec((d, LANES), lambda i, j: (0, 0)),
                  pl.BlockSpec((1, LANES), lambda i, j: (0, 0)),
                  pl.BlockSpec((t, t), lambda i, j: (0, 0)),
                  pl.BlockSpec((3, LANES, nc), lambda i, j: (0, 0, 0))],
        out_specs=[pl.BlockSpec((1, t, ng), lambda i, j: (i, j, 0)),
                   pl.BlockSpec((1, t, nc), lambda i, j: (i, j, 0))],
        out_shape=[jax.ShapeDtypeStruct((b, s, ng), F32),
                   jax.ShapeDtypeStruct((b, s, nc), BF16)],
        scratch_shapes=[pltpu.VMEM((8, LANES), F32)],
        compiler_params=_cparams(("parallel", "arbitrary")),
        name="gates_cumsum",
    )(h3, wg, wf, bf, jnp.asarray(tri, BF16), jnp.asarray(place, BF16))


def _proj_heads_kernel(x_ref, w_ref, sc_ref, o_ref, *, heads_per_step):
    acc = _dot(x_ref[0], w_ref[...]) * sc_ref[...]
    for j in range(heads_per_step):
        o_ref[0, j] = acc[:, j * HEAD_DIM:(j + 1) * HEAD_DIM].astype(o_ref.dtype)


def proj_heads_pallas(x3, w, colscale, tm, heads_per_step):
    b, s, d = x3.shape
    n = w.shape[1]
    tn = heads_per_step * HEAD_DIM
    return pl.pallas_call(
        functools.partial(_proj_heads_kernel, heads_per_step=heads_per_step),
        grid=(b, s // tm, n // tn),
        in_specs=[pl.BlockSpec((1, tm, d), lambda i, j, k: (i, j, 0)),
                  pl.BlockSpec((d, tn), lambda i, j, k: (0, k)),
                  pl.BlockSpec((1, tn), lambda i, j, k: (0, k))],
        out_specs=pl.BlockSpec((1, heads_per_step, tm, HEAD_DIM), lambda i, j, k: (i, k, j, 0)),
        out_shape=jax.ShapeDtypeStruct((b, n // HEAD_DIM, s, HEAD_DIM), BF16),
        compiler_params=_cparams(("parallel", "parallel", "arbitrary")),
        name="proj_heads",
    )(x3, w, colscale.reshape(1, n))


def _proj_t_kernel(x_ref, wt_ref, sc_ref, o_ref, *, heads_per_step):
    acc = _dot_nt(wt_ref[...], x_ref[0]) * sc_ref[...]
    o_ref[0] = acc.reshape(heads_per_step, HEAD_DIM, acc.shape[1]).astype(o_ref.dtype)


def proj_t_pallas(x3, wt, rowscale, tm, heads_per_step):
    b, s, d = x3.shape
    n = wt.shape[0]
    tn = heads_per_step * HEAD_DIM
    return pl.pallas_call(
        functools.partial(_proj_t_kernel, heads_per_step=heads_per_step),
        grid=(b, s // tm, n // tn),
        in_specs=[pl.BlockSpec((1, tm, d), lambda i, j, k: (i, j, 0)),
                  pl.BlockSpec((tn, d), lambda i, j, k: (k, 0)),
                  pl.BlockSpec((tn, 1), lambda i, j, k: (k, 0))],
        out_specs=pl.BlockSpec((1, heads_per_step, HEAD_DIM, tm), lambda i, j, k: (i, k, 0, j)),
        out_shape=jax.ShapeDtypeStruct((b, n // HEAD_DIM, HEAD_DIM, s), BF16),
        compiler_params=_cparams(("parallel", "parallel", "arbitrary")),
        name="proj_transposed",
    )(x3, wt, rowscale.reshape(n, 1))


def _proj_kaug_kernel(x_ref, w_ref, *rest, heads_per_step, positional):
    o_ref = rest[-1]
    acc = _dot(x_ref[0], w_ref[...])
    tm = acc.shape[0]
    if positional:
        pos = (pl.program_id(1) * tm + lax.broadcasted_iota(jnp.int32, acc.shape, 0)) % POS_PERIOD
        lane = lax.broadcasted_iota(jnp.int32, acc.shape, 1) % KAUG
        acc = acc + jnp.where((lane >= HEAD_DIM) & (lane < HEAD_DIM + 3), pos.astype(F32), 0.0)
    else:
        acc = acc + rest[0][0].astype(F32)
    for j in range(heads_per_step):
        o_ref[0, j] = acc[:, j * KAUG:(j + 1) * KAUG].astype(o_ref.dtype)


def proj_kaug_pallas(x3, w, aug, tm, heads_per_step):
    b, s, d = x3.shape
    n = w.shape[1]
    tn = heads_per_step * KAUG
    in_specs = [pl.BlockSpec((1, tm, d), lambda i, j, k: (i, j, 0)),
                pl.BlockSpec((d, tn), lambda i, j, k: (0, k))]
    args = [x3, w]
    if aug is not None:
        in_specs.append(pl.BlockSpec((1, tm, tn), lambda i, j, k: (i, j, k)))
        args.append(aug)
    return pl.pallas_call(
        functools.partial(_proj_kaug_kernel, heads_per_step=heads_per_step, positional=aug is None),
        grid=(b, s // tm, n // tn),
        in_specs=in_specs,
        out_specs=pl.BlockSpec((1, heads_per_step, tm, KAUG), lambda i, j, k: (i, k, j, 0)),
        out_shape=jax.ShapeDtypeStruct((b, n // KAUG, s, KAUG), BF16),
        compiler_params=_cparams(("parallel", "parallel", "arbitrary")),
        name="proj_keys_aug",
    )(*args)


def _matmul_ln_kernel(x_ref, w_ref, r_ref, g_ref, b_ref, o32_ref, o16_ref, acc_ref):
    k = pl.program_id(1)

    @pl.when(k == 0)
    def _():
        acc_ref[...] = jnp.zeros_like(acc_ref)

    acc_ref[...] += _dot(x_ref[...], w_ref[...])

    @pl.when(k == pl.num_programs(1) - 1)
    def _():
        y = _layer_norm_rows(DEEPNORM_ALPHA * r_ref[...] + acc_ref[...], g_ref[...], b_ref[...])
        o32_ref[...] = y
        o16_ref[...] = y.astype(BF16)


def matmul_ln_pallas(x, w, res, g, b, tm, tk):
    m, kk = x.shape
    d = w.shape[1]
    return pl.pallas_call(
        _matmul_ln_kernel,
        grid=(m // tm, kk // tk),
        in_specs=[pl.BlockSpec((tm, tk), lambda i, k: (i, k)),
                  pl.BlockSpec((tk, d), lambda i, k: (k, 0)),
                  pl.BlockSpec((tm, d), lambda i, k: (i, 0)),
                  pl.BlockSpec((1, d), lambda i, k: (0, 0)),
                  pl.BlockSpec((1, d), lambda i, k: (0, 0))],
        out_specs=[pl.BlockSpec((tm, d), lambda i, k: (i, 0)),
                   pl.BlockSpec((tm, d), lambda i, k: (i, 0))],
        out_shape=[jax.ShapeDtypeStruct((m, d), F32), jax.ShapeDtypeStruct((m, d), BF16)],
        scratch_shapes=[pltpu.VMEM((tm, d), F32)],
        compiler_params=_cparams(("parallel", "arbitrary")),
        name="matmul_ln",
    )(x, w, res, g.reshape(1, d), b.reshape(1, d))


def _out_proj_ln_kernel(*refs, widths):
    n = len(widths)
    x_refs, (w_ref, r_ref, g_ref, b_ref, o32_ref, o16_ref) = refs[:n], refs[n:]
    acc = None
    off = 0
    for x_ref, width in zip(x_refs, widths):
        part = _dot_tn(x_ref[0], w_ref[off:off + width, :])
        acc = part if acc is None else acc + part
        off += width
    y = _layer_norm_rows(DEEPNORM_ALPHA * r_ref[0] + acc, g_ref[...], b_ref[...])
    o32_ref[0] = y
    o16_ref[0] = y.astype(BF16)


def out_proj_ln_pallas(xts, w, res3, g, b, tm):
    bsz, s, d = res3.shape
    widths = tuple(x.shape[1] for x in xts)
    in_specs = [pl.BlockSpec((1, wd, tm), lambda i, j: (i, 0, j)) for wd in widths]
    in_specs += [pl.BlockSpec((w.shape[0], d), lambda i, j: (0, 0)),
                 pl.BlockSpec((1, tm, d), lambda i, j: (i, j, 0)),
                 pl.BlockSpec((1, d), lambda i, j: (0, 0)),
                 pl.BlockSpec((1, d), lambda i, j: (0, 0))]
    return pl.pallas_call(
        functools.partial(_out_proj_ln_kernel, widths=widths),
        grid=(bsz, s // tm),
        in_specs=in_specs,
        out_specs=[pl.BlockSpec((1, tm, d), lambda i, j: (i, j, 0)),
                   pl.BlockSpec((1, tm, d), lambda i, j: (i, j, 0))],
        out_shape=[jax.ShapeDtypeStruct((bsz, s, d), F32), jax.ShapeDtypeStruct((bsz, s, d), BF16)],
        compiler_params=_cparams(("parallel", "parallel")),
        name="out_proj_ln",
    )(*xts, w, res3, g.reshape(1, d), b.reshape(1, d))


HALO = 16


def _gelu_tanh(x):
    return 0.5 * x * (1.0 + jnp.tanh(math.sqrt(2.0 / math.pi) * (x + 0.044715 * x * x * x)))


def _ffn_up_kernel(x_ref, xh_ref, wu_ref, wg_ref, cw_ref, cb_ref, o_ref):
    j = pl.program_id(1)
    x = x_ref[0]
    u = _dot(x, wu_ref[...])
    g = _dot(x, wg_ref[...])
    gh = _dot(xh_ref[0], wg_ref[...]) * jnp.where(j > 0, 1.0, 0.0)
    row = lax.broadcasted_iota(jnp.int32, g.shape, 0)
    prev1 = gh[HALO - 1:HALO, :]
    prev2 = gh[HALO - 2:HALO - 1, :]
    g_m1 = jnp.where(row == 0, prev1, pltpu.roll(g, 1, axis=0))
    g_m2 = jnp.where(row == 0, prev2, jnp.where(row == 1, prev1, pltpu.roll(g, 2, axis=0)))
    cw = cw_ref[...]
    gc = cb_ref[...] + cw[0:1] * g_m2 + cw[1:2] * g_m1 + cw[2:3] * g
    o_ref[0] = (_gelu_tanh(gc) * u).astype(o_ref.dtype)


def ffn_up_pallas(x3, w_up, conv_w, conv_b, tm, tn):
    b, s, d = x3.shape
    dff = w_up.shape[1] // 2
    nt = dff // tn
    hb = tm // HALO
    cw = jnp.zeros((8, dff), F32).at[:CONV_WIDTH].set(conv_w)
    return pl.pallas_call(
        _ffn_up_kernel,
        grid=(b, s // tm, nt),
        in_specs=[pl.BlockSpec((1, tm, d), lambda i, j, k: (i, j, 0)),
                  pl.BlockSpec((1, HALO, d), lambda i, j, k: (i, jnp.maximum(j * hb - 1, 0), 0)),
                  pl.BlockSpec((d, tn), lambda i, j, k: (0, k)),
                  pl.BlockSpec((d, tn), lambda i, j, k: (0, k + nt)),
                  pl.BlockSpec((8, tn), lambda i, j, k: (0, k)),
                  pl.BlockSpec((1, tn), lambda i, j, k: (0, k))],
        out_specs=pl.BlockSpec((1, tm, tn), lambda i, j, k: (i, j, k)),
        out_shape=jax.ShapeDtypeStruct((b, s, dff), BF16),
        compiler_params=_cparams(("parallel", "parallel", "arbitrary")),
        name="ffn_up",
    )(x3, x3, w_up, w_up, cw, conv_b.reshape(1, dff))


def _online_step_t(st, delta, vaug, m_ref, acc_ref):
    m_prev = m_ref[...]
    m_cur = jnp.max(st, axis=0, keepdims=True) + delta
    m_new = jnp.maximum(m_prev, m_cur)
    alpha = jnp.exp2(m_prev - m_new)
    pt = jnp.exp2(st - (m_new - delta)).astype(BF16)
    acc_ref[...] = alpha * acc_ref[...] + _dot(vaug, pt)
    m_ref[...] = m_new


def _init_state_t(m_ref, acc_ref):
    m_ref[...] = jnp.full(m_ref.shape, NEG, F32)
    acc_ref[...] = jnp.zeros(acc_ref.shape, F32)


def _finish_t(acc_ref):
    acc = acc_ref[...]
    return acc[:HEAD_DIM] / acc[HEAD_DIM:HEAD_DIM + 1]


def _fill_vaug(vaug_ref, vt_ref):
    s = vaug_ref.shape[1]
    vaug_ref[0:HEAD_DIM, :] = vt_ref[0, 0]
    pad = lax.broadcasted_iota(jnp.int32, (VROWS - HEAD_DIM, s), 0)
    vaug_ref[HEAD_DIM:VROWS, :] = jnp.where(pad == 0, 1.0, 0.0).astype(BF16)


def _ktile(ref, idx, size):
    return ref[0, 0, pl.ds(pl.multiple_of(idx * size, size), size), :]


def _vtile(ref, idx, size):
    return ref[:, pl.ds(pl.multiple_of(idx * size, size), size)]


FOX_TQ = 1024
FOX_TK = 256


def _fox_kernel(qt_ref, k_ref, vt_ref, o_ref, m_ref, acc_ref, vaug_ref):
    tq, tk = o_ref.shape[3], FOX_TK
    nd = tq // tk
    qi = pl.program_id(2)

    @pl.when(qi == 0)
    def _():
        _fill_vaug(vaug_ref, vt_ref)

    ones3 = jnp.where(lax.broadcasted_iota(jnp.int32, (HEAD_DIM, tq), 0) < 3, 1.0, 0.0).astype(BF16)
    qaug = jnp.concatenate([qt_ref[0, 0], ones3], axis=0)
    _init_state_t(m_ref, acc_ref)
    key = lax.broadcasted_iota(jnp.int32, (tk, tq), 0)
    qry = lax.broadcasted_iota(jnp.int32, (tk, tq), 1)
    for d in range(nd):
        kj = qi * nd + d
        st = jnp.where(key + d * tk <= qry, _dot(_ktile(k_ref, kj, tk), qaug), NEG)
        _online_step_t(st, 0.0, _vtile(vaug_ref, kj, tk), m_ref, acc_ref)

    def body(kj, carry):
        _online_step_t(_dot(_ktile(k_ref, kj, tk), qaug), 0.0, _vtile(vaug_ref, kj, tk), m_ref, acc_ref)
        return carry

    lax.fori_loop(0, qi * nd, body, 0)
    o_ref[0, 0] = _finish_t(acc_ref).astype(o_ref.dtype)


def fox_pallas(qvt, kaug):
    b, _, dh, s = qvt.shape
    tq = min(FOX_TQ, s)
    assert tq % FOX_TK == 0
    out = pl.pallas_call(
        _fox_kernel,
        grid=(b, FOX_HEADS, s // tq),
        in_specs=[pl.BlockSpec((1, 1, dh, tq), lambda i, h, j: (i, TQ_FOX + h, 0, j)),
                  pl.BlockSpec((1, 1, s, KAUG), lambda i, h, j: (i, h, 0, 0)),
                  pl.BlockSpec((1, 1, dh, s), lambda i, h, j: (i, TV_FOX + h, 0, 0))],
        out_specs=pl.BlockSpec((1, 1, dh, tq), lambda i, h, j: (i, h, 0, j)),
        out_shape=jax.ShapeDtypeStruct((b, FOX_HEADS, dh, s), BF16),
        scratch_shapes=[pltpu.VMEM((1, tq), F32), pltpu.VMEM((VROWS, tq), F32),
                        pltpu.VMEM((VROWS, s), BF16)],
        compiler_params=_cparams(("parallel", "parallel", "arbitrary")),
        name="fox_attention",
    )(qvt, kaug, qvt)
    return out.reshape(b, FOX_HEADS * dh, s)


def _moba_kernel(slope_ref, qt_ref, k_ref, vt_ref, ind_ref, srow_ref, o_ref,
                 m_ref, acc_ref, vaug_ref, km_ref, sel_ref):
    t = MOBA_BLOCK
    tq = o_ref.shape[3]
    nd = tq // t
    h = pl.program_id(1)
    qi = pl.program_id(2)
    slope2 = slope_ref[h]

    @pl.when(qi == 0)
    def _():
        _fill_vaug(vaug_ref, vt_ref)
        km_ref[...] = _dot(ind_ref[...], k_ref[0, 0]) * (1.0 / MOBA_BLOCK)

    qt = qt_ref[0, 0]
    km_hi, km_lo = _split2(km_ref[...])
    q0 = jnp.concatenate([qt, jnp.zeros_like(qt)], axis=0)
    gate = _dot(km_hi, q0) + _dot(km_lo, q0)
    blk = lax.broadcasted_iota(jnp.int32, gate.shape, 0)
    lane_blk = lax.broadcasted_iota(jnp.int32, (1, tq), 1) // t
    valid = blk < qi * nd + lane_blk
    work = jnp.where(valid, gate, -jnp.inf)
    sel = jnp.zeros(gate.shape, F32)
    big = jnp.int32(2 ** 30)
    for _ in range(MOBA_TOPK):
        mx = jnp.max(work, axis=0, keepdims=True)
        first = jnp.min(jnp.where(work == mx, blk, big), axis=0, keepdims=True)
        hit = blk == first
        sel = jnp.where(hit, 1.0, sel)
        work = jnp.where(hit, -jnp.inf, work)
    sel_ref[...] = jnp.where(valid, sel, 0.0)

    qaug = jnp.concatenate([qt, srow_ref[...]], axis=0)
    _init_state_t(m_ref, acc_ref)
    key = lax.broadcasted_iota(jnp.int32, (t, tq), 0)
    qry = lax.broadcasted_iota(jnp.int32, (t, tq), 1)

    def tile_constant(kj, d):
        sees = sel_ref[pl.ds(kj, 1), :] > 0.0
        if d is not None:
            sees = sees | (lane_blk == d)
        return jnp.where(sees, slope2 * ((kj - qi * nd) * t).astype(F32), SKIP)

    for d in range(nd):
        kj = qi * nd + d
        st = jnp.where(key + d * t <= qry, _dot(_ktile(k_ref, kj, t), qaug), NEG)
        _online_step_t(st, tile_constant(kj, d), _vtile(vaug_ref, kj, t), m_ref, acc_ref)

    def body(kj, carry):
        _online_step_t(_dot(_ktile(k_ref, kj, t), qaug), tile_constant(kj, None), _vtile(vaug_ref, kj, t),
                       m_ref, acc_ref)
        return carry

    lax.fori_loop(0, qi * nd, body, 0)
    o_ref[0, 0] = _finish_t(acc_ref).astype(o_ref.dtype)


MOBA_TQ = 1024


def moba_pallas(qvt, kaug):
    b, _, dh, s = qvt.shape
    t = MOBA_BLOCK
    tq = min(MOBA_TQ, s)
    assert POS_PERIOD == t and tq % t == 0
    nb = s // t
    nbp = max(LANES, nb)
    ind = np.zeros((nbp, s), np.float32)
    ind[np.arange(s) // t, np.arange(s)] = 1.0
    slopes2 = _alibi_slopes2(MOBA_HEADS)
    grid_spec = pltpu.PrefetchScalarGridSpec(
        num_scalar_prefetch=1,
        grid=(b, MOBA_HEADS, s // tq),
        in_specs=[pl.BlockSpec((1, 1, dh, tq), lambda i, h, j, sl: (i, TQ_MOBA + h, 0, j)),
                  pl.BlockSpec((1, 1, s, KAUG), lambda i, h, j, sl: (i, KA_MOBA + h, 0, 0)),
                  pl.BlockSpec((1, 1, dh, s), lambda i, h, j, sl: (i, TV_MOBA + h, 0, 0)),
                  pl.BlockSpec((nbp, s), lambda i, h, j, sl: (0, 0)),
                  pl.BlockSpec((None, dh, tq), lambda i, h, j, sl: (h, 0, 0))],
        out_specs=pl.BlockSpec((1, 1, dh, tq), lambda i, h, j, sl: (i, h, 0, j)),
        scratch_shapes=[pltpu.VMEM((1, tq), F32), pltpu.VMEM((VROWS, tq), F32),
                        pltpu.VMEM((VROWS, s), BF16), pltpu.VMEM((nbp, KAUG), F32),
                        pltpu.VMEM((nbp, tq), F32)])
    out = pl.pallas_call(
        _moba_kernel,
        grid_spec=grid_spec,
        out_shape=jax.ShapeDtypeStruct((b, MOBA_HEADS, dh, s), BF16),
        compiler_params=_cparams(("parallel", "parallel", "arbitrary")),
        name="moba_attention",
    )(jnp.asarray(slopes2), qvt, kaug, qvt, jnp.asarray(ind, BF16), _slope_rows(slopes2, tq))
    return out.reshape(b, MOBA_HEADS * dh, s)


def _nsa_compress_kernel(x_ref, w1a_ref, w1b_ref, pe_ref, w1_ref, w2_ref, w2t_ref, o_ref, ot_ref):
    nr = x_ref.shape[2]
    x = x_ref[0, 0]

    def near_f32(xb, w):
        w_hi, w_lo = _split2(w)
        return _dot(xb, w_hi) + _dot(xb, w_lo)

    a = near_f32(x, w1a_ref[0])
    bm = near_f32(x, w1b_ref[0])
    pe_hi, pe_mid, pe_lo = _split3(pe_ref[0])
    w1_hi, w1_lo = _split2(w1_ref[0])
    pe_term = (_dot(pe_hi, w1_hi) + _dot(pe_mid, w1_hi) + _dot(pe_lo, w1_hi)
               + _dot(pe_hi, w1_lo) + _dot(pe_mid, w1_lo))[0:1]
    pre = a + pltpu.roll(bm, nr - 1, axis=0) + pe_term
    hid = _gelu_tanh(pre)
    h_hi, h_mid, h_lo = _split3(hid)
    w2_hi, w2_lo = _split2(w2_ref[0])
    o_ref[0, 0] = (_dot(h_hi, w2_hi) + _dot(h_mid, w2_hi) + _dot(h_lo, w2_hi)
                   + _dot(h_hi, w2_lo) + _dot(h_mid, w2_lo))
    t_hi, t_lo = _split2(w2t_ref[0])
    ot_ref[0, 0] = (_dot_nt(t_hi, h_hi) + _dot_nt(t_hi, h_mid) + _dot_nt(t_hi, h_lo)
                    + _dot_nt(t_lo, h_hi) + _dot_nt(t_lo, h_mid))


def nsa_compress_pallas(nat, pe, w1, w2):
    b, _, s, dh = nat.shape
    nr = s // NSA_CMP_STRIDE
    half = NSA_CMP_STRIDE * dh
    hid = w1.shape[-1]
    x = nat[:, N_CMP:N_CMP + 4].reshape(b, 4, nr, half)
    w1f = w1.reshape(2, NSA_CMP_LEN * dh, hid)
    pef = jnp.zeros((2, 8, NSA_CMP_LEN * dh), F32).at[:, 0].set(pe.reshape(2, NSA_CMP_LEN * dh))
    return pl.pallas_call(
        _nsa_compress_kernel,
        grid=(b, 4),
        in_specs=[pl.BlockSpec((1, 1, nr, half), lambda i, j: (i, j, 0, 0)),
                  pl.BlockSpec((1, half, hid), lambda i, j: (j // 2, 0, 0)),
                  pl.BlockSpec((1, half, hid), lambda i, j: (j // 2, 1, 0)),
                  pl.BlockSpec((1, 8, 2 * half), lambda i, j: (j // 2, 0, 0)),
                  pl.BlockSpec((1, 2 * half, hid), lambda i, j: (j // 2, 0, 0)),
                  pl.BlockSpec((1, hid, dh), lambda i, j: (j // 2, 0, 0)),
                  pl.BlockSpec((1, dh, hid), lambda i, j: (j // 2, 0, 0))],
        out_specs=[pl.BlockSpec((1, 1, nr, dh), lambda i, j: (i, j, 0, 0)),
                   pl.BlockSpec((1, 1, dh, nr), lambda i, j: (i, j, 0, 0))],
        out_shape=[jax.ShapeDtypeStruct((b, 4, nr, dh), F32),
                   jax.ShapeDtypeStruct((b, 4, dh, nr), F32)],
        compiler_params=_cparams(("parallel", "parallel")),
        name="nsa_compress",
    )(x, w1f, w1f, pef, w1f, w2, jnp.swapaxes(w2, 1, 2))


NSA_TQ = 128


def _nsa_cmp_kernel(slope_ref, q_ref, kc_ref, vct_ref, mimp_ref, oct_ref, selt_ref):
    tq = NSA_TQ
    g = pl.program_id(1)
    q0 = pl.program_id(2) * tq
    nr = kc_ref.shape[2]
    nsb = selt_ref.shape[2]
    kc_hi, kc_lo = _split2(kc_ref[0, 0])
    vct = vct_ref[0, 0].astype(BF16)
    t_rows = q0 + lax.broadcasted_iota(jnp.int32, (tq, 1), 0)
    cmp_end = NSA_CMP_STRIDE * lax.broadcasted_iota(jnp.int32, (1, nr), 1) + (NSA_CMP_LEN - 1)
    mask = cmp_end <= t_rows
    colrel = (cmp_end - q0).astype(F32)
    psum = jnp.zeros((tq, nr), F32)
    for hh in range(NSA_GROUP):
        q = q_ref[0, hh]
        s = _dot_nt(q, kc_hi) + _dot_nt(q, kc_lo) + slope_ref[g * NSA_GROUP + hh] * colrel
        s = jnp.where(mask, s, -jnp.inf)
        m = jnp.max(s, axis=-1, keepdims=True)
        m = jnp.where(m > -jnp.inf, m, 0.0)
        e = jnp.where(mask, jnp.exp2(s - m), 0.0)
        p = e / jnp.maximum(jnp.sum(e, axis=-1, keepdims=True), 1e-30)
        oct_ref[0, hh] = _dot_nt(vct, p.astype(BF16))
        psum = psum + p
    p_hi, p_mid, p_lo = _split3(psum)
    mimp = mimp_ref[...]
    imp = _dot(p_hi, mimp) + _dot(p_mid, mimp) + _dot(p_lo, mimp)
    blk = lax.broadcasted_iota(jnp.int32, (tq, nsb), 1)
    jt = t_rows // NSA_SEL_BLOCK
    forced = (blk == 0) | (blk == jt) | (blk == jt - 1)
    imp = jnp.where(forced, NSA_FORCE_SCORE, imp)
    valid = blk * NSA_SEL_BLOCK <= t_rows
    work0 = jnp.where(valid, imp, -jnp.inf)
    big = jnp.int32(2 ** 30)

    def pick(_, carry):
        work, sel = carry
        mx = jnp.max(work, axis=-1, keepdims=True)
        first = jnp.min(jnp.where(work == mx, blk, big), axis=-1, keepdims=True)
        hit = blk == first
        return jnp.where(hit, -jnp.inf, work), jnp.where(hit, 1.0, sel)

    _, sel = lax.fori_loop(0, min(NSA_TOPK, nsb), pick, (work0, jnp.zeros((tq, nsb), F32)))
    selt_ref[0, 0] = jnp.where(valid, sel, 0.0).T


def nsa_cmp_pallas(nat, cmp_kv, cmp_kvt):
    b, _, s, dh = nat.shape
    tq = NSA_TQ
    nr = cmp_kv.shape[2]
    nsb = s // NSA_SEL_BLOCK
    ratio = NSA_SEL_BLOCK // NSA_CMP_STRIDE
    front = NSA_CMP_LEN // NSA_CMP_STRIDE - 1
    n_int = ratio + front
    n_idx = np.arange(nr)[:, None]
    j_idx = np.arange(nsb)[None, :]
    mimp = ((n_idx >= ratio * j_idx - front) & (n_idx <= ratio * j_idx + n_int - 1 - front)
            & (n_idx < nr - 1)).astype(np.float32)
    qslot = N_NSAQ // NSA_GROUP
    grid_spec = pltpu.PrefetchScalarGridSpec(
        num_scalar_prefetch=1,
        grid=(b, NSA_KV_HEADS, s // tq),
        in_specs=[pl.BlockSpec((1, NSA_GROUP, tq, dh), lambda i, g, j, sl: (i, qslot + g, j, 0)),
                  pl.BlockSpec((1, 1, nr, dh), lambda i, g, j, sl: (i, g, 0, 0)),
                  pl.BlockSpec((1, 1, dh, nr), lambda i, g, j, sl: (i, 2 + g, 0, 0)),
                  pl.BlockSpec((nr, nsb), lambda i, g, j, sl: (0, 0))],
        out_specs=[pl.BlockSpec((1, NSA_GROUP, dh, tq), lambda i, g, j, sl: (i, g, 0, j)),
                   pl.BlockSpec((1, 1, nsb, tq), lambda i, g, j, sl: (i, g, 0, j))])
    return pl.pallas_call(
        _nsa_cmp_kernel,
        grid_spec=grid_spec,
        out_shape=[jax.ShapeDtypeStruct((b, NSA_HEADS, dh, s), F32),
                   jax.ShapeDtypeStruct((b, NSA_KV_HEADS, nsb, s), F32)],
        compiler_params=_cparams(("parallel", "parallel", "parallel")),
        name="nsa_compressed_select",
    )(jnp.asarray(_alibi_slopes2(NSA_HEADS)), nat, cmp_kv, cmp_kvt, jnp.asarray(mimp, BF16))


NSA_TK = 256
NSA_SEL_TQ = 256
NSA_LANES = NSA_GROUP * NSA_TQ


def _nsa_qaug(qt_ref, srow_ref):
    return jnp.concatenate(
        [jnp.concatenate([qt_ref[0, hh], srow_ref[hh]], axis=0) for hh in range(NSA_GROUP)], axis=1)


def _nsa_sel_kernel(qt_ref, k_ref, vt_ref, selt_ref, srow_ref, slane_ref, o_ref, m_ref, acc_ref, vaug_ref):
    tq, tk = NSA_SEL_TQ, NSA_TK
    per_tile = tk // NSA_SEL_BLOCK
    qi = pl.program_id(2)
    q0 = qi * tq

    @pl.when(qi == 0)
    def _():
        _fill_vaug(vaug_ref, vt_ref)

    qaug = _nsa_qaug(qt_ref, srow_ref)
    slane = slane_ref[0:1, :]
    diag = q0 // tk

    def scores(kj):
        st = _dot(_ktile(k_ref, kj, tk), qaug)
        rows = [jnp.broadcast_to(selt_ref[0, 0, pl.ds(kj * per_tile + c, 1), :], (NSA_SEL_BLOCK, tq))
                for c in range(per_tile)]
        bias = (jnp.concatenate(rows, axis=0) - 1.0) * (-NEG)
        st = st + jnp.concatenate([bias] * NSA_GROUP, axis=1)
        return st, slane * (kj * tk - q0).astype(F32)

    _init_state_t(m_ref, acc_ref)

    def body(kj, carry):
        st, delta = scores(kj)
        _online_step_t(st, delta, _vtile(vaug_ref, kj, tk), m_ref, acc_ref)
        return carry

    lax.fori_loop(0, diag, body, 0)
    st, delta = scores(diag)
    key = diag * tk + lax.broadcasted_iota(jnp.int32, (tk, tq), 0)
    qry = q0 + lax.broadcasted_iota(jnp.int32, (tk, tq), 1)
    causal = jnp.concatenate([key <= qry] * NSA_GROUP, axis=1)
    _online_step_t(jnp.where(causal, st, NEG), delta, _vtile(vaug_ref, diag, tk), m_ref, acc_ref)
    out = _finish_t(acc_ref)
    for hh in range(NSA_GROUP):
        o_ref[0, hh] = out[:, hh * tq:(hh + 1) * tq]


def _nsa_tables(tq):
    slopes2 = _alibi_slopes2(NSA_HEADS)
    srow = _slope_rows(slopes2, tq)
    slane = np.repeat(slopes2.reshape(NSA_KV_HEADS, NSA_GROUP), tq, axis=1)
    slane8 = np.repeat(slane[:, None, :], 8, axis=1)
    return srow, jnp.asarray(slane8, F32)


def nsa_sel_pallas(qvt, kaug, selt):
    b, _, dh, s = qvt.shape
    tq = NSA_SEL_TQ
    assert tq == NSA_TK
    lanes = NSA_GROUP * tq
    nsb = s // NSA_SEL_BLOCK
    srow, slane = _nsa_tables(tq)
    return pl.pallas_call(
        _nsa_sel_kernel,
        grid=(b, NSA_KV_HEADS, s // tq),
        in_specs=[pl.BlockSpec((1, NSA_GROUP, dh, tq), lambda i, g, j: (i, TQ_NSA // NSA_GROUP + g, 0, j)),
                  pl.BlockSpec((1, 1, s, KAUG), lambda i, g, j: (i, KA_NSA + g, 0, 0)),
                  pl.BlockSpec((1, 1, dh, s), lambda i, g, j: (i, TV_NSA + g, 0, 0)),
                  pl.BlockSpec((1, 1, nsb, tq), lambda i, g, j: (i, g, 0, j)),
                  pl.BlockSpec((NSA_GROUP, dh, tq), lambda i, g, j: (g, 0, 0)),
                  pl.BlockSpec((None, 8, lanes), lambda i, g, j: (g, 0, 0))],
        out_specs=pl.BlockSpec((1, NSA_GROUP, dh, tq), lambda i, g, j: (i, g, 0, j)),
        out_shape=jax.ShapeDtypeStruct((b, NSA_HEADS, dh, s), F32),
        scratch_shapes=[pltpu.VMEM((1, lanes), F32), pltpu.VMEM((VROWS, lanes), F32),
                        pltpu.VMEM((VROWS, s), BF16)],
        compiler_params=_cparams(("parallel", "parallel", "arbitrary")),
        name="nsa_selected",
    )(qvt, kaug, qvt, selt, srow, slane)


NSA_WT = 128


def _nsa_win_kernel(qt_ref, k_ref, vt_ref, srow_ref, slane_ref, oc_ref, os_ref, gate_ref, o_ref,
                    m_ref, acc_ref, vaug_ref):
    tq = NSA_TQ
    wt = NSA_WT
    qi = pl.program_id(2)

    @pl.when(qi == 0)
    def _():
        _fill_vaug(vaug_ref, vt_ref)

    qaug = _nsa_qaug(qt_ref, srow_ref)
    slane = slane_ref[0:1, :]
    key = lax.broadcasted_iota(jnp.int32, (wt, tq), 0)
    qry = lax.broadcasted_iota(jnp.int32, (wt, tq), 1)
    span = NSA_WINDOW // wt

    def step(kj, keep):
        st = _dot(_ktile(k_ref, kj, wt), qaug)
        if keep is not None:
            st = jnp.where(jnp.concatenate([keep] * NSA_GROUP, axis=1), st, NEG)
        base = (kj * wt) // POS_PERIOD * POS_PERIOD - qi * tq
        _online_step_t(st, slane * base.astype(F32), _vtile(vaug_ref, kj, wt), m_ref, acc_ref)

    _init_state_t(m_ref, acc_ref)
    step(qi, key <= qry)

    def body(d, carry):
        @pl.when(qi - d >= 0)
        def _():
            step(qi - d, None)
        return carry

    lax.fori_loop(1, span, body, 0)

    @pl.when(qi - span >= 0)
    def _():
        step(qi - span, key > qry)

    o_w = _finish_t(acc_ref)
    gt = gate_ref[0].T
    for hh in range(NSA_GROUP):
        c0 = 3 * hh
        mix = (gt[c0:c0 + 1] * oc_ref[0, hh] + gt[c0 + 1:c0 + 2] * os_ref[0, hh]
               + gt[c0 + 2:c0 + 3] * o_w[:, hh * tq:(hh + 1) * tq])
        o_ref[0, hh] = mix.astype(o_ref.dtype)


def nsa_win_pallas(qvt, kaug, o_c, o_s, gates):
    b, _, dh, s = qvt.shape
    tq, wt = NSA_TQ, NSA_WT
    assert tq == wt and tq == LANES
    srow, slane = _nsa_tables(tq)
    head_blk = pl.BlockSpec((1, NSA_GROUP, dh, tq), lambda i, g, j: (i, g, 0, j))
    out = pl.pallas_call(
        _nsa_win_kernel,
        grid=(b, NSA_KV_HEADS, s // tq),
        in_specs=[pl.BlockSpec((1, NSA_GROUP, dh, tq), lambda i, g, j: (i, TQ_NSA // NSA_GROUP + g, 0, j)),
                  pl.BlockSpec((1, 1, s, KAUG), lambda i, g, j: (i, KA_NSA + 2 + g, 0, 0)),
                  pl.BlockSpec((1, 1, dh, s), lambda i, g, j: (i, TV_NSA + 2 + g, 0, 0)),
                  pl.BlockSpec((NSA_GROUP, dh, tq), lambda i, g, j: (g, 0, 0)),
                  pl.BlockSpec((None, 8, NSA_LANES), lambda i, g, j: (g, 0, 0)),
                  head_blk, head_blk,
                  pl.BlockSpec((1, tq, LANES), lambda i, g, j: (i, j, g))],
        out_specs=head_blk,
        out_shape=jax.ShapeDtypeStruct((b, NSA_HEADS, dh, s), BF16),
        scratch_shapes=[pltpu.VMEM((1, NSA_LANES), F32), pltpu.VMEM((VROWS, NSA_LANES), F32),
                        pltpu.VMEM((VROWS, s), BF16)],
        compiler_params=_cparams(("parallel", "parallel", "arbitrary")),
        name="nsa_window_mix",
    )(qvt, kaug, qvt, srow, slane, o_c, o_s, gates)
    return out.reshape(b, NSA_HEADS * dh, s)


def _mem_attn_kernel(q_ref, k_ref, vt_ref, o_ref):
    for hh in range(MEM_HEADS):
        s = _dot_nt(q_ref[0, hh], k_ref[0, hh])
        m = jnp.max(s, axis=-1, keepdims=True)
        e = jnp.exp2(s - m)
        p = e / jnp.sum(e, axis=-1, keepdims=True)
        o_ref[0, hh] = _dot_nt(vt_ref[0, hh], p.astype(BF16)).astype(o_ref.dtype)


def mem_attn_pallas(nat, mem_k, mem_vt, tq=512):
    b, _, s, dh = nat.shape
    n_mem = mem_k.shape[2]
    out = pl.pallas_call(
        _mem_attn_kernel,
        grid=(b, s // tq),
        in_specs=[pl.BlockSpec((1, MEM_HEADS, tq, dh), lambda i, j: (i, N_MEMQ // MEM_HEADS, j, 0)),
                  pl.BlockSpec((1, MEM_HEADS, n_mem, dh), lambda i, j: (i, 0, 0, 0)),
                  pl.BlockSpec((1, MEM_HEADS, dh, n_mem), lambda i, j: (i, 0, 0, 0))],
        out_specs=pl.BlockSpec((1, MEM_HEADS, dh, tq), lambda i, j: (i, 0, 0, j)),
        out_shape=jax.ShapeDtypeStruct((b, MEM_HEADS, dh, s), BF16),
        compiler_params=_cparams(("parallel", "parallel")),
        name="memory_attention",
    )(nat, mem_k, mem_vt)
    return out.reshape(b, MEM_HEADS * dh, s)


def _pad_key_cols(w):
    d, n = w.shape
    w3 = w.reshape(d, n // HEAD_DIM, HEAD_DIM)
    return jnp.concatenate([w3, jnp.zeros_like(w3)], axis=2).reshape(d, 2 * n)


def _in_proj_weights(w_in):
    hd = HEAD_DIM
    sizes = (3 * MOBA_HEADS * hd, NSA_HEADS * hd, 6 * NSA_KV_HEADS * hd, 3 * NSA_HEADS,
             3 * FOX_HEADS * hd, FOX_HEADS, MEM_HEADS * hd)
    offs = np.concatenate([[0], np.cumsum(sizes)])
    moba, nsa_q, nsa_kv, nsa_g, fox, fox_f, mem_q = (w_in[:, offs[i]:offs[i + 1]] for i in range(7))
    mh, fh, g2 = MOBA_HEADS * hd, FOX_HEADS * hd, NSA_KV_HEADS * hd
    moba_q, moba_k, moba_v = moba[:, :mh], moba[:, mh:2 * mh], moba[:, 2 * mh:]
    fox_q, fox_k, fox_v = fox[:, :fh], fox[:, fh:2 * fh], fox[:, 2 * fh:]
    k_cmp, v_cmp, k_slc, v_slc, k_win, v_win = (nsa_kv[:, i * g2:(i + 1) * g2] for i in range(6))
    w_t = jnp.concatenate([moba_q, nsa_q, fox_q, moba_v, v_slc, v_win, fox_v], axis=1).T.astype(BF16)
    t_scale = np.ones((T_SLOTS * hd,), np.float32)
    t_scale[:TV_MOBA * hd] = Q_SCALE
    w_ka = _pad_key_cols(jnp.concatenate([moba_k, k_slc, k_win], axis=1)).astype(BF16)
    w_kf = _pad_key_cols(fox_k).astype(BF16)
    w_nat = jnp.concatenate([k_cmp, v_cmp, mem_q, nsa_q], axis=1).astype(BF16)
    n_scale = np.ones((N_SLOTS * hd,), np.float32)
    n_scale[N_MEMQ * hd:] = Q_SCALE
    return w_t, jnp.asarray(t_scale), w_ka, w_kf, w_nat, jnp.asarray(n_scale), nsa_g, fox_f


def _mixer(h32, h16, mem16, w_in, b_forget, w_mem_kv, cmp_pe, cmp_w1, cmp_w2):
    b, s, d = h16.shape
    tm = min(1024, s)
    w_t, t_scale, w_ka, w_kf, w_nat, n_scale, w_gate, w_forget = _in_proj_weights(w_in)
    gates, caug = gates_pallas(h32, w_gate, w_forget, b_forget)
    qvt = proj_t_pallas(h16, w_t, t_scale, tm=tm, heads_per_step=13)
    k_alibi = proj_kaug_pallas(h16, w_ka, None, tm=tm, heads_per_step=6)
    k_fox = proj_kaug_pallas(h16, w_kf, caug, tm=tm, heads_per_step=6)
    nat = proj_heads_pallas(h16, w_nat, n_scale, tm=tm, heads_per_step=8)
    n_mem = mem16.shape[1]
    mk = MEM_HEADS * HEAD_DIM
    mem_k = proj_heads_pallas(mem16, w_mem_kv[:, :mk].astype(BF16), jnp.ones((mk,), F32),
                              tm=n_mem, heads_per_step=MEM_HEADS)
    mem_vt = proj_t_pallas(mem16, w_mem_kv[:, mk:].T.astype(BF16), jnp.ones((mk,), F32),
                           tm=n_mem, heads_per_step=MEM_HEADS)
    o_moba = moba_pallas(qvt, k_alibi)
    o_fox = fox_pallas(qvt, k_fox)
    cmp_kv, cmp_kvt = nsa_compress_pallas(nat, cmp_pe, cmp_w1, cmp_w2)
    o_c, selt = nsa_cmp_pallas(nat, cmp_kv, cmp_kvt)
    o_s = nsa_sel_pallas(qvt, k_alibi, selt)
    o_nsa = nsa_win_pallas(qvt, k_alibi, o_c, o_s, gates)
    o_mem = mem_attn_pallas(nat, mem_k, mem_vt)
    return [o_moba, o_nsa, o_fox, o_mem]


def kernel(x, mem, emb_ln_g, emb_ln_b, w_in, b_forget, w_mem_kv, nsa_cmp_pe, nsa_cmp_w1, nsa_cmp_w2,
           w_out, ln1_g, ln1_b, ffn_w_up, ffn_conv_w, ffn_conv_b, ffn_w_down, ln2_g, ln2_b):
    b, s, d = x.shape
    depth = w_in.shape[0]
    dff = ffn_w_down.shape[1]
    mem16 = mem.astype(BF16)
    h32, h16 = layer_norm_pallas(x.reshape(b * s, d), emb_ln_g, emb_ln_b)
    for l in range(depth):
        heads = _mixer(h32.reshape(b, s, d), h16.reshape(b, s, d), mem16, w_in[l], b_forget[l], w_mem_kv[l],
                       nsa_cmp_pe[l], nsa_cmp_w1[l], nsa_cmp_w2[l])
        h32, h16 = out_proj_ln_pallas(heads, w_out[l].astype(BF16), h32.reshape(b, s, d),
                                      ln1_g[l], ln1_b[l], tm=512)
        a = ffn_up_pallas(h16, ffn_w_up[l].astype(BF16), ffn_conv_w[l], ffn_conv_b[l],
                          tm=min(1024, s), tn=512)
        h32, h16 = matmul_ln_pallas(a.reshape(b * s, dff), ffn_w_down[l].astype(BF16), h32.reshape(b * s, d),
                                    ln2_g[l], ln2_b[l], tm=512, tk=dff // 4)
    return h32.reshape(b, s, d)
```

```python
import functools
import math

import jax
import jax.numpy as jnp
import ml_dtypes
import numpy as np
from jax import lax
from jax.experimental import pallas as pl
from jax.experimental.pallas import tpu as pltpu

F32 = jnp.float32
BF16 = jnp.bfloat16

HEAD_DIM = 64
MOBA_HEADS = 8
NSA_HEADS = 8
NSA_KV_HEADS = 2
NSA_GROUP = NSA_HEADS // NSA_KV_HEADS
FOX_HEADS = 12
MEM_HEADS = 4
MOBA_BLOCK = 256
MOBA_TOPK = 3
NSA_CMP_LEN = 32
NSA_CMP_STRIDE = 16
NSA_SEL_BLOCK = 64
NSA_TOPK = 16
NSA_WINDOW = 512
NSA_FORCE_SCORE = 1.0e4
CONV_WIDTH = 3
LN_EPS = 1e-5
DEPTH = 2
DEEPNORM_ALPHA = (2 * DEPTH) ** 0.25

LOG2E = math.log2(math.e)
Q_SCALE = HEAD_DIM ** -0.5 * LOG2E
NEG = -1.0e30
SKIP = -3.0e38
VMEM_LIMIT = 56 * 1024 * 1024
LANES = 128
KAUG = 2 * HEAD_DIM
VROWS = HEAD_DIM + 16
POS_PERIOD = 256

TQ_MOBA, TQ_NSA, TQ_FOX = 0, 8, 16
TV_MOBA, TV_NSA, TV_FOX = 28, 36, 40
T_SLOTS = 52
KA_MOBA, KA_NSA = 0, 8
KA_SLOTS = 12
N_CMP, N_MEMQ, N_NSAQ = 0, 4, 8
N_SLOTS = 16


def _cparams(sem):
    return pltpu.CompilerParams(dimension_semantics=sem, vmem_limit_bytes=VMEM_LIMIT)


def _split2(x):
    hi = x.astype(BF16)
    return hi, (x - hi.astype(F32)).astype(BF16)


def _split3(x):
    hi = x.astype(BF16)
    r1 = x - hi.astype(F32)
    mid = r1.astype(BF16)
    lo = (r1 - mid.astype(F32)).astype(BF16)
    return hi, mid, lo


def _np_split3(x):
    x = np.asarray(x, np.float32)
    hi = x.astype(ml_dtypes.bfloat16).astype(np.float32)
    r1 = x - hi
    mid = r1.astype(ml_dtypes.bfloat16).astype(np.float32)
    lo = (r1 - mid).astype(ml_dtypes.bfloat16).astype(np.float32)
    return hi, mid, lo


def _dot_nt(a, b):
    return lax.dot_general(a, b, (((1,), (1,)), ((), ())), preferred_element_type=F32)


def _dot_tn(a, b):
    return lax.dot_general(a, b, (((0,), (0,)), ((), ())), preferred_element_type=F32)


def _dot(a, b):
    return jnp.dot(a, b, preferred_element_type=F32)


def _layer_norm_rows(x, g, b):
    mu = jnp.mean(x, axis=-1, keepdims=True)
    xc = x - mu
    var = jnp.mean(xc * xc, axis=-1, keepdims=True)
    return xc * lax.rsqrt(var + LN_EPS) * g + b


def _alibi_slopes2(n):
    return (np.exp2(-8.0 * np.arange(1, n + 1, dtype=np.float64) / n) * LOG2E).astype(np.float32)


def _slope_rows(slopes2, lanes):
    pieces = np.stack(_np_split3(slopes2), axis=1)
    rows = np.zeros((len(slopes2), HEAD_DIM, lanes), np.float32)
    rows[:, :3, :] = pieces[:, :, None]
    return jnp.asarray(rows, BF16)


def _ln_kernel(x_ref, g_ref, b_ref, o32_ref, o16_ref):
    y = _layer_norm_rows(x_ref[...], g_ref[...], b_ref[...])
    o32_ref[...] = y
    o16_ref[...] = y.astype(BF16)


def layer_norm_pallas(x, g, b, tm=512):
    m, d = x.shape
    return pl.pallas_call(
        _ln_kernel,
        grid=(m // tm,),
        in_specs=[pl.BlockSpec((tm, d), lambda i: (i, 0)),
                  pl.BlockSpec((1, d), lambda i: (0, 0)),
                  pl.BlockSpec((1, d), lambda i: (0, 0))],
        out_specs=[pl.BlockSpec((tm, d), lambda i: (i, 0)),
                   pl.BlockSpec((tm, d), lambda i: (i, 0))],
        out_shape=[jax.ShapeDtypeStruct((m, d), F32), jax.ShapeDtypeStruct((m, d), BF16)],
        compiler_params=_cparams(("parallel",)),
        name="layer_norm",
    )(x, g.reshape(1, d), b.reshape(1, d))


def _gates_kernel(h_ref, wg_ref, wf_ref, bf_ref, tri_ref, place_ref, g_ref, caug_ref, carry_ref):
    si = pl.program_id(1)

    @pl.when(si == 0)
    def _():
        carry_ref[...] = jnp.zeros_like(carry_ref)

    h_hi, h_lo = _split2(h_ref[0])
    wg_hi, wg_lo = _split2(wg_ref[...])
    g = _dot(h_hi, wg_hi) + _dot(h_lo, wg_hi) + _dot(h_hi, wg_lo)
    g_ref[0] = 1.0 / (1.0 + jnp.exp(-g))
    wf_hi, wf_lo = _split2(wf_ref[...])
    x = _dot(h_hi, wf_hi) + _dot(h_lo, wf_hi) + _dot(h_hi, wf_lo) + bf_ref[...]
    logf = jnp.minimum(x, 0.0) - jnp.log(1.0 + jnp.exp(-jnp.abs(x)))
    tri = tri_ref[...]
    l_hi, l_mid, l_lo = _split3(logf)
    c = _dot(tri, l_hi) + _dot(tri, l_mid) + _dot(tri, l_lo) + carry_ref[0:1, :]
    carry_ref[...] = jnp.broadcast_to(c[-1:, :], carry_ref.shape)
    n_hi, n_mid, n_lo = _split3(-LOG2E * c)
    caug = _dot(n_hi, place_ref[0]) + _dot(n_mid, place_ref[1]) + _dot(n_lo, place_ref[2])
    caug_ref[0] = caug.astype(BF16)


def gates_pallas(h3, w_gate, w_forget, b_forget, t=512):
    b, s, d = h3.shape
    gw = 3 * NSA_GROUP
    ng = NSA_KV_HEADS * LANES
    wg = jnp.zeros((d, ng), F32)
    for g in range(NSA_KV_HEADS):
        wg = wg.at[:, g * LANES:g * LANES + gw].set(w_gate[:, g * gw:(g + 1) * gw])
    wf = jnp.zeros((d, LANES), F32).at[:, :FOX_HEADS].set(w_forget)
    bf = jnp.zeros((1, LANES), F32).at[0, :FOX_HEADS].set(b_forget)
    tri = (np.arange(t)[None, :] <= np.arange(t)[:, None]).astype(np.float32)
    place = np.zeros((3, LANES, FOX_HEADS * KAUG), np.float32)
    for piece in range(3):
        for hh in range(FOX_HEADS):
            place[piece, hh, hh * KAUG + HEAD_DIM + piece] = 1.0
    nc = FOX_HEADS * KAUG
    return pl.pallas_call(
        _gates_kernel,
        grid=(b, s // t),
        in_specs=[pl.BlockSpec((1, t, d), lambda i, j: (i, j, 0)),
                  pl.BlockSpec((d, ng), lambda i, j: (0, 0)),
                  pl.BlockSpec((d, LANES), lambda i, j: (0, 0)),
                  pl.BlockSpec((1, LANES), lambda i, j: (0, 0)),
                  pl.BlockSpec((t, t), lambda i, j: (0, 0)),
                  pl.BlockSpec((3, LANES, nc), lambda i, j: (0, 0, 0))],
        out_specs=[pl.BlockSpec((1, t, ng), lambda i, j: (i, j, 0)),
                   pl.BlockSpec((1, t, nc), lambda i, j: (i, j, 0))],
        out_shape=[jax.ShapeDtypeStruct((b, s, ng), F32),
                   jax.ShapeDtypeStruct((b, s, nc), BF16)],
        scratch_shapes=[pltpu.VMEM((8, LANES), F32)],
        compiler_params=_cparams(("parallel", "arbitrary")),
        name="gates_cumsum",
    )(h3, wg, wf, bf, jnp.asarray(tri, BF16), jnp.asarray(place, BF16))


def _proj_heads_kernel(x_ref, w_ref, sc_ref, o_ref, *, heads_per_step):
    acc = _dot(x_ref[0], w_ref[...]) * sc_ref[...]
    for j in range(heads_per_step):
        o_ref[0, j] = acc[:, j * HEAD_DIM:(j + 1) * HEAD_DIM].astype(o_ref.dtype)


def proj_heads_pallas(x3, w, colscale, tm, heads_per_step):
    b, s, d = x3.shape
    n = w.shape[1]
    tn = heads_per_step * HEAD_DIM
    return pl.pallas_call(
        functools.partial(_proj_heads_kernel, heads_per_step=heads_per_step),
        grid=(b, s // tm, n // tn),
        in_specs=[pl.BlockSpec((1, tm, d), lambda i, j, k: (i, j, 0)),
                  pl.BlockSpec((d, tn), lambda i, j, k: (0, k)),
                  pl.BlockSpec((1, tn), lambda i, j, k: (0, k))],
        out_specs=pl.BlockSpec((1, heads_per_step, tm, HEAD_DIM), lambda i, j, k: (i, k, j, 0)),
        out_shape=jax.ShapeDtypeStruct((b, n // HEAD_DIM, s, HEAD_DIM), BF16),
        compiler_params=_cparams(("parallel", "parallel", "arbitrary")),
        name="proj_heads",
    )(x3, w, colscale.reshape(1, n))


def _proj_t_kernel(x_ref, wt_ref, sc_ref, o_ref, *, heads_per_step):
    acc = _dot_nt(wt_ref[...], x_ref[0]) * sc_ref[...]
    o_ref[0] = acc.reshape(heads_per_step, HEAD_DIM, acc.shape[1]).astype(o_ref.dtype)


def proj_t_pallas(x3, wt, rowscale, tm, heads_per_step):
    b, s, d = x3.shape
    n = wt.shape[0]
    tn = heads_per_step * HEAD_DIM
    return pl.pallas_call(
        functools.partial(_proj_t_kernel, heads_per_step=heads_per_step),
        grid=(b, s // tm, n // tn),
        in_specs=[pl.BlockSpec((1, tm, d), lambda i, j, k: (i, j, 0)),
                  pl.BlockSpec((tn, d), lambda i, j, k: (k, 0)),
                  pl.BlockSpec((tn, 1), lambda i, j, k: (k, 0))],
        out_specs=pl.BlockSpec((1, heads_per_step, HEAD_DIM, tm), lambda i, j, k: (i, k, 0, j)),
        out_shape=jax.ShapeDtypeStruct((b, n // HEAD_DIM, HEAD_DIM, s), BF16),
        compiler_params=_cparams(("parallel", "parallel", "arbitrary")),
        name="proj_transposed",
    )(x3, wt, rowscale.reshape(n, 1))


def _proj_kaug_kernel(x_ref, w_ref, *rest, heads_per_step, positional):
    o_ref = rest[-1]
    acc = _dot(x_ref[0], w_ref[...])
    tm = acc.shape[0]
    if positional:
        pos = (pl.program_id(1) * tm + lax.broadcasted_iota(jnp.int32, acc.shape, 0)) % POS_PERIOD
        lane = lax.broadcasted_iota(jnp.int32, acc.shape, 1) % KAUG
        acc = acc + jnp.where((lane >= HEAD_DIM) & (lane < HEAD_DIM + 3), pos.astype(F32), 0.0)
    else:
        acc = acc + rest[0][0].astype(F32)
    for j in range(heads_per_step):
        o_ref[0, j] = acc[:, j * KAUG:(j + 1) * KAUG].astype(o_ref.dtype)


def proj_kaug_pallas(x3, w, aug, tm, heads_per_step):
    b, s, d = x3.shape
    n = w.shape[1]
    tn = heads_per_step * KAUG
    in_specs = [pl.BlockSpec((1, tm, d), lambda i, j, k: (i, j, 0)),
                pl.BlockSpec((d, tn), lambda i, j, k: (0, k))]
    args = [x3, w]
    if aug is not None:
        in_specs.append(pl.BlockSpec((1, tm, tn), lambda i, j, k: (i, j, k)))
        args.append(aug)
    return pl.pallas_call(
        functools.partial(_proj_kaug_kernel, heads_per_step=heads_per_step, positional=aug is None),
        grid=(b, s // tm, n // tn),
        in_specs=in_specs,
        out_specs=pl.BlockSpec((1, heads_per_step, tm, KAUG), lambda i, j, k: (i, k, j, 0)),
        out_shape=jax.ShapeDtypeStruct((b, n // KAUG, s, KAUG), BF16),
        compiler_params=_cparams(("parallel", "parallel", "arbitrary")),
        name="proj_keys_aug",
    )(*args)


def _matmul_ln_kernel(x_ref, w_ref, r_ref, g_ref, b_ref, o32_ref, o16_ref, acc_ref):
    k = pl.program_id(1)

    @pl.when(k == 0)
    def _():
        acc_ref[...] = jnp.zeros_like(acc_ref)

    acc_ref[...] += _dot(x_ref[...], w_ref[...])

    @pl.when(k == pl.num_programs(1) - 1)
    def _():
        y = _layer_norm_rows(DEEPNORM_ALPHA * r_ref[...] + acc_ref[...], g_ref[...], b_ref[...])
        o32_ref[...] = y
        o16_ref[...] = y.astype(BF16)


def matmul_ln_pallas(x, w, res, g, b, tm, tk):
    m, kk = x.shape
    d = w.shape[1]
    return pl.pallas_call(
        _matmul_ln_kernel,
        grid=(m // tm, kk // tk),
        in_specs=[pl.BlockSpec((tm, tk), lambda i, k: (i, k)),
                  pl.BlockSpec((tk, d), lambda i, k: (k, 0)),
                  pl.BlockSpec((tm, d), lambda i, k: (i, 0)),
                  pl.BlockSpec((1, d), lambda i, k: (0, 0)),
                  pl.BlockSpec((1, d), lambda i, k: (0, 0))],
        out_specs=[pl.BlockSpec((tm, d), lambda i, k: (i, 0)),
                   pl.BlockSpec((tm, d), lambda i, k: (i, 0))],
        out_shape=[jax.ShapeDtypeStruct((m, d), F32), jax.ShapeDtypeStruct((m, d), BF16)],
        scratch_shapes=[pltpu.VMEM((tm, d), F32)],
        compiler_params=_cparams(("parallel", "arbitrary")),
        name="matmul_ln",
    )(x, w, res, g.reshape(1, d), b.reshape(1, d))


def _out_proj_ln_kernel(*refs, widths):
    n = len(widths)
    x_refs, (w_ref, r_ref, g_ref, b_ref, o32_ref, o16_ref) = refs[:n], refs[n:]
    acc = None
    off = 0
    for x_ref, width in zip(x_refs, widths):
        part = _dot_tn(x_ref[0], w_ref[off:off + width, :])
        acc = part if acc is None else acc + part
        off += width
    y = _layer_norm_rows(DEEPNORM_ALPHA * r_ref[0] + acc, g_ref[...], b_ref[...])
    o32_ref[0] = y
    o16_ref[0] = y.astype(BF16)


def out_proj_ln_pallas(xts, w, res3, g, b, tm):
    bsz, s, d = res3.shape
    widths = tuple(x.shape[1] for x in xts)
    in_specs = [pl.BlockSpec((1, wd, tm), lambda i, j: (i, 0, j)) for wd in widths]
    in_specs += [pl.BlockSpec((w.shape[0], d), lambda i, j: (0, 0)),
                 pl.BlockSpec((1, tm, d), lambda i, j: (i, j, 0)),
                 pl.BlockSpec((1, d), lambda i, j: (0, 0)),
                 pl.BlockSpec((1, d), lambda i, j: (0, 0))]
    return pl.pallas_call(
        functools.partial(_out_proj_ln_kernel, widths=widths),
        grid=(bsz, s // tm),
        in_specs=in_specs,
        out_specs=[pl.BlockSpec((1, tm, d), lambda i, j: (i, j, 0)),
                   pl.BlockSpec((1, tm, d), lambda i, j: (i, j, 0))],
        out_shape=[jax.ShapeDtypeStruct((bsz, s, d), F32), jax.ShapeDtypeStruct((bsz, s, d), BF16)],
        compiler_params=_cparams(("parallel", "parallel")),
        name="out_proj_ln",
    )(*xts, w, res3, g.reshape(1, d), b.reshape(1, d))


HALO = 16


def _gelu_tanh(x):
    return 0.5 * x * (1.0 + jnp.tanh(math.sqrt(2.0 / math.pi) * (x + 0.044715 * x * x * x)))


def _ffn_up_kernel(x_ref, xh_ref, wu_ref, wg_ref, cw_ref, cb_ref, o_ref):
    j = pl.program_id(1)
    x = x_ref[0]
    u = _dot(x, wu_ref[...])
    g = _dot(x, wg_ref[...])
    gh = _dot(xh_ref[0], wg_ref[...]) * jnp.where(j > 0, 1.0, 0.0)
    row = lax.broadcasted_iota(jnp.int32, g.shape, 0)
    prev1 = gh[HALO - 1:HALO, :]
    prev2 = gh[HALO - 2:HALO - 1, :]
    g_m1 = jnp.where(row == 0, prev1, pltpu.roll(g, 1, axis=0))
    g_m2 = jnp.where(row == 0, prev2, jnp.where(row == 1, prev1, pltpu.roll(g, 2, axis=0)))
    cw = cw_ref[...]
    gc = cb_ref[...] + cw[0:1] * g_m2 + cw[1:2] * g_m1 + cw[2:3] * g
    o_ref[0] = (_gelu_tanh(gc) * u).astype(o_ref.dtype)


def ffn_up_pallas(x3, w_up, conv_w, conv_b, tm, tn):
    b, s, d = x3.shape
    dff = w_up.shape[1] // 2
    nt = dff // tn
    hb = tm // HALO
    cw = jnp.zeros((8, dff), F32).at[:CONV_WIDTH].set(conv_w)
    return pl.pallas_call(
        _ffn_up_kernel,
        grid=(b, s // tm, nt),
        in_specs=[pl.BlockSpec((1, tm, d), lambda i, j, k: (i, j, 0)),
                  pl.BlockSpec((1, HALO, d), lambda i, j, k: (i, jnp.maximum(j * hb - 1, 0), 0)),
                  pl.BlockSpec((d, tn), lambda i, j, k: (0, k)),
                  pl.BlockSpec((d, tn), lambda i, j, k: (0, k + nt)),
                  pl.BlockSpec((8, tn), lambda i, j, k: (0, k)),
                  pl.BlockSpec((1, tn), lambda i, j, k: (0, k))],
        out_specs=pl.BlockSpec((1, tm, tn), lambda i, j, k: (i, j, k)),
        out_shape=jax.ShapeDtypeStruct((b, s, dff), BF16),
        compiler_params=_cparams(("parallel", "parallel", "arbitrary")),
        name="ffn_up",
    )(x3, x3, w_up, w_up, cw, conv_b.reshape(1, dff))


def _online_step_t(st, delta, vaug, m_ref, acc_ref):
    m_prev = m_ref[...]
    m_cur = jnp.max(st, axis=0, keepdims=True) + delta
    m_new = jnp.maximum(m_prev, m_cur)
    alpha = jnp.exp2(m_prev - m_new)
    pt = jnp.exp2(st - (m_new - delta)).astype(BF16)
    acc_ref[...] = alpha * acc_ref[...] + _dot(vaug, pt)
    m_ref[...] = m_new


def _init_state_t(m_ref, acc_ref):
    m_ref[...] = jnp.full(m_ref.shape, NEG, F32)
    acc_ref[...] = jnp.zeros(acc_ref.shape, F32)


def _finish_t(acc_ref):
    acc = acc_ref[...]
    return acc[:HEAD_DIM] / acc[HEAD_DIM:HEAD_DIM + 1]


def _fill_vaug(vaug_ref, vt_ref):
    s = vaug_ref.shape[1]
    vaug_ref[0:HEAD_DIM, :] = vt_ref[0, 0]
    pad = lax.broadcasted_iota(jnp.int32, (VROWS - HEAD_DIM, s), 0)
    vaug_ref[HEAD_DIM:VROWS, :] = jnp.where(pad == 0, 1.0, 0.0).astype(BF16)


def _ktile(ref, idx, size):
    return ref[0, 0, pl.ds(pl.multiple_of(idx * size, size), size), :]


def _vtile(ref, idx, size):
    return ref[:, pl.ds(pl.multiple_of(idx * size, size), size)]


def _stage_scores(st, s_ref, mc_ref):
    s_ref[...] = st
    mc_ref[...] = jnp.max(st, axis=0, keepdims=True)


def _stage_update(s_ref, mc_ref, delta, vaug, m_ref, acc_ref):
    m_prev = m_ref[...]
    m_new = jnp.maximum(m_prev, mc_ref[...] + delta)
    alpha = jnp.exp2(m_prev - m_new)
    pt = jnp.exp2(s_ref[...] - (m_new - delta)).astype(BF16)
    acc_ref[...] = alpha * acc_ref[...] + _dot(vaug, pt)
    m_ref[...] = m_new


def _score_buffers(tk, lanes):
    return [pltpu.VMEM((tk, lanes), F32), pltpu.VMEM((1, lanes), F32),
            pltpu.VMEM((tk, lanes), F32), pltpu.VMEM((1, lanes), F32)]


def _flash_pipeline(lead, n_loop, scores, meta, bufs, m_ref, acc_ref, n_even):
    _stage_scores(lead[0][0](), *bufs[0])
    for i in range(1, len(lead)):
        _stage_scores(lead[i][0](), *bufs[i % 2])
        _stage_update(*bufs[(i - 1) % 2], *lead[i - 1][1](), m_ref, acc_ref)
    cur = (len(lead) - 1) % 2
    nxt = 1 - cur
    _stage_scores(scores(0), *bufs[nxt])
    _stage_update(*bufs[cur], *lead[-1][1](), m_ref, acc_ref)
    last = jnp.maximum(n_loop - 1, 0)

    def body(kp, carry):
        k0 = 2 * kp
        k1 = k0 + 1
        _stage_scores(scores(jnp.minimum(k1, last)), *bufs[cur])
        _stage_update(*bufs[nxt], *meta(k0), m_ref, acc_ref)
        _stage_scores(scores(jnp.minimum(k0 + 2, last)), *bufs[nxt])
        delta1, vaug1 = meta(jnp.minimum(k1, last))
        if not n_even:
            delta1 = jnp.where(k1 < n_loop, delta1, SKIP)
        _stage_update(*bufs[cur], delta1, vaug1, m_ref, acc_ref)
        return carry

    lax.fori_loop(0, (n_loop + 1) // 2, body, 0)


FOX_TQ = 1024
FOX_TK = 256


def _fox_kernel(qt_ref, k_ref, vt_ref, o_ref, m_ref, acc_ref, vaug_ref, sa_ref, ma_ref, sb_ref, mb_ref):
    tq, tk = o_ref.shape[3], FOX_TK
    nd = tq // tk
    qi = pl.program_id(2)

    @pl.when(qi == 0)
    def _():
        _fill_vaug(vaug_ref, vt_ref)

    ones3 = jnp.where(lax.broadcasted_iota(jnp.int32, (HEAD_DIM, tq), 0) < 3, 1.0, 0.0).astype(BF16)
    qaug = jnp.concatenate([qt_ref[0, 0], ones3], axis=0)
    _init_state_t(m_ref, acc_ref)
    key = lax.broadcasted_iota(jnp.int32, (tk, tq), 0)
    qry = lax.broadcasted_iota(jnp.int32, (tk, tq), 1)

    def scores(kj):
        return _dot(_ktile(k_ref, kj, tk), qaug)

    def meta(kj):
        return 0.0, _vtile(vaug_ref, kj, tk)

    lead = [(functools.partial(lambda d: jnp.where(key + d * tk <= qry, scores(qi * nd + d), NEG), d),
             functools.partial(lambda d: meta(qi * nd + d), d)) for d in range(nd)]
    _flash_pipeline(lead, qi * nd, scores, meta, ((sa_ref, ma_ref), (sb_ref, mb_ref)), m_ref, acc_ref,
                    n_even=nd % 2 == 0)
    o_ref[0, 0] = _finish_t(acc_ref).astype(o_ref.dtype)


def fox_pallas(qvt, kaug):
    b, _, dh, s = qvt.shape
    tq = min(FOX_TQ, s)
    assert tq % FOX_TK == 0
    out = pl.pallas_call(
        _fox_kernel,
        grid=(b, FOX_HEADS, s // tq),
        in_specs=[pl.BlockSpec((1, 1, dh, tq), lambda i, h, j: (i, TQ_FOX + h, 0, j)),
                  pl.BlockSpec((1, 1, s, KAUG), lambda i, h, j: (i, h, 0, 0)),
                  pl.BlockSpec((1, 1, dh, s), lambda i, h, j: (i, TV_FOX + h, 0, 0))],
        out_specs=pl.BlockSpec((1, 1, dh, tq), lambda i, h, j: (i, h, 0, j)),
        out_shape=jax.ShapeDtypeStruct((b, FOX_HEADS, dh, s), BF16),
        scratch_shapes=[pltpu.VMEM((1, tq), F32), pltpu.VMEM((VROWS, tq), F32),
                        pltpu.VMEM((VROWS, s), BF16)] + _score_buffers(FOX_TK, tq),
        compiler_params=_cparams(("parallel", "parallel", "arbitrary")),
        name="fox_attention",
    )(qvt, kaug, qvt)
    return out.reshape(b, FOX_HEADS * dh, s)


def _moba_kernel(slope_ref, qt_ref, k_ref, vt_ref, ind_ref, srow_ref, o_ref,
                 m_ref, acc_ref, vaug_ref, km_ref, sel_ref, sa_ref, ma_ref, sb_ref, mb_ref):
    t = MOBA_BLOCK
    tq = o_ref.shape[3]
    nd = tq // t
    h = pl.program_id(1)
    qi = pl.program_id(2)
    slope2 = slope_ref[h]

    @pl.when(qi == 0)
    def _():
        _fill_vaug(vaug_ref, vt_ref)
        km_ref[...] = _dot(ind_ref[...], k_ref[0, 0]) * (1.0 / MOBA_BLOCK)

    qt = qt_ref[0, 0]
    km_hi, km_lo = _split2(km_ref[...])
    q0 = jnp.concatenate([qt, jnp.zeros_like(qt)], axis=0)
    gate = _dot(km_hi, q0) + _dot(km_lo, q0)
    blk = lax.broadcasted_iota(jnp.int32, gate.shape, 0)
    lane_blk = lax.broadcasted_iota(jnp.int32, (1, tq), 1) // t
    valid = blk < qi * nd + lane_blk
    work = jnp.where(valid, gate, -jnp.inf)
    sel = jnp.zeros(gate.shape, F32)
    big = jnp.int32(2 ** 30)
    for _ in range(MOBA_TOPK):
        mx = jnp.max(work, axis=0, keepdims=True)
        first = jnp.min(jnp.where(work == mx, blk, big), axis=0, keepdims=True)
        hit = blk == first
        sel = jnp.where(hit, 1.0, sel)
        work = jnp.where(hit, -jnp.inf, work)
    sel_ref[...] = jnp.where(valid, sel, 0.0)

    qaug = jnp.concatenate([qt, srow_ref[...]], axis=0)
    _init_state_t(m_ref, acc_ref)
    key = lax.broadcasted_iota(jnp.int32, (t, tq), 0)
    qry = lax.broadcasted_iota(jnp.int32, (t, tq), 1)

    def tile_constant(kj, d):
        sees = sel_ref[pl.ds(kj, 1), :] > 0.0
        if d is not None:
            sees = sees | (lane_blk == d)
        return jnp.where(sees, slope2 * ((kj - qi * nd) * t).astype(F32), SKIP)

    def scores(kj):
        return _dot(_ktile(k_ref, kj, t), qaug)

    def meta(kj):
        return tile_constant(kj, None), _vtile(vaug_ref, kj, t)

    lead = [(functools.partial(lambda d: jnp.where(key + d * t <= qry, scores(qi * nd + d), NEG), d),
             functools.partial(lambda d: (tile_constant(qi * nd + d, d), _vtile(vaug_ref, qi * nd + d, t)), d))
            for d in range(nd)]
    _flash_pipeline(lead, qi * nd, scores, meta, ((sa_ref, ma_ref), (sb_ref, mb_ref)), m_ref, acc_ref,
                    n_even=nd % 2 == 0)
    o_ref[0, 0] = _finish_t(acc_ref).astype(o_ref.dtype)


MOBA_TQ = 1024


def moba_pallas(qvt, kaug):
    b, _, dh, s = qvt.shape
    t = MOBA_BLOCK
    tq = min(MOBA_TQ, s)
    assert POS_PERIOD == t and tq % t == 0
    nb = s // t
    nbp = max(LANES, nb)
    ind = np.zeros((nbp, s), np.float32)
    ind[np.arange(s) // t, np.arange(s)] = 1.0
    slopes2 = _alibi_slopes2(MOBA_HEADS)
    grid_spec = pltpu.PrefetchScalarGridSpec(
        num_scalar_prefetch=1,
        grid=(b, MOBA_HEADS, s // tq),
        in_specs=[pl.BlockSpec((1, 1, dh, tq), lambda i, h, j, sl: (i, TQ_MOBA + h, 0, j)),
                  pl.BlockSpec((1, 1, s, KAUG), lambda i, h, j, sl: (i, KA_MOBA + h, 0, 0)),
                  pl.BlockSpec((1, 1, dh, s), lambda i, h, j, sl: (i, TV_MOBA + h, 0, 0)),
                  pl.BlockSpec((nbp, s), lambda i, h, j, sl: (0, 0)),
                  pl.BlockSpec((None, dh, tq), lambda i, h, j, sl: (h, 0, 0))],
        out_specs=pl.BlockSpec((1, 1, dh, tq), lambda i, h, j, sl: (i, h, 0, j)),
        scratch_shapes=[pltpu.VMEM((1, tq), F32), pltpu.VMEM((VROWS, tq), F32),
                        pltpu.VMEM((VROWS, s), BF16), pltpu.VMEM((nbp, KAUG), F32),
                        pltpu.VMEM((nbp, tq), F32)] + _score_buffers(t, tq))
    out = pl.pallas_call(
        _moba_kernel,
        grid_spec=grid_spec,
        out_shape=jax.ShapeDtypeStruct((b, MOBA_HEADS, dh, s), BF16),
        compiler_params=_cparams(("parallel", "parallel", "arbitrary")),
        name="moba_attention",
    )(jnp.asarray(slopes2), qvt, kaug, qvt, jnp.asarray(ind, BF16), _slope_rows(slopes2, tq))
    return out.reshape(b, MOBA_HEADS * dh, s)


def _nsa_compress_kernel(x_ref, w1a_ref, w1b_ref, pe_ref, w1_ref, w2_ref, w2t_ref, o_ref, ot_ref):
    nr = x_ref.shape[2]
    x = x_ref[0, 0]

    def near_f32(xb, w):
        w_hi, w_lo = _split2(w)
        return _dot(xb, w_hi) + _dot(xb, w_lo)

    a = near_f32(x, w1a_ref[0])
    bm = near_f32(x, w1b_ref[0])
    pe_hi, pe_mid, pe_lo = _split3(pe_ref[0])
    w1_hi, w1_lo = _split2(w1_ref[0])
    pe_term = (_dot(pe_hi, w1_hi) + _dot(pe_mid, w1_hi) + _dot(pe_lo, w1_hi)
               + _dot(pe_hi, w1_lo) + _dot(pe_mid, w1_lo))[0:1]
    pre = a + pltpu.roll(bm, nr - 1, axis=0) + pe_term
    hid = _gelu_tanh(pre)
    h_hi, h_mid, h_lo = _split3(hid)
    w2_hi, w2_lo = _split2(w2_ref[0])
    o_ref[0, 0] = (_dot(h_hi, w2_hi) + _dot(h_mid, w2_hi) + _dot(h_lo, w2_hi)
                   + _dot(h_hi, w2_lo) + _dot(h_mid, w2_lo))
    t_hi, t_lo = _split2(w2t_ref[0])
    ot_ref[0, 0] = (_dot_nt(t_hi, h_hi) + _dot_nt(t_hi, h_mid) + _dot_nt(t_hi, h_lo)
                    + _dot_nt(t_lo, h_hi) + _dot_nt(t_lo, h_mid))


def nsa_compress_pallas(nat, pe, w1, w2):
    b, _, s, dh = nat.shape
    nr = s // NSA_CMP_STRIDE
    half = NSA_CMP_STRIDE * dh
    hid = w1.shape[-1]
    x = nat[:, N_CMP:N_CMP + 4].reshape(b, 4, nr, half)
    w1f = w1.reshape(2, NSA_CMP_LEN * dh, hid)
    pef = jnp.zeros((2, 8, NSA_CMP_LEN * dh), F32).at[:, 0].set(pe.reshape(2, NSA_CMP_LEN * dh))
    return pl.pallas_call(
        _nsa_compress_kernel,
        grid=(b, 4),
        in_specs=[pl.BlockSpec((1, 1, nr, half), lambda i, j: (i, j, 0, 0)),
                  pl.BlockSpec((1, half, hid), lambda i, j: (j // 2, 0, 0)),
                  pl.BlockSpec((1, half, hid), lambda i, j: (j // 2, 1, 0)),
                  pl.BlockSpec((1, 8, 2 * half), lambda i, j: (j // 2, 0, 0)),
                  pl.BlockSpec((1, 2 * half, hid), lambda i, j: (j // 2, 0, 0)),
                  pl.BlockSpec((1, hid, dh), lambda i, j: (j // 2, 0, 0)),
                  pl.BlockSpec((1, dh, hid), lambda i, j: (j // 2, 0, 0))],
        out_specs=[pl.BlockSpec((1, 1, nr, dh), lambda i, j: (i, j, 0, 0)),
                   pl.BlockSpec((1, 1, dh, nr), lambda i, j: (i, j, 0, 0))],
        out_shape=[jax.ShapeDtypeStruct((b, 4, nr, dh), F32),
                   jax.ShapeDtypeStruct((b, 4, dh, nr), F32)],
        compiler_params=_cparams(("parallel", "parallel")),
        name="nsa_compress",
    )(x, w1f, w1f, pef, w1f, w2, jnp.swapaxes(w2, 1, 2))


NSA_TQ = 128


def _nsa_cmp_kernel(slope_ref, q_ref, kc_ref, vct_ref, mimp_ref, oct_ref, selt_ref):
    tq = NSA_TQ
    g = pl.program_id(1)
    q0 = pl.program_id(2) * tq
    nr = kc_ref.shape[2]
    nsb = selt_ref.shape[2]
    kc_hi, kc_lo = _split2(kc_ref[0, 0])
    vct = vct_ref[0, 0].astype(BF16)
    t_rows = q0 + lax.broadcasted_iota(jnp.int32, (tq, 1), 0)
    cmp_end = NSA_CMP_STRIDE * lax.broadcasted_iota(jnp.int32, (1, nr), 1) + (NSA_CMP_LEN - 1)
    mask = cmp_end <= t_rows
    colrel = (cmp_end - q0).astype(F32)
    psum = jnp.zeros((tq, nr), F32)
    for hh in range(NSA_GROUP):
        q = q_ref[0, hh]
        s = _dot_nt(q, kc_hi) + _dot_nt(q, kc_lo) + slope_ref[g * NSA_GROUP + hh] * colrel
        s = jnp.where(mask, s, -jnp.inf)
        m = jnp.max(s, axis=-1, keepdims=True)
        m = jnp.where(m > -jnp.inf, m, 0.0)
        e = jnp.where(mask, jnp.exp2(s - m), 0.0)
        p = e / jnp.maximum(jnp.sum(e, axis=-1, keepdims=True), 1e-30)
        oct_ref[0, hh] = _dot_nt(vct, p.astype(BF16))
        psum = psum + p
    p_hi, p_mid, p_lo = _split3(psum)
    mimp = mimp_ref[...]
    imp = _dot(p_hi, mimp) + _dot(p_mid, mimp) + _dot(p_lo, mimp)
    blk = lax.broadcasted_iota(jnp.int32, (tq, nsb), 1)
    jt = t_rows // NSA_SEL_BLOCK
    forced = (blk == 0) | (blk == jt) | (blk == jt - 1)
    imp = jnp.where(forced, NSA_FORCE_SCORE, imp)
    valid = blk * NSA_SEL_BLOCK <= t_rows
    work0 = jnp.where(valid, imp, -jnp.inf)
    big = jnp.int32(2 ** 30)

    def pick(_, carry):
        work, sel = carry
        mx = jnp.max(work, axis=-1, keepdims=True)
        first = jnp.min(jnp.where(work == mx, blk, big), axis=-1, keepdims=True)
        hit = blk == first
        return jnp.where(hit, -jnp.inf, work), jnp.where(hit, 1.0, sel)

    _, sel = lax.fori_loop(0, min(NSA_TOPK, nsb), pick, (work0, jnp.zeros((tq, nsb), F32)))
    selt_ref[0, 0] = jnp.where(valid, sel, 0.0).T


def nsa_cmp_pallas(nat, cmp_kv, cmp_kvt):
    b, _, s, dh = nat.shape
    tq = NSA_TQ
    nr = cmp_kv.shape[2]
    nsb = s // NSA_SEL_BLOCK
    ratio = NSA_SEL_BLOCK // NSA_CMP_STRIDE
    front = NSA_CMP_LEN // NSA_CMP_STRIDE - 1
    n_int = ratio + front
    n_idx = np.arange(nr)[:, None]
    j_idx = np.arange(nsb)[None, :]
    mimp = ((n_idx >= ratio * j_idx - front) & (n_idx <= ratio * j_idx + n_int - 1 - front)
            & (n_idx < nr - 1)).astype(np.float32)
    qslot = N_NSAQ // NSA_GROUP
    grid_spec = pltpu.PrefetchScalarGridSpec(
        num_scalar_prefetch=1,
        grid=(b, NSA_KV_HEADS, s // tq),
        in_specs=[pl.BlockSpec((1, NSA_GROUP, tq, dh), lambda i, g, j, sl: (i, qslot + g, j, 0)),
                  pl.BlockSpec((1, 1, nr, dh), lambda i, g, j, sl: (i, g, 0, 0)),
                  pl.BlockSpec((1, 1, dh, nr), lambda i, g, j, sl: (i, 2 + g, 0, 0)),
                  pl.BlockSpec((nr, nsb), lambda i, g, j, sl: (0, 0))],
        out_specs=[pl.BlockSpec((1, NSA_GROUP, dh, tq), lambda i, g, j, sl: (i, g, 0, j)),
                   pl.BlockSpec((1, 1, nsb, tq), lambda i, g, j, sl: (i, g, 0, j))])
    return pl.pallas_call(
        _nsa_cmp_kernel,
        grid_spec=grid_spec,
        out_shape=[jax.ShapeDtypeStruct((b, NSA_HEADS, dh, s), F32),
                   jax.ShapeDtypeStruct((b, NSA_KV_HEADS, nsb, s), F32)],
        compiler_params=_cparams(("parallel", "parallel", "parallel")),
        name="nsa_compressed_select",
    )(jnp.asarray(_alibi_slopes2(NSA_HEADS)), nat, cmp_kv, cmp_kvt, jnp.asarray(mimp, BF16))


NSA_TK = 256
NSA_SEL_TQ = 256
NSA_LANES = NSA_GROUP * NSA_TQ


def _nsa_qaug(qt_ref, srow_ref):
    return jnp.concatenate(
        [jnp.concatenate([qt_ref[0, hh], srow_ref[hh]], axis=0) for hh in range(NSA_GROUP)], axis=1)


def _nsa_sel_kernel(qt_ref, k_ref, vt_ref, selt_ref, srow_ref, slane_ref, o_ref, m_ref, acc_ref, vaug_ref,
                    sa_ref, ma_ref, sb_ref, mb_ref):
    tq, tk = NSA_SEL_TQ, NSA_TK
    per_tile = tk // NSA_SEL_BLOCK
    qi = pl.program_id(2)
    q0 = qi * tq

    @pl.when(qi == 0)
    def _():
        _fill_vaug(vaug_ref, vt_ref)

    qaug = _nsa_qaug(qt_ref, srow_ref)
    slane = slane_ref[0:1, :]
    diag = q0 // tk

    def scores(kj):
        st = _dot(_ktile(k_ref, kj, tk), qaug)
        rows = [jnp.broadcast_to(selt_ref[0, 0, pl.ds(kj * per_tile + c, 1), :], (NSA_SEL_BLOCK, tq))
                for c in range(per_tile)]
        bias = (jnp.concatenate(rows, axis=0) - 1.0) * (-NEG)
        return st + jnp.concatenate([bias] * NSA_GROUP, axis=1)

    def meta(kj):
        return slane * (kj * tk - q0).astype(F32), _vtile(vaug_ref, kj, tk)

    def own_tile():
        key = lax.broadcasted_iota(jnp.int32, (tk, tq), 0)
        qry = lax.broadcasted_iota(jnp.int32, (tk, tq), 1)
        causal = jnp.concatenate([key <= qry] * NSA_GROUP, axis=1)
        return jnp.where(causal, scores(diag), NEG)

    _init_state_t(m_ref, acc_ref)
    _flash_pipeline([(own_tile, lambda: meta(diag))], diag, scores, meta,
                    ((sa_ref, ma_ref), (sb_ref, mb_ref)), m_ref, acc_ref, n_even=False)
    out = _finish_t(acc_ref)
    for hh in range(NSA_GROUP):
        o_ref[0, hh] = out[:, hh * tq:(hh + 1) * tq]


def _nsa_tables(tq):
    slopes2 = _alibi_slopes2(NSA_HEADS)
    srow = _slope_rows(slopes2, tq)
    slane = np.repeat(slopes2.reshape(NSA_KV_HEADS, NSA_GROUP), tq, axis=1)
    slane8 = np.repeat(slane[:, None, :], 8, axis=1)
    return srow, jnp.asarray(slane8, F32)


def nsa_sel_pallas(qvt, kaug, selt):
    b, _, dh, s = qvt.shape
    tq = NSA_SEL_TQ
    assert tq == NSA_TK
    lanes = NSA_GROUP * tq
    nsb = s // NSA_SEL_BLOCK
    srow, slane = _nsa_tables(tq)
    return pl.pallas_call(
        _nsa_sel_kernel,
        grid=(b, NSA_KV_HEADS, s // tq),
        in_specs=[pl.BlockSpec((1, NSA_GROUP, dh, tq), lambda i, g, j: (i, TQ_NSA // NSA_GROUP + g, 0, j)),
                  pl.BlockSpec((1, 1, s, KAUG), lambda i, g, j: (i, KA_NSA + g, 0, 0)),
                  pl.BlockSpec((1, 1, dh, s), lambda i, g, j: (i, TV_NSA + g, 0, 0)),
                  pl.BlockSpec((1, 1, nsb, tq), lambda i, g, j: (i, g, 0, j)),
                  pl.BlockSpec((NSA_GROUP, dh, tq), lambda i, g, j: (g, 0, 0)),
                  pl.BlockSpec((None, 8, lanes), lambda i, g, j: (g, 0, 0))],
        out_specs=pl.BlockSpec((1, NSA_GROUP, dh, tq), lambda i, g, j: (i, g, 0, j)),
        out_shape=jax.ShapeDtypeStruct((b, NSA_HEADS, dh, s), F32),
        scratch_shapes=[pltpu.VMEM((1, lanes), F32), pltpu.VMEM((VROWS, lanes), F32),
                        pltpu.VMEM((VROWS, s), BF16)] + _score_buffers(NSA_TK, lanes),
        compiler_params=_cparams(("parallel", "parallel", "arbitrary")),
        name="nsa_selected",
    )(qvt, kaug, qvt, selt, srow, slane)


NSA_WT = 128


def _nsa_win_kernel(qt_ref, k_ref, vt_ref, srow_ref, slane_ref, oc_ref, os_ref, gate_ref, o_ref,
                    m_ref, acc_ref, vaug_ref):
    tq = NSA_TQ
    wt = NSA_WT
    qi = pl.program_id(2)

    @pl.when(qi == 0)
    def _():
        _fill_vaug(vaug_ref, vt_ref)

    qaug = _nsa_qaug(qt_ref, srow_ref)
    slane = slane_ref[0:1, :]
    key = lax.broadcasted_iota(jnp.int32, (wt, tq), 0)
    qry = lax.broadcasted_iota(jnp.int32, (wt, tq), 1)
    span = NSA_WINDOW // wt

    def step(kj, keep):
        st = _dot(_ktile(k_ref, kj, wt), qaug)
        if keep is not None:
            st = jnp.where(jnp.concatenate([keep] * NSA_GROUP, axis=1), st, NEG)
        base = (kj * wt) // POS_PERIOD * POS_PERIOD - qi * tq
        _online_step_t(st, slane * base.astype(F32), _vtile(vaug_ref, kj, wt), m_ref, acc_ref)

    _init_state_t(m_ref, acc_ref)
    step(qi, key <= qry)

    def body(d, carry):
        @pl.when(qi - d >= 0)
        def _():
            step(qi - d, None)
        return carry

    lax.fori_loop(1, span, body, 0)

    @pl.when(qi - span >= 0)
    def _():
        step(qi - span, key > qry)

    o_w = _finish_t(acc_ref)
    gt = gate_ref[0].T
    for hh in range(NSA_GROUP):
        c0 = 3 * hh
        mix = (gt[c0:c0 + 1] * oc_ref[0, hh] + gt[c0 + 1:c0 + 2] * os_ref[0, hh]
               + gt[c0 + 2:c0 + 3] * o_w[:, hh * tq:(hh + 1) * tq])
        o_ref[0, hh] = mix.astype(o_ref.dtype)


def nsa_win_pallas(qvt, kaug, o_c, o_s, gates):
    b, _, dh, s = qvt.shape
    tq, wt = NSA_TQ, NSA_WT
    assert tq == wt and tq == LANES
    srow, slane = _nsa_tables(tq)
    head_blk = pl.BlockSpec((1, NSA_GROUP, dh, tq), lambda i, g, j: (i, g, 0, j))
    out = pl.pallas_call(
        _nsa_win_kernel,
        grid=(b, NSA_KV_HEADS, s // tq),
        in_specs=[pl.BlockSpec((1, NSA_GROUP, dh, tq), lambda i, g, j: (i, TQ_NSA // NSA_GROUP + g, 0, j)),
                  pl.BlockSpec((1, 1, s, KAUG), lambda i, g, j: (i, KA_NSA + 2 + g, 0, 0)),
                  pl.BlockSpec((1, 1, dh, s), lambda i, g, j: (i, TV_NSA + 2 + g, 0, 0)),
                  pl.BlockSpec((NSA_GROUP, dh, tq), lambda i, g, j: (g, 0, 0)),
                  pl.BlockSpec((None, 8, NSA_LANES), lambda i, g, j: (g, 0, 0)),
                  head_blk, head_blk,
                  pl.BlockSpec((1, tq, LANES), lambda i, g, j: (i, j, g))],
        out_specs=head_blk,
        out_shape=jax.ShapeDtypeStruct((b, NSA_HEADS, dh, s), BF16),
        scratch_shapes=[pltpu.VMEM((1, NSA_LANES), F32), pltpu.VMEM((VROWS, NSA_LANES), F32),
                        pltpu.VMEM((VROWS, s), BF16)],
        compiler_params=_cparams(("parallel", "parallel", "arbitrary")),
        name="nsa_window_mix",
    )(qvt, kaug, qvt, srow, slane, o_c, o_s, gates)
    return out.reshape(b, NSA_HEADS * dh, s)


def _mem_attn_kernel(q_ref, k_ref, vt_ref, o_ref):
    for hh in range(MEM_HEADS):
        s = _dot_nt(q_ref[0, hh], k_ref[0, hh])
        m = jnp.max(s, axis=-1, keepdims=True)
        e = jnp.exp2(s - m)
        p = e / jnp.sum(e, axis=-1, keepdims=True)
        o_ref[0, hh] = _dot_nt(vt_ref[0, hh], p.astype(BF16)).astype(o_ref.dtype)


def mem_attn_pallas(nat, mem_k, mem_vt, tq=512):
    b, _, s, dh = nat.shape
    n_mem = mem_k.shape[2]
    out = pl.pallas_call(
        _mem_attn_kernel,
        grid=(b, s // tq),
        in_specs=[pl.BlockSpec((1, MEM_HEADS, tq, dh), lambda i, j: (i, N_MEMQ // MEM_HEADS, j, 0)),
                  pl.BlockSpec((1, MEM_HEADS, n_mem, dh), lambda i, j: (i, 0, 0, 0)),
                  pl.BlockSpec((1, MEM_HEADS, dh, n_mem), lambda i, j: (i, 0, 0, 0))],
        out_specs=pl.BlockSpec((1, MEM_HEADS, dh, tq), lambda i, j: (i, 0, 0, j)),
        out_shape=jax.ShapeDtypeStruct((b, MEM_HEADS, dh, s), BF16),
        compiler_params=_cparams(("parallel", "parallel")),
        name="memory_attention",
    )(nat, mem_k, mem_vt)
    return out.reshape(b, MEM_HEADS * dh, s)


def _pad_key_cols(w):
    d, n = w.shape
    w3 = w.reshape(d, n // HEAD_DIM, HEAD_DIM)
    return jnp.concatenate([w3, jnp.zeros_like(w3)], axis=2).reshape(d, 2 * n)


def _in_proj_weights(w_in):
    hd = HEAD_DIM
    sizes = (3 * MOBA_HEADS * hd, NSA_HEADS * hd, 6 * NSA_KV_HEADS * hd, 3 * NSA_HEADS,
             3 * FOX_HEADS * hd, FOX_HEADS, MEM_HEADS * hd)
    offs = np.concatenate([[0], np.cumsum(sizes)])
    moba, nsa_q, nsa_kv, nsa_g, fox, fox_f, mem_q = (w_in[:, offs[i]:offs[i + 1]] for i in range(7))
    mh, fh, g2 = MOBA_HEADS * hd, FOX_HEADS * hd, NSA_KV_HEADS * hd
    moba_q, moba_k, moba_v = moba[:, :mh], moba[:, mh:2 * mh], moba[:, 2 * mh:]
    fox_q, fox_k, fox_v = fox[:, :fh], fox[:, fh:2 * fh], fox[:, 2 * fh:]
    k_cmp, v_cmp, k_slc, v_slc, k_win, v_win = (nsa_kv[:, i * g2:(i + 1) * g2] for i in range(6))
    w_t = jnp.concatenate([moba_q, nsa_q, fox_q, moba_v, v_slc, v_win, fox_v], axis=1).T.astype(BF16)
    t_scale = np.ones((T_SLOTS * hd,), np.float32)
    t_scale[:TV_MOBA * hd] = Q_SCALE
    w_ka = _pad_key_cols(jnp.concatenate([moba_k, k_slc, k_win], axis=1)).astype(BF16)
    w_kf = _pad_key_cols(fox_k).astype(BF16)
    w_nat = jnp.concatenate([k_cmp, v_cmp, mem_q, nsa_q], axis=1).astype(BF16)
    n_scale = np.ones((N_SLOTS * hd,), np.float32)
    n_scale[N_MEMQ * hd:] = Q_SCALE
    return w_t, jnp.asarray(t_scale), w_ka, w_kf, w_nat, jnp.asarray(n_scale), nsa_g, fox_f


def _mixer(h32, h16, mem16, w_in, b_forget, w_mem_kv, cmp_pe, cmp_w1, cmp_w2):
    b, s, d = h16.shape
    tm = min(1024, s)
    w_t, t_scale, w_ka, w_kf, w_nat, n_scale, w_gate, w_forget = _in_proj_weights(w_in)
    gates, caug = gates_pallas(h32, w_gate, w_forget, b_forget)
    qvt = proj_t_pallas(h16, w_t, t_scale, tm=tm, heads_per_step=13)
    k_alibi = proj_kaug_pallas(h16, w_ka, None, tm=tm, heads_per_step=6)
    k_fox = proj_kaug_pallas(h16, w_kf, caug, tm=tm, heads_per_step=6)
    nat = proj_heads_pallas(h16, w_nat, n_scale, tm=tm, heads_per_step=8)
    n_mem = mem16.shape[1]
    mk = MEM_HEADS * HEAD_DIM
    mem_k = proj_heads_pallas(mem16, w_mem_kv[:, :mk].astype(BF16), jnp.ones((mk,), F32),
                              tm=n_mem, heads_per_step=MEM_HEADS)
    mem_vt = proj_t_pallas(mem16, w_mem_kv[:, mk:].T.astype(BF16), jnp.ones((mk,), F32),
                           tm=n_mem, heads_per_step=MEM_HEADS)
    o_moba = moba_pallas(qvt, k_alibi)
    o_fox = fox_pallas(qvt, k_fox)
    cmp_kv, cmp_kvt = nsa_compress_pallas(nat, cmp_pe, cmp_w1, cmp_w2)
    o_c, selt = nsa_cmp_pallas(nat, cmp_kv, cmp_kvt)
    o_s = nsa_sel_pallas(qvt, k_alibi, selt)
    o_nsa = nsa_win_pallas(qvt, k_alibi, o_c, o_s, gates)
    o_mem = mem_attn_pallas(nat, mem_k, mem_vt)
    return [o_moba, o_nsa, o_fox, o_mem]


def kernel(x, mem, emb_ln_g, emb_ln_b, w_in, b_forget, w_mem_kv, nsa_cmp_pe, nsa_cmp_w1, nsa_cmp_w2,
           w_out, ln1_g, ln1_b, ffn_w_up, ffn_conv_w, ffn_conv_b, ffn_w_down, ln2_g, ln2_b):
    b, s, d = x.shape
    depth = w_in.shape[0]
    dff = ffn_w_down.shape[1]
    mem16 = mem.astype(BF16)
    h32, h16 = layer_norm_pallas(x.reshape(b * s, d), emb_ln_g, emb_ln_b)
    for l in range(depth):
        heads = _mixer(h32.reshape(b, s, d), h16.reshape(b, s, d), mem16, w_in[l], b_forget[l], w_mem_kv[l],
                       nsa_cmp_pe[l], nsa_cmp_w1[l], nsa_cmp_w2[l])
        h32, h16 = out_proj_ln_pallas(heads, w_out[l].astype(BF16), h32.reshape(b, s, d),
                                      ln1_g[l], ln1_b[l], tm=512)
        a = ffn_up_pallas(h16, ffn_w_up[l].astype(BF16), ffn_conv_w[l], ffn_conv_b[l],
                          tm=min(1024, s), tn=512)
        h32, h16 = matmul_ln_pallas(a.reshape(b * s, dff), ffn_w_down[l].astype(BF16), h32.reshape(b * s, d),
                                    ln2_g[l], ln2_b[l], tm=512, tk=dff // 4)
    return h32.reshape(b, s, d)
```

```python
import functools
import math

import jax
import jax.numpy as jnp
import ml_dtypes
import numpy as np
from jax import lax
from jax.experimental import pallas as pl
from jax.experimental.pallas import tpu as pltpu

F32 = jnp.float32
BF16 = jnp.bfloat16

HEAD_DIM = 64
MOBA_HEADS = 8
NSA_HEADS = 8
NSA_KV_HEADS = 2
NSA_GROUP = NSA_HEADS // NSA_KV_HEADS
FOX_HEADS = 12
MEM_HEADS = 4
MOBA_BLOCK = 256
MOBA_TOPK = 3
NSA_CMP_LEN = 32
NSA_CMP_STRIDE = 16
NSA_SEL_BLOCK = 64
NSA_TOPK = 16
NSA_WINDOW = 512
NSA_FORCE_SCORE = 1.0e4
CONV_WIDTH = 3
LN_EPS = 1e-5
DEPTH = 2
DEEPNORM_ALPHA = (2 * DEPTH) ** 0.25

LOG2E = math.log2(math.e)
Q_SCALE = HEAD_DIM ** -0.5 * LOG2E
NEG = -1.0e30
SKIP = -3.0e38
VMEM_LIMIT = 56 * 1024 * 1024
LANES = 128
KAUG = 2 * HEAD_DIM
VROWS = HEAD_DIM + 16
POS_PERIOD = 256

TQ_MOBA, TQ_NSA, TQ_FOX = 0, 8, 16
TV_MOBA, TV_NSA, TV_FOX = 28, 36, 40
T_SLOTS = 52
KA_MOBA, KA_NSA = 0, 8
KA_SLOTS = 12
N_CMP, N_MEMQ = 0, 4
N_SLOTS = 8


def _cparams(sem):
    return pltpu.CompilerParams(dimension_semantics=sem, vmem_limit_bytes=VMEM_LIMIT)


def _split2(x):
    hi = x.astype(BF16)
    return hi, (x - hi.astype(F32)).astype(BF16)


def _split3(x):
    hi = x.astype(BF16)
    r1 = x - hi.astype(F32)
    mid = r1.astype(BF16)
    lo = (r1 - mid.astype(F32)).astype(BF16)
    return hi, mid, lo


def _np_split3(x):
    x = np.asarray(x, np.float32)
    hi = x.astype(ml_dtypes.bfloat16).astype(np.float32)
    r1 = x - hi
    mid = r1.astype(ml_dtypes.bfloat16).astype(np.float32)
    lo = (r1 - mid).astype(ml_dtypes.bfloat16).astype(np.float32)
    return hi, mid, lo


def _dot_nt(a, b):
    return lax.dot_general(a, b, (((1,), (1,)), ((), ())), preferred_element_type=F32)


def _dot_tn(a, b):
    return lax.dot_general(a, b, (((0,), (0,)), ((), ())), preferred_element_type=F32)


def _dot(a, b):
    return jnp.dot(a, b, preferred_element_type=F32)


def _layer_norm_rows(x, g, b):
    mu = jnp.mean(x, axis=-1, keepdims=True)
    xc = x - mu
    var = jnp.mean(xc * xc, axis=-1, keepdims=True)
    return xc * lax.rsqrt(var + LN_EPS) * g + b


def _alibi_slopes2(n):
    return (np.exp2(-8.0 * np.arange(1, n + 1, dtype=np.float64) / n) * LOG2E).astype(np.float32)


def _slope_rows(slopes2, lanes):
    pieces = np.stack(_np_split3(slopes2), axis=1)
    rows = np.zeros((len(slopes2), HEAD_DIM, lanes), np.float32)
    rows[:, :3, :] = pieces[:, :, None]
    return jnp.asarray(rows, BF16)


def _ln_kernel(x_ref, g_ref, b_ref, o32_ref, o16_ref):
    y = _layer_norm_rows(x_ref[...], g_ref[...], b_ref[...])
    o32_ref[...] = y
    o16_ref[...] = y.astype(BF16)


def layer_norm_pallas(x, g, b, tm=512):
    m, d = x.shape
    return pl.pallas_call(
        _ln_kernel,
        grid=(m // tm,),
        in_specs=[pl.BlockSpec((tm, d), lambda i: (i, 0)),
                  pl.BlockSpec((1, d), lambda i: (0, 0)),
                  pl.BlockSpec((1, d), lambda i: (0, 0))],
        out_specs=[pl.BlockSpec((tm, d), lambda i: (i, 0)),
                   pl.BlockSpec((tm, d), lambda i: (i, 0))],
        out_shape=[jax.ShapeDtypeStruct((m, d), F32), jax.ShapeDtypeStruct((m, d), BF16)],
        compiler_params=_cparams(("parallel",)),
        name="layer_norm",
    )(x, g.reshape(1, d), b.reshape(1, d))


def _gates_kernel(h_ref, wg_ref, wf_ref, bf_ref, tri_ref, place_ref, g_ref, caug_ref, carry_ref):
    si = pl.program_id(1)

    @pl.when(si == 0)
    def _():
        carry_ref[...] = jnp.zeros_like(carry_ref)

    h_hi, h_lo = _split2(h_ref[0])
    wg_hi, wg_lo = _split2(wg_ref[...])
    g = _dot(h_hi, wg_hi) + _dot(h_lo, wg_hi) + _dot(h_hi, wg_lo)
    g_ref[0] = 1.0 / (1.0 + jnp.exp(-g))
    wf_hi, wf_lo = _split2(wf_ref[...])
    x = _dot(h_hi, wf_hi) + _dot(h_lo, wf_hi) + _dot(h_hi, wf_lo) + bf_ref[...]
    logf = jnp.minimum(x, 0.0) - jnp.log(1.0 + jnp.exp(-jnp.abs(x)))
    tri = tri_ref[...]
    l_hi, l_mid, l_lo = _split3(logf)
    c = _dot(tri, l_hi) + _dot(tri, l_mid) + _dot(tri, l_lo) + carry_ref[0:1, :]
    carry_ref[...] = jnp.broadcast_to(c[-1:, :], carry_ref.shape)
    n_hi, n_mid, n_lo = _split3(-LOG2E * c)
    caug = _dot(n_hi, place_ref[0]) + _dot(n_mid, place_ref[1]) + _dot(n_lo, place_ref[2])
    caug_ref[0] = caug.astype(BF16)


def gates_pallas(h3, w_gate, w_forget, b_forget, t=512):
    b, s, d = h3.shape
    gw = 3 * NSA_GROUP
    ng = NSA_KV_HEADS * LANES
    wg = jnp.zeros((d, ng), F32)
    for g in range(NSA_KV_HEADS):
        wg = wg.at[:, g * LANES:g * LANES + gw].set(w_gate[:, g * gw:(g + 1) * gw])
    wf = jnp.zeros((d, LANES), F32).at[:, :FOX_HEADS].set(w_forget)
    bf = jnp.zeros((1, LANES), F32).at[0, :FOX_HEADS].set(b_forget)
    tri = (np.arange(t)[None, :] <= np.arange(t)[:, None]).astype(np.float32)
    place = np.zeros((3, LANES, FOX_HEADS * KAUG), np.float32)
    for piece in range(3):
        for hh in range(FOX_HEADS):
            place[piece, hh, hh * KAUG + HEAD_DIM + piece] = 1.0
    nc = FOX_HEADS * KAUG
    return pl.pallas_call(
        _gates_kernel,
        grid=(b, s // t),
        in_specs=[pl.BlockSpec((1, t, d), lambda i, j: (i, j, 0)),
                  pl.BlockSpec((d, ng), lambda i, j: (0, 0)),
                  pl.BlockSpec((d, LANES), lambda i, j: (0, 0)),
                  pl.BlockSpec((1, LANES), lambda i, j: (0, 0)),
                  pl.BlockSpec((t, t), lambda i, j: (0, 0)),
                  pl.BlockSpec((3, LANES, nc), lambda i, j: (0, 0, 0))],
        out_specs=[pl.BlockSpec((1, t, ng), lambda i, j: (i, j, 0)),
                   pl.BlockSpec((1, t, nc), lambda i, j: (i, j, 0))],
        out_shape=[jax.ShapeDtypeStruct((b, s, ng), F32),
                   jax.ShapeDtypeStruct((b, s, nc), BF16)],
        scratch_shapes=[pltpu.VMEM((8, LANES), F32)],
        compiler_params=_cparams(("parallel", "arbitrary")),
        name="gates_cumsum",
    )(h3, wg, wf, bf, jnp.asarray(tri, BF16), jnp.asarray(place, BF16))


def _proj_heads_kernel(x_ref, w_ref, sc_ref, o_ref, *, heads_per_step):
    acc = _dot(x_ref[0], w_ref[...]) * sc_ref[...]
    for j in range(heads_per_step):
        o_ref[0, j] = acc[:, j * HEAD_DIM:(j + 1) * HEAD_DIM].astype(o_ref.dtype)


def proj_heads_pallas(x3, w, colscale, tm, heads_per_step):
    b, s, d = x3.shape
    n = w.shape[1]
    tn = heads_per_step * HEAD_DIM
    return pl.pallas_call(
        functools.partial(_proj_heads_kernel, heads_per_step=heads_per_step),
        grid=(b, s // tm, n // tn),
        in_specs=[pl.BlockSpec((1, tm, d), lambda i, j, k: (i, j, 0)),
                  pl.BlockSpec((d, tn), lambda i, j, k: (0, k)),
                  pl.BlockSpec((1, tn), lambda i, j, k: (0, k))],
        out_specs=pl.BlockSpec((1, heads_per_step, tm, HEAD_DIM), lambda i, j, k: (i, k, j, 0)),
        out_shape=jax.ShapeDtypeStruct((b, n // HEAD_DIM, s, HEAD_DIM), BF16),
        compiler_params=_cparams(("parallel", "parallel", "arbitrary")),
        name="proj_heads",
    )(x3, w, colscale.reshape(1, n))


def _proj_t_kernel(x_ref, wt_ref, sc_ref, o_ref, *, heads_per_step):
    acc = _dot_nt(wt_ref[...], x_ref[0]) * sc_ref[...]
    o_ref[0] = acc.reshape(heads_per_step, HEAD_DIM, acc.shape[1]).astype(o_ref.dtype)


def proj_t_pallas(x3, wt, rowscale, tm, heads_per_step):
    b, s, d = x3.shape
    n = wt.shape[0]
    tn = heads_per_step * HEAD_DIM
    return pl.pallas_call(
        functools.partial(_proj_t_kernel, heads_per_step=heads_per_step),
        grid=(b, s // tm, n // tn),
        in_specs=[pl.BlockSpec((1, tm, d), lambda i, j, k: (i, j, 0)),
                  pl.BlockSpec((tn, d), lambda i, j, k: (k, 0)),
                  pl.BlockSpec((tn, 1), lambda i, j, k: (k, 0))],
        out_specs=pl.BlockSpec((1, heads_per_step, HEAD_DIM, tm), lambda i, j, k: (i, k, 0, j)),
        out_shape=jax.ShapeDtypeStruct((b, n // HEAD_DIM, HEAD_DIM, s), BF16),
        compiler_params=_cparams(("parallel", "parallel", "arbitrary")),
        name="proj_transposed",
    )(x3, wt, rowscale.reshape(n, 1))


def _proj_kaug_kernel(x_ref, w_ref, *rest, heads_per_step, positional):
    o_ref = rest[-1]
    acc = _dot(x_ref[0], w_ref[...])
    tm = acc.shape[0]
    if positional:
        pos = (pl.program_id(1) * tm + lax.broadcasted_iota(jnp.int32, acc.shape, 0)) % POS_PERIOD
        lane = lax.broadcasted_iota(jnp.int32, acc.shape, 1) % KAUG
        acc = acc + jnp.where((lane >= HEAD_DIM) & (lane < HEAD_DIM + 3), pos.astype(F32), 0.0)
    else:
        acc = acc + rest[0][0].astype(F32)
    for j in range(heads_per_step):
        o_ref[0, j] = acc[:, j * KAUG:(j + 1) * KAUG].astype(o_ref.dtype)


def proj_kaug_pallas(x3, w, aug, tm, heads_per_step):
    b, s, d = x3.shape
    n = w.shape[1]
    tn = heads_per_step * KAUG
    in_specs = [pl.BlockSpec((1, tm, d), lambda i, j, k: (i, j, 0)),
                pl.BlockSpec((d, tn), lambda i, j, k: (0, k))]
    args = [x3, w]
    if aug is not None:
        in_specs.append(pl.BlockSpec((1, tm, tn), lambda i, j, k: (i, j, k)))
        args.append(aug)
    return pl.pallas_call(
        functools.partial(_proj_kaug_kernel, heads_per_step=heads_per_step, positional=aug is None),
        grid=(b, s // tm, n // tn),
        in_specs=in_specs,
        out_specs=pl.BlockSpec((1, heads_per_step, tm, KAUG), lambda i, j, k: (i, k, j, 0)),
        out_shape=jax.ShapeDtypeStruct((b, n // KAUG, s, KAUG), BF16),
        compiler_params=_cparams(("parallel", "parallel", "arbitrary")),
        name="proj_keys_aug",
    )(*args)


def _matmul_ln_kernel(x_ref, w_ref, r_ref, g_ref, b_ref, o32_ref, o16_ref, acc_ref):
    k = pl.program_id(1)

    @pl.when(k == 0)
    def _():
        acc_ref[...] = jnp.zeros_like(acc_ref)

    acc_ref[...] += _dot(x_ref[...], w_ref[...])

    @pl.when(k == pl.num_programs(1) - 1)
    def _():
        y = _layer_norm_rows(DEEPNORM_ALPHA * r_ref[...] + acc_ref[...], g_ref[...], b_ref[...])
        o32_ref[...] = y
        o16_ref[...] = y.astype(BF16)


def matmul_ln_pallas(x, w, res, g, b, tm, tk):
    m, kk = x.shape
    d = w.shape[1]
    return pl.pallas_call(
        _matmul_ln_kernel,
        grid=(m // tm, kk // tk),
        in_specs=[pl.BlockSpec((tm, tk), lambda i, k: (i, k)),
                  pl.BlockSpec((tk, d), lambda i, k: (k, 0)),
                  pl.BlockSpec((tm, d), lambda i, k: (i, 0)),
                  pl.BlockSpec((1, d), lambda i, k: (0, 0)),
                  pl.BlockSpec((1, d), lambda i, k: (0, 0))],
        out_specs=[pl.BlockSpec((tm, d), lambda i, k: (i, 0)),
                   pl.BlockSpec((tm, d), lambda i, k: (i, 0))],
        out_shape=[jax.ShapeDtypeStruct((m, d), F32), jax.ShapeDtypeStruct((m, d), BF16)],
        scratch_shapes=[pltpu.VMEM((tm, d), F32)],
        compiler_params=_cparams(("parallel", "arbitrary")),
        name="matmul_ln",
    )(x, w, res, g.reshape(1, d), b.reshape(1, d))


def _out_proj_ln_kernel(*refs, widths):
    n = len(widths)
    x_refs, (w_ref, r_ref, g_ref, b_ref, o32_ref, o16_ref) = refs[:n], refs[n:]
    acc = None
    off = 0
    for x_ref, width in zip(x_refs, widths):
        part = _dot_tn(x_ref[0], w_ref[off:off + width, :])
        acc = part if acc is None else acc + part
        off += width
    y = _layer_norm_rows(DEEPNORM_ALPHA * r_ref[0] + acc, g_ref[...], b_ref[...])
    o32_ref[0] = y
    o16_ref[0] = y.astype(BF16)


def out_proj_ln_pallas(xts, w, res3, g, b, tm):
    bsz, s, d = res3.shape
    widths = tuple(x.shape[1] for x in xts)
    in_specs = [pl.BlockSpec((1, wd, tm), lambda i, j: (i, 0, j)) for wd in widths]
    in_specs += [pl.BlockSpec((w.shape[0], d), lambda i, j: (0, 0)),
                 pl.BlockSpec((1, tm, d), lambda i, j: (i, j, 0)),
                 pl.BlockSpec((1, d), lambda i, j: (0, 0)),
                 pl.BlockSpec((1, d), lambda i, j: (0, 0))]
    return pl.pallas_call(
        functools.partial(_out_proj_ln_kernel, widths=widths),
        grid=(bsz, s // tm),
        in_specs=in_specs,
        out_specs=[pl.BlockSpec((1, tm, d), lambda i, j: (i, j, 0)),
                   pl.BlockSpec((1, tm, d), lambda i, j: (i, j, 0))],
        out_shape=[jax.ShapeDtypeStruct((bsz, s, d), F32), jax.ShapeDtypeStruct((bsz, s, d), BF16)],
        compiler_params=_cparams(("parallel", "parallel")),
        name="out_proj_ln",
    )(*xts, w, res3, g.reshape(1, d), b.reshape(1, d))


HALO = 16


def _gelu_tanh(x):
    return 0.5 * x * (1.0 + jnp.tanh(math.sqrt(2.0 / math.pi) * (x + 0.044715 * x * x * x)))


def _ffn_up_kernel(x_ref, xh_ref, wu_ref, wg_ref, cw_ref, cb_ref, o_ref):
    j = pl.program_id(1)
    x = x_ref[0]
    u = _dot(x, wu_ref[...])
    g = _dot(x, wg_ref[...])
    gh = _dot(xh_ref[0], wg_ref[...]) * jnp.where(j > 0, 1.0, 0.0)
    row = lax.broadcasted_iota(jnp.int32, g.shape, 0)
    prev1 = gh[HALO - 1:HALO, :]
    prev2 = gh[HALO - 2:HALO - 1, :]
    g_m1 = jnp.where(row == 0, prev1, pltpu.roll(g, 1, axis=0))
    g_m2 = jnp.where(row == 0, prev2, jnp.where(row == 1, prev1, pltpu.roll(g, 2, axis=0)))
    cw = cw_ref[...]
    gc = cb_ref[...] + cw[0:1] * g_m2 + cw[1:2] * g_m1 + cw[2:3] * g
    o_ref[0] = (_gelu_tanh(gc) * u).astype(o_ref.dtype)


def ffn_up_pallas(x3, w_up, conv_w, conv_b, tm, tn):
    b, s, d = x3.shape
    dff = w_up.shape[1] // 2
    nt = dff // tn
    hb = tm // HALO
    cw = jnp.zeros((8, dff), F32).at[:CONV_WIDTH].set(conv_w)
    return pl.pallas_call(
        _ffn_up_kernel,
        grid=(b, s // tm, nt),
        in_specs=[pl.BlockSpec((1, tm, d), lambda i, j, k: (i, j, 0)),
                  pl.BlockSpec((1, HALO, d), lambda i, j, k: (i, jnp.maximum(j * hb - 1, 0), 0)),
                  pl.BlockSpec((d, tn), lambda i, j, k: (0, k)),
                  pl.BlockSpec((d, tn), lambda i, j, k: (0, k + nt)),
                  pl.BlockSpec((8, tn), lambda i, j, k: (0, k)),
                  pl.BlockSpec((1, tn), lambda i, j, k: (0, k))],
        out_specs=pl.BlockSpec((1, tm, tn), lambda i, j, k: (i, j, k)),
        out_shape=jax.ShapeDtypeStruct((b, s, dff), BF16),
        compiler_params=_cparams(("parallel", "parallel", "arbitrary")),
        name="ffn_up",
    )(x3, x3, w_up, w_up, cw, conv_b.reshape(1, dff))


def _online_step_t(st, delta, vaug, m_ref, acc_ref):
    m_prev = m_ref[...]
    m_cur = jnp.max(st, axis=0, keepdims=True) + delta
    m_new = jnp.maximum(m_prev, m_cur)
    alpha = jnp.exp2(m_prev - m_new)
    pt = jnp.exp2(st - (m_new - delta)).astype(BF16)
    acc_ref[...] = alpha * acc_ref[...] + _dot(vaug, pt)
    m_ref[...] = m_new


def _init_state_t(m_ref, acc_ref):
    m_ref[...] = jnp.full(m_ref.shape, NEG, F32)
    acc_ref[...] = jnp.zeros(acc_ref.shape, F32)


def _finish_t(acc_ref):
    acc = acc_ref[...]
    return acc[:HEAD_DIM] / acc[HEAD_DIM:HEAD_DIM + 1]


def _fill_vaug(vaug_ref, vt_ref):
    s = vaug_ref.shape[1]
    vaug_ref[0:HEAD_DIM, :] = vt_ref[0, 0]
    pad = lax.broadcasted_iota(jnp.int32, (VROWS - HEAD_DIM, s), 0)
    vaug_ref[HEAD_DIM:VROWS, :] = jnp.where(pad == 0, 1.0, 0.0).astype(BF16)


def _ktile(ref, idx, size):
    return ref[0, 0, pl.ds(pl.multiple_of(idx * size, size), size), :]


def _vtile(ref, idx, size):
    return ref[:, pl.ds(pl.multiple_of(idx * size, size), size)]


def _stage_scores(st, s_ref, mc_ref):
    s_ref[...] = st
    mc_ref[...] = jnp.max(st, axis=0, keepdims=True)


def _stage_update(s_ref, mc_ref, delta, vaug, m_ref, acc_ref):
    m_prev = m_ref[...]
    m_new = jnp.maximum(m_prev, mc_ref[...] + delta)
    alpha = jnp.exp2(m_prev - m_new)
    pt = jnp.exp2(s_ref[...] - (m_new - delta)).astype(BF16)
    acc_ref[...] = alpha * acc_ref[...] + _dot(vaug, pt)
    m_ref[...] = m_new


def _score_buffers(tk, lanes):
    return [pltpu.VMEM((tk, lanes), F32), pltpu.VMEM((1, lanes), F32),
            pltpu.VMEM((tk, lanes), F32), pltpu.VMEM((1, lanes), F32)]


def _flash_pipeline(lead, n_loop, scores, meta, bufs, m_ref, acc_ref, n_even):
    _stage_scores(lead[0][0](), *bufs[0])
    for i in range(1, len(lead)):
        _stage_scores(lead[i][0](), *bufs[i % 2])
        _stage_update(*bufs[(i - 1) % 2], *lead[i - 1][1](), m_ref, acc_ref)
    cur = (len(lead) - 1) % 2
    nxt = 1 - cur
    if scores is None:
        _stage_update(*bufs[cur], *lead[-1][1](), m_ref, acc_ref)
        return
    _stage_scores(scores(0), *bufs[nxt])
    _stage_update(*bufs[cur], *lead[-1][1](), m_ref, acc_ref)
    last = jnp.maximum(n_loop - 1, 0)

    def body(kp, carry):
        k0 = 2 * kp
        k1 = k0 + 1
        _stage_scores(scores(jnp.minimum(k1, last)), *bufs[cur])
        _stage_update(*bufs[nxt], *meta(k0), m_ref, acc_ref)
        _stage_scores(scores(jnp.minimum(k0 + 2, last)), *bufs[nxt])
        delta1, vaug1 = meta(jnp.minimum(k1, last))
        if not n_even:
            delta1 = jnp.where(k1 < n_loop, delta1, SKIP)
        _stage_update(*bufs[cur], delta1, vaug1, m_ref, acc_ref)
        return carry

    lax.fori_loop(0, (n_loop + 1) // 2, body, 0)


FOX_TQ = 1024
FOX_TK = 256


def _fox_kernel(qt_ref, k_ref, vt_ref, o_ref, m_ref, acc_ref, vaug_ref, sa_ref, ma_ref, sb_ref, mb_ref):
    tq, tk = o_ref.shape[3], FOX_TK
    nd = tq // tk
    qi = pl.program_id(2)

    @pl.when(qi == 0)
    def _():
        _fill_vaug(vaug_ref, vt_ref)

    ones3 = jnp.where(lax.broadcasted_iota(jnp.int32, (HEAD_DIM, tq), 0) < 3, 1.0, 0.0).astype(BF16)
    qaug = jnp.concatenate([qt_ref[0, 0], ones3], axis=0)
    _init_state_t(m_ref, acc_ref)
    key = lax.broadcasted_iota(jnp.int32, (tk, tq), 0)
    qry = lax.broadcasted_iota(jnp.int32, (tk, tq), 1)

    def scores(kj):
        return _dot(_ktile(k_ref, kj, tk), qaug)

    def meta(kj):
        return 0.0, _vtile(vaug_ref, kj, tk)

    lead = [(functools.partial(lambda d: jnp.where(key + d * tk <= qry, scores(qi * nd + d), NEG), d),
             functools.partial(lambda d: meta(qi * nd + d), d)) for d in range(nd)]
    _flash_pipeline(lead, qi * nd, scores, meta, ((sa_ref, ma_ref), (sb_ref, mb_ref)), m_ref, acc_ref,
                    n_even=nd % 2 == 0)
    o_ref[0, 0] = _finish_t(acc_ref).astype(o_ref.dtype)


def fox_pallas(qvt, kaug):
    b, _, dh, s = qvt.shape
    tq = min(FOX_TQ, s)
    assert tq % FOX_TK == 0
    out = pl.pallas_call(
        _fox_kernel,
        grid=(b, FOX_HEADS, s // tq),
        in_specs=[pl.BlockSpec((1, 1, dh, tq), lambda i, h, j: (i, TQ_FOX + h, 0, j)),
                  pl.BlockSpec((1, 1, s, KAUG), lambda i, h, j: (i, h, 0, 0)),
                  pl.BlockSpec((1, 1, dh, s), lambda i, h, j: (i, TV_FOX + h, 0, 0))],
        out_specs=pl.BlockSpec((1, 1, dh, tq), lambda i, h, j: (i, h, 0, j)),
        out_shape=jax.ShapeDtypeStruct((b, FOX_HEADS, dh, s), BF16),
        scratch_shapes=[pltpu.VMEM((1, tq), F32), pltpu.VMEM((VROWS, tq), F32),
                        pltpu.VMEM((VROWS, s), BF16)] + _score_buffers(FOX_TK, tq),
        compiler_params=_cparams(("parallel", "parallel", "arbitrary")),
        name="fox_attention",
    )(qvt, kaug, qvt)
    return out.reshape(b, FOX_HEADS * dh, s)


def _moba_kernel(slope_ref, qt_ref, k_ref, vt_ref, ind_ref, srow_ref, o_ref,
                 m_ref, acc_ref, vaug_ref, km_ref, sel_ref, sa_ref, ma_ref, sb_ref, mb_ref):
    t = MOBA_BLOCK
    tq = o_ref.shape[3]
    nd = tq // t
    h = pl.program_id(1)
    qi = pl.program_id(2)
    slope2 = slope_ref[h]

    @pl.when(qi == 0)
    def _():
        _fill_vaug(vaug_ref, vt_ref)
        km_ref[...] = _dot(ind_ref[...], k_ref[0, 0]) * (1.0 / MOBA_BLOCK)

    qt = qt_ref[0, 0]
    km_hi, km_lo = _split2(km_ref[...])
    q0 = jnp.concatenate([qt, jnp.zeros_like(qt)], axis=0)
    gate = _dot(km_hi, q0) + _dot(km_lo, q0)
    blk = lax.broadcasted_iota(jnp.int32, gate.shape, 0)
    lane_blk = lax.broadcasted_iota(jnp.int32, (1, tq), 1) // t
    valid = blk < qi * nd + lane_blk
    work = jnp.where(valid, gate, -jnp.inf)
    sel = jnp.zeros(gate.shape, F32)
    big = jnp.int32(2 ** 30)
    for _ in range(MOBA_TOPK):
        mx = jnp.max(work, axis=0, keepdims=True)
        first = jnp.min(jnp.where(work == mx, blk, big), axis=0, keepdims=True)
        hit = blk == first
        sel = jnp.where(hit, 1.0, sel)
        work = jnp.where(hit, -jnp.inf, work)
    sel_ref[...] = jnp.where(valid, sel, 0.0)

    qaug = jnp.concatenate([qt, srow_ref[...]], axis=0)
    _init_state_t(m_ref, acc_ref)
    key = lax.broadcasted_iota(jnp.int32, (t, tq), 0)
    qry = lax.broadcasted_iota(jnp.int32, (t, tq), 1)

    def tile_constant(kj, d):
        sees = sel_ref[pl.ds(kj, 1), :] > 0.0
        if d is not None:
            sees = sees | (lane_blk == d)
        return jnp.where(sees, slope2 * ((kj - qi * nd) * t).astype(F32), SKIP)

    def scores(kj):
        return _dot(_ktile(k_ref, kj, t), qaug)

    def meta(kj):
        return tile_constant(kj, None), _vtile(vaug_ref, kj, t)

    lead = [(functools.partial(lambda d: jnp.where(key + d * t <= qry, scores(qi * nd + d), NEG), d),
             functools.partial(lambda d: (tile_constant(qi * nd + d, d), _vtile(vaug_ref, qi * nd + d, t)), d))
            for d in range(nd)]
    _flash_pipeline(lead, qi * nd, scores, meta, ((sa_ref, ma_ref), (sb_ref, mb_ref)), m_ref, acc_ref,
                    n_even=nd % 2 == 0)
    o_ref[0, 0] = _finish_t(acc_ref).astype(o_ref.dtype)


MOBA_TQ = 1024


def moba_pallas(qvt, kaug):
    b, _, dh, s = qvt.shape
    t = MOBA_BLOCK
    tq = min(MOBA_TQ, s)
    assert POS_PERIOD == t and tq % t == 0
    nb = s // t
    nbp = max(LANES, nb)
    ind = np.zeros((nbp, s), np.float32)
    ind[np.arange(s) // t, np.arange(s)] = 1.0
    slopes2 = _alibi_slopes2(MOBA_HEADS)
    grid_spec = pltpu.PrefetchScalarGridSpec(
        num_scalar_prefetch=1,
        grid=(b, MOBA_HEADS, s // tq),
        in_specs=[pl.BlockSpec((1, 1, dh, tq), lambda i, h, j, sl: (i, TQ_MOBA + h, 0, j)),
                  pl.BlockSpec((1, 1, s, KAUG), lambda i, h, j, sl: (i, KA_MOBA + h, 0, 0)),
                  pl.BlockSpec((1, 1, dh, s), lambda i, h, j, sl: (i, TV_MOBA + h, 0, 0)),
                  pl.BlockSpec((nbp, s), lambda i, h, j, sl: (0, 0)),
                  pl.BlockSpec((None, dh, tq), lambda i, h, j, sl: (h, 0, 0))],
        out_specs=pl.BlockSpec((1, 1, dh, tq), lambda i, h, j, sl: (i, h, 0, j)),
        scratch_shapes=[pltpu.VMEM((1, tq), F32), pltpu.VMEM((VROWS, tq), F32),
                        pltpu.VMEM((VROWS, s), BF16), pltpu.VMEM((nbp, KAUG), F32),
                        pltpu.VMEM((nbp, tq), F32)] + _score_buffers(t, tq))
    out = pl.pallas_call(
        _moba_kernel,
        grid_spec=grid_spec,
        out_shape=jax.ShapeDtypeStruct((b, MOBA_HEADS, dh, s), BF16),
        compiler_params=_cparams(("parallel", "parallel", "arbitrary")),
        name="moba_attention",
    )(jnp.asarray(slopes2), qvt, kaug, qvt, jnp.asarray(ind, BF16), _slope_rows(slopes2, tq))
    return out.reshape(b, MOBA_HEADS * dh, s)


def _nsa_compress_kernel(x_ref, w1a_ref, w1b_ref, pe_ref, w1_ref, w2_ref, w2t_ref, o_ref, ot_ref):
    nr = x_ref.shape[2]
    x = x_ref[0, 0]

    def near_f32(xb, w):
        w_hi, w_lo = _split2(w)
        return _dot(xb, w_hi) + _dot(xb, w_lo)

    a = near_f32(x, w1a_ref[0])
    bm = near_f32(x, w1b_ref[0])
    pe_hi, pe_mid, pe_lo = _split3(pe_ref[0])
    w1_hi, w1_lo = _split2(w1_ref[0])
    pe_term = (_dot(pe_hi, w1_hi) + _dot(pe_mid, w1_hi) + _dot(pe_lo, w1_hi)
               + _dot(pe_hi, w1_lo) + _dot(pe_mid, w1_lo))[0:1]
    pre = a + pltpu.roll(bm, nr - 1, axis=0) + pe_term
    hid = _gelu_tanh(pre)
    h_hi, h_mid, h_lo = _split3(hid)
    w2_hi, w2_lo = _split2(w2_ref[0])
    o_ref[0, 0] = (_dot(h_hi, w2_hi) + _dot(h_mid, w2_hi) + _dot(h_lo, w2_hi)
                   + _dot(h_hi, w2_lo) + _dot(h_mid, w2_lo))
    t_hi, t_lo = _split2(w2t_ref[0])
    ot_ref[0, 0] = (_dot_nt(t_hi, h_hi) + _dot_nt(t_hi, h_mid) + _dot_nt(t_hi, h_lo)
                    + _dot_nt(t_lo, h_hi) + _dot_nt(t_lo, h_mid))


def nsa_compress_pallas(nat, pe, w1, w2):
    b, _, s, dh = nat.shape
    nr = s // NSA_CMP_STRIDE
    half = NSA_CMP_STRIDE * dh
    hid = w1.shape[-1]
    x = nat[:, N_CMP:N_CMP + 4].reshape(b, 4, nr, half)
    w1f = w1.reshape(2, NSA_CMP_LEN * dh, hid)
    pef = jnp.zeros((2, 8, NSA_CMP_LEN * dh), F32).at[:, 0].set(pe.reshape(2, NSA_CMP_LEN * dh))
    return pl.pallas_call(
        _nsa_compress_kernel,
        grid=(b, 4),
        in_specs=[pl.BlockSpec((1, 1, nr, half), lambda i, j: (i, j, 0, 0)),
                  pl.BlockSpec((1, half, hid), lambda i, j: (j // 2, 0, 0)),
                  pl.BlockSpec((1, half, hid), lambda i, j: (j // 2, 1, 0)),
                  pl.BlockSpec((1, 8, 2 * half), lambda i, j: (j // 2, 0, 0)),
                  pl.BlockSpec((1, 2 * half, hid), lambda i, j: (j // 2, 0, 0)),
                  pl.BlockSpec((1, hid, dh), lambda i, j: (j // 2, 0, 0)),
                  pl.BlockSpec((1, dh, hid), lambda i, j: (j // 2, 0, 0))],
        out_specs=[pl.BlockSpec((1, 1, nr, dh), lambda i, j: (i, j, 0, 0)),
                   pl.BlockSpec((1, 1, dh, nr), lambda i, j: (i, j, 0, 0))],
        out_shape=[jax.ShapeDtypeStruct((b, 4, nr, dh), F32),
                   jax.ShapeDtypeStruct((b, 4, dh, nr), F32)],
        compiler_params=_cparams(("parallel", "parallel")),
        name="nsa_compress",
    )(x, w1f, w1f, pef, w1f, w2, jnp.swapaxes(w2, 1, 2))


NSA_TQ = 128


NSA_CMP_CHUNK = 256


def _nsa_cmp_kernel(slope_ref, qt_ref, kc_ref, vct_ref, mimpt_ref, oct_ref, selt_ref, imp_ref):
    tq = NSA_TQ
    g = pl.program_id(1)
    qi = pl.program_id(2)
    q0 = qi * tq
    nr = kc_ref.shape[2]
    nsb = selt_ref.shape[2]
    t_lane = q0 + lax.broadcasted_iota(jnp.int32, (1, tq), 1)
    chunk = min(NSA_CMP_CHUNK, nr)
    tiles_per_chunk = chunk * NSA_CMP_STRIDE // tq

    def branch(n):
        kc_hi, kc_lo = _split2(kc_ref[0, 0, 0:n, :])
        vct = vct_ref[0, 0, :, 0:n].astype(BF16)
        cmp_end = NSA_CMP_STRIDE * lax.broadcasted_iota(jnp.int32, (n, tq), 0) + (NSA_CMP_LEN - 1)
        mask = cmp_end <= t_lane
        rel = (cmp_end - q0).astype(F32)
        psum = jnp.zeros((n, tq), F32)
        qt4 = jnp.concatenate([qt_ref[0, hh] for hh in range(NSA_GROUP)], axis=1)
        st4 = _dot(kc_hi, qt4) + _dot(kc_lo, qt4)
        probs = []
        for hh in range(NSA_GROUP):
            st = st4[:, hh * tq:(hh + 1) * tq] + slope_ref[g * NSA_GROUP + hh] * rel
            st = jnp.where(mask, st, -jnp.inf)
            m = jnp.max(st, axis=0, keepdims=True)
            m = jnp.where(m > -jnp.inf, m, 0.0)
            e = jnp.exp2(st - m)
            p = e * (1.0 / jnp.maximum(jnp.sum(e, axis=0, keepdims=True), 1e-30))
            probs.append(p.astype(BF16))
            psum = psum + p
        o4 = _dot(vct, jnp.concatenate(probs, axis=1))
        for hh in range(NSA_GROUP):
            oct_ref[0, hh] = o4[:, hh * tq:(hh + 1) * tq]
        p_hi, p_mid, p_lo = _split3(psum)
        mimpt = mimpt_ref[:, 0:n]
        imp_ref[...] = _dot(mimpt, p_hi) + _dot(mimpt, p_mid) + _dot(mimpt, p_lo)

    n_chunks = nr // chunk
    for c in range(n_chunks):
        pl.when(jnp.minimum(qi // tiles_per_chunk, n_chunks - 1) == c)(
            functools.partial(branch, (c + 1) * chunk))

    imp = imp_ref[...]
    blk = lax.broadcasted_iota(jnp.int32, (nsb, tq), 0)
    jt = t_lane // NSA_SEL_BLOCK
    forced = (blk == 0) | (blk == jt) | (blk == jt - 1)
    imp = jnp.where(forced, NSA_FORCE_SCORE, imp)
    valid = blk * NSA_SEL_BLOCK <= t_lane
    work0 = jnp.where(valid, imp, -jnp.inf)
    big = jnp.int32(2 ** 30)

    def pick(_, work):
        mx = jnp.max(work, axis=0, keepdims=True)
        first = jnp.min(jnp.where(work == mx, blk, big), axis=0, keepdims=True)
        return jnp.where(blk == first, -jnp.inf, work)

    work = lax.fori_loop(0, min(NSA_TOPK, nsb), pick, work0)
    selt_ref[0, 0] = jnp.where(valid & (work == -jnp.inf), 1.0, 0.0)


def nsa_cmp_pallas(qvt, cmp_kv, cmp_kvt):
    b, _, dh, s = qvt.shape
    tq = NSA_TQ
    nr = cmp_kv.shape[2]
    nsb = s // NSA_SEL_BLOCK
    ratio = NSA_SEL_BLOCK // NSA_CMP_STRIDE
    front = NSA_CMP_LEN // NSA_CMP_STRIDE - 1
    n_int = ratio + front
    n_idx = np.arange(nr)[None, :]
    j_idx = np.arange(nsb)[:, None]
    mimpt = ((n_idx >= ratio * j_idx - front) & (n_idx <= ratio * j_idx + n_int - 1 - front)
             & (n_idx < nr - 1)).astype(np.float32)
    grid_spec = pltpu.PrefetchScalarGridSpec(
        num_scalar_prefetch=1,
        grid=(b, NSA_KV_HEADS, s // tq),
        in_specs=[pl.BlockSpec((1, NSA_GROUP, dh, tq), lambda i, g, j, sl: (i, TQ_NSA // NSA_GROUP + g, 0, j)),
                  pl.BlockSpec((1, 1, nr, dh), lambda i, g, j, sl: (i, g, 0, 0)),
                  pl.BlockSpec((1, 1, dh, nr), lambda i, g, j, sl: (i, 2 + g, 0, 0)),
                  pl.BlockSpec((nsb, nr), lambda i, g, j, sl: (0, 0))],
        out_specs=[pl.BlockSpec((1, NSA_GROUP, dh, tq), lambda i, g, j, sl: (i, g, 0, j)),
                   pl.BlockSpec((1, 1, nsb, tq), lambda i, g, j, sl: (i, g, 0, j))],
        scratch_shapes=[pltpu.VMEM((nsb, tq), F32)])
    assert nr % min(NSA_CMP_CHUNK, nr) == 0
    return pl.pallas_call(
        _nsa_cmp_kernel,
        grid_spec=grid_spec,
        out_shape=[jax.ShapeDtypeStruct((b, NSA_HEADS, dh, s), F32),
                   jax.ShapeDtypeStruct((b, NSA_KV_HEADS, nsb, s), F32)],
        compiler_params=_cparams(("parallel", "parallel", "parallel")),
        name="nsa_compressed_select",
    )(jnp.asarray(_alibi_slopes2(NSA_HEADS)), qvt, cmp_kv, cmp_kvt, jnp.asarray(mimpt, BF16))


NSA_TK = 256
NSA_SEL_TQ = 256
NSA_LANES = NSA_GROUP * NSA_TQ


def _nsa_qaug(qt_ref, srow_ref):
    return jnp.concatenate(
        [jnp.concatenate([qt_ref[0, hh], srow_ref[hh]], axis=0) for hh in range(NSA_GROUP)], axis=1)


def _nsa_sel_kernel(qt_ref, k_ref, vt_ref, selt_ref, srow_ref, slane_ref, o_ref, m_ref, acc_ref, vaug_ref,
                    sa_ref, ma_ref, sb_ref, mb_ref):
    tq, tk = NSA_SEL_TQ, NSA_TK
    per_tile = tk // NSA_SEL_BLOCK
    qi = pl.program_id(2)
    q0 = qi * tq

    @pl.when(qi == 0)
    def _():
        _fill_vaug(vaug_ref, vt_ref)

    qaug = _nsa_qaug(qt_ref, srow_ref)
    slane = slane_ref[0:1, :]
    diag = q0 // tk

    def scores(kj):
        st = _dot(_ktile(k_ref, kj, tk), qaug)
        rows = [jnp.broadcast_to(selt_ref[0, 0, pl.ds(kj * per_tile + c, 1), :], (NSA_SEL_BLOCK, tq))
                for c in range(per_tile)]
        bias = (jnp.concatenate(rows, axis=0) - 1.0) * (-NEG)
        return st + jnp.concatenate([bias] * NSA_GROUP, axis=1)

    def meta(kj):
        return slane * (kj * tk - q0).astype(F32), _vtile(vaug_ref, kj, tk)

    def own_tile():
        key = lax.broadcasted_iota(jnp.int32, (tk, tq), 0)
        qry = lax.broadcasted_iota(jnp.int32, (tk, tq), 1)
        causal = jnp.concatenate([key <= qry] * NSA_GROUP, axis=1)
        return jnp.where(causal, scores(diag), NEG)

    _init_state_t(m_ref, acc_ref)
    _flash_pipeline([(own_tile, lambda: meta(diag))], diag, scores, meta,
                    ((sa_ref, ma_ref), (sb_ref, mb_ref)), m_ref, acc_ref, n_even=False)
    out = _finish_t(acc_ref)
    for hh in range(NSA_GROUP):
        o_ref[0, hh] = out[:, hh * tq:(hh + 1) * tq]


def _nsa_tables(tq):
    slopes2 = _alibi_slopes2(NSA_HEADS)
    srow = _slope_rows(slopes2, tq)
    slane = np.repeat(slopes2.reshape(NSA_KV_HEADS, NSA_GROUP), tq, axis=1)
    slane8 = np.repeat(slane[:, None, :], 8, axis=1)
    return srow, jnp.asarray(slane8, F32)


def nsa_sel_pallas(qvt, kaug, selt):
    b, _, dh, s = qvt.shape
    tq = NSA_SEL_TQ
    assert tq == NSA_TK
    lanes = NSA_GROUP * tq
    nsb = s // NSA_SEL_BLOCK
    srow, slane = _nsa_tables(tq)
    return pl.pallas_call(
        _nsa_sel_kernel,
        grid=(b, NSA_KV_HEADS, s // tq),
        in_specs=[pl.BlockSpec((1, NSA_GROUP, dh, tq), lambda i, g, j: (i, TQ_NSA // NSA_GROUP + g, 0, j)),
                  pl.BlockSpec((1, 1, s, KAUG), lambda i, g, j: (i, KA_NSA + g, 0, 0)),
                  pl.BlockSpec((1, 1, dh, s), lambda i, g, j: (i, TV_NSA + g, 0, 0)),
                  pl.BlockSpec((1, 1, nsb, tq), lambda i, g, j: (i, g, 0, j)),
                  pl.BlockSpec((NSA_GROUP, dh, tq), lambda i, g, j: (g, 0, 0)),
                  pl.BlockSpec((None, 8, lanes), lambda i, g, j: (g, 0, 0))],
        out_specs=pl.BlockSpec((1, NSA_GROUP, dh, tq), lambda i, g, j: (i, g, 0, j)),
        out_shape=jax.ShapeDtypeStruct((b, NSA_HEADS, dh, s), F32),
        scratch_shapes=[pltpu.VMEM((1, lanes), F32), pltpu.VMEM((VROWS, lanes), F32),
                        pltpu.VMEM((VROWS, s), BF16)] + _score_buffers(NSA_TK, lanes),
        compiler_params=_cparams(("parallel", "parallel", "arbitrary")),
        name="nsa_selected",
    )(qvt, kaug, qvt, selt, srow, slane)


NSA_WT = 128


def _nsa_win_kernel(qt_ref, k_ref, vt_ref, srow_ref, slane_ref, oc_ref, os_ref, gate_ref, o_ref,
                    m_ref, acc_ref, vaug_ref, sa_ref, ma_ref, sb_ref, mb_ref):
    tq = NSA_TQ
    wt = NSA_WT
    qi = pl.program_id(2)

    @pl.when(qi == 0)
    def _():
        _fill_vaug(vaug_ref, vt_ref)

    qaug = _nsa_qaug(qt_ref, srow_ref)
    slane = slane_ref[0:1, :]
    key = lax.broadcasted_iota(jnp.int32, (wt, tq), 0)
    qry = lax.broadcasted_iota(jnp.int32, (wt, tq), 1)
    span = NSA_WINDOW // wt

    def tile(d, keep):
        kj = jnp.maximum(qi - d, 0)

        def scores():
            st = _dot(_ktile(k_ref, kj, wt), qaug)
            if keep is not None:
                st = jnp.where(jnp.concatenate([keep] * NSA_GROUP, axis=1), st, NEG)
            return st

        def meta():
            base = (kj * wt) // POS_PERIOD * POS_PERIOD - qi * tq
            delta = jnp.where(qi - d >= 0, slane * base.astype(F32), SKIP)
            return delta, _vtile(vaug_ref, kj, wt)

        return scores, meta

    _init_state_t(m_ref, acc_ref)
    tiles = [tile(0, key <= qry)] + [tile(d, None) for d in range(1, span)] + [tile(span, key > qry)]
    _flash_pipeline(tiles, None, None, None, ((sa_ref, ma_ref), (sb_ref, mb_ref)), m_ref, acc_ref, True)

    o_w = _finish_t(acc_ref)
    gt = gate_ref[0].T
    for hh in range(NSA_GROUP):
        c0 = 3 * hh
        mix = (gt[c0:c0 + 1] * oc_ref[0, hh] + gt[c0 + 1:c0 + 2] * os_ref[0, hh]
               + gt[c0 + 2:c0 + 3] * o_w[:, hh * tq:(hh + 1) * tq])
        o_ref[0, hh] = mix.astype(o_ref.dtype)


def nsa_win_pallas(qvt, kaug, o_c, o_s, gates):
    b, _, dh, s = qvt.shape
    tq, wt = NSA_TQ, NSA_WT
    assert tq == wt and tq == LANES
    srow, slane = _nsa_tables(tq)
    head_blk = pl.BlockSpec((1, NSA_GROUP, dh, tq), lambda i, g, j: (i, g, 0, j))
    out = pl.pallas_call(
        _nsa_win_kernel,
        grid=(b, NSA_KV_HEADS, s // tq),
        in_specs=[pl.BlockSpec((1, NSA_GROUP, dh, tq), lambda i, g, j: (i, TQ_NSA // NSA_GROUP + g, 0, j)),
                  pl.BlockSpec((1, 1, s, KAUG), lambda i, g, j: (i, KA_NSA + 2 + g, 0, 0)),
                  pl.BlockSpec((1, 1, dh, s), lambda i, g, j: (i, TV_NSA + 2 + g, 0, 0)),
                  pl.BlockSpec((NSA_GROUP, dh, tq), lambda i, g, j: (g, 0, 0)),
                  pl.BlockSpec((None, 8, NSA_LANES), lambda i, g, j: (g, 0, 0)),
                  head_blk, head_blk,
                  pl.BlockSpec((1, tq, LANES), lambda i, g, j: (i, j, g))],
        out_specs=head_blk,
        out_shape=jax.ShapeDtypeStruct((b, NSA_HEADS, dh, s), BF16),
        scratch_shapes=[pltpu.VMEM((1, NSA_LANES), F32), pltpu.VMEM((VROWS, NSA_LANES), F32),
                        pltpu.VMEM((VROWS, s), BF16)] + _score_buffers(wt, NSA_LANES),
        compiler_params=_cparams(("parallel", "parallel", "arbitrary")),
        name="nsa_window_mix",
    )(qvt, kaug, qvt, srow, slane, o_c, o_s, gates)
    return out.reshape(b, NSA_HEADS * dh, s)


def _mem_attn_kernel(q_ref, k_ref, vt_ref, o_ref):
    for hh in range(MEM_HEADS):
        s = _dot_nt(q_ref[0, hh], k_ref[0, hh])
        m = jnp.max(s, axis=-1, keepdims=True)
        e = jnp.exp2(s - m)
        p = e / jnp.sum(e, axis=-1, keepdims=True)
        o_ref[0, hh] = _dot_nt(vt_ref[0, hh], p.astype(BF16)).astype(o_ref.dtype)


def mem_attn_pallas(nat, mem_k, mem_vt, tq=512):
    b, _, s, dh = nat.shape
    n_mem = mem_k.shape[2]
    out = pl.pallas_call(
        _mem_attn_kernel,
        grid=(b, s // tq),
        in_specs=[pl.BlockSpec((1, MEM_HEADS, tq, dh), lambda i, j: (i, N_MEMQ // MEM_HEADS, j, 0)),
                  pl.BlockSpec((1, MEM_HEADS, n_mem, dh), lambda i, j: (i, 0, 0, 0)),
                  pl.BlockSpec((1, MEM_HEADS, dh, n_mem), lambda i, j: (i, 0, 0, 0))],
        out_specs=pl.BlockSpec((1, MEM_HEADS, dh, tq), lambda i, j: (i, 0, 0, j)),
        out_shape=jax.ShapeDtypeStruct((b, MEM_HEADS, dh, s), BF16),
        compiler_params=_cparams(("parallel", "parallel")),
        name="memory_attention",
    )(nat, mem_k, mem_vt)
    return out.reshape(b, MEM_HEADS * dh, s)


def _pad_key_cols(w):
    d, n = w.shape
    w3 = w.reshape(d, n // HEAD_DIM, HEAD_DIM)
    return jnp.concatenate([w3, jnp.zeros_like(w3)], axis=2).reshape(d, 2 * n)


def _in_proj_weights(w_in):
    hd = HEAD_DIM
    sizes = (3 * MOBA_HEADS * hd, NSA_HEADS * hd, 6 * NSA_KV_HEADS * hd, 3 * NSA_HEADS,
             3 * FOX_HEADS * hd, FOX_HEADS, MEM_HEADS * hd)
    offs = np.concatenate([[0], np.cumsum(sizes)])
    moba, nsa_q, nsa_kv, nsa_g, fox, fox_f, mem_q = (w_in[:, offs[i]:offs[i + 1]] for i in range(7))
    mh, fh, g2 = MOBA_HEADS * hd, FOX_HEADS * hd, NSA_KV_HEADS * hd
    moba_q, moba_k, moba_v = moba[:, :mh], moba[:, mh:2 * mh], moba[:, 2 * mh:]
    fox_q, fox_k, fox_v = fox[:, :fh], fox[:, fh:2 * fh], fox[:, 2 * fh:]
    k_cmp, v_cmp, k_slc, v_slc, k_win, v_win = (nsa_kv[:, i * g2:(i + 1) * g2] for i in range(6))
    w_t = jnp.concatenate([moba_q, nsa_q, fox_q, moba_v, v_slc, v_win, fox_v], axis=1).T.astype(BF16)
    t_scale = np.ones((T_SLOTS * hd,), np.float32)
    t_scale[:TV_MOBA * hd] = Q_SCALE
    w_ka = _pad_key_cols(jnp.concatenate([moba_k, k_slc, k_win], axis=1)).astype(BF16)
    w_kf = _pad_key_cols(fox_k).astype(BF16)
    w_nat = jnp.concatenate([k_cmp, v_cmp, mem_q], axis=1).astype(BF16)
    n_scale = np.ones((N_SLOTS * hd,), np.float32)
    n_scale[N_MEMQ * hd:] = Q_SCALE
    return w_t, jnp.asarray(t_scale), w_ka, w_kf, w_nat, jnp.asarray(n_scale), nsa_g, fox_f


def _mixer(h32, h16, mem16, w_in, b_forget, w_mem_kv, cmp_pe, cmp_w1, cmp_w2):
    b, s, d = h16.shape
    tm = min(1024, s)
    w_t, t_scale, w_ka, w_kf, w_nat, n_scale, w_gate, w_forget = _in_proj_weights(w_in)
    gates, caug = gates_pallas(h32, w_gate, w_forget, b_forget)
    qvt = proj_t_pallas(h16, w_t, t_scale, tm=tm, heads_per_step=13)
    k_alibi = proj_kaug_pallas(h16, w_ka, None, tm=tm, heads_per_step=6)
    k_fox = proj_kaug_pallas(h16, w_kf, caug, tm=tm, heads_per_step=6)
    nat = proj_heads_pallas(h16, w_nat, n_scale, tm=tm, heads_per_step=8)
    n_mem = mem16.shape[1]
    mk = MEM_HEADS * HEAD_DIM
    mem_k = proj_heads_pallas(mem16, w_mem_kv[:, :mk].astype(BF16), jnp.ones((mk,), F32),
                              tm=n_mem, heads_per_step=MEM_HEADS)
    mem_vt = proj_t_pallas(mem16, w_mem_kv[:, mk:].T.astype(BF16), jnp.ones((mk,), F32),
                           tm=n_mem, heads_per_step=MEM_HEADS)
    o_moba = moba_pallas(qvt, k_alibi)
    o_fox = fox_pallas(qvt, k_fox)
    cmp_kv, cmp_kvt = nsa_compress_pallas(nat, cmp_pe, cmp_w1, cmp_w2)
    o_c, selt = nsa_cmp_pallas(qvt, cmp_kv, cmp_kvt)
    o_s = nsa_sel_pallas(qvt, k_alibi, selt)
    o_nsa = nsa_win_pallas(qvt, k_alibi, o_c, o_s, gates)
    o_mem = mem_attn_pallas(nat, mem_k, mem_vt)
    return [o_moba, o_nsa, o_fox, o_mem]


def kernel(x, mem, emb_ln_g, emb_ln_b, w_in, b_forget, w_mem_kv, nsa_cmp_pe, nsa_cmp_w1, nsa_cmp_w2,
           w_out, ln1_g, ln1_b, ffn_w_up, ffn_conv_w, ffn_conv_b, ffn_w_down, ln2_g, ln2_b):
    b, s, d = x.shape
    depth = w_in.shape[0]
    dff = ffn_w_down.shape[1]
    mem16 = mem.astype(BF16)
    h32, h16 = layer_norm_pallas(x.reshape(b * s, d), emb_ln_g, emb_ln_b)
    for l in range(depth):
        heads = _mixer(h32.reshape(b, s, d), h16.reshape(b, s, d), mem16, w_in[l], b_forget[l], w_mem_kv[l],
                       nsa_cmp_pe[l], nsa_cmp_w1[l], nsa_cmp_w2[l])
        h32, h16 = out_proj_ln_pallas(heads, w_out[l].astype(BF16), h32.reshape(b, s, d),
                                      ln1_g[l], ln1_b[l], tm=512)
        a = ffn_up_pallas(h16, ffn_w_up[l].astype(BF16), ffn_conv_w[l], ffn_conv_b[l],
                          tm=min(1024, s), tn=512)
        h32, h16 = matmul_ln_pallas(a.reshape(b * s, dff), ffn_w_down[l].astype(BF16), h32.reshape(b * s, d),
                                    ln2_g[l], ln2_b[l], tm=512, tk=dff // 4)
    return h32.reshape(b, s, d)
```

```python
import functools
import math

import jax
import jax.numpy as jnp
import ml_dtypes
import numpy as np
from jax import lax
from jax.experimental import pallas as pl
from jax.experimental.pallas import tpu as pltpu

F32 = jnp.float32
BF16 = jnp.bfloat16

HEAD_DIM = 64
MOBA_HEADS = 8
NSA_HEADS = 8
NSA_KV_HEADS = 2
NSA_GROUP = NSA_HEADS // NSA_KV_HEADS
FOX_HEADS = 12
MEM_HEADS = 4
MOBA_BLOCK = 256
MOBA_TOPK = 3
NSA_CMP_LEN = 32
NSA_CMP_STRIDE = 16
NSA_SEL_BLOCK = 64
NSA_TOPK = 16
NSA_WINDOW = 512
NSA_FORCE_SCORE = 1.0e4
CONV_WIDTH = 3
LN_EPS = 1e-5
DEPTH = 2
DEEPNORM_ALPHA = (2 * DEPTH) ** 0.25

LOG2E = math.log2(math.e)
Q_SCALE = HEAD_DIM ** -0.5 * LOG2E
NEG = -1.0e30
SKIP = -3.0e38
VMEM_LIMIT = 56 * 1024 * 1024
LANES = 128
KAUG = 2 * HEAD_DIM
VROWS = HEAD_DIM + 16
POS_PERIOD = 256

TQ_MOBA, TQ_NSA, TQ_FOX = 0, 8, 16
TV_MOBA, TV_NSA, TV_FOX = 28, 36, 40
T_SLOTS = 52
KA_MOBA, KA_NSA = 0, 8
KA_SLOTS = 12
N_CMP, N_MEMQ = 0, 4
N_SLOTS = 8


def _cparams(sem):
    return pltpu.CompilerParams(dimension_semantics=sem, vmem_limit_bytes=VMEM_LIMIT)


def _split2(x):
    hi = x.astype(BF16)
    return hi, (x - hi.astype(F32)).astype(BF16)


def _split3(x):
    hi = x.astype(BF16)
    r1 = x - hi.astype(F32)
    mid = r1.astype(BF16)
    lo = (r1 - mid.astype(F32)).astype(BF16)
    return hi, mid, lo


def _np_split3(x):
    x = np.asarray(x, np.float32)
    hi = x.astype(ml_dtypes.bfloat16).astype(np.float32)
    r1 = x - hi
    mid = r1.astype(ml_dtypes.bfloat16).astype(np.float32)
    lo = (r1 - mid).astype(ml_dtypes.bfloat16).astype(np.float32)
    return hi, mid, lo


def _dot_nt(a, b):
    return lax.dot_general(a, b, (((1,), (1,)), ((), ())), preferred_element_type=F32)


def _dot_tn(a, b):
    return lax.dot_general(a, b, (((0,), (0,)), ((), ())), preferred_element_type=F32)


def _dot(a, b):
    return jnp.dot(a, b, preferred_element_type=F32)


def _layer_norm_rows(x, g, b):
    mu = jnp.mean(x, axis=-1, keepdims=True)
    xc = x - mu
    var = jnp.mean(xc * xc, axis=-1, keepdims=True)
    return xc * lax.rsqrt(var + LN_EPS) * g + b


def _alibi_slopes2(n):
    return (np.exp2(-8.0 * np.arange(1, n + 1, dtype=np.float64) / n) * LOG2E).astype(np.float32)


def _slope_rows(slopes2, lanes):
    pieces = np.stack(_np_split3(slopes2), axis=1)
    rows = np.zeros((len(slopes2), HEAD_DIM, lanes), np.float32)
    rows[:, :3, :] = pieces[:, :, None]
    return jnp.asarray(rows, BF16)


def _ln_kernel(x_ref, g_ref, b_ref, o32_ref, o16_ref):
    y = _layer_norm_rows(x_ref[...], g_ref[...], b_ref[...])
    o32_ref[...] = y
    o16_ref[...] = y.astype(BF16)


def layer_norm_pallas(x, g, b, tm=512):
    m, d = x.shape
    return pl.pallas_call(
        _ln_kernel,
        grid=(m // tm,),
        in_specs=[pl.BlockSpec((tm, d), lambda i: (i, 0)),
                  pl.BlockSpec((1, d), lambda i: (0, 0)),
                  pl.BlockSpec((1, d), lambda i: (0, 0))],
        out_specs=[pl.BlockSpec((tm, d), lambda i: (i, 0)),
                   pl.BlockSpec((tm, d), lambda i: (i, 0))],
        out_shape=[jax.ShapeDtypeStruct((m, d), F32), jax.ShapeDtypeStruct((m, d), BF16)],
        compiler_params=_cparams(("parallel",)),
        name="layer_norm",
    )(x, g.reshape(1, d), b.reshape(1, d))


def _gates_kernel(h_ref, wg_ref, wf_ref, bf_ref, tri_ref, place_ref, g_ref, caug_ref, carry_ref):
    si = pl.program_id(1)

    @pl.when(si == 0)
    def _():
        carry_ref[...] = jnp.zeros_like(carry_ref)

    h_hi, h_lo = _split2(h_ref[0])
    wg_hi, wg_lo = _split2(wg_ref[...])
    g = _dot(h_hi, wg_hi) + _dot(h_lo, wg_hi) + _dot(h_hi, wg_lo)
    g_ref[0] = 1.0 / (1.0 + jnp.exp(-g))
    wf_hi, wf_lo = _split2(wf_ref[...])
    x = _dot(h_hi, wf_hi) + _dot(h_lo, wf_hi) + _dot(h_hi, wf_lo) + bf_ref[...]
    logf = jnp.minimum(x, 0.0) - jnp.log(1.0 + jnp.exp(-jnp.abs(x)))
    tri = tri_ref[...]
    l_hi, l_mid, l_lo = _split3(logf)
    c = _dot(tri, l_hi) + _dot(tri, l_mid) + _dot(tri, l_lo) + carry_ref[0:1, :]
    carry_ref[...] = jnp.broadcast_to(c[-1:, :], carry_ref.shape)
    n_hi, n_mid, n_lo = _split3(-LOG2E * c)
    caug = _dot(n_hi, place_ref[0]) + _dot(n_mid, place_ref[1]) + _dot(n_lo, place_ref[2])
    caug_ref[0] = caug.astype(BF16)


def gates_pallas(h3, w_gate, w_forget, b_forget, t=512):
    b, s, d = h3.shape
    gw = 3 * NSA_GROUP
    ng = NSA_KV_HEADS * LANES
    wg = jnp.zeros((d, ng), F32)
    for g in range(NSA_KV_HEADS):
        wg = wg.at[:, g * LANES:g * LANES + gw].set(w_gate[:, g * gw:(g + 1) * gw])
    wf = jnp.zeros((d, LANES), F32).at[:, :FOX_HEADS].set(w_forget)
    bf = jnp.zeros((1, LANES), F32).at[0, :FOX_HEADS].set(b_forget)
    tri = (np.arange(t)[None, :] <= np.arange(t)[:, None]).astype(np.float32)
    place = np.zeros((3, LANES, FOX_HEADS * KAUG), np.float32)
    for piece in range(3):
        for hh in range(FOX_HEADS):
            place[piece, hh, hh * KAUG + HEAD_DIM + piece] = 1.0
    nc = FOX_HEADS * KAUG
    return pl.pallas_call(
        _gates_kernel,
        grid=(b, s // t),
        in_specs=[pl.BlockSpec((1, t, d), lambda i, j: (i, j, 0)),
                  pl.BlockSpec((d, ng), lambda i, j: (0, 0)),
                  pl.BlockSpec((d, LANES), lambda i, j: (0, 0)),
                  pl.BlockSpec((1, LANES), lambda i, j: (0, 0)),
                  pl.BlockSpec((t, t), lambda i, j: (0, 0)),
                  pl.BlockSpec((3, LANES, nc), lambda i, j: (0, 0, 0))],
        out_specs=[pl.BlockSpec((1, t, ng), lambda i, j: (i, j, 0)),
                   pl.BlockSpec((1, t, nc), lambda i, j: (i, j, 0))],
        out_shape=[jax.ShapeDtypeStruct((b, s, ng), F32),
                   jax.ShapeDtypeStruct((b, s, nc), BF16)],
        scratch_shapes=[pltpu.VMEM((8, LANES), F32)],
        compiler_params=_cparams(("parallel", "arbitrary")),
        name="gates_cumsum",
    )(h3, wg, wf, bf, jnp.asarray(tri, BF16), jnp.asarray(place, BF16))


def _proj_heads_kernel(x_ref, w_ref, sc_ref, o_ref, *, heads_per_step):
    acc = _dot(x_ref[0], w_ref[...]) * sc_ref[...]
    for j in range(heads_per_step):
        o_ref[0, j] = acc[:, j * HEAD_DIM:(j + 1) * HEAD_DIM].astype(o_ref.dtype)


def proj_heads_pallas(x3, w, colscale, tm, heads_per_step):
    b, s, d = x3.shape
    n = w.shape[1]
    tn = heads_per_step * HEAD_DIM
    return pl.pallas_call(
        functools.partial(_proj_heads_kernel, heads_per_step=heads_per_step),
        grid=(b, s // tm, n // tn),
        in_specs=[pl.BlockSpec((1, tm, d), lambda i, j, k: (i, j, 0)),
                  pl.BlockSpec((d, tn), lambda i, j, k: (0, k)),
                  pl.BlockSpec((1, tn), lambda i, j, k: (0, k))],
        out_specs=pl.BlockSpec((1, heads_per_step, tm, HEAD_DIM), lambda i, j, k: (i, k, j, 0)),
        out_shape=jax.ShapeDtypeStruct((b, n // HEAD_DIM, s, HEAD_DIM), BF16),
        compiler_params=_cparams(("parallel", "parallel", "arbitrary")),
        name="proj_heads",
    )(x3, w, colscale.reshape(1, n))


def _proj_t_kernel(x_ref, wt_ref, sc_ref, o_ref, *, heads_per_step):
    acc = _dot_nt(wt_ref[...], x_ref[0]) * sc_ref[...]
    o_ref[0] = acc.reshape(heads_per_step, HEAD_DIM, acc.shape[1]).astype(o_ref.dtype)


def proj_t_pallas(x3, wt, rowscale, tm, heads_per_step):
    b, s, d = x3.shape
    n = wt.shape[0]
    tn = heads_per_step * HEAD_DIM
    return pl.pallas_call(
        functools.partial(_proj_t_kernel, heads_per_step=heads_per_step),
        grid=(b, s // tm, n // tn),
        in_specs=[pl.BlockSpec((1, tm, d), lambda i, j, k: (i, j, 0)),
                  pl.BlockSpec((tn, d), lambda i, j, k: (k, 0)),
                  pl.BlockSpec((tn, 1), lambda i, j, k: (k, 0))],
        out_specs=pl.BlockSpec((1, heads_per_step, HEAD_DIM, tm), lambda i, j, k: (i, k, 0, j)),
        out_shape=jax.ShapeDtypeStruct((b, n // HEAD_DIM, HEAD_DIM, s), BF16),
        compiler_params=_cparams(("parallel", "parallel", "arbitrary")),
        name="proj_transposed",
    )(x3, wt, rowscale.reshape(n, 1))


def _proj_kaug_kernel(x_ref, w_ref, *rest, heads_per_step, positional):
    o_ref = rest[-1]
    acc = _dot(x_ref[0], w_ref[...])
    tm = acc.shape[0]
    if positional:
        pos = (pl.program_id(1) * tm + lax.broadcasted_iota(jnp.int32, acc.shape, 0)) % POS_PERIOD
        lane = lax.broadcasted_iota(jnp.int32, acc.shape, 1) % KAUG
        acc = acc + jnp.where((lane >= HEAD_DIM) & (lane < HEAD_DIM + 3), pos.astype(F32), 0.0)
    else:
        acc = acc + rest[0][0].astype(F32)
    for j in range(heads_per_step):
        o_ref[0, j] = acc[:, j * KAUG:(j + 1) * KAUG].astype(o_ref.dtype)


def proj_kaug_pallas(x3, w, aug, tm, heads_per_step):
    b, s, d = x3.shape
    n = w.shape[1]
    tn = heads_per_step * KAUG
    in_specs = [pl.BlockSpec((1, tm, d), lambda i, j, k: (i, j, 0)),
                pl.BlockSpec((d, tn), lambda i, j, k: (0, k))]
    args = [x3, w]
    if aug is not None:
        in_specs.append(pl.BlockSpec((1, tm, tn), lambda i, j, k: (i, j, k)))
        args.append(aug)
    return pl.pallas_call(
        functools.partial(_proj_kaug_kernel, heads_per_step=heads_per_step, positional=aug is None),
        grid=(b, s // tm, n // tn),
        in_specs=in_specs,
        out_specs=pl.BlockSpec((1, heads_per_step, tm, KAUG), lambda i, j, k: (i, k, j, 0)),
        out_shape=jax.ShapeDtypeStruct((b, n // KAUG, s, KAUG), BF16),
        compiler_params=_cparams(("parallel", "parallel", "arbitrary")),
        name="proj_keys_aug",
    )(*args)


def _matmul_ln_kernel(x_ref, w_ref, r_ref, g_ref, b_ref, o32_ref, o16_ref, acc_ref):
    k = pl.program_id(1)

    @pl.when(k == 0)
    def _():
        acc_ref[...] = jnp.zeros_like(acc_ref)

    acc_ref[...] += _dot(x_ref[...], w_ref[...])

    @pl.when(k == pl.num_programs(1) - 1)
    def _():
        y = _layer_norm_rows(DEEPNORM_ALPHA * r_ref[...] + acc_ref[...], g_ref[...], b_ref[...])
        o32_ref[...] = y
        o16_ref[...] = y.astype(BF16)


def matmul_ln_pallas(x, w, res, g, b, tm, tk):
    m, kk = x.shape
    d = w.shape[1]
    return pl.pallas_call(
        _matmul_ln_kernel,
        grid=(m // tm, kk // tk),
        in_specs=[pl.BlockSpec((tm, tk), lambda i, k: (i, k)),
                  pl.BlockSpec((tk, d), lambda i, k: (k, 0)),
                  pl.BlockSpec((tm, d), lambda i, k: (i, 0)),
                  pl.BlockSpec((1, d), lambda i, k: (0, 0)),
                  pl.BlockSpec((1, d), lambda i, k: (0, 0))],
        out_specs=[pl.BlockSpec((tm, d), lambda i, k: (i, 0)),
                   pl.BlockSpec((tm, d), lambda i, k: (i, 0))],
        out_shape=[jax.ShapeDtypeStruct((m, d), F32), jax.ShapeDtypeStruct((m, d), BF16)],
        scratch_shapes=[pltpu.VMEM((tm, d), F32)],
        compiler_params=_cparams(("parallel", "arbitrary")),
        name="matmul_ln",
    )(x, w, res, g.reshape(1, d), b.reshape(1, d))


def _matmul_ln_resident_kernel(x_ref, w_ref, r_ref, g_ref, b_ref, o32_ref, o16_ref):
    y = _layer_norm_rows(DEEPNORM_ALPHA * r_ref[...] + _dot(x_ref[...], w_ref[...]), g_ref[...], b_ref[...])
    o32_ref[...] = y
    o16_ref[...] = y.astype(BF16)


def matmul_ln_resident_pallas(x, w, res, g, b, tm):
    m, kk = x.shape
    d = w.shape[1]
    return pl.pallas_call(
        _matmul_ln_resident_kernel,
        grid=(m // tm,),
        in_specs=[pl.BlockSpec((tm, kk), lambda i: (i, 0)),
                  pl.BlockSpec((kk, d), lambda i: (0, 0), pipeline_mode=pl.Buffered(1)),
                  pl.BlockSpec((tm, d), lambda i: (i, 0)),
                  pl.BlockSpec((1, d), lambda i: (0, 0)),
                  pl.BlockSpec((1, d), lambda i: (0, 0))],
        out_specs=[pl.BlockSpec((tm, d), lambda i: (i, 0)),
                   pl.BlockSpec((tm, d), lambda i: (i, 0))],
        out_shape=[jax.ShapeDtypeStruct((m, d), F32), jax.ShapeDtypeStruct((m, d), BF16)],
        compiler_params=_cparams(("parallel",)),
        name="matmul_ln_resident",
    )(x, w, res, g.reshape(1, d), b.reshape(1, d))


def _out_proj_ln_kernel(*refs, widths):
    n = len(widths)
    x_refs, (w_ref, r_ref, g_ref, b_ref, o32_ref, o16_ref) = refs[:n], refs[n:]
    acc = None
    off = 0
    for x_ref, width in zip(x_refs, widths):
        part = _dot_tn(x_ref[0], w_ref[off:off + width, :])
        acc = part if acc is None else acc + part
        off += width
    y = _layer_norm_rows(DEEPNORM_ALPHA * r_ref[0] + acc, g_ref[...], b_ref[...])
    o32_ref[0] = y
    o16_ref[0] = y.astype(BF16)


def out_proj_ln_pallas(xts, w, res3, g, b, tm):
    bsz, s, d = res3.shape
    widths = tuple(x.shape[1] for x in xts)
    in_specs = [pl.BlockSpec((1, wd, tm), lambda i, j: (i, 0, j)) for wd in widths]
    in_specs += [pl.BlockSpec((w.shape[0], d), lambda i, j: (0, 0)),
                 pl.BlockSpec((1, tm, d), lambda i, j: (i, j, 0)),
                 pl.BlockSpec((1, d), lambda i, j: (0, 0)),
                 pl.BlockSpec((1, d), lambda i, j: (0, 0))]
    return pl.pallas_call(
        functools.partial(_out_proj_ln_kernel, widths=widths),
        grid=(bsz, s // tm),
        in_specs=in_specs,
        out_specs=[pl.BlockSpec((1, tm, d), lambda i, j: (i, j, 0)),
                   pl.BlockSpec((1, tm, d), lambda i, j: (i, j, 0))],
        out_shape=[jax.ShapeDtypeStruct((bsz, s, d), F32), jax.ShapeDtypeStruct((bsz, s, d), BF16)],
        compiler_params=_cparams(("parallel", "parallel")),
        name="out_proj_ln",
    )(*xts, w, res3, g.reshape(1, d), b.reshape(1, d))


HALO = 16
FFN_CHUNK = 256


def _gelu_tanh(x):
    return 0.5 * x * (1.0 + jnp.tanh(math.sqrt(2.0 / math.pi) * (x + 0.044715 * x * x * x)))


def _ffn_up_kernel(x_ref, xh_ref, wu_ref, wg_ref, cw_ref, cb_ref, o_ref):
    j = pl.program_id(1)
    x = x_ref[0]
    xh = xh_ref[0]
    first = jnp.where(j > 0, 1.0, 0.0)
    tn = o_ref.shape[2]
    row = lax.broadcasted_iota(jnp.int32, (x.shape[0], FFN_CHUNK), 0)
    for c in range(tn // FFN_CHUNK):
        cols = slice(c * FFN_CHUNK, (c + 1) * FFN_CHUNK)
        u = _dot(x, wu_ref[:, cols])
        g = _dot(x, wg_ref[:, cols])
        gh = _dot(xh, wg_ref[:, cols]) * first
        prev1 = gh[HALO - 1:HALO, :]
        prev2 = gh[HALO - 2:HALO - 1, :]
        g_m1 = jnp.where(row == 0, prev1, pltpu.roll(g, 1, axis=0))
        g_m2 = jnp.where(row == 0, prev2, jnp.where(row == 1, prev1, pltpu.roll(g, 2, axis=0)))
        cw = cw_ref[:, cols]
        gc = cb_ref[:, cols] + cw[0:1] * g_m2 + cw[1:2] * g_m1 + cw[2:3] * g
        o_ref[0, :, cols] = (_gelu_tanh(gc) * u).astype(o_ref.dtype)


def ffn_up_pallas(x3, w_up, conv_w, conv_b, tm, tn):
    b, s, d = x3.shape
    dff = w_up.shape[1] // 2
    nt = dff // tn
    hb = tm // HALO
    cw = jnp.zeros((8, dff), F32).at[:CONV_WIDTH].set(conv_w)
    return pl.pallas_call(
        _ffn_up_kernel,
        grid=(b, s // tm, nt),
        in_specs=[pl.BlockSpec((1, tm, d), lambda i, j, k: (i, j, 0)),
                  pl.BlockSpec((1, HALO, d), lambda i, j, k: (i, jnp.maximum(j * hb - 1, 0), 0)),
                  pl.BlockSpec((d, tn), lambda i, j, k: (0, k)),
                  pl.BlockSpec((d, tn), lambda i, j, k: (0, k + nt)),
                  pl.BlockSpec((8, tn), lambda i, j, k: (0, k)),
                  pl.BlockSpec((1, tn), lambda i, j, k: (0, k))],
        out_specs=pl.BlockSpec((1, tm, tn), lambda i, j, k: (i, j, k)),
        out_shape=jax.ShapeDtypeStruct((b, s, dff), BF16),
        compiler_params=_cparams(("parallel", "parallel", "arbitrary")),
        name="ffn_up",
    )(x3, x3, w_up, w_up, cw, conv_b.reshape(1, dff))


def _online_step_t(st, delta, vaug, m_ref, acc_ref):
    m_prev = m_ref[...]
    m_cur = jnp.max(st, axis=0, keepdims=True) + delta
    m_new = jnp.maximum(m_prev, m_cur)
    alpha = jnp.exp2(m_prev - m_new)
    pt = jnp.exp2(st - (m_new - delta)).astype(BF16)
    acc_ref[...] = alpha * acc_ref[...] + _dot(vaug, pt)
    m_ref[...] = m_new


def _init_state_t(m_ref, acc_ref):
    m_ref[...] = jnp.full(m_ref.shape, NEG, F32)
    acc_ref[...] = jnp.zeros(acc_ref.shape, F32)


def _finish_t(acc_ref):
    acc = acc_ref[...]
    return acc[:HEAD_DIM] / acc[HEAD_DIM:HEAD_DIM + 1]


def _fill_vaug(vaug_ref, vt_ref):
    s = vaug_ref.shape[1]
    vaug_ref[0:HEAD_DIM, :] = vt_ref[0, 0]
    pad = lax.broadcasted_iota(jnp.int32, (VROWS - HEAD_DIM, s), 0)
    vaug_ref[HEAD_DIM:VROWS, :] = jnp.where(pad == 0, 1.0, 0.0).astype(BF16)


def _ktile(ref, idx, size):
    return ref[0, 0, pl.ds(pl.multiple_of(idx * size, size), size), :]


def _vtile(ref, idx, size):
    return ref[:, pl.ds(pl.multiple_of(idx * size, size), size)]


def _stage_scores(st, s_ref, mc_ref):
    s_ref[...] = st
    mc_ref[...] = jnp.max(st, axis=0, keepdims=True)


def _stage_update(s_ref, mc_ref, delta, vaug, m_ref, acc_ref):
    m_prev = m_ref[...]
    m_new = jnp.maximum(m_prev, mc_ref[...] + delta)
    alpha = jnp.exp2(m_prev - m_new)
    pt = jnp.exp2(s_ref[...] - (m_new - delta)).astype(BF16)
    acc_ref[...] = alpha * acc_ref[...] + _dot(vaug, pt)
    m_ref[...] = m_new


def _score_buffers(tk, lanes):
    return [pltpu.VMEM((tk, lanes), F32), pltpu.VMEM((1, lanes), F32),
            pltpu.VMEM((tk, lanes), F32), pltpu.VMEM((1, lanes), F32)]


def _flash_pipeline(lead, n_loop, scores, meta, bufs, m_ref, acc_ref, n_even):
    _stage_scores(lead[0][0](), *bufs[0])
    for i in range(1, len(lead)):
        _stage_scores(lead[i][0](), *bufs[i % 2])
        _stage_update(*bufs[(i - 1) % 2], *lead[i - 1][1](), m_ref, acc_ref)
    cur = (len(lead) - 1) % 2
    nxt = 1 - cur
    if scores is None:
        _stage_update(*bufs[cur], *lead[-1][1](), m_ref, acc_ref)
        return
    _stage_scores(scores(0), *bufs[nxt])
    _stage_update(*bufs[cur], *lead[-1][1](), m_ref, acc_ref)
    last = jnp.maximum(n_loop - 1, 0)

    def body(kp, carry):
        k0 = 2 * kp
        k1 = k0 + 1
        _stage_scores(scores(jnp.minimum(k1, last)), *bufs[cur])
        _stage_update(*bufs[nxt], *meta(k0), m_ref, acc_ref)
        _stage_scores(scores(jnp.minimum(k0 + 2, last)), *bufs[nxt])
        delta1, vaug1 = meta(jnp.minimum(k1, last))
        if not n_even:
            delta1 = jnp.where(k1 < n_loop, delta1, SKIP)
        _stage_update(*bufs[cur], delta1, vaug1, m_ref, acc_ref)
        return carry

    lax.fori_loop(0, (n_loop + 1) // 2, body, 0)


FOX_TQ = 1024
FOX_TK = 256


def _fox_kernel(qt_ref, k_ref, vt_ref, o_ref, m_ref, acc_ref, vaug_ref, sa_ref, ma_ref, sb_ref, mb_ref):
    tq, tk = o_ref.shape[3], FOX_TK
    nd = tq // tk
    qi = pl.program_id(2)

    @pl.when(qi == 0)
    def _():
        _fill_vaug(vaug_ref, vt_ref)

    ones3 = jnp.where(lax.broadcasted_iota(jnp.int32, (HEAD_DIM, tq), 0) < 3, 1.0, 0.0).astype(BF16)
    qaug = jnp.concatenate([qt_ref[0, 0], ones3], axis=0)
    _init_state_t(m_ref, acc_ref)
    key = lax.broadcasted_iota(jnp.int32, (tk, tq), 0)
    qry = lax.broadcasted_iota(jnp.int32, (tk, tq), 1)

    def scores(kj):
        return _dot(_ktile(k_ref, kj, tk), qaug)

    def meta(kj):
        return 0.0, _vtile(vaug_ref, kj, tk)

    lead = [(functools.partial(lambda d: jnp.where(key + d * tk <= qry, scores(qi * nd + d), NEG), d),
             functools.partial(lambda d: meta(qi * nd + d), d)) for d in range(nd)]
    _flash_pipeline(lead, qi * nd, scores, meta, ((sa_ref, ma_ref), (sb_ref, mb_ref)), m_ref, acc_ref,
                    n_even=nd % 2 == 0)
    o_ref[0, 0] = _finish_t(acc_ref).astype(o_ref.dtype)


def fox_pallas(qvt, kaug):
    b, _, dh, s = qvt.shape
    tq = min(FOX_TQ, s)
    assert tq % FOX_TK == 0
    out = pl.pallas_call(
        _fox_kernel,
        grid=(b, FOX_HEADS, s // tq),
        in_specs=[pl.BlockSpec((1, 1, dh, tq), lambda i, h, j: (i, TQ_FOX + h, 0, j)),
                  pl.BlockSpec((1, 1, s, KAUG), lambda i, h, j: (i, h, 0, 0)),
                  pl.BlockSpec((1, 1, dh, s), lambda i, h, j: (i, TV_FOX + h, 0, 0))],
        out_specs=pl.BlockSpec((1, 1, dh, tq), lambda i, h, j: (i, h, 0, j)),
        out_shape=jax.ShapeDtypeStruct((b, FOX_HEADS, dh, s), BF16),
        scratch_shapes=[pltpu.VMEM((1, tq), F32), pltpu.VMEM((VROWS, tq), F32),
                        pltpu.VMEM((VROWS, s), BF16)] + _score_buffers(FOX_TK, tq),
        compiler_params=_cparams(("parallel", "parallel", "arbitrary")),
        name="fox_attention",
    )(qvt, kaug, qvt)
    return out.reshape(b, FOX_HEADS * dh, s)


def _moba_kernel(slope_ref, qt_ref, k_ref, vt_ref, ind_ref, srow_ref, o_ref,
                 m_ref, acc_ref, vaug_ref, km_ref, sel_ref, sa_ref, ma_ref, sb_ref, mb_ref):
    t = MOBA_BLOCK
    tq = o_ref.shape[3]
    nd = tq // t
    h = pl.program_id(1)
    qi = pl.program_id(2)
    slope2 = slope_ref[h]

    @pl.when(qi == 0)
    def _():
        _fill_vaug(vaug_ref, vt_ref)
        km_ref[...] = _dot(ind_ref[...], k_ref[0, 0]) * (1.0 / MOBA_BLOCK)

    qt = qt_ref[0, 0]
    km_hi, km_lo = _split2(km_ref[...])
    q0 = jnp.concatenate([qt, jnp.zeros_like(qt)], axis=0)
    gate = _dot(km_hi, q0) + _dot(km_lo, q0)
    blk = lax.broadcasted_iota(jnp.int32, gate.shape, 0)
    lane_blk = lax.broadcasted_iota(jnp.int32, (1, tq), 1) // t
    valid = blk < qi * nd + lane_blk
    work = jnp.where(valid, gate, -jnp.inf)
    sel = jnp.zeros(gate.shape, F32)
    big = jnp.int32(2 ** 30)
    for _ in range(MOBA_TOPK):
        mx = jnp.max(work, axis=0, keepdims=True)
        first = jnp.min(jnp.where(work == mx, blk, big), axis=0, keepdims=True)
        hit = blk == first
        sel = jnp.where(hit, 1.0, sel)
        work = jnp.where(hit, -jnp.inf, work)
    sel_ref[...] = jnp.where(valid, sel, 0.0)

    qaug = jnp.concatenate([qt, srow_ref[...]], axis=0)
    _init_state_t(m_ref, acc_ref)
    key = lax.broadcasted_iota(jnp.int32, (t, tq), 0)
    qry = lax.broadcasted_iota(jnp.int32, (t, tq), 1)

    def tile_constant(kj, d):
        sees = sel_ref[pl.ds(kj, 1), :] > 0.0
        if d is not None:
            sees = sees | (lane_blk == d)
        return jnp.where(sees, slope2 * ((kj - qi * nd) * t).astype(F32), SKIP)

    def scores(kj):
        return _dot(_ktile(k_ref, kj, t), qaug)

    def meta(kj):
        return tile_constant(kj, None), _vtile(vaug_ref, kj, t)

    lead = [(functools.partial(lambda d: jnp.where(key + d * t <= qry, scores(qi * nd + d), NEG), d),
             functools.partial(lambda d: (tile_constant(qi * nd + d, d), _vtile(vaug_ref, qi * nd + d, t)), d))
            for d in range(nd)]
    _flash_pipeline(lead, qi * nd, scores, meta, ((sa_ref, ma_ref), (sb_ref, mb_ref)), m_ref, acc_ref,
                    n_even=nd % 2 == 0)
    o_ref[0, 0] = _finish_t(acc_ref).astype(o_ref.dtype)


MOBA_TQ = 1024


def moba_pallas(qvt, kaug):
    b, _, dh, s = qvt.shape
    t = MOBA_BLOCK
    tq = min(MOBA_TQ, s)
    assert POS_PERIOD == t and tq % t == 0
    nb = s // t
    nbp = max(LANES, nb)
    ind = np.zeros((nbp, s), np.float32)
    ind[np.arange(s) // t, np.arange(s)] = 1.0
    slopes2 = _alibi_slopes2(MOBA_HEADS)
    grid_spec = pltpu.PrefetchScalarGridSpec(
        num_scalar_prefetch=1,
        grid=(b, MOBA_HEADS, s // tq),
        in_specs=[pl.BlockSpec((1, 1, dh, tq), lambda i, h, j, sl: (i, TQ_MOBA + h, 0, j)),
                  pl.BlockSpec((1, 1, s, KAUG), lambda i, h, j, sl: (i, KA_MOBA + h, 0, 0)),
                  pl.BlockSpec((1, 1, dh, s), lambda i, h, j, sl: (i, TV_MOBA + h, 0, 0)),
                  pl.BlockSpec((nbp, s), lambda i, h, j, sl: (0, 0)),
                  pl.BlockSpec((None, dh, tq), lambda i, h, j, sl: (h, 0, 0))],
        out_specs=pl.BlockSpec((1, 1, dh, tq), lambda i, h, j, sl: (i, h, 0, j)),
        scratch_shapes=[pltpu.VMEM((1, tq), F32), pltpu.VMEM((VROWS, tq), F32),
                        pltpu.VMEM((VROWS, s), BF16), pltpu.VMEM((nbp, KAUG), F32),
                        pltpu.VMEM((nbp, tq), F32)] + _score_buffers(t, tq))
    out = pl.pallas_call(
        _moba_kernel,
        grid_spec=grid_spec,
        out_shape=jax.ShapeDtypeStruct((b, MOBA_HEADS, dh, s), BF16),
        compiler_params=_cparams(("parallel", "parallel", "arbitrary")),
        name="moba_attention",
    )(jnp.asarray(slopes2), qvt, kaug, qvt, jnp.asarray(ind, BF16), _slope_rows(slopes2, tq))
    return out.reshape(b, MOBA_HEADS * dh, s)


def _nsa_compress_kernel(x_ref, w1a_ref, w1b_ref, pe_ref, w1_ref, w2_ref, w2t_ref, o_ref, ot_ref):
    nr = x_ref.shape[2]
    x = x_ref[0, 0]

    def near_f32(xb, w):
        w_hi, w_lo = _split2(w)
        return _dot(xb, w_hi) + _dot(xb, w_lo)

    a = near_f32(x, w1a_ref[0])
    bm = near_f32(x, w1b_ref[0])
    pe_hi, pe_mid, pe_lo = _split3(pe_ref[0])
    w1_hi, w1_lo = _split2(w1_ref[0])
    pe_term = (_dot(pe_hi, w1_hi) + _dot(pe_mid, w1_hi) + _dot(pe_lo, w1_hi)
               + _dot(pe_hi, w1_lo) + _dot(pe_mid, w1_lo))[0:1]
    pre = a + pltpu.roll(bm, nr - 1, axis=0) + pe_term
    hid = _gelu_tanh(pre)
    h_hi, h_mid, h_lo = _split3(hid)
    w2_hi, w2_lo = _split2(w2_ref[0])
    o_ref[0, 0] = (_dot(h_hi, w2_hi) + _dot(h_mid, w2_hi) + _dot(h_lo, w2_hi)
                   + _dot(h_hi, w2_lo) + _dot(h_mid, w2_lo))
    t_hi, t_lo = _split2(w2t_ref[0])
    ot_ref[0, 0] = (_dot_nt(t_hi, h_hi) + _dot_nt(t_hi, h_mid) + _dot_nt(t_hi, h_lo)
                    + _dot_nt(t_lo, h_hi) + _dot_nt(t_lo, h_mid))


def nsa_compress_pallas(nat, pe, w1, w2):
    b, _, s, dh = nat.shape
    nr = s // NSA_CMP_STRIDE
    half = NSA_CMP_STRIDE * dh
    hid = w1.shape[-1]
    x = nat[:, N_CMP:N_CMP + 4].reshape(b, 4, nr, half)
    w1f = w1.reshape(2, NSA_CMP_LEN * dh, hid)
    pef = jnp.zeros((2, 8, NSA_CMP_LEN * dh), F32).at[:, 0].set(pe.reshape(2, NSA_CMP_LEN * dh))
    return pl.pallas_call(
        _nsa_compress_kernel,
        grid=(b, 4),
        in_specs=[pl.BlockSpec((1, 1, nr, half), lambda i, j: (i, j, 0, 0)),
                  pl.BlockSpec((1, half, hid), lambda i, j: (j // 2, 0, 0)),
                  pl.BlockSpec((1, half, hid), lambda i, j: (j // 2, 1, 0)),
                  pl.BlockSpec((1, 8, 2 * half), lambda i, j: (j // 2, 0, 0)),
                  pl.BlockSpec((1, 2 * half, hid), lambda i, j: (j // 2, 0, 0)),
                  pl.BlockSpec((1, hid, dh), lambda i, j: (j // 2, 0, 0)),
                  pl.BlockSpec((1, dh, hid), lambda i, j: (j // 2, 0, 0))],
        out_specs=[pl.BlockSpec((1, 1, nr, dh), lambda i, j: (i, j, 0, 0)),
                   pl.BlockSpec((1, 1, dh, nr), lambda i, j: (i, j, 0, 0))],
        out_shape=[jax.ShapeDtypeStruct((b, 4, nr, dh), F32),
                   jax.ShapeDtypeStruct((b, 4, dh, nr), F32)],
        compiler_params=_cparams(("parallel", "parallel")),
        name="nsa_compress",
    )(x, w1f, w1f, pef, w1f, w2, jnp.swapaxes(w2, 1, 2))


NSA_TQ = 128


NSA_CMP_CHUNK = 256


def _nsa_cmp_kernel(slope_ref, qt_ref, kc_ref, vct_ref, mimpt_ref, oct_ref, selt_ref, cnt_ref, imp_ref):
    tq = NSA_TQ
    g = pl.program_id(1)
    qi = pl.program_id(2)
    q0 = qi * tq
    nr = kc_ref.shape[2]
    nsb = selt_ref.shape[2]
    t_lane = q0 + lax.broadcasted_iota(jnp.int32, (1, tq), 1)
    chunk = min(NSA_CMP_CHUNK, nr)
    tiles_per_chunk = chunk * NSA_CMP_STRIDE // tq

    def branch(n):
        kc_hi, kc_lo = _split2(kc_ref[0, 0, 0:n, :])
        vct = vct_ref[0, 0, :, 0:n].astype(BF16)
        cmp_end = NSA_CMP_STRIDE * lax.broadcasted_iota(jnp.int32, (n, tq), 0) + (NSA_CMP_LEN - 1)
        mask = cmp_end <= t_lane
        rel = (cmp_end - q0).astype(F32)
        psum = jnp.zeros((n, tq), F32)
        qt4 = jnp.concatenate([qt_ref[0, hh] for hh in range(NSA_GROUP)], axis=1)
        st4 = _dot(kc_hi, qt4) + _dot(kc_lo, qt4)
        probs = []
        for hh in range(NSA_GROUP):
            st = st4[:, hh * tq:(hh + 1) * tq] + slope_ref[g * NSA_GROUP + hh] * rel
            st = jnp.where(mask, st, -jnp.inf)
            m = jnp.max(st, axis=0, keepdims=True)
            m = jnp.where(m > -jnp.inf, m, 0.0)
            e = jnp.exp2(st - m)
            p = e * (1.0 / jnp.maximum(jnp.sum(e, axis=0, keepdims=True), 1e-30))
            probs.append(p.astype(BF16))
            psum = psum + p
        o4 = _dot(vct, jnp.concatenate(probs, axis=1))
        for hh in range(NSA_GROUP):
            oct_ref[0, hh] = o4[:, hh * tq:(hh + 1) * tq]
        p_hi, p_mid, p_lo = _split3(psum)
        mimpt = mimpt_ref[:, 0:n]
        imp_ref[...] = _dot(mimpt, p_hi) + _dot(mimpt, p_mid) + _dot(mimpt, p_lo)

    n_chunks = nr // chunk
    for c in range(n_chunks):
        pl.when(jnp.minimum(qi // tiles_per_chunk, n_chunks - 1) == c)(
            functools.partial(branch, (c + 1) * chunk))

    imp = imp_ref[...]
    blk = lax.broadcasted_iota(jnp.int32, (nsb, tq), 0)
    jt = t_lane // NSA_SEL_BLOCK
    forced = (blk == 0) | (blk == jt) | (blk == jt - 1)
    imp = jnp.where(forced, NSA_FORCE_SCORE, imp)
    valid = blk * NSA_SEL_BLOCK <= t_lane
    work0 = jnp.where(valid, imp, -jnp.inf)
    big = jnp.int32(2 ** 30)

    def pick(_, work):
        mx = jnp.max(work, axis=0, keepdims=True)
        first = jnp.min(jnp.where(work == mx, blk, big), axis=0, keepdims=True)
        return jnp.where(blk == first, -jnp.inf, work)

    work = lax.fori_loop(0, min(NSA_TOPK, nsb), pick, work0)
    sel = jnp.where(valid & (work == -jnp.inf), 1.0, 0.0)
    selt_ref[0, 0] = sel
    cnt_ref[0, 0, 0] = _dot_nt(jnp.ones((8, tq), BF16), sel.astype(BF16))


def nsa_cmp_pallas(qvt, cmp_kv, cmp_kvt):
    b, _, dh, s = qvt.shape
    tq = NSA_TQ
    nr = cmp_kv.shape[2]
    nsb = s // NSA_SEL_BLOCK
    ratio = NSA_SEL_BLOCK // NSA_CMP_STRIDE
    front = NSA_CMP_LEN // NSA_CMP_STRIDE - 1
    n_int = ratio + front
    n_idx = np.arange(nr)[None, :]
    j_idx = np.arange(nsb)[:, None]
    mimpt = ((n_idx >= ratio * j_idx - front) & (n_idx <= ratio * j_idx + n_int - 1 - front)
             & (n_idx < nr - 1)).astype(np.float32)
    grid_spec = pltpu.PrefetchScalarGridSpec(
        num_scalar_prefetch=1,
        grid=(b, NSA_KV_HEADS, s // tq),
        in_specs=[pl.BlockSpec((1, NSA_GROUP, dh, tq), lambda i, g, j, sl: (i, TQ_NSA // NSA_GROUP + g, 0, j)),
                  pl.BlockSpec((1, 1, nr, dh), lambda i, g, j, sl: (i, g, 0, 0)),
                  pl.BlockSpec((1, 1, dh, nr), lambda i, g, j, sl: (i, 2 + g, 0, 0)),
                  pl.BlockSpec((nsb, nr), lambda i, g, j, sl: (0, 0))],
        out_specs=[pl.BlockSpec((1, NSA_GROUP, dh, tq), lambda i, g, j, sl: (i, g, 0, j)),
                   pl.BlockSpec((1, 1, nsb, tq), lambda i, g, j, sl: (i, g, 0, j)),
                   pl.BlockSpec((1, 1, 1, 8, nsb), lambda i, g, j, sl: (i, g, j, 0, 0))],
        scratch_shapes=[pltpu.VMEM((nsb, tq), F32)])
    assert nr % min(NSA_CMP_CHUNK, nr) == 0
    return pl.pallas_call(
        _nsa_cmp_kernel,
        grid_spec=grid_spec,
        out_shape=[jax.ShapeDtypeStruct((b, NSA_HEADS, dh, s), F32),
                   jax.ShapeDtypeStruct((b, NSA_KV_HEADS, nsb, s), F32),
                   jax.ShapeDtypeStruct((b, NSA_KV_HEADS, s // tq, 8, nsb), F32)],
        compiler_params=_cparams(("parallel", "parallel", "parallel")),
        name="nsa_compressed_select",
    )(jnp.asarray(_alibi_slopes2(NSA_HEADS)), qvt, cmp_kv, cmp_kvt, jnp.asarray(mimpt, BF16))


NSA_TK = 256
NSA_SEL_TQ = 256
NSA_LANES = NSA_GROUP * NSA_TQ


def _nsa_qaug(qt_ref, srow_ref):
    return jnp.concatenate(
        [jnp.concatenate([qt_ref[0, hh], srow_ref[hh]], axis=0) for hh in range(NSA_GROUP)], axis=1)


def _nsa_sel_kernel(bits_ref, qt_ref, k_ref, vt_ref, selt_ref, srow_ref, slane_ref, o_ref,
                    m_ref, acc_ref, vaug_ref, sa_ref, ma_ref, sb_ref, mb_ref, list_ref, *, nq, words):
    tq, tk = NSA_SEL_TQ, NSA_TK
    per_tile = tk // NSA_SEL_BLOCK
    qi = pl.program_id(2)
    q0 = qi * tq

    @pl.when(qi == 0)
    def _():
        _fill_vaug(vaug_ref, vt_ref)

    base = ((pl.program_id(0) * NSA_KV_HEADS + pl.program_id(1)) * nq + qi) * words
    list_ref[0] = 0

    def note(j, n):
        list_ref[n] = j
        return n + ((bits_ref[base + j // 32] >> (j % 32)) & 1)

    n_tiles = lax.fori_loop(0, qi, note, 0)

    qaug = _nsa_qaug(qt_ref, srow_ref)
    slane = slane_ref[0:1, :]
    diag = q0 // tk

    def scores(kj):
        st = _dot(_ktile(k_ref, kj, tk), qaug)
        rows = [jnp.broadcast_to(selt_ref[0, 0, pl.ds(kj * per_tile + c, 1), :], (NSA_SEL_BLOCK, tq))
                for c in range(per_tile)]
        bias = (jnp.concatenate(rows, axis=0) - 1.0) * (-NEG)
        return st + jnp.concatenate([bias] * NSA_GROUP, axis=1)

    def meta(kj):
        return slane * (kj * tk - q0).astype(F32), _vtile(vaug_ref, kj, tk)

    def own_tile():
        key = lax.broadcasted_iota(jnp.int32, (tk, tq), 0)
        qry = lax.broadcasted_iota(jnp.int32, (tk, tq), 1)
        causal = jnp.concatenate([key <= qry] * NSA_GROUP, axis=1)
        return jnp.where(causal, scores(diag), NEG)

    _init_state_t(m_ref, acc_ref)
    _flash_pipeline([(own_tile, lambda: meta(diag))], n_tiles,
                    lambda i: scores(list_ref[i]), lambda i: meta(list_ref[i]),
                    ((sa_ref, ma_ref), (sb_ref, mb_ref)), m_ref, acc_ref, n_even=False)
    out = _finish_t(acc_ref)
    for hh in range(NSA_GROUP):
        o_ref[0, hh] = out[:, hh * tq:(hh + 1) * tq]


def _nsa_tables(tq):
    slopes2 = _alibi_slopes2(NSA_HEADS)
    srow = _slope_rows(slopes2, tq)
    slane = np.repeat(slopes2.reshape(NSA_KV_HEADS, NSA_GROUP), tq, axis=1)
    slane8 = np.repeat(slane[:, None, :], 8, axis=1)
    return srow, jnp.asarray(slane8, F32)


def _active_tile_bits(cnt, tq, tk):
    b, g, nq128, _, nsb = cnt.shape
    qper, bper = tq // NSA_TQ, tk // NSA_SEL_BLOCK
    nq, nkv = nq128 // qper, nsb // bper
    act = cnt[:, :, :, 0, :].reshape(b, g, nq, qper, nkv, bper).sum(axis=(3, 5)) > 0.0
    words = -(-nkv // 32)
    act = jnp.pad(act, ((0, 0), (0, 0), (0, 0), (0, words * 32 - nkv))).reshape(b, g, nq, words, 32)
    bits = jnp.sum(act.astype(jnp.uint32) << jnp.arange(32, dtype=jnp.uint32), axis=-1, dtype=jnp.uint32)
    return lax.bitcast_convert_type(bits, jnp.int32).reshape(-1), nq, words


def nsa_sel_pallas(qvt, kaug, selt, cnt):
    b, _, dh, s = qvt.shape
    tq = NSA_SEL_TQ
    assert tq == NSA_TK
    lanes = NSA_GROUP * tq
    nsb = s // NSA_SEL_BLOCK
    srow, slane = _nsa_tables(tq)
    bits, nq, words = _active_tile_bits(cnt, tq, NSA_TK)
    grid_spec = pltpu.PrefetchScalarGridSpec(
        num_scalar_prefetch=1,
        grid=(b, NSA_KV_HEADS, nq),
        in_specs=[pl.BlockSpec((1, NSA_GROUP, dh, tq), lambda i, g, j, bt: (i, TQ_NSA // NSA_GROUP + g, 0, j)),
                  pl.BlockSpec((1, 1, s, KAUG), lambda i, g, j, bt: (i, KA_NSA + g, 0, 0)),
                  pl.BlockSpec((1, 1, dh, s), lambda i, g, j, bt: (i, TV_NSA + g, 0, 0)),
                  pl.BlockSpec((1, 1, nsb, tq), lambda i, g, j, bt: (i, g, 0, j)),
                  pl.BlockSpec((NSA_GROUP, dh, tq), lambda i, g, j, bt: (g, 0, 0)),
                  pl.BlockSpec((None, 8, lanes), lambda i, g, j, bt: (g, 0, 0))],
        out_specs=pl.BlockSpec((1, NSA_GROUP, dh, tq), lambda i, g, j, bt: (i, g, 0, j)),
        scratch_shapes=[pltpu.VMEM((1, lanes), F32), pltpu.VMEM((VROWS, lanes), F32),
                        pltpu.VMEM((VROWS, s), BF16)] + _score_buffers(NSA_TK, lanes)
        + [pltpu.SMEM((max(nq, 8),), jnp.int32)])
    return pl.pallas_call(
        functools.partial(_nsa_sel_kernel, nq=nq, words=words),
        grid_spec=grid_spec,
        out_shape=jax.ShapeDtypeStruct((b, NSA_HEADS, dh, s), F32),
        compiler_params=_cparams(("parallel", "parallel", "arbitrary")),
        name="nsa_selected",
    )(bits, qvt, kaug, qvt, selt, srow, slane)


NSA_WT = 128


def _nsa_win_kernel(qt_ref, k_ref, vt_ref, srow_ref, slane_ref, oc_ref, os_ref, gate_ref, o_ref,
                    m_ref, acc_ref, vaug_ref, sa_ref, ma_ref, sb_ref, mb_ref):
    tq = NSA_TQ
    wt = NSA_WT
    qi = pl.program_id(2)

    @pl.when(qi == 0)
    def _():
        _fill_vaug(vaug_ref, vt_ref)

    qaug = _nsa_qaug(qt_ref, srow_ref)
    slane = slane_ref[0:1, :]
    key = lax.broadcasted_iota(jnp.int32, (wt, tq), 0)
    qry = lax.broadcasted_iota(jnp.int32, (wt, tq), 1)
    span = NSA_WINDOW // wt

    def tile(d, keep):
        kj = jnp.maximum(qi - d, 0)

        def scores():
            st = _dot(_ktile(k_ref, kj, wt), qaug)
            if keep is not None:
                st = jnp.where(jnp.concatenate([keep] * NSA_GROUP, axis=1), st, NEG)
            return st

        def meta():
            base = (kj * wt) // POS_PERIOD * POS_PERIOD - qi * tq
            delta = jnp.where(qi - d >= 0, slane * base.astype(F32), SKIP)
            return delta, _vtile(vaug_ref, kj, wt)

        return scores, meta

    _init_state_t(m_ref, acc_ref)
    tiles = [tile(0, key <= qry)] + [tile(d, None) for d in range(1, span)] + [tile(span, key > qry)]
    _flash_pipeline(tiles, None, None, None, ((sa_ref, ma_ref), (sb_ref, mb_ref)), m_ref, acc_ref, True)

    o_w = _finish_t(acc_ref)
    gt = gate_ref[0].T
    for hh in range(NSA_GROUP):
        c0 = 3 * hh
        mix = (gt[c0:c0 + 1] * oc_ref[0, hh] + gt[c0 + 1:c0 + 2] * os_ref[0, hh]
               + gt[c0 + 2:c0 + 3] * o_w[:, hh * tq:(hh + 1) * tq])
        o_ref[0, hh] = mix.astype(o_ref.dtype)


def nsa_win_pallas(qvt, kaug, o_c, o_s, gates):
    b, _, dh, s = qvt.shape
    tq, wt = NSA_TQ, NSA_WT
    assert tq == wt and tq == LANES
    srow, slane = _nsa_tables(tq)
    head_blk = pl.BlockSpec((1, NSA_GROUP, dh, tq), lambda i, g, j: (i, g, 0, j))
    out = pl.pallas_call(
        _nsa_win_kernel,
        grid=(b, NSA_KV_HEADS, s // tq),
        in_specs=[pl.BlockSpec((1, NSA_GROUP, dh, tq), lambda i, g, j: (i, TQ_NSA // NSA_GROUP + g, 0, j)),
                  pl.BlockSpec((1, 1, s, KAUG), lambda i, g, j: (i, KA_NSA + 2 + g, 0, 0)),
                  pl.BlockSpec((1, 1, dh, s), lambda i, g, j: (i, TV_NSA + 2 + g, 0, 0)),
                  pl.BlockSpec((NSA_GROUP, dh, tq), lambda i, g, j: (g, 0, 0)),
                  pl.BlockSpec((None, 8, NSA_LANES), lambda i, g, j: (g, 0, 0)),
                  head_blk, head_blk,
                  pl.BlockSpec((1, tq, LANES), lambda i, g, j: (i, j, g))],
        out_specs=head_blk,
        out_shape=jax.ShapeDtypeStruct((b, NSA_HEADS, dh, s), BF16),
        scratch_shapes=[pltpu.VMEM((1, NSA_LANES), F32), pltpu.VMEM((VROWS, NSA_LANES), F32),
                        pltpu.VMEM((VROWS, s), BF16)] + _score_buffers(wt, NSA_LANES),
        compiler_params=_cparams(("parallel", "parallel", "arbitrary")),
        name="nsa_window_mix",
    )(qvt, kaug, qvt, srow, slane, o_c, o_s, gates)
    return out.reshape(b, NSA_HEADS * dh, s)


def _mem_attn_kernel(q_ref, k_ref, vt_ref, o_ref):
    for hh in range(MEM_HEADS):
        s = _dot_nt(q_ref[0, hh], k_ref[0, hh])
        m = jnp.max(s, axis=-1, keepdims=True)
        e = jnp.exp2(s - m)
        p = e / jnp.sum(e, axis=-1, keepdims=True)
        o_ref[0, hh] = _dot_nt(vt_ref[0, hh], p.astype(BF16)).astype(o_ref.dtype)


def mem_attn_pallas(nat, mem_k, mem_vt, tq=512):
    b, _, s, dh = nat.shape
    n_mem = mem_k.shape[2]
    out = pl.pallas_call(
        _mem_attn_kernel,
        grid=(b, s // tq),
        in_specs=[pl.BlockSpec((1, MEM_HEADS, tq, dh), lambda i, j: (i, N_MEMQ // MEM_HEADS, j, 0)),
                  pl.BlockSpec((1, MEM_HEADS, n_mem, dh), lambda i, j: (i, 0, 0, 0)),
                  pl.BlockSpec((1, MEM_HEADS, dh, n_mem), lambda i, j: (i, 0, 0, 0))],
        out_specs=pl.BlockSpec((1, MEM_HEADS, dh, tq), lambda i, j: (i, 0, 0, j)),
        out_shape=jax.ShapeDtypeStruct((b, MEM_HEADS, dh, s), BF16),
        compiler_params=_cparams(("parallel", "parallel")),
        name="memory_attention",
    )(nat, mem_k, mem_vt)
    return out.reshape(b, MEM_HEADS * dh, s)


def _pad_key_cols(w):
    d, n = w.shape
    w3 = w.reshape(d, n // HEAD_DIM, HEAD_DIM)
    return jnp.concatenate([w3, jnp.zeros_like(w3)], axis=2).reshape(d, 2 * n)


def _in_proj_weights(w_in):
    hd = HEAD_DIM
    sizes = (3 * MOBA_HEADS * hd, NSA_HEADS * hd, 6 * NSA_KV_HEADS * hd, 3 * NSA_HEADS,
             3 * FOX_HEADS * hd, FOX_HEADS, MEM_HEADS * hd)
    offs = np.concatenate([[0], np.cumsum(sizes)])
    moba, nsa_q, nsa_kv, nsa_g, fox, fox_f, mem_q = (w_in[:, offs[i]:offs[i + 1]] for i in range(7))
    mh, fh, g2 = MOBA_HEADS * hd, FOX_HEADS * hd, NSA_KV_HEADS * hd
    moba_q, moba_k, moba_v = moba[:, :mh], moba[:, mh:2 * mh], moba[:, 2 * mh:]
    fox_q, fox_k, fox_v = fox[:, :fh], fox[:, fh:2 * fh], fox[:, 2 * fh:]
    k_cmp, v_cmp, k_slc, v_slc, k_win, v_win = (nsa_kv[:, i * g2:(i + 1) * g2] for i in range(6))
    w_t = jnp.concatenate([moba_q, nsa_q, fox_q, moba_v, v_slc, v_win, fox_v], axis=1).T.astype(BF16)
    t_scale = np.ones((T_SLOTS * hd,), np.float32)
    t_scale[:TV_MOBA * hd] = Q_SCALE
    w_ka = _pad_key_cols(jnp.concatenate([moba_k, k_slc, k_win], axis=1)).astype(BF16)
    w_kf = _pad_key_cols(fox_k).astype(BF16)
    w_nat = jnp.concatenate([k_cmp, v_cmp, mem_q], axis=1).astype(BF16)
    n_scale = np.ones((N_SLOTS * hd,), np.float32)
    n_scale[N_MEMQ * hd:] = Q_SCALE
    return w_t, jnp.asarray(t_scale), w_ka, w_kf, w_nat, jnp.asarray(n_scale), nsa_g, fox_f


def _mixer(h32, h16, mem16, w_in, b_forget, w_mem_kv, cmp_pe, cmp_w1, cmp_w2):
    b, s, d = h16.shape
    tm = min(1024, s)
    w_t, t_scale, w_ka, w_kf, w_nat, n_scale, w_gate, w_forget = _in_proj_weights(w_in)
    gates, caug = gates_pallas(h32, w_gate, w_forget, b_forget)
    qvt = proj_t_pallas(h16, w_t, t_scale, tm=tm, heads_per_step=13)
    k_alibi = proj_kaug_pallas(h16, w_ka, None, tm=tm, heads_per_step=6)
    k_fox = proj_kaug_pallas(h16, w_kf, caug, tm=tm, heads_per_step=6)
    nat = proj_heads_pallas(h16, w_nat, n_scale, tm=tm, heads_per_step=8)
    n_mem = mem16.shape[1]
    mk = MEM_HEADS * HEAD_DIM
    mem_k = proj_heads_pallas(mem16, w_mem_kv[:, :mk].astype(BF16), jnp.ones((mk,), F32),
                              tm=n_mem, heads_per_step=MEM_HEADS)
    mem_vt = proj_t_pallas(mem16, w_mem_kv[:, mk:].T.astype(BF16), jnp.ones((mk,), F32),
                           tm=n_mem, heads_per_step=MEM_HEADS)
    o_moba = moba_pallas(qvt, k_alibi)
    o_fox = fox_pallas(qvt, k_fox)
    cmp_kv, cmp_kvt = nsa_compress_pallas(nat, cmp_pe, cmp_w1, cmp_w2)
    o_c, selt, cnt = nsa_cmp_pallas(qvt, cmp_kv, cmp_kvt)
    o_s = nsa_sel_pallas(qvt, k_alibi, selt, cnt)
    o_nsa = nsa_win_pallas(qvt, k_alibi, o_c, o_s, gates)
    o_mem = mem_attn_pallas(nat, mem_k, mem_vt)
    return [o_moba, o_nsa, o_fox, o_mem]


def kernel(x, mem, emb_ln_g, emb_ln_b, w_in, b_forget, w_mem_kv, nsa_cmp_pe, nsa_cmp_w1, nsa_cmp_w2,
           w_out, ln1_g, ln1_b, ffn_w_up, ffn_conv_w, ffn_conv_b, ffn_w_down, ln2_g, ln2_b):
    b, s, d = x.shape
    depth = w_in.shape[0]
    dff = ffn_w_down.shape[1]
    mem16 = mem.astype(BF16)
    h32, h16 = layer_norm_pallas(x.reshape(b * s, d), emb_ln_g, emb_ln_b)
    for l in range(depth):
        heads = _mixer(h32.reshape(b, s, d), h16.reshape(b, s, d), mem16, w_in[l], b_forget[l], w_mem_kv[l],
                       nsa_cmp_pe[l], nsa_cmp_w1[l], nsa_cmp_w2[l])
        h32, h16 = out_proj_ln_pallas(heads, w_out[l].astype(BF16), h32.reshape(b, s, d),
                                      ln1_g[l], ln1_b[l], tm=512)
        a = ffn_up_pallas(h16, ffn_w_up[l].astype(BF16), ffn_conv_w[l], ffn_conv_b[l],
                          tm=min(1024, s), tn=512)
        h32, h16 = matmul_ln_resident_pallas(a.reshape(b * s, dff), ffn_w_down[l].astype(BF16),
                                             h32.reshape(b * s, d), ln2_g[l], ln2_b[l], tm=256)
    return h32.reshape(b, s, d)
```

```python
import functools
import math

import jax
import jax.numpy as jnp
import ml_dtypes
import numpy as np
from jax import lax
from jax.experimental import pallas as pl
from jax.experimental.pallas import tpu as pltpu

F32 = jnp.float32
BF16 = jnp.bfloat16

HEAD_DIM = 64
MOBA_HEADS = 8
NSA_HEADS = 8
NSA_KV_HEADS = 2
NSA_GROUP = NSA_HEADS // NSA_KV_HEADS
FOX_HEADS = 12
MEM_HEADS = 4
MOBA_BLOCK = 256
MOBA_TOPK = 3
NSA_CMP_LEN = 32
NSA_CMP_STRIDE = 16
NSA_SEL_BLOCK = 64
NSA_TOPK = 16
NSA_WINDOW = 512
NSA_FORCE_SCORE = 1.0e4
CONV_WIDTH = 3
LN_EPS = 1e-5
DEPTH = 2
DEEPNORM_ALPHA = (2 * DEPTH) ** 0.25

LOG2E = math.log2(math.e)
Q_SCALE = HEAD_DIM ** -0.5 * LOG2E
NEG = -1.0e30
SKIP = -3.0e38
VMEM_LIMIT = 56 * 1024 * 1024
LANES = 128
KAUG = 2 * HEAD_DIM
VROWS = HEAD_DIM + 16
POS_PERIOD = 256

TQ_MOBA, TQ_NSA, TQ_FOX = 0, 8, 16
TV_MOBA, TV_NSA, TV_FOX = 28, 36, 40
T_SLOTS = 52
KA_MOBA, KA_NSA = 0, 8
KA_SLOTS = 12
N_CMP, N_MEMQ = 0, 4
N_SLOTS = 8


def _cparams(sem):
    return pltpu.CompilerParams(dimension_semantics=sem, vmem_limit_bytes=VMEM_LIMIT)


def _split2(x):
    hi = x.astype(BF16)
    return hi, (x - hi.astype(F32)).astype(BF16)


def _split3(x):
    hi = x.astype(BF16)
    r1 = x - hi.astype(F32)
    mid = r1.astype(BF16)
    lo = (r1 - mid.astype(F32)).astype(BF16)
    return hi, mid, lo


def _np_split3(x):
    x = np.asarray(x, np.float32)
    hi = x.astype(ml_dtypes.bfloat16).astype(np.float32)
    r1 = x - hi
    mid = r1.astype(ml_dtypes.bfloat16).astype(np.float32)
    lo = (r1 - mid).astype(ml_dtypes.bfloat16).astype(np.float32)
    return hi, mid, lo


def _dot_nt(a, b):
    return lax.dot_general(a, b, (((1,), (1,)), ((), ())), preferred_element_type=F32)


def _dot_tn(a, b):
    return lax.dot_general(a, b, (((0,), (0,)), ((), ())), preferred_element_type=F32)


def _dot(a, b):
    return jnp.dot(a, b, preferred_element_type=F32)


def _layer_norm_rows(x, g, b):
    mu = jnp.mean(x, axis=-1, keepdims=True)
    xc = x - mu
    var = jnp.mean(xc * xc, axis=-1, keepdims=True)
    return xc * lax.rsqrt(var + LN_EPS) * g + b


def _alibi_slopes2(n):
    return (np.exp2(-8.0 * np.arange(1, n + 1, dtype=np.float64) / n) * LOG2E).astype(np.float32)


def _slope_rows(slopes2, lanes):
    pieces = np.stack(_np_split3(slopes2), axis=1)
    rows = np.zeros((len(slopes2), HEAD_DIM, lanes), np.float32)
    rows[:, :3, :] = pieces[:, :, None]
    return jnp.asarray(rows, BF16)


def _ln_kernel(x_ref, g_ref, b_ref, o32_ref, o16_ref):
    y = _layer_norm_rows(x_ref[...], g_ref[...], b_ref[...])
    o32_ref[...] = y
    o16_ref[...] = y.astype(BF16)


def layer_norm_pallas(x, g, b, tm=512):
    m, d = x.shape
    return pl.pallas_call(
        _ln_kernel,
        grid=(m // tm,),
        in_specs=[pl.BlockSpec((tm, d), lambda i: (i, 0)),
                  pl.BlockSpec((1, d), lambda i: (0, 0)),
                  pl.BlockSpec((1, d), lambda i: (0, 0))],
        out_specs=[pl.BlockSpec((tm, d), lambda i: (i, 0)),
                   pl.BlockSpec((tm, d), lambda i: (i, 0))],
        out_shape=[jax.ShapeDtypeStruct((m, d), F32), jax.ShapeDtypeStruct((m, d), BF16)],
        compiler_params=_cparams(("parallel",)),
        name="layer_norm",
    )(x, g.reshape(1, d), b.reshape(1, d))


def _gates_kernel(h_ref, wg_ref, wf_ref, bf_ref, tri_ref, place_ref, g_ref, caug_ref, carry_ref):
    si = pl.program_id(1)

    @pl.when(si == 0)
    def _():
        carry_ref[...] = jnp.zeros_like(carry_ref)

    h_hi, h_lo = _split2(h_ref[0])
    wg_hi, wg_lo = _split2(wg_ref[...])
    g = _dot(h_hi, wg_hi) + _dot(h_lo, wg_hi) + _dot(h_hi, wg_lo)
    g_ref[0] = 1.0 / (1.0 + jnp.exp(-g))
    wf_hi, wf_lo = _split2(wf_ref[...])
    x = _dot(h_hi, wf_hi) + _dot(h_lo, wf_hi) + _dot(h_hi, wf_lo) + bf_ref[...]
    logf = jnp.minimum(x, 0.0) - jnp.log(1.0 + jnp.exp(-jnp.abs(x)))
    tri = tri_ref[...]
    l_hi, l_mid, l_lo = _split3(logf)
    c = _dot(tri, l_hi) + _dot(tri, l_mid) + _dot(tri, l_lo) + carry_ref[0:1, :]
    carry_ref[...] = jnp.broadcast_to(c[-1:, :], carry_ref.shape)
    n_hi, n_mid, n_lo = _split3(-LOG2E * c)
    caug = _dot(n_hi, place_ref[0]) + _dot(n_mid, place_ref[1]) + _dot(n_lo, place_ref[2])
    caug_ref[0] = caug.astype(BF16)


def gates_pallas(h3, w_gate, w_forget, b_forget, t=512):
    b, s, d = h3.shape
    gw = 3 * NSA_GROUP
    ng = NSA_KV_HEADS * LANES
    wg = jnp.zeros((d, ng), F32)
    for g in range(NSA_KV_HEADS):
        wg = wg.at[:, g * LANES:g * LANES + gw].set(w_gate[:, g * gw:(g + 1) * gw])
    wf = jnp.zeros((d, LANES), F32).at[:, :FOX_HEADS].set(w_forget)
    bf = jnp.zeros((1, LANES), F32).at[0, :FOX_HEADS].set(b_forget)
    tri = (np.arange(t)[None, :] <= np.arange(t)[:, None]).astype(np.float32)
    place = np.zeros((3, LANES, FOX_HEADS * KAUG), np.float32)
    for piece in range(3):
        for hh in range(FOX_HEADS):
            place[piece, hh, hh * KAUG + HEAD_DIM + piece] = 1.0
    nc = FOX_HEADS * KAUG
    return pl.pallas_call(
        _gates_kernel,
        grid=(b, s // t),
        in_specs=[pl.BlockSpec((1, t, d), lambda i, j: (i, j, 0)),
                  pl.BlockSpec((d, ng), lambda i, j: (0, 0)),
                  pl.BlockSpec((d, LANES), lambda i, j: (0, 0)),
                  pl.BlockSpec((1, LANES), lambda i, j: (0, 0)),
                  pl.BlockSpec((t, t), lambda i, j: (0, 0)),
                  pl.BlockSpec((3, LANES, nc), lambda i, j: (0, 0, 0))],
        out_specs=[pl.BlockSpec((1, t, ng), lambda i, j: (i, j, 0)),
                   pl.BlockSpec((1, t, nc), lambda i, j: (i, j, 0))],
        out_shape=[jax.ShapeDtypeStruct((b, s, ng), F32),
                   jax.ShapeDtypeStruct((b, s, nc), BF16)],
        scratch_shapes=[pltpu.VMEM((8, LANES), F32)],
        compiler_params=_cparams(("parallel", "arbitrary")),
        name="gates_cumsum",
    )(h3, wg, wf, bf, jnp.asarray(tri, BF16), jnp.asarray(place, BF16))


def _proj_heads_kernel(x_ref, w_ref, sc_ref, o_ref, *, heads_per_step):
    acc = _dot(x_ref[0], w_ref[...]) * sc_ref[...]
    for j in range(heads_per_step):
        o_ref[0, j] = acc[:, j * HEAD_DIM:(j + 1) * HEAD_DIM].astype(o_ref.dtype)


def proj_heads_pallas(x3, w, colscale, tm, heads_per_step):
    b, s, d = x3.shape
    n = w.shape[1]
    tn = heads_per_step * HEAD_DIM
    return pl.pallas_call(
        functools.partial(_proj_heads_kernel, heads_per_step=heads_per_step),
        grid=(b, s // tm, n // tn),
        in_specs=[pl.BlockSpec((1, tm, d), lambda i, j, k: (i, j, 0)),
                  pl.BlockSpec((d, tn), lambda i, j, k: (0, k)),
                  pl.BlockSpec((1, tn), lambda i, j, k: (0, k))],
        out_specs=pl.BlockSpec((1, heads_per_step, tm, HEAD_DIM), lambda i, j, k: (i, k, j, 0)),
        out_shape=jax.ShapeDtypeStruct((b, n // HEAD_DIM, s, HEAD_DIM), BF16),
        compiler_params=_cparams(("parallel", "parallel", "arbitrary")),
        name="proj_heads",
    )(x3, w, colscale.reshape(1, n))


def _proj_t_kernel(x_ref, wt_ref, sc_ref, o_ref, *, heads_per_step):
    acc = _dot_nt(wt_ref[...], x_ref[0]) * sc_ref[...]
    o_ref[0] = acc.reshape(heads_per_step, HEAD_DIM, acc.shape[1]).astype(o_ref.dtype)


def proj_t_pallas(x3, wt, rowscale, tm, heads_per_step):
    b, s, d = x3.shape
    n = wt.shape[0]
    tn = heads_per_step * HEAD_DIM
    return pl.pallas_call(
        functools.partial(_proj_t_kernel, heads_per_step=heads_per_step),
        grid=(b, s // tm, n // tn),
        in_specs=[pl.BlockSpec((1, tm, d), lambda i, j, k: (i, j, 0)),
                  pl.BlockSpec((tn, d), lambda i, j, k: (k, 0)),
                  pl.BlockSpec((tn, 1), lambda i, j, k: (k, 0))],
        out_specs=pl.BlockSpec((1, heads_per_step, HEAD_DIM, tm), lambda i, j, k: (i, k, 0, j)),
        out_shape=jax.ShapeDtypeStruct((b, n // HEAD_DIM, HEAD_DIM, s), BF16),
        compiler_params=_cparams(("parallel", "parallel", "arbitrary")),
        name="proj_transposed",
    )(x3, wt, rowscale.reshape(n, 1))


def _proj_kaug_kernel(x_ref, w_ref, *rest, heads_per_step, positional):
    o_ref = rest[-1]
    acc = _dot(x_ref[0], w_ref[...])
    tm = acc.shape[0]
    if positional:
        pos = (pl.program_id(1) * tm + lax.broadcasted_iota(jnp.int32, acc.shape, 0)) % POS_PERIOD
        lane = lax.broadcasted_iota(jnp.int32, acc.shape, 1) % KAUG
        acc = acc + jnp.where((lane >= HEAD_DIM) & (lane < HEAD_DIM + 3), pos.astype(F32), 0.0)
    else:
        acc = acc + rest[0][0].astype(F32)
    for j in range(heads_per_step):
        o_ref[0, j] = acc[:, j * KAUG:(j + 1) * KAUG].astype(o_ref.dtype)


def proj_kaug_pallas(x3, w, aug, tm, heads_per_step):
    b, s, d = x3.shape
    n = w.shape[1]
    tn = heads_per_step * KAUG
    in_specs = [pl.BlockSpec((1, tm, d), lambda i, j, k: (i, j, 0)),
                pl.BlockSpec((d, tn), lambda i, j, k: (0, k))]
    args = [x3, w]
    if aug is not None:
        in_specs.append(pl.BlockSpec((1, tm, tn), lambda i, j, k: (i, j, k)))
        args.append(aug)
    return pl.pallas_call(
        functools.partial(_proj_kaug_kernel, heads_per_step=heads_per_step, positional=aug is None),
        grid=(b, s // tm, n // tn),
        in_specs=in_specs,
        out_specs=pl.BlockSpec((1, heads_per_step, tm, KAUG), lambda i, j, k: (i, k, j, 0)),
        out_shape=jax.ShapeDtypeStruct((b, n // KAUG, s, KAUG), BF16),
        compiler_params=_cparams(("parallel", "parallel", "arbitrary")),
        name="proj_keys_aug",
    )(*args)


def _matmul_ln_kernel(x_ref, w_ref, r_ref, g_ref, b_ref, o32_ref, o16_ref, acc_ref):
    k = pl.program_id(1)

    @pl.when(k == 0)
    def _():
        acc_ref[...] = jnp.zeros_like(acc_ref)

    acc_ref[...] += _dot(x_ref[...], w_ref[...])

    @pl.when(k == pl.num_programs(1) - 1)
    def _():
        y = _layer_norm_rows(DEEPNORM_ALPHA * r_ref[...] + acc_ref[...], g_ref[...], b_ref[...])
        o32_ref[...] = y
        o16_ref[...] = y.astype(BF16)


def matmul_ln_pallas(x, w, res, g, b, tm, tk):
    m, kk = x.shape
    d = w.shape[1]
    return pl.pallas_call(
        _matmul_ln_kernel,
        grid=(m // tm, kk // tk),
        in_specs=[pl.BlockSpec((tm, tk), lambda i, k: (i, k)),
                  pl.BlockSpec((tk, d), lambda i, k: (k, 0)),
                  pl.BlockSpec((tm, d), lambda i, k: (i, 0)),
                  pl.BlockSpec((1, d), lambda i, k: (0, 0)),
                  pl.BlockSpec((1, d), lambda i, k: (0, 0))],
        out_specs=[pl.BlockSpec((tm, d), lambda i, k: (i, 0)),
                   pl.BlockSpec((tm, d), lambda i, k: (i, 0))],
        out_shape=[jax.ShapeDtypeStruct((m, d), F32), jax.ShapeDtypeStruct((m, d), BF16)],
        scratch_shapes=[pltpu.VMEM((tm, d), F32)],
        compiler_params=_cparams(("parallel", "arbitrary")),
        name="matmul_ln",
    )(x, w, res, g.reshape(1, d), b.reshape(1, d))


def _matmul_ln_resident_kernel(x_ref, w_ref, r_ref, g_ref, b_ref, o32_ref, o16_ref):
    y = _layer_norm_rows(DEEPNORM_ALPHA * r_ref[...] + _dot(x_ref[...], w_ref[...]), g_ref[...], b_ref[...])
    o32_ref[...] = y
    o16_ref[...] = y.astype(BF16)


def matmul_ln_resident_pallas(x, w, res, g, b, tm):
    m, kk = x.shape
    d = w.shape[1]
    return pl.pallas_call(
        _matmul_ln_resident_kernel,
        grid=(m // tm,),
        in_specs=[pl.BlockSpec((tm, kk), lambda i: (i, 0)),
                  pl.BlockSpec((kk, d), lambda i: (0, 0), pipeline_mode=pl.Buffered(1)),
                  pl.BlockSpec((tm, d), lambda i: (i, 0)),
                  pl.BlockSpec((1, d), lambda i: (0, 0)),
                  pl.BlockSpec((1, d), lambda i: (0, 0))],
        out_specs=[pl.BlockSpec((tm, d), lambda i: (i, 0)),
                   pl.BlockSpec((tm, d), lambda i: (i, 0))],
        out_shape=[jax.ShapeDtypeStruct((m, d), F32), jax.ShapeDtypeStruct((m, d), BF16)],
        compiler_params=_cparams(("parallel",)),
        name="matmul_ln_resident",
    )(x, w, res, g.reshape(1, d), b.reshape(1, d))


def _out_proj_ln_kernel(*refs, widths):
    n = len(widths)
    x_refs, (w_ref, r_ref, g_ref, b_ref, o32_ref, o16_ref) = refs[:n], refs[n:]
    acc = None
    off = 0
    for x_ref, width in zip(x_refs, widths):
        part = _dot_tn(x_ref[0], w_ref[off:off + width, :])
        acc = part if acc is None else acc + part
        off += width
    y = _layer_norm_rows(DEEPNORM_ALPHA * r_ref[0] + acc, g_ref[...], b_ref[...])
    o32_ref[0] = y
    o16_ref[0] = y.astype(BF16)


def out_proj_ln_pallas(xts, w, res3, g, b, tm):
    bsz, s, d = res3.shape
    widths = tuple(x.shape[1] for x in xts)
    in_specs = [pl.BlockSpec((1, wd, tm), lambda i, j: (i, 0, j)) for wd in widths]
    in_specs += [pl.BlockSpec((w.shape[0], d), lambda i, j: (0, 0)),
                 pl.BlockSpec((1, tm, d), lambda i, j: (i, j, 0)),
                 pl.BlockSpec((1, d), lambda i, j: (0, 0)),
                 pl.BlockSpec((1, d), lambda i, j: (0, 0))]
    return pl.pallas_call(
        functools.partial(_out_proj_ln_kernel, widths=widths),
        grid=(bsz, s // tm),
        in_specs=in_specs,
        out_specs=[pl.BlockSpec((1, tm, d), lambda i, j: (i, j, 0)),
                   pl.BlockSpec((1, tm, d), lambda i, j: (i, j, 0))],
        out_shape=[jax.ShapeDtypeStruct((bsz, s, d), F32), jax.ShapeDtypeStruct((bsz, s, d), BF16)],
        compiler_params=_cparams(("parallel", "parallel")),
        name="out_proj_ln",
    )(*xts, w, res3, g.reshape(1, d), b.reshape(1, d))


HALO = 16
FFN_CHUNK = 256


def _gelu_tanh(x):
    return 0.5 * x * (1.0 + jnp.tanh(math.sqrt(2.0 / math.pi) * (x + 0.044715 * x * x * x)))


def _ffn_up_kernel(x_ref, xh_ref, wu_ref, wg_ref, cw_ref, cb_ref, o_ref):
    j = pl.program_id(1)
    x = x_ref[0]
    xh = xh_ref[0]
    first = jnp.where(j > 0, 1.0, 0.0)
    tn = o_ref.shape[2]
    row = lax.broadcasted_iota(jnp.int32, (x.shape[0], FFN_CHUNK), 0)
    for c in range(tn // FFN_CHUNK):
        cols = slice(c * FFN_CHUNK, (c + 1) * FFN_CHUNK)
        u = _dot(x, wu_ref[:, cols])
        g = _dot(x, wg_ref[:, cols])
        gh = _dot(xh, wg_ref[:, cols]) * first
        prev1 = gh[HALO - 1:HALO, :]
        prev2 = gh[HALO - 2:HALO - 1, :]
        g_m1 = jnp.where(row == 0, prev1, pltpu.roll(g, 1, axis=0))
        g_m2 = jnp.where(row == 0, prev2, jnp.where(row == 1, prev1, pltpu.roll(g, 2, axis=0)))
        cw = cw_ref[:, cols]
        gc = cb_ref[:, cols] + cw[0:1] * g_m2 + cw[1:2] * g_m1 + cw[2:3] * g
        o_ref[0, :, cols] = (_gelu_tanh(gc) * u).astype(o_ref.dtype)


def ffn_up_pallas(x3, w_up, conv_w, conv_b, tm, tn):
    b, s, d = x3.shape
    dff = w_up.shape[1] // 2
    nt = dff // tn
    hb = tm // HALO
    cw = jnp.zeros((8, dff), F32).at[:CONV_WIDTH].set(conv_w)
    return pl.pallas_call(
        _ffn_up_kernel,
        grid=(b, s // tm, nt),
        in_specs=[pl.BlockSpec((1, tm, d), lambda i, j, k: (i, j, 0)),
                  pl.BlockSpec((1, HALO, d), lambda i, j, k: (i, jnp.maximum(j * hb - 1, 0), 0)),
                  pl.BlockSpec((d, tn), lambda i, j, k: (0, k)),
                  pl.BlockSpec((d, tn), lambda i, j, k: (0, k + nt)),
                  pl.BlockSpec((8, tn), lambda i, j, k: (0, k)),
                  pl.BlockSpec((1, tn), lambda i, j, k: (0, k))],
        out_specs=pl.BlockSpec((1, tm, tn), lambda i, j, k: (i, j, k)),
        out_shape=jax.ShapeDtypeStruct((b, s, dff), BF16),
        compiler_params=_cparams(("parallel", "parallel", "arbitrary")),
        name="ffn_up",
    )(x3, x3, w_up, w_up, cw, conv_b.reshape(1, dff))


def _online_step_t(st, delta, vaug, m_ref, acc_ref):
    m_prev = m_ref[...]
    m_cur = jnp.max(st, axis=0, keepdims=True) + delta
    m_new = jnp.maximum(m_prev, m_cur)
    alpha = jnp.exp2(m_prev - m_new)
    pt = jnp.exp2(st - (m_new - delta)).astype(BF16)
    acc_ref[...] = alpha * acc_ref[...] + _dot(vaug, pt)
    m_ref[...] = m_new


def _init_state_t(m_ref, acc_ref):
    m_ref[...] = jnp.full(m_ref.shape, NEG, F32)
    acc_ref[...] = jnp.zeros(acc_ref.shape, F32)


def _finish_t(acc_ref):
    acc = acc_ref[...]
    return acc[:HEAD_DIM] / acc[HEAD_DIM:HEAD_DIM + 1]


def _fill_vaug(vaug_ref, vt_ref):
    s = vaug_ref.shape[1]
    vaug_ref[0:HEAD_DIM, :] = vt_ref[0, 0]
    pad = lax.broadcasted_iota(jnp.int32, (VROWS - HEAD_DIM, s), 0)
    vaug_ref[HEAD_DIM:VROWS, :] = jnp.where(pad == 0, 1.0, 0.0).astype(BF16)


def _ktile(ref, idx, size):
    return ref[0, 0, pl.ds(pl.multiple_of(idx * size, size), size), :]


def _vtile(ref, idx, size):
    return ref[:, pl.ds(pl.multiple_of(idx * size, size), size)]


def _stage_scores(st, s_ref, mc_ref):
    nsub = mc_ref.shape[0]
    s_ref[...] = st
    if nsub == 1:
        mc_ref[...] = jnp.max(st, axis=0, keepdims=True)
    else:
        mc_ref[...] = jnp.max(st.reshape(nsub, st.shape[0] // nsub, st.shape[1]), axis=1)


def _stage_update(s_ref, mc_ref, delta, vaug, m_ref, acc_ref):
    nsub = mc_ref.shape[0]
    m_prev = m_ref[...]
    m_new = jnp.maximum(m_prev, jnp.max(mc_ref[...] + delta, axis=0, keepdims=True))
    alpha = jnp.exp2(m_prev - m_new)
    shift = m_new - delta
    if nsub == 1:
        pt = jnp.exp2(s_ref[...] - shift)
    else:
        tk, lanes = s_ref.shape
        pt = jnp.exp2(s_ref[...].reshape(nsub, tk // nsub, lanes) - shift[:, None, :]).reshape(tk, lanes)
    acc_ref[...] = alpha * acc_ref[...] + _dot(vaug, pt.astype(BF16))
    m_ref[...] = m_new


def _score_buffers(tk, lanes, nsub=1):
    return [pltpu.VMEM((tk, lanes), F32), pltpu.VMEM((nsub, lanes), F32),
            pltpu.VMEM((tk, lanes), F32), pltpu.VMEM((nsub, lanes), F32)]


def _flash_pipeline(lead, n_loop, scores, meta, bufs, m_ref, acc_ref, n_even):
    _stage_scores(lead[0][0](), *bufs[0])
    for i in range(1, len(lead)):
        _stage_scores(lead[i][0](), *bufs[i % 2])
        _stage_update(*bufs[(i - 1) % 2], *lead[i - 1][1](), m_ref, acc_ref)
    cur = (len(lead) - 1) % 2
    nxt = 1 - cur
    if scores is None:
        _stage_update(*bufs[cur], *lead[-1][1](), m_ref, acc_ref)
        return
    _stage_scores(scores(0), *bufs[nxt])
    _stage_update(*bufs[cur], *lead[-1][1](), m_ref, acc_ref)
    last = jnp.maximum(n_loop - 1, 0)

    def body(kp, carry):
        k0 = 2 * kp
        k1 = k0 + 1
        _stage_scores(scores(jnp.minimum(k1, last)), *bufs[cur])
        _stage_update(*bufs[nxt], *meta(k0), m_ref, acc_ref)
        _stage_scores(scores(jnp.minimum(k0 + 2, last)), *bufs[nxt])
        delta1, vaug1 = meta(jnp.minimum(k1, last))
        if not n_even:
            delta1 = jnp.where(k1 < n_loop, delta1, SKIP)
        _stage_update(*bufs[cur], delta1, vaug1, m_ref, acc_ref)
        return carry

    lax.fori_loop(0, (n_loop + 1) // 2, body, 0)


FOX_TQ = 1024
FOX_TK = 512


def _fox_kernel(qt_ref, k_ref, vt_ref, o_ref, m_ref, acc_ref, vaug_ref, sa_ref, ma_ref, sb_ref, mb_ref):
    tq, tk = o_ref.shape[3], FOX_TK
    nd = tq // tk
    qi = pl.program_id(2)

    @pl.when(qi == 0)
    def _():
        _fill_vaug(vaug_ref, vt_ref)

    ones3 = jnp.where(lax.broadcasted_iota(jnp.int32, (HEAD_DIM, tq), 0) < 3, 1.0, 0.0).astype(BF16)
    qaug = jnp.concatenate([qt_ref[0, 0], ones3], axis=0)
    _init_state_t(m_ref, acc_ref)
    key = lax.broadcasted_iota(jnp.int32, (tk, tq), 0)
    qry = lax.broadcasted_iota(jnp.int32, (tk, tq), 1)

    def scores(kj):
        return _dot(_ktile(k_ref, kj, tk), qaug)

    def meta(kj):
        return 0.0, _vtile(vaug_ref, kj, tk)

    lead = [(functools.partial(lambda d: jnp.where(key + d * tk <= qry, scores(qi * nd + d), NEG), d),
             functools.partial(lambda d: meta(qi * nd + d), d)) for d in range(nd)]
    _flash_pipeline(lead, qi * nd, scores, meta, ((sa_ref, ma_ref), (sb_ref, mb_ref)), m_ref, acc_ref,
                    n_even=nd % 2 == 0)
    o_ref[0, 0] = _finish_t(acc_ref).astype(o_ref.dtype)


def fox_pallas(qvt, kaug):
    b, _, dh, s = qvt.shape
    tq = min(FOX_TQ, s)
    assert tq % FOX_TK == 0
    out = pl.pallas_call(
        _fox_kernel,
        grid=(b, FOX_HEADS, s // tq),
        in_specs=[pl.BlockSpec((1, 1, dh, tq), lambda i, h, j: (i, TQ_FOX + h, 0, j)),
                  pl.BlockSpec((1, 1, s, KAUG), lambda i, h, j: (i, h, 0, 0)),
                  pl.BlockSpec((1, 1, dh, s), lambda i, h, j: (i, TV_FOX + h, 0, 0))],
        out_specs=pl.BlockSpec((1, 1, dh, tq), lambda i, h, j: (i, h, 0, j)),
        out_shape=jax.ShapeDtypeStruct((b, FOX_HEADS, dh, s), BF16),
        scratch_shapes=[pltpu.VMEM((1, tq), F32), pltpu.VMEM((VROWS, tq), F32),
                        pltpu.VMEM((VROWS, s), BF16)] + _score_buffers(FOX_TK, tq),
        compiler_params=_cparams(("parallel", "parallel", "arbitrary")),
        name="fox_attention",
    )(qvt, kaug, qvt)
    return out.reshape(b, FOX_HEADS * dh, s)


def _moba_kernel(slope_ref, qt_ref, k_ref, vt_ref, ind_ref, srow_ref, o_ref,
                 m_ref, acc_ref, vaug_ref, km_ref, sel_ref, sa_ref, ma_ref, sb_ref, mb_ref):
    t = MOBA_BLOCK
    tq = o_ref.shape[3]
    nd = tq // t
    h = pl.program_id(1)
    qi = pl.program_id(2)
    slope2 = slope_ref[h]

    @pl.when(qi == 0)
    def _():
        _fill_vaug(vaug_ref, vt_ref)
        km_ref[...] = _dot(ind_ref[...], k_ref[0, 0]) * (1.0 / MOBA_BLOCK)

    qt = qt_ref[0, 0]
    km_hi, km_lo = _split2(km_ref[...])
    q0 = jnp.concatenate([qt, jnp.zeros_like(qt)], axis=0)
    gate = _dot(km_hi, q0) + _dot(km_lo, q0)
    blk = lax.broadcasted_iota(jnp.int32, gate.shape, 0)
    lane_blk = lax.broadcasted_iota(jnp.int32, (1, tq), 1) // t
    valid = blk < qi * nd + lane_blk
    work = jnp.where(valid, gate, -jnp.inf)
    sel = jnp.zeros(gate.shape, F32)
    big = jnp.int32(2 ** 30)
    for _ in range(MOBA_TOPK):
        mx = jnp.max(work, axis=0, keepdims=True)
        first = jnp.min(jnp.where(work == mx, blk, big), axis=0, keepdims=True)
        hit = blk == first
        sel = jnp.where(hit, 1.0, sel)
        work = jnp.where(hit, -jnp.inf, work)
    sel_ref[...] = jnp.where(valid, sel, 0.0)

    qaug = jnp.concatenate([qt, srow_ref[...]], axis=0)
    _init_state_t(m_ref, acc_ref)
    tk = sa_ref.shape[0]
    nsub = tk // t
    ntile = tq // tk
    key = lax.broadcasted_iota(jnp.int32, (tk, tq), 0)
    qry = lax.broadcasted_iota(jnp.int32, (tk, tq), 1)
    sub = lax.broadcasted_iota(jnp.int32, (nsub, tq), 0)

    def tile_constant(kj, d):
        sees = jnp.concatenate([sel_ref[pl.ds(kj * nsub + c, 1), :] for c in range(nsub)], axis=0) > 0.0
        if d is not None:
            sees = sees | (lane_blk == d * nsub + sub)
        offset = ((kj * nsub + sub - qi * nd) * t).astype(F32)
        return jnp.where(sees, slope2 * offset, SKIP)

    def scores(kj):
        return _dot(_ktile(k_ref, kj, tk), qaug)

    def meta(kj):
        return tile_constant(kj, None), _vtile(vaug_ref, kj, tk)

    lead = [(functools.partial(lambda d: jnp.where(key + d * tk <= qry, scores(qi * ntile + d), NEG), d),
             functools.partial(lambda d: (tile_constant(qi * ntile + d, d),
                                          _vtile(vaug_ref, qi * ntile + d, tk)), d))
            for d in range(ntile)]
    _flash_pipeline(lead, qi * ntile, scores, meta, ((sa_ref, ma_ref), (sb_ref, mb_ref)), m_ref, acc_ref,
                    n_even=ntile % 2 == 0)
    o_ref[0, 0] = _finish_t(acc_ref).astype(o_ref.dtype)


MOBA_TQ = 1024
MOBA_TK = 512


def moba_pallas(qvt, kaug):
    b, _, dh, s = qvt.shape
    t = MOBA_BLOCK
    tq = min(MOBA_TQ, s)
    tk = min(MOBA_TK, tq)
    assert POS_PERIOD == t and tq % tk == 0 and tk % t == 0
    nb = s // t
    nbp = max(16, nb)
    ind = np.zeros((nbp, s), np.float32)
    ind[np.arange(s) // t, np.arange(s)] = 1.0
    slopes2 = _alibi_slopes2(MOBA_HEADS)
    grid_spec = pltpu.PrefetchScalarGridSpec(
        num_scalar_prefetch=1,
        grid=(b, MOBA_HEADS, s // tq),
        in_specs=[pl.BlockSpec((1, 1, dh, tq), lambda i, h, j, sl: (i, TQ_MOBA + h, 0, j)),
                  pl.BlockSpec((1, 1, s, KAUG), lambda i, h, j, sl: (i, KA_MOBA + h, 0, 0)),
                  pl.BlockSpec((1, 1, dh, s), lambda i, h, j, sl: (i, TV_MOBA + h, 0, 0)),
                  pl.BlockSpec((nbp, s), lambda i, h, j, sl: (0, 0)),
                  pl.BlockSpec((None, dh, tq), lambda i, h, j, sl: (h, 0, 0))],
        out_specs=pl.BlockSpec((1, 1, dh, tq), lambda i, h, j, sl: (i, h, 0, j)),
        scratch_shapes=[pltpu.VMEM((1, tq), F32), pltpu.VMEM((VROWS, tq), F32),
                        pltpu.VMEM((VROWS, s), BF16), pltpu.VMEM((nbp, KAUG), F32),
                        pltpu.VMEM((nbp, tq), F32)] + _score_buffers(tk, tq, tk // t))
    out = pl.pallas_call(
        _moba_kernel,
        grid_spec=grid_spec,
        out_shape=jax.ShapeDtypeStruct((b, MOBA_HEADS, dh, s), BF16),
        compiler_params=_cparams(("parallel", "parallel", "arbitrary")),
        name="moba_attention",
    )(jnp.asarray(slopes2), qvt, kaug, qvt, jnp.asarray(ind, BF16), _slope_rows(slopes2, tq))
    return out.reshape(b, MOBA_HEADS * dh, s)


def _nsa_compress_kernel(x_ref, w1a_ref, w1b_ref, pe_ref, w1_ref, w2_ref, w2t_ref, o_ref, ot_ref):
    nr = x_ref.shape[2]
    x = x_ref[0, 0]

    def near_f32(xb, w):
        w_hi, w_lo = _split2(w)
        return _dot(xb, w_hi) + _dot(xb, w_lo)

    a = near_f32(x, w1a_ref[0])
    bm = near_f32(x, w1b_ref[0])
    pe_hi, pe_mid, pe_lo = _split3(pe_ref[0])
    w1_hi, w1_lo = _split2(w1_ref[0])
    pe_term = (_dot(pe_hi, w1_hi) + _dot(pe_mid, w1_hi) + _dot(pe_lo, w1_hi)
               + _dot(pe_hi, w1_lo) + _dot(pe_mid, w1_lo))[0:1]
    pre = a + pltpu.roll(bm, nr - 1, axis=0) + pe_term
    hid = _gelu_tanh(pre)
    h_hi, h_mid, h_lo = _split3(hid)
    w2_hi, w2_lo = _split2(w2_ref[0])
    o_ref[0, 0] = (_dot(h_hi, w2_hi) + _dot(h_mid, w2_hi) + _dot(h_lo, w2_hi)
                   + _dot(h_hi, w2_lo) + _dot(h_mid, w2_lo))
    t_hi, t_lo = _split2(w2t_ref[0])
    ot_ref[0, 0] = (_dot_nt(t_hi, h_hi) + _dot_nt(t_hi, h_mid) + _dot_nt(t_hi, h_lo)
                    + _dot_nt(t_lo, h_hi) + _dot_nt(t_lo, h_mid))


def nsa_compress_pallas(nat, pe, w1, w2):
    b, _, s, dh = nat.shape
    nr = s // NSA_CMP_STRIDE
    half = NSA_CMP_STRIDE * dh
    hid = w1.shape[-1]
    x = nat[:, N_CMP:N_CMP + 4].reshape(b, 4, nr, half)
    w1f = w1.reshape(2, NSA_CMP_LEN * dh, hid)
    pef = jnp.zeros((2, 8, NSA_CMP_LEN * dh), F32).at[:, 0].set(pe.reshape(2, NSA_CMP_LEN * dh))
    return pl.pallas_call(
        _nsa_compress_kernel,
        grid=(b, 4),
        in_specs=[pl.BlockSpec((1, 1, nr, half), lambda i, j: (i, j, 0, 0)),
                  pl.BlockSpec((1, half, hid), lambda i, j: (j // 2, 0, 0)),
                  pl.BlockSpec((1, half, hid), lambda i, j: (j // 2, 1, 0)),
                  pl.BlockSpec((1, 8, 2 * half), lambda i, j: (j // 2, 0, 0)),
                  pl.BlockSpec((1, 2 * half, hid), lambda i, j: (j // 2, 0, 0)),
                  pl.BlockSpec((1, hid, dh), lambda i, j: (j // 2, 0, 0)),
                  pl.BlockSpec((1, dh, hid), lambda i, j: (j // 2, 0, 0))],
        out_specs=[pl.BlockSpec((1, 1, nr, dh), lambda i, j: (i, j, 0, 0)),
                   pl.BlockSpec((1, 1, dh, nr), lambda i, j: (i, j, 0, 0))],
        out_shape=[jax.ShapeDtypeStruct((b, 4, nr, dh), F32),
                   jax.ShapeDtypeStruct((b, 4, dh, nr), F32)],
        compiler_params=_cparams(("parallel", "parallel")),
        name="nsa_compress",
    )(x, w1f, w1f, pef, w1f, w2, jnp.swapaxes(w2, 1, 2))


NSA_TQ = 128


NSA_CMP_CHUNK = 256


def _nsa_cmp_kernel(slope_ref, qt_ref, kc_ref, vct_ref, mimpt_ref, oct_ref, selt_ref, cnt_ref, imp_ref):
    tq = NSA_TQ
    g = pl.program_id(1)
    qi = pl.program_id(2)
    q0 = qi * tq
    nr = kc_ref.shape[2]
    nsb = selt_ref.shape[2]
    t_lane = q0 + lax.broadcasted_iota(jnp.int32, (1, tq), 1)
    chunk = min(NSA_CMP_CHUNK, nr)
    tiles_per_chunk = chunk * NSA_CMP_STRIDE // tq

    def branch(n):
        kc_hi, kc_lo = _split2(kc_ref[0, 0, 0:n, :])
        vct = vct_ref[0, 0, :, 0:n].astype(BF16)
        cmp_end = NSA_CMP_STRIDE * lax.broadcasted_iota(jnp.int32, (n, tq), 0) + (NSA_CMP_LEN - 1)
        mask = cmp_end <= t_lane
        rel = (cmp_end - q0).astype(F32)
        psum = jnp.zeros((n, tq), F32)
        qt4 = jnp.concatenate([qt_ref[0, hh] for hh in range(NSA_GROUP)], axis=1)
        st4 = _dot(kc_hi, qt4) + _dot(kc_lo, qt4)
        probs = []
        for hh in range(NSA_GROUP):
            st = st4[:, hh * tq:(hh + 1) * tq] + slope_ref[g * NSA_GROUP + hh] * rel
            st = jnp.where(mask, st, -jnp.inf)
            m = jnp.max(st, axis=0, keepdims=True)
            m = jnp.where(m > -jnp.inf, m, 0.0)
            e = jnp.exp2(st - m)
            p = e * (1.0 / jnp.maximum(jnp.sum(e, axis=0, keepdims=True), 1e-30))
            probs.append(p.astype(BF16))
            psum = psum + p
        o4 = _dot(vct, jnp.concatenate(probs, axis=1))
        for hh in range(NSA_GROUP):
            oct_ref[0, hh] = o4[:, hh * tq:(hh + 1) * tq]
        p_hi, p_mid, p_lo = _split3(psum)
        mimpt = mimpt_ref[:, 0:n]
        imp_ref[...] = _dot(mimpt, p_hi) + _dot(mimpt, p_mid) + _dot(mimpt, p_lo)

    n_chunks = nr // chunk
    for c in range(n_chunks):
        pl.when(jnp.minimum(qi // tiles_per_chunk, n_chunks - 1) == c)(
            functools.partial(branch, (c + 1) * chunk))

    imp = imp_ref[...]
    blk = lax.broadcasted_iota(jnp.int32, (nsb, tq), 0)
    jt = t_lane // NSA_SEL_BLOCK
    forced = (blk == 0) | (blk == jt) | (blk == jt - 1)
    imp = jnp.where(forced, NSA_FORCE_SCORE, imp)
    valid = blk * NSA_SEL_BLOCK <= t_lane
    work0 = jnp.where(valid, imp, -jnp.inf)
    big = jnp.int32(2 ** 30)

    def pick(_, work):
        mx = jnp.max(work, axis=0, keepdims=True)
        first = jnp.min(jnp.where(work == mx, blk, big), axis=0, keepdims=True)
        return jnp.where(blk == first, -jnp.inf, work)

    work = lax.fori_loop(0, min(NSA_TOPK, nsb), pick, work0)
    sel = jnp.where(valid & (work == -jnp.inf), 1.0, 0.0)
    selt_ref[0, 0] = sel
    cnt_ref[0, 0, 0] = _dot_nt(jnp.ones((8, tq), BF16), sel.astype(BF16))


def nsa_cmp_pallas(qvt, cmp_kv, cmp_kvt):
    b, _, dh, s = qvt.shape
    tq = NSA_TQ
    nr = cmp_kv.shape[2]
    nsb = s // NSA_SEL_BLOCK
    ratio = NSA_SEL_BLOCK // NSA_CMP_STRIDE
    front = NSA_CMP_LEN // NSA_CMP_STRIDE - 1
    n_int = ratio + front
    n_idx = np.arange(nr)[None, :]
    j_idx = np.arange(nsb)[:, None]
    mimpt = ((n_idx >= ratio * j_idx - front) & (n_idx <= ratio * j_idx + n_int - 1 - front)
             & (n_idx < nr - 1)).astype(np.float32)
    grid_spec = pltpu.PrefetchScalarGridSpec(
        num_scalar_prefetch=1,
        grid=(b, NSA_KV_HEADS, s // tq),
        in_specs=[pl.BlockSpec((1, NSA_GROUP, dh, tq), lambda i, g, j, sl: (i, TQ_NSA // NSA_GROUP + g, 0, j)),
                  pl.BlockSpec((1, 1, nr, dh), lambda i, g, j, sl: (i, g, 0, 0)),
                  pl.BlockSpec((1, 1, dh, nr), lambda i, g, j, sl: (i, 2 + g, 0, 0)),
                  pl.BlockSpec((nsb, nr), lambda i, g, j, sl: (0, 0))],
        out_specs=[pl.BlockSpec((1, NSA_GROUP, dh, tq), lambda i, g, j, sl: (i, g, 0, j)),
                   pl.BlockSpec((1, 1, nsb, tq), lambda i, g, j, sl: (i, g, 0, j)),
                   pl.BlockSpec((1, 1, 1, 8, nsb), lambda i, g, j, sl: (i, g, j, 0, 0))],
        scratch_shapes=[pltpu.VMEM((nsb, tq), F32)])
    assert nr % min(NSA_CMP_CHUNK, nr) == 0
    return pl.pallas_call(
        _nsa_cmp_kernel,
        grid_spec=grid_spec,
        out_shape=[jax.ShapeDtypeStruct((b, NSA_HEADS, dh, s), F32),
                   jax.ShapeDtypeStruct((b, NSA_KV_HEADS, nsb, s), F32),
                   jax.ShapeDtypeStruct((b, NSA_KV_HEADS, s // tq, 8, nsb), F32)],
        compiler_params=_cparams(("parallel", "parallel", "parallel")),
        name="nsa_compressed_select",
    )(jnp.asarray(_alibi_slopes2(NSA_HEADS)), qvt, cmp_kv, cmp_kvt, jnp.asarray(mimpt, BF16))


NSA_TK = 256
NSA_SEL_TQ = 256
NSA_LANES = NSA_GROUP * NSA_TQ


def _nsa_qaug(qt_ref, srow_ref):
    return jnp.concatenate(
        [jnp.concatenate([qt_ref[0, hh], srow_ref[hh]], axis=0) for hh in range(NSA_GROUP)], axis=1)


def _nsa_sel_kernel(bits_ref, qt_ref, k_ref, vt_ref, selt_ref, srow_ref, slane_ref, o_ref,
                    m_ref, acc_ref, vaug_ref, sa_ref, ma_ref, sb_ref, mb_ref, list_ref, *, nq, words):
    tq, tk = NSA_SEL_TQ, NSA_TK
    per_tile = tk // NSA_SEL_BLOCK
    qi = pl.program_id(2)
    q0 = qi * tq

    @pl.when(qi == 0)
    def _():
        _fill_vaug(vaug_ref, vt_ref)

    base = ((pl.program_id(0) * NSA_KV_HEADS + pl.program_id(1)) * nq + qi) * words
    list_ref[0] = 0

    def note(j, n):
        list_ref[n] = j
        return n + ((bits_ref[base + j // 32] >> (j % 32)) & 1)

    n_tiles = lax.fori_loop(0, qi, note, 0)

    qaug = _nsa_qaug(qt_ref, srow_ref)
    slane = slane_ref[0:1, :]
    diag = q0 // tk

    def scores(kj):
        st = _dot(_ktile(k_ref, kj, tk), qaug)
        rows = [jnp.broadcast_to(selt_ref[0, 0, pl.ds(kj * per_tile + c, 1), :], (NSA_SEL_BLOCK, tq))
                for c in range(per_tile)]
        bias = (jnp.concatenate(rows, axis=0) - 1.0) * (-NEG)
        return st + jnp.concatenate([bias] * NSA_GROUP, axis=1)

    def meta(kj):
        return slane * (kj * tk - q0).astype(F32), _vtile(vaug_ref, kj, tk)

    def own_tile():
        key = lax.broadcasted_iota(jnp.int32, (tk, tq), 0)
        qry = lax.broadcasted_iota(jnp.int32, (tk, tq), 1)
        causal = jnp.concatenate([key <= qry] * NSA_GROUP, axis=1)
        return jnp.where(causal, scores(diag), NEG)

    _init_state_t(m_ref, acc_ref)
    _flash_pipeline([(own_tile, lambda: meta(diag))], n_tiles,
                    lambda i: scores(list_ref[i]), lambda i: meta(list_ref[i]),
                    ((sa_ref, ma_ref), (sb_ref, mb_ref)), m_ref, acc_ref, n_even=False)
    out = _finish_t(acc_ref)
    for hh in range(NSA_GROUP):
        o_ref[0, hh] = out[:, hh * tq:(hh + 1) * tq]


def _nsa_tables(tq):
    slopes2 = _alibi_slopes2(NSA_HEADS)
    srow = _slope_rows(slopes2, tq)
    slane = np.repeat(slopes2.reshape(NSA_KV_HEADS, NSA_GROUP), tq, axis=1)
    slane8 = np.repeat(slane[:, None, :], 8, axis=1)
    return srow, jnp.asarray(slane8, F32)


def _active_tile_bits(cnt, tq, tk):
    b, g, nq128, _, nsb = cnt.shape
    qper, bper = tq // NSA_TQ, tk // NSA_SEL_BLOCK
    nq, nkv = nq128 // qper, nsb // bper
    act = cnt[:, :, :, 0, :].reshape(b, g, nq, qper, nkv, bper).sum(axis=(3, 5)) > 0.0
    words = -(-nkv // 32)
    act = jnp.pad(act, ((0, 0), (0, 0), (0, 0), (0, words * 32 - nkv))).reshape(b, g, nq, words, 32)
    bits = jnp.sum(act.astype(jnp.uint32) << jnp.arange(32, dtype=jnp.uint32), axis=-1, dtype=jnp.uint32)
    return lax.bitcast_convert_type(bits, jnp.int32).reshape(-1), nq, words


def nsa_sel_pallas(qvt, kaug, selt, cnt):
    b, _, dh, s = qvt.shape
    tq = NSA_SEL_TQ
    assert tq == NSA_TK
    lanes = NSA_GROUP * tq
    nsb = s // NSA_SEL_BLOCK
    srow, slane = _nsa_tables(tq)
    bits, nq, words = _active_tile_bits(cnt, tq, NSA_TK)
    grid_spec = pltpu.PrefetchScalarGridSpec(
        num_scalar_prefetch=1,
        grid=(b, NSA_KV_HEADS, nq),
        in_specs=[pl.BlockSpec((1, NSA_GROUP, dh, tq), lambda i, g, j, bt: (i, TQ_NSA // NSA_GROUP + g, 0, j)),
                  pl.BlockSpec((1, 1, s, KAUG), lambda i, g, j, bt: (i, KA_NSA + g, 0, 0)),
                  pl.BlockSpec((1, 1, dh, s), lambda i, g, j, bt: (i, TV_NSA + g, 0, 0)),
                  pl.BlockSpec((1, 1, nsb, tq), lambda i, g, j, bt: (i, g, 0, j)),
                  pl.BlockSpec((NSA_GROUP, dh, tq), lambda i, g, j, bt: (g, 0, 0)),
                  pl.BlockSpec((None, 8, lanes), lambda i, g, j, bt: (g, 0, 0))],
        out_specs=pl.BlockSpec((1, NSA_GROUP, dh, tq), lambda i, g, j, bt: (i, g, 0, j)),
        scratch_shapes=[pltpu.VMEM((1, lanes), F32), pltpu.VMEM((VROWS, lanes), F32),
                        pltpu.VMEM((VROWS, s), BF16)] + _score_buffers(NSA_TK, lanes)
        + [pltpu.SMEM((max(nq, 8),), jnp.int32)])
    return pl.pallas_call(
        functools.partial(_nsa_sel_kernel, nq=nq, words=words),
        grid_spec=grid_spec,
        out_shape=jax.ShapeDtypeStruct((b, NSA_HEADS, dh, s), F32),
        compiler_params=_cparams(("parallel", "parallel", "arbitrary")),
        name="nsa_selected",
    )(bits, qvt, kaug, qvt, selt, srow, slane)


NSA_WT = 128


def _nsa_win_kernel(qt_ref, k_ref, vt_ref, srow_ref, slane_ref, oc_ref, os_ref, gate_ref, o_ref,
                    m_ref, acc_ref, vaug_ref, sa_ref, ma_ref, sb_ref, mb_ref):
    tq = NSA_TQ
    wt = NSA_WT
    qi = pl.program_id(2)

    @pl.when(qi == 0)
    def _():
        _fill_vaug(vaug_ref, vt_ref)

    qaug = _nsa_qaug(qt_ref, srow_ref)
    slane = slane_ref[0:1, :]
    key = lax.broadcasted_iota(jnp.int32, (wt, tq), 0)
    qry = lax.broadcasted_iota(jnp.int32, (wt, tq), 1)
    span = NSA_WINDOW // wt

    def tile(d, keep):
        kj = jnp.maximum(qi - d, 0)

        def scores():
            st = _dot(_ktile(k_ref, kj, wt), qaug)
            if keep is not None:
                st = jnp.where(jnp.concatenate([keep] * NSA_GROUP, axis=1), st, NEG)
            return st

        def meta():
            base = (kj * wt) // POS_PERIOD * POS_PERIOD - qi * tq
            delta = jnp.where(qi - d >= 0, slane * base.astype(F32), SKIP)
            return delta, _vtile(vaug_ref, kj, wt)

        return scores, meta

    _init_state_t(m_ref, acc_ref)
    tiles = [tile(0, key <= qry)] + [tile(d, None) for d in range(1, span)] + [tile(span, key > qry)]
    _flash_pipeline(tiles, None, None, None, ((sa_ref, ma_ref), (sb_ref, mb_ref)), m_ref, acc_ref, True)

    o_w = _finish_t(acc_ref)
    gt = gate_ref[0].T
    for hh in range(NSA_GROUP):
        c0 = 3 * hh
        mix = (gt[c0:c0 + 1] * oc_ref[0, hh] + gt[c0 + 1:c0 + 2] * os_ref[0, hh]
               + gt[c0 + 2:c0 + 3] * o_w[:, hh * tq:(hh + 1) * tq])
        o_ref[0, hh] = mix.astype(o_ref.dtype)


def nsa_win_pallas(qvt, kaug, o_c, o_s, gates):
    b, _, dh, s = qvt.shape
    tq, wt = NSA_TQ, NSA_WT
    assert tq == wt and tq == LANES
    srow, slane = _nsa_tables(tq)
    head_blk = pl.BlockSpec((1, NSA_GROUP, dh, tq), lambda i, g, j: (i, g, 0, j))
    out = pl.pallas_call(
        _nsa_win_kernel,
        grid=(b, NSA_KV_HEADS, s // tq),
        in_specs=[pl.BlockSpec((1, NSA_GROUP, dh, tq), lambda i, g, j: (i, TQ_NSA // NSA_GROUP + g, 0, j)),
                  pl.BlockSpec((1, 1, s, KAUG), lambda i, g, j: (i, KA_NSA + 2 + g, 0, 0)),
                  pl.BlockSpec((1, 1, dh, s), lambda i, g, j: (i, TV_NSA + 2 + g, 0, 0)),
                  pl.BlockSpec((NSA_GROUP, dh, tq), lambda i, g, j: (g, 0, 0)),
                  pl.BlockSpec((None, 8, NSA_LANES), lambda i, g, j: (g, 0, 0)),
                  head_blk, head_blk,
                  pl.BlockSpec((1, tq, LANES), lambda i, g, j: (i, j, g))],
        out_specs=head_blk,
        out_shape=jax.ShapeDtypeStruct((b, NSA_HEADS, dh, s), BF16),
        scratch_shapes=[pltpu.VMEM((1, NSA_LANES), F32), pltpu.VMEM((VROWS, NSA_LANES), F32),
                        pltpu.VMEM((VROWS, s), BF16)] + _score_buffers(wt, NSA_LANES),
        compiler_params=_cparams(("parallel", "parallel", "arbitrary")),
        name="nsa_window_mix",
    )(qvt, kaug, qvt, srow, slane, o_c, o_s, gates)
    return out.reshape(b, NSA_HEADS * dh, s)


def _mem_attn_kernel(q_ref, k_ref, vt_ref, o_ref):
    for hh in range(MEM_HEADS):
        s = _dot_nt(q_ref[0, hh], k_ref[0, hh])
        m = jnp.max(s, axis=-1, keepdims=True)
        e = jnp.exp2(s - m)
        p = e / jnp.sum(e, axis=-1, keepdims=True)
        o_ref[0, hh] = _dot_nt(vt_ref[0, hh], p.astype(BF16)).astype(o_ref.dtype)


def mem_attn_pallas(nat, mem_k, mem_vt, tq=512):
    b, _, s, dh = nat.shape
    n_mem = mem_k.shape[2]
    out = pl.pallas_call(
        _mem_attn_kernel,
        grid=(b, s // tq),
        in_specs=[pl.BlockSpec((1, MEM_HEADS, tq, dh), lambda i, j: (i, N_MEMQ // MEM_HEADS, j, 0)),
                  pl.BlockSpec((1, MEM_HEADS, n_mem, dh), lambda i, j: (i, 0, 0, 0)),
                  pl.BlockSpec((1, MEM_HEADS, dh, n_mem), lambda i, j: (i, 0, 0, 0))],
        out_specs=pl.BlockSpec((1, MEM_HEADS, dh, tq), lambda i, j: (i, 0, 0, j)),
        out_shape=jax.ShapeDtypeStruct((b, MEM_HEADS, dh, s), BF16),
        compiler_params=_cparams(("parallel", "parallel")),
        name="memory_attention",
    )(nat, mem_k, mem_vt)
    return out.reshape(b, MEM_HEADS * dh, s)


def _pad_key_cols(w):
    d, n = w.shape
    w3 = w.reshape(d, n // HEAD_DIM, HEAD_DIM)
    return jnp.concatenate([w3, jnp.zeros_like(w3)], axis=2).reshape(d, 2 * n)


def _in_proj_weights(w_in):
    hd = HEAD_DIM
    sizes = (3 * MOBA_HEADS * hd, NSA_HEADS * hd, 6 * NSA_KV_HEADS * hd, 3 * NSA_HEADS,
             3 * FOX_HEADS * hd, FOX_HEADS, MEM_HEADS * hd)
    offs = np.concatenate([[0], np.cumsum(sizes)])
    moba, nsa_q, nsa_kv, nsa_g, fox, fox_f, mem_q = (w_in[:, offs[i]:offs[i + 1]] for i in range(7))
    mh, fh, g2 = MOBA_HEADS * hd, FOX_HEADS * hd, NSA_KV_HEADS * hd
    moba_q, moba_k, moba_v = moba[:, :mh], moba[:, mh:2 * mh], moba[:, 2 * mh:]
    fox_q, fox_k, fox_v = fox[:, :fh], fox[:, fh:2 * fh], fox[:, 2 * fh:]
    k_cmp, v_cmp, k_slc, v_slc, k_win, v_win = (nsa_kv[:, i * g2:(i + 1) * g2] for i in range(6))
    w_t = jnp.concatenate([moba_q, nsa_q, fox_q, moba_v, v_slc, v_win, fox_v], axis=1).T.astype(BF16)
    t_scale = np.ones((T_SLOTS * hd,), np.float32)
    t_scale[:TV_MOBA * hd] = Q_SCALE
    w_ka = _pad_key_cols(jnp.concatenate([moba_k, k_slc, k_win], axis=1)).astype(BF16)
    w_kf = _pad_key_cols(fox_k).astype(BF16)
    w_nat = jnp.concatenate([k_cmp, v_cmp, mem_q], axis=1).astype(BF16)
    n_scale = np.ones((N_SLOTS * hd,), np.float32)
    n_scale[N_MEMQ * hd:] = Q_SCALE
    return w_t, jnp.asarray(t_scale), w_ka, w_kf, w_nat, jnp.asarray(n_scale), nsa_g, fox_f


def _mixer(h32, h16, mem16, w_in, b_forget, w_mem_kv, cmp_pe, cmp_w1, cmp_w2):
    b, s, d = h16.shape
    tm = min(1024, s)
    w_t, t_scale, w_ka, w_kf, w_nat, n_scale, w_gate, w_forget = _in_proj_weights(w_in)
    gates, caug = gates_pallas(h32, w_gate, w_forget, b_forget)
    qvt = proj_t_pallas(h16, w_t, t_scale, tm=tm, heads_per_step=13)
    k_alibi = proj_kaug_pallas(h16, w_ka, None, tm=tm, heads_per_step=6)
    k_fox = proj_kaug_pallas(h16, w_kf, caug, tm=tm, heads_per_step=6)
    nat = proj_heads_pallas(h16, w_nat, n_scale, tm=tm, heads_per_step=8)
    n_mem = mem16.shape[1]
    mk = MEM_HEADS * HEAD_DIM
    mem_k = proj_heads_pallas(mem16, w_mem_kv[:, :mk].astype(BF16), jnp.ones((mk,), F32),
                              tm=n_mem, heads_per_step=MEM_HEADS)
    mem_vt = proj_t_pallas(mem16, w_mem_kv[:, mk:].T.astype(BF16), jnp.ones((mk,), F32),
                           tm=n_mem, heads_per_step=MEM_HEADS)
    o_moba = moba_pallas(qvt, k_alibi)
    o_fox = fox_pallas(qvt, k_fox)
    cmp_kv, cmp_kvt = nsa_compress_pallas(nat, cmp_pe, cmp_w1, cmp_w2)
    o_c, selt, cnt = nsa_cmp_pallas(qvt, cmp_kv, cmp_kvt)
    o_s = nsa_sel_pallas(qvt, k_alibi, selt, cnt)
    o_nsa = nsa_win_pallas(qvt, k_alibi, o_c, o_s, gates)
    o_mem = mem_attn_pallas(nat, mem_k, mem_vt)
    return [o_moba, o_nsa, o_fox, o_mem]


def kernel(x, mem, emb_ln_g, emb_ln_b, w_in, b_forget, w_mem_kv, nsa_cmp_pe, nsa_cmp_w1, nsa_cmp_w2,
           w_out, ln1_g, ln1_b, ffn_w_up, ffn_conv_w, ffn_conv_b, ffn_w_down, ln2_g, ln2_b):
    b, s, d = x.shape
    depth = w_in.shape[0]
    dff = ffn_w_down.shape[1]
    mem16 = mem.astype(BF16)
    h32, h16 = layer_norm_pallas(x.reshape(b * s, d), emb_ln_g, emb_ln_b)
    for l in range(depth):
        heads = _mixer(h32.reshape(b, s, d), h16.reshape(b, s, d), mem16, w_in[l], b_forget[l], w_mem_kv[l],
                       nsa_cmp_pe[l], nsa_cmp_w1[l], nsa_cmp_w2[l])
        h32, h16 = out_proj_ln_pallas(heads, w_out[l].astype(BF16), h32.reshape(b, s, d),
                                      ln1_g[l], ln1_b[l], tm=512)
        a = ffn_up_pallas(h16, ffn_w_up[l].astype(BF16), ffn_conv_w[l], ffn_conv_b[l],
                          tm=min(1024, s), tn=512)
        h32, h16 = matmul_ln_resident_pallas(a.reshape(b * s, dff), ffn_w_down[l].astype(BF16),
                                             h32.reshape(b * s, d), ln2_g[l], ln2_b[l], tm=256)
    return h32.reshape(b, s, d)
```

```python
import functools
import math

import jax
import jax.numpy as jnp
import ml_dtypes
import numpy as np
from jax import lax
from jax.experimental import pallas as pl
from jax.experimental.pallas import tpu as pltpu

F32 = jnp.float32
BF16 = jnp.bfloat16

HEAD_DIM = 64
MOBA_HEADS = 8
NSA_HEADS = 8
NSA_KV_HEADS = 2
NSA_GROUP = NSA_HEADS // NSA_KV_HEADS
FOX_HEADS = 12
MEM_HEADS = 4
MOBA_BLOCK = 256
MOBA_TOPK = 3
NSA_CMP_LEN = 32
NSA_CMP_STRIDE = 16
NSA_SEL_BLOCK = 64
NSA_TOPK = 16
NSA_WINDOW = 512
NSA_FORCE_SCORE = 1.0e4
CONV_WIDTH = 3
LN_EPS = 1e-5
DEPTH = 2
DEEPNORM_ALPHA = (2 * DEPTH) ** 0.25

LOG2E = math.log2(math.e)
Q_SCALE = HEAD_DIM ** -0.5 * LOG2E
NEG = -1.0e30
SKIP = -3.0e38
VMEM_LIMIT = 56 * 1024 * 1024
LANES = 128
KAUG = 2 * HEAD_DIM
VROWS = HEAD_DIM + 16
POS_PERIOD = 256

TQ_MOBA, TQ_NSA, TQ_FOX = 0, 8, 16
TV_MOBA, TV_NSA, TV_FOX = 28, 36, 40
T_SLOTS = 52
KA_MOBA, KA_NSA = 0, 8
KA_SLOTS = 12
N_CMP, N_MEMQ = 0, 4
N_SLOTS = 8


def _cparams(sem):
    return pltpu.CompilerParams(dimension_semantics=sem, vmem_limit_bytes=VMEM_LIMIT)


def _split2(x):
    hi = x.astype(BF16)
    return hi, (x - hi.astype(F32)).astype(BF16)


def _split3(x):
    hi = x.astype(BF16)
    r1 = x - hi.astype(F32)
    mid = r1.astype(BF16)
    lo = (r1 - mid.astype(F32)).astype(BF16)
    return hi, mid, lo


def _np_split3(x):
    x = np.asarray(x, np.float32)
    hi = x.astype(ml_dtypes.bfloat16).astype(np.float32)
    r1 = x - hi
    mid = r1.astype(ml_dtypes.bfloat16).astype(np.float32)
    lo = (r1 - mid).astype(ml_dtypes.bfloat16).astype(np.float32)
    return hi, mid, lo


def _dot_nt(a, b):
    return lax.dot_general(a, b, (((1,), (1,)), ((), ())), preferred_element_type=F32)


def _dot_tn(a, b):
    return lax.dot_general(a, b, (((0,), (0,)), ((), ())), preferred_element_type=F32)


def _dot(a, b):
    return jnp.dot(a, b, preferred_element_type=F32)


def _layer_norm_rows(x, g, b):
    mu = jnp.mean(x, axis=-1, keepdims=True)
    xc = x - mu
    var = jnp.mean(xc * xc, axis=-1, keepdims=True)
    return xc * lax.rsqrt(var + LN_EPS) * g + b


def _alibi_slopes2(n):
    return (np.exp2(-8.0 * np.arange(1, n + 1, dtype=np.float64) / n) * LOG2E).astype(np.float32)


def _slope_rows(slopes2, lanes):
    pieces = np.stack(_np_split3(slopes2), axis=1)
    rows = np.zeros((len(slopes2), HEAD_DIM, lanes), np.float32)
    rows[:, :3, :] = pieces[:, :, None]
    return jnp.asarray(rows, BF16)


def _ln_kernel(x_ref, g_ref, b_ref, o32_ref, o16_ref):
    y = _layer_norm_rows(x_ref[...], g_ref[...], b_ref[...])
    o32_ref[...] = y
    o16_ref[...] = y.astype(BF16)


def layer_norm_pallas(x, g, b, tm=512):
    m, d = x.shape
    return pl.pallas_call(
        _ln_kernel,
        grid=(m // tm,),
        in_specs=[pl.BlockSpec((tm, d), lambda i: (i, 0)),
                  pl.BlockSpec((1, d), lambda i: (0, 0)),
                  pl.BlockSpec((1, d), lambda i: (0, 0))],
        out_specs=[pl.BlockSpec((tm, d), lambda i: (i, 0)),
                   pl.BlockSpec((tm, d), lambda i: (i, 0))],
        out_shape=[jax.ShapeDtypeStruct((m, d), F32), jax.ShapeDtypeStruct((m, d), BF16)],
        compiler_params=_cparams(("parallel",)),
        name="layer_norm",
    )(x, g.reshape(1, d), b.reshape(1, d))


def _gates_kernel(h_ref, wg_ref, wf_ref, bf_ref, tri_ref, place_ref, g_ref, caug_ref, carry_ref):
    si = pl.program_id(1)

    @pl.when(si == 0)
    def _():
        carry_ref[...] = jnp.zeros_like(carry_ref)

    h_hi, h_lo = _split2(h_ref[0])
    wg_hi, wg_lo = _split2(wg_ref[...])
    g = _dot(h_hi, wg_hi) + _dot(h_lo, wg_hi) + _dot(h_hi, wg_lo)
    g_ref[0] = 1.0 / (1.0 + jnp.exp(-g))
    wf_hi, wf_lo = _split2(wf_ref[...])
    x = _dot(h_hi, wf_hi) + _dot(h_lo, wf_hi) + _dot(h_hi, wf_lo) + bf_ref[...]
    logf = jnp.minimum(x, 0.0) - jnp.log(1.0 + jnp.exp(-jnp.abs(x)))
    tri = tri_ref[...]
    l_hi, l_mid, l_lo = _split3(logf)
    c = _dot(tri, l_hi) + _dot(tri, l_mid) + _dot(tri, l_lo) + carry_ref[0:1, :]
    carry_ref[...] = jnp.broadcast_to(c[-1:, :], carry_ref.shape)
    n_hi, n_mid, n_lo = _split3(-LOG2E * c)
    caug = _dot(n_hi, place_ref[0]) + _dot(n_mid, place_ref[1]) + _dot(n_lo, place_ref[2])
    caug_ref[0] = caug.astype(BF16)


def gates_pallas(h3, w_gate, w_forget, b_forget, t=512):
    b, s, d = h3.shape
    gw = 3 * NSA_GROUP
    ng = NSA_KV_HEADS * LANES
    wg = jnp.zeros((d, ng), F32)
    for g in range(NSA_KV_HEADS):
        wg = wg.at[:, g * LANES:g * LANES + gw].set(w_gate[:, g * gw:(g + 1) * gw])
    wf = jnp.zeros((d, LANES), F32).at[:, :FOX_HEADS].set(w_forget)
    bf = jnp.zeros((1, LANES), F32).at[0, :FOX_HEADS].set(b_forget)
    tri = (np.arange(t)[None, :] <= np.arange(t)[:, None]).astype(np.float32)
    place = np.zeros((3, LANES, FOX_HEADS * KAUG), np.float32)
    for piece in range(3):
        for hh in range(FOX_HEADS):
            place[piece, hh, hh * KAUG + HEAD_DIM + piece] = 1.0
    nc = FOX_HEADS * KAUG
    return pl.pallas_call(
        _gates_kernel,
        grid=(b, s // t),
        in_specs=[pl.BlockSpec((1, t, d), lambda i, j: (i, j, 0)),
                  pl.BlockSpec((d, ng), lambda i, j: (0, 0)),
                  pl.BlockSpec((d, LANES), lambda i, j: (0, 0)),
                  pl.BlockSpec((1, LANES), lambda i, j: (0, 0)),
                  pl.BlockSpec((t, t), lambda i, j: (0, 0)),
                  pl.BlockSpec((3, LANES, nc), lambda i, j: (0, 0, 0))],
        out_specs=[pl.BlockSpec((1, t, ng), lambda i, j: (i, j, 0)),
                   pl.BlockSpec((1, t, nc), lambda i, j: (i, j, 0))],
        out_shape=[jax.ShapeDtypeStruct((b, s, ng), F32),
                   jax.ShapeDtypeStruct((b, s, nc), BF16)],
        scratch_shapes=[pltpu.VMEM((8, LANES), F32)],
        compiler_params=_cparams(("parallel", "arbitrary")),
        name="gates_cumsum",
    )(h3, wg, wf, bf, jnp.asarray(tri, BF16), jnp.asarray(place, BF16))


def _proj_heads_kernel(x_ref, w_ref, sc_ref, o_ref, *, heads_per_step):
    acc = _dot(x_ref[0], w_ref[...]) * sc_ref[...]
    for j in range(heads_per_step):
        o_ref[0, j] = acc[:, j * HEAD_DIM:(j + 1) * HEAD_DIM].astype(o_ref.dtype)


def proj_heads_pallas(x3, w, colscale, tm, heads_per_step):
    b, s, d = x3.shape
    n = w.shape[1]
    tn = heads_per_step * HEAD_DIM
    return pl.pallas_call(
        functools.partial(_proj_heads_kernel, heads_per_step=heads_per_step),
        grid=(b, s // tm, n // tn),
        in_specs=[pl.BlockSpec((1, tm, d), lambda i, j, k: (i, j, 0)),
                  pl.BlockSpec((d, tn), lambda i, j, k: (0, k)),
                  pl.BlockSpec((1, tn), lambda i, j, k: (0, k))],
        out_specs=pl.BlockSpec((1, heads_per_step, tm, HEAD_DIM), lambda i, j, k: (i, k, j, 0)),
        out_shape=jax.ShapeDtypeStruct((b, n // HEAD_DIM, s, HEAD_DIM), BF16),
        compiler_params=_cparams(("parallel", "parallel", "arbitrary")),
        name="proj_heads",
    )(x3, w, colscale.reshape(1, n))


def _proj_t_kernel(x_ref, wt_ref, sc_ref, o_ref, *, heads_per_step):
    acc = _dot_nt(wt_ref[...], x_ref[0]) * sc_ref[...]
    o_ref[0] = acc.reshape(heads_per_step, HEAD_DIM, acc.shape[1]).astype(o_ref.dtype)


def proj_t_pallas(x3, wt, rowscale, tm, heads_per_step):
    b, s, d = x3.shape
    n = wt.shape[0]
    tn = heads_per_step * HEAD_DIM
    return pl.pallas_call(
        functools.partial(_proj_t_kernel, heads_per_step=heads_per_step),
        grid=(b, s // tm, n // tn),
        in_specs=[pl.BlockSpec((1, tm, d), lambda i, j, k: (i, j, 0)),
                  pl.BlockSpec((tn, d), lambda i, j, k: (k, 0)),
                  pl.BlockSpec((tn, 1), lambda i, j, k: (k, 0))],
        out_specs=pl.BlockSpec((1, heads_per_step, HEAD_DIM, tm), lambda i, j, k: (i, k, 0, j)),
        out_shape=jax.ShapeDtypeStruct((b, n // HEAD_DIM, HEAD_DIM, s), BF16),
        compiler_params=_cparams(("parallel", "parallel", "arbitrary")),
        name="proj_transposed",
    )(x3, wt, rowscale.reshape(n, 1))


def _proj_kaug_kernel(x_ref, w_ref, *rest, heads_per_step, positional):
    o_ref = rest[-1]
    acc = _dot(x_ref[0], w_ref[...])
    tm = acc.shape[0]
    if positional:
        pos = (pl.program_id(1) * tm + lax.broadcasted_iota(jnp.int32, acc.shape, 0)) % POS_PERIOD
        lane = lax.broadcasted_iota(jnp.int32, acc.shape, 1) % KAUG
        acc = acc + jnp.where((lane >= HEAD_DIM) & (lane < HEAD_DIM + 3), pos.astype(F32), 0.0)
    else:
        acc = acc + rest[0][0].astype(F32)
    for j in range(heads_per_step):
        o_ref[0, j] = acc[:, j * KAUG:(j + 1) * KAUG].astype(o_ref.dtype)


def proj_kaug_pallas(x3, w, aug, tm, heads_per_step):
    b, s, d = x3.shape
    n = w.shape[1]
    tn = heads_per_step * KAUG
    in_specs = [pl.BlockSpec((1, tm, d), lambda i, j, k: (i, j, 0)),
                pl.BlockSpec((d, tn), lambda i, j, k: (0, k))]
    args = [x3, w]
    if aug is not None:
        in_specs.append(pl.BlockSpec((1, tm, tn), lambda i, j, k: (i, j, k)))
        args.append(aug)
    return pl.pallas_call(
        functools.partial(_proj_kaug_kernel, heads_per_step=heads_per_step, positional=aug is None),
        grid=(b, s // tm, n // tn),
        in_specs=in_specs,
        out_specs=pl.BlockSpec((1, heads_per_step, tm, KAUG), lambda i, j, k: (i, k, j, 0)),
        out_shape=jax.ShapeDtypeStruct((b, n // KAUG, s, KAUG), BF16),
        compiler_params=_cparams(("parallel", "parallel", "arbitrary")),
        name="proj_keys_aug",
    )(*args)


def _matmul_ln_kernel(x_ref, w_ref, r_ref, g_ref, b_ref, o32_ref, o16_ref, acc_ref):
    k = pl.program_id(1)

    @pl.when(k == 0)
    def _():
        acc_ref[...] = jnp.zeros_like(acc_ref)

    acc_ref[...] += _dot(x_ref[...], w_ref[...])

    @pl.when(k == pl.num_programs(1) - 1)
    def _():
        y = _layer_norm_rows(DEEPNORM_ALPHA * r_ref[...] + acc_ref[...], g_ref[...], b_ref[...])
        o32_ref[...] = y
        o16_ref[...] = y.astype(BF16)


def matmul_ln_pallas(x, w, res, g, b, tm, tk):
    m, kk = x.shape
    d = w.shape[1]
    return pl.pallas_call(
        _matmul_ln_kernel,
        grid=(m // tm, kk // tk),
        in_specs=[pl.BlockSpec((tm, tk), lambda i, k: (i, k)),
                  pl.BlockSpec((tk, d), lambda i, k: (k, 0)),
                  pl.BlockSpec((tm, d), lambda i, k: (i, 0)),
                  pl.BlockSpec((1, d), lambda i, k: (0, 0)),
                  pl.BlockSpec((1, d), lambda i, k: (0, 0))],
        out_specs=[pl.BlockSpec((tm, d), lambda i, k: (i, 0)),
                   pl.BlockSpec((tm, d), lambda i, k: (i, 0))],
        out_shape=[jax.ShapeDtypeStruct((m, d), F32), jax.ShapeDtypeStruct((m, d), BF16)],
        scratch_shapes=[pltpu.VMEM((tm, d), F32)],
        compiler_params=_cparams(("parallel", "arbitrary")),
        name="matmul_ln",
    )(x, w, res, g.reshape(1, d), b.reshape(1, d))


def _matmul_ln_resident_kernel(x_ref, w_ref, r_ref, g_ref, b_ref, o32_ref, o16_ref):
    y = _layer_norm_rows(DEEPNORM_ALPHA * r_ref[...] + _dot(x_ref[...], w_ref[...]), g_ref[...], b_ref[...])
    o32_ref[...] = y
    o16_ref[...] = y.astype(BF16)


def matmul_ln_resident_pallas(x, w, res, g, b, tm):
    m, kk = x.shape
    d = w.shape[1]
    return pl.pallas_call(
        _matmul_ln_resident_kernel,
        grid=(m // tm,),
        in_specs=[pl.BlockSpec((tm, kk), lambda i: (i, 0)),
                  pl.BlockSpec((kk, d), lambda i: (0, 0), pipeline_mode=pl.Buffered(1)),
                  pl.BlockSpec((tm, d), lambda i: (i, 0)),
                  pl.BlockSpec((1, d), lambda i: (0, 0)),
                  pl.BlockSpec((1, d), lambda i: (0, 0))],
        out_specs=[pl.BlockSpec((tm, d), lambda i: (i, 0)),
                   pl.BlockSpec((tm, d), lambda i: (i, 0))],
        out_shape=[jax.ShapeDtypeStruct((m, d), F32), jax.ShapeDtypeStruct((m, d), BF16)],
        compiler_params=_cparams(("parallel",)),
        name="matmul_ln_resident",
    )(x, w, res, g.reshape(1, d), b.reshape(1, d))


def _out_proj_ln_kernel(*refs, widths):
    n = len(widths)
    x_refs, (w_ref, r_ref, g_ref, b_ref, o32_ref, o16_ref) = refs[:n], refs[n:]
    acc = None
    off = 0
    for x_ref, width in zip(x_refs, widths):
        part = _dot_tn(x_ref[0], w_ref[off:off + width, :])
        acc = part if acc is None else acc + part
        off += width
    y = _layer_norm_rows(DEEPNORM_ALPHA * r_ref[0] + acc, g_ref[...], b_ref[...])
    o32_ref[0] = y
    o16_ref[0] = y.astype(BF16)


def out_proj_ln_pallas(xts, w, res3, g, b, tm):
    bsz, s, d = res3.shape
    widths = tuple(x.shape[1] for x in xts)
    in_specs = [pl.BlockSpec((1, wd, tm), lambda i, j: (i, 0, j)) for wd in widths]
    in_specs += [pl.BlockSpec((w.shape[0], d), lambda i, j: (0, 0)),
                 pl.BlockSpec((1, tm, d), lambda i, j: (i, j, 0)),
                 pl.BlockSpec((1, d), lambda i, j: (0, 0)),
                 pl.BlockSpec((1, d), lambda i, j: (0, 0))]
    return pl.pallas_call(
        functools.partial(_out_proj_ln_kernel, widths=widths),
        grid=(bsz, s // tm),
        in_specs=in_specs,
        out_specs=[pl.BlockSpec((1, tm, d), lambda i, j: (i, j, 0)),
                   pl.BlockSpec((1, tm, d), lambda i, j: (i, j, 0))],
        out_shape=[jax.ShapeDtypeStruct((bsz, s, d), F32), jax.ShapeDtypeStruct((bsz, s, d), BF16)],
        compiler_params=_cparams(("parallel", "parallel")),
        name="out_proj_ln",
    )(*xts, w, res3, g.reshape(1, d), b.reshape(1, d))


HALO = 16
FFN_CHUNK = 256
FFN_ROWS = 256


def _gelu_tanh(x):
    return 0.5 * x * (1.0 + jnp.tanh(math.sqrt(2.0 / math.pi) * (x + 0.044715 * x * x * x)))


def _ffn_up_kernel(x_ref, xh_ref, wu_ref, wg_ref, cw_ref, cb_ref, o_ref):
    j = pl.program_id(1)
    tm, tn = o_ref.shape[1], o_ref.shape[2]
    rows = min(FFN_ROWS, tm)
    first = jnp.where(j > 0, 1.0, 0.0)
    row = lax.broadcasted_iota(jnp.int32, (rows, FFN_CHUNK), 0)
    for c in range(tn // FFN_CHUNK):
        cols = slice(c * FFN_CHUNK, (c + 1) * FFN_CHUNK)
        wu = wu_ref[:, cols]
        wg = wg_ref[:, cols]
        cw = cw_ref[:, cols]
        cb = cb_ref[:, cols]
        gh = _dot(xh_ref[0], wg) * first
        prev1 = gh[HALO - 1:HALO, :]
        prev2 = gh[HALO - 2:HALO - 1, :]
        for r in range(tm // rows):
            x = x_ref[0, r * rows:(r + 1) * rows, :]
            u = _dot(x, wu)
            g = _dot(x, wg)
            g_m1 = jnp.where(row == 0, prev1, pltpu.roll(g, 1, axis=0))
            g_m2 = jnp.where(row == 0, prev2, jnp.where(row == 1, prev1, pltpu.roll(g, 2, axis=0)))
            gc = cb + cw[0:1] * g_m2 + cw[1:2] * g_m1 + cw[2:3] * g
            o_ref[0, r * rows:(r + 1) * rows, cols] = (_gelu_tanh(gc) * u).astype(o_ref.dtype)
            prev1 = g[rows - 1:rows, :]
            prev2 = g[rows - 2:rows - 1, :]


def ffn_up_pallas(x3, w_up, conv_w, conv_b, tm, tn):
    b, s, d = x3.shape
    dff = w_up.shape[1] // 2
    nt = dff // tn
    hb = tm // HALO
    cw = jnp.zeros((8, dff), F32).at[:CONV_WIDTH].set(conv_w)
    return pl.pallas_call(
        _ffn_up_kernel,
        grid=(b, s // tm, nt),
        in_specs=[pl.BlockSpec((1, tm, d), lambda i, j, k: (i, j, 0)),
                  pl.BlockSpec((1, HALO, d), lambda i, j, k: (i, jnp.maximum(j * hb - 1, 0), 0)),
                  pl.BlockSpec((d, tn), lambda i, j, k: (0, k)),
                  pl.BlockSpec((d, tn), lambda i, j, k: (0, k + nt)),
                  pl.BlockSpec((8, tn), lambda i, j, k: (0, k)),
                  pl.BlockSpec((1, tn), lambda i, j, k: (0, k))],
        out_specs=pl.BlockSpec((1, tm, tn), lambda i, j, k: (i, j, k)),
        out_shape=jax.ShapeDtypeStruct((b, s, dff), BF16),
        compiler_params=_cparams(("parallel", "parallel", "arbitrary")),
        name="ffn_up",
    )(x3, x3, w_up, w_up, cw, conv_b.reshape(1, dff))


def _online_step_t(st, delta, vaug, m_ref, acc_ref):
    m_prev = m_ref[...]
    m_cur = jnp.max(st, axis=0, keepdims=True) + delta
    m_new = jnp.maximum(m_prev, m_cur)
    alpha = jnp.exp2(m_prev - m_new)
    pt = jnp.exp2(st - (m_new - delta)).astype(BF16)
    acc_ref[...] = alpha * acc_ref[...] + _dot(vaug, pt)
    m_ref[...] = m_new


def _init_state_t(m_ref, acc_ref):
    m_ref[...] = jnp.full(m_ref.shape, NEG, F32)
    acc_ref[...] = jnp.zeros(acc_ref.shape, F32)


def _finish_t(acc_ref):
    acc = acc_ref[...]
    return acc[:HEAD_DIM] / acc[HEAD_DIM:HEAD_DIM + 1]


def _fill_vaug(vaug_ref, vt_ref):
    s = vaug_ref.shape[1]
    vaug_ref[0:HEAD_DIM, :] = vt_ref[0, 0]
    pad = lax.broadcasted_iota(jnp.int32, (VROWS - HEAD_DIM, s), 0)
    vaug_ref[HEAD_DIM:VROWS, :] = jnp.where(pad == 0, 1.0, 0.0).astype(BF16)


def _ktile(ref, idx, size):
    return ref[0, 0, pl.ds(pl.multiple_of(idx * size, size), size), :]


def _vtile(ref, idx, size):
    return ref[:, pl.ds(pl.multiple_of(idx * size, size), size)]


def _stage_scores(st, s_ref, mc_ref):
    nsub = mc_ref.shape[0]
    s_ref[...] = st
    if nsub == 1:
        mc_ref[...] = jnp.max(st, axis=0, keepdims=True)
    else:
        mc_ref[...] = jnp.max(st.reshape(nsub, st.shape[0] // nsub, st.shape[1]), axis=1)


def _stage_update(s_ref, mc_ref, delta, vaug, m_ref, acc_ref):
    nsub = mc_ref.shape[0]
    m_prev = m_ref[...]
    m_new = jnp.maximum(m_prev, jnp.max(mc_ref[...] + delta, axis=0, keepdims=True))
    alpha = jnp.exp2(m_prev - m_new)
    shift = m_new - delta
    if nsub == 1:
        pt = jnp.exp2(s_ref[...] - shift)
    else:
        tk, lanes = s_ref.shape
        pt = jnp.exp2(s_ref[...].reshape(nsub, tk // nsub, lanes) - shift[:, None, :]).reshape(tk, lanes)
    acc_ref[...] = alpha * acc_ref[...] + _dot(vaug, pt.astype(BF16))
    m_ref[...] = m_new


def _score_buffers(tk, lanes, nsub=1):
    return [pltpu.VMEM((tk, lanes), F32), pltpu.VMEM((nsub, lanes), F32),
            pltpu.VMEM((tk, lanes), F32), pltpu.VMEM((nsub, lanes), F32)]


def _flash_pipeline(lead, n_loop, scores, meta, bufs, m_ref, acc_ref, n_even):
    _stage_scores(lead[0][0](), *bufs[0])
    for i in range(1, len(lead)):
        _stage_scores(lead[i][0](), *bufs[i % 2])
        _stage_update(*bufs[(i - 1) % 2], *lead[i - 1][1](), m_ref, acc_ref)
    cur = (len(lead) - 1) % 2
    nxt = 1 - cur
    if scores is None:
        _stage_update(*bufs[cur], *lead[-1][1](), m_ref, acc_ref)
        return
    _stage_scores(scores(0), *bufs[nxt])
    _stage_update(*bufs[cur], *lead[-1][1](), m_ref, acc_ref)
    last = jnp.maximum(n_loop - 1, 0)

    def body(kp, carry):
        k0 = 2 * kp
        k1 = k0 + 1
        _stage_scores(scores(jnp.minimum(k1, last)), *bufs[cur])
        _stage_update(*bufs[nxt], *meta(k0), m_ref, acc_ref)
        _stage_scores(scores(jnp.minimum(k0 + 2, last)), *bufs[nxt])
        delta1, vaug1 = meta(jnp.minimum(k1, last))
        if not n_even:
            delta1 = jnp.where(k1 < n_loop, delta1, SKIP)
        _stage_update(*bufs[cur], delta1, vaug1, m_ref, acc_ref)
        return carry

    lax.fori_loop(0, (n_loop + 1) // 2, body, 0)


FOX_TQ = 1024
FOX_TK = 512


def _fox_kernel(qt_ref, k_ref, vt_ref, o_ref, m_ref, acc_ref, vaug_ref, sa_ref, ma_ref, sb_ref, mb_ref):
    tq, tk = o_ref.shape[3], FOX_TK
    nd = tq // tk
    qi = pl.program_id(2)

    @pl.when(qi == 0)
    def _():
        _fill_vaug(vaug_ref, vt_ref)

    ones3 = jnp.where(lax.broadcasted_iota(jnp.int32, (HEAD_DIM, tq), 0) < 3, 1.0, 0.0).astype(BF16)
    qaug = jnp.concatenate([qt_ref[0, 0], ones3], axis=0)
    _init_state_t(m_ref, acc_ref)
    key = lax.broadcasted_iota(jnp.int32, (tk, tq), 0)
    qry = lax.broadcasted_iota(jnp.int32, (tk, tq), 1)

    def scores(kj):
        return _dot(_ktile(k_ref, kj, tk), qaug)

    def meta(kj):
        return 0.0, _vtile(vaug_ref, kj, tk)

    lead = [(functools.partial(lambda d: jnp.where(key + d * tk <= qry, scores(qi * nd + d), NEG), d),
             functools.partial(lambda d: meta(qi * nd + d), d)) for d in range(nd)]
    _flash_pipeline(lead, qi * nd, scores, meta, ((sa_ref, ma_ref), (sb_ref, mb_ref)), m_ref, acc_ref,
                    n_even=nd % 2 == 0)
    o_ref[0, 0] = _finish_t(acc_ref).astype(o_ref.dtype)


def fox_pallas(qvt, kaug):
    b, _, dh, s = qvt.shape
    tq = min(FOX_TQ, s)
    assert tq % FOX_TK == 0
    out = pl.pallas_call(
        _fox_kernel,
        grid=(b, FOX_HEADS, s // tq),
        in_specs=[pl.BlockSpec((1, 1, dh, tq), lambda i, h, j: (i, TQ_FOX + h, 0, j)),
                  pl.BlockSpec((1, 1, s, KAUG), lambda i, h, j: (i, h, 0, 0)),
                  pl.BlockSpec((1, 1, dh, s), lambda i, h, j: (i, TV_FOX + h, 0, 0))],
        out_specs=pl.BlockSpec((1, 1, dh, tq), lambda i, h, j: (i, h, 0, j)),
        out_shape=jax.ShapeDtypeStruct((b, FOX_HEADS, dh, s), BF16),
        scratch_shapes=[pltpu.VMEM((1, tq), F32), pltpu.VMEM((VROWS, tq), F32),
                        pltpu.VMEM((VROWS, s), BF16)] + _score_buffers(FOX_TK, tq),
        compiler_params=_cparams(("parallel", "parallel", "arbitrary")),
        name="fox_attention",
    )(qvt, kaug, qvt)
    return out.reshape(b, FOX_HEADS * dh, s)


def _moba_kernel(slope_ref, qt_ref, k_ref, vt_ref, ind_ref, srow_ref, o_ref,
                 m_ref, acc_ref, vaug_ref, km_ref, sel_ref, sa_ref, ma_ref, sb_ref, mb_ref):
    t = MOBA_BLOCK
    tq = o_ref.shape[3]
    nd = tq // t
    h = pl.program_id(1)
    qi = pl.program_id(2)
    slope2 = slope_ref[h]

    @pl.when(qi == 0)
    def _():
        _fill_vaug(vaug_ref, vt_ref)
        km_ref[...] = _dot(ind_ref[...], k_ref[0, 0]) * (1.0 / MOBA_BLOCK)

    qt = qt_ref[0, 0]
    km_hi, km_lo = _split2(km_ref[...])
    q0 = jnp.concatenate([qt, jnp.zeros_like(qt)], axis=0)
    gate = _dot(km_hi, q0) + _dot(km_lo, q0)
    blk = lax.broadcasted_iota(jnp.int32, gate.shape, 0)
    lane_blk = lax.broadcasted_iota(jnp.int32, (1, tq), 1) // t
    valid = blk < qi * nd + lane_blk
    work = jnp.where(valid, gate, -jnp.inf)
    sel = jnp.zeros(gate.shape, F32)
    big = jnp.int32(2 ** 30)
    for _ in range(MOBA_TOPK):
        mx = jnp.max(work, axis=0, keepdims=True)
        first = jnp.min(jnp.where(work == mx, blk, big), axis=0, keepdims=True)
        hit = blk == first
        sel = jnp.where(hit, 1.0, sel)
        work = jnp.where(hit, -jnp.inf, work)
    sel_ref[...] = jnp.where(valid, sel, 0.0)

    qaug = jnp.concatenate([qt, srow_ref[...]], axis=0)
    _init_state_t(m_ref, acc_ref)
    tk = sa_ref.shape[0]
    nsub = tk // t
    ntile = tq // tk
    key = lax.broadcasted_iota(jnp.int32, (tk, tq), 0)
    qry = lax.broadcasted_iota(jnp.int32, (tk, tq), 1)
    sub = lax.broadcasted_iota(jnp.int32, (nsub, tq), 0)

    def tile_constant(kj, d):
        sees = jnp.concatenate([sel_ref[pl.ds(kj * nsub + c, 1), :] for c in range(nsub)], axis=0) > 0.0
        if d is not None:
            sees = sees | (lane_blk == d * nsub + sub)
        offset = ((kj * nsub + sub - qi * nd) * t).astype(F32)
        return jnp.where(sees, slope2 * offset, SKIP)

    def scores(kj):
        return _dot(_ktile(k_ref, kj, tk), qaug)

    def meta(kj):
        return tile_constant(kj, None), _vtile(vaug_ref, kj, tk)

    lead = [(functools.partial(lambda d: jnp.where(key + d * tk <= qry, scores(qi * ntile + d), NEG), d),
             functools.partial(lambda d: (tile_constant(qi * ntile + d, d),
                                          _vtile(vaug_ref, qi * ntile + d, tk)), d))
            for d in range(ntile)]
    _flash_pipeline(lead, qi * ntile, scores, meta, ((sa_ref, ma_ref), (sb_ref, mb_ref)), m_ref, acc_ref,
                    n_even=ntile % 2 == 0)
    o_ref[0, 0] = _finish_t(acc_ref).astype(o_ref.dtype)


MOBA_TQ = 1024
MOBA_TK = 512


def moba_pallas(qvt, kaug):
    b, _, dh, s = qvt.shape
    t = MOBA_BLOCK
    tq = min(MOBA_TQ, s)
    tk = min(MOBA_TK, tq)
    assert POS_PERIOD == t and tq % tk == 0 and tk % t == 0
    nb = s // t
    nbp = max(16, nb)
    ind = np.zeros((nbp, s), np.float32)
    ind[np.arange(s) // t, np.arange(s)] = 1.0
    slopes2 = _alibi_slopes2(MOBA_HEADS)
    grid_spec = pltpu.PrefetchScalarGridSpec(
        num_scalar_prefetch=1,
        grid=(b, MOBA_HEADS, s // tq),
        in_specs=[pl.BlockSpec((1, 1, dh, tq), lambda i, h, j, sl: (i, TQ_MOBA + h, 0, j)),
                  pl.BlockSpec((1, 1, s, KAUG), lambda i, h, j, sl: (i, KA_MOBA + h, 0, 0)),
                  pl.BlockSpec((1, 1, dh, s), lambda i, h, j, sl: (i, TV_MOBA + h, 0, 0)),
                  pl.BlockSpec((nbp, s), lambda i, h, j, sl: (0, 0)),
                  pl.BlockSpec((None, dh, tq), lambda i, h, j, sl: (h, 0, 0))],
        out_specs=pl.BlockSpec((1, 1, dh, tq), lambda i, h, j, sl: (i, h, 0, j)),
        scratch_shapes=[pltpu.VMEM((1, tq), F32), pltpu.VMEM((VROWS, tq), F32),
                        pltpu.VMEM((VROWS, s), BF16), pltpu.VMEM((nbp, KAUG), F32),
                        pltpu.VMEM((nbp, tq), F32)] + _score_buffers(tk, tq, tk // t))
    out = pl.pallas_call(
        _moba_kernel,
        grid_spec=grid_spec,
        out_shape=jax.ShapeDtypeStruct((b, MOBA_HEADS, dh, s), BF16),
        compiler_params=_cparams(("parallel", "parallel", "arbitrary")),
        name="moba_attention",
    )(jnp.asarray(slopes2), qvt, kaug, qvt, jnp.asarray(ind, BF16), _slope_rows(slopes2, tq))
    return out.reshape(b, MOBA_HEADS * dh, s)


def _nsa_compress_kernel(x_ref, w1a_ref, w1b_ref, pe_ref, w1_ref, w2_ref, w2t_ref, o_ref, ot_ref):
    nr = x_ref.shape[2]
    x = x_ref[0, 0]

    def near_f32(xb, w):
        w_hi, w_lo = _split2(w)
        return _dot(xb, w_hi) + _dot(xb, w_lo)

    a = near_f32(x, w1a_ref[0])
    bm = near_f32(x, w1b_ref[0])
    pe_hi, pe_mid, pe_lo = _split3(pe_ref[0])
    w1_hi, w1_lo = _split2(w1_ref[0])
    pe_term = (_dot(pe_hi, w1_hi) + _dot(pe_mid, w1_hi) + _dot(pe_lo, w1_hi)
               + _dot(pe_hi, w1_lo) + _dot(pe_mid, w1_lo))[0:1]
    pre = a + pltpu.roll(bm, nr - 1, axis=0) + pe_term
    hid = _gelu_tanh(pre)
    h_hi, h_mid, h_lo = _split3(hid)
    w2_hi, w2_lo = _split2(w2_ref[0])
    o_ref[0, 0] = (_dot(h_hi, w2_hi) + _dot(h_mid, w2_hi) + _dot(h_lo, w2_hi)
                   + _dot(h_hi, w2_lo) + _dot(h_mid, w2_lo))
    t_hi, t_lo = _split2(w2t_ref[0])
    ot_ref[0, 0] = (_dot_nt(t_hi, h_hi) + _dot_nt(t_hi, h_mid) + _dot_nt(t_hi, h_lo)
                    + _dot_nt(t_lo, h_hi) + _dot_nt(t_lo, h_mid))


def nsa_compress_pallas(nat, pe, w1, w2):
    b, _, s, dh = nat.shape
    nr = s // NSA_CMP_STRIDE
    half = NSA_CMP_STRIDE * dh
    hid = w1.shape[-1]
    x = nat[:, N_CMP:N_CMP + 4].reshape(b, 4, nr, half)
    w1f = w1.reshape(2, NSA_CMP_LEN * dh, hid)
    pef = jnp.zeros((2, 8, NSA_CMP_LEN * dh), F32).at[:, 0].set(pe.reshape(2, NSA_CMP_LEN * dh))
    return pl.pallas_call(
        _nsa_compress_kernel,
        grid=(b, 4),
        in_specs=[pl.BlockSpec((1, 1, nr, half), lambda i, j: (i, j, 0, 0)),
                  pl.BlockSpec((1, half, hid), lambda i, j: (j // 2, 0, 0)),
                  pl.BlockSpec((1, half, hid), lambda i, j: (j // 2, 1, 0)),
                  pl.BlockSpec((1, 8, 2 * half), lambda i, j: (j // 2, 0, 0)),
                  pl.BlockSpec((1, 2 * half, hid), lambda i, j: (j // 2, 0, 0)),
                  pl.BlockSpec((1, hid, dh), lambda i, j: (j // 2, 0, 0)),
                  pl.BlockSpec((1, dh, hid), lambda i, j: (j // 2, 0, 0))],
        out_specs=[pl.BlockSpec((1, 1, nr, dh), lambda i, j: (i, j, 0, 0)),
                   pl.BlockSpec((1, 1, dh, nr), lambda i, j: (i, j, 0, 0))],
        out_shape=[jax.ShapeDtypeStruct((b, 4, nr, dh), F32),
                   jax.ShapeDtypeStruct((b, 4, dh, nr), F32)],
        compiler_params=_cparams(("parallel", "parallel")),
        name="nsa_compress",
    )(x, w1f, w1f, pef, w1f, w2, jnp.swapaxes(w2, 1, 2))


NSA_TQ = 128


NSA_CMP_CHUNK = 256


def _nsa_cmp_kernel(slope_ref, qt_ref, kc_ref, vct_ref, mimpt_ref, oct_ref, selt_ref, cnt_ref):
    tq = NSA_TQ
    g = pl.program_id(1)
    qi = pl.program_id(2)
    q0 = qi * tq
    nr = kc_ref.shape[2]
    nsb = selt_ref.shape[2]
    t_lane = q0 + lax.broadcasted_iota(jnp.int32, (1, tq), 1)
    chunk = min(NSA_CMP_CHUNK, nr)
    tiles_per_chunk = chunk * NSA_CMP_STRIDE // tq

    def branch(n):
        kc_hi, kc_lo = _split2(kc_ref[0, 0, 0:n, :])
        vct = vct_ref[0, 0, :, 0:n].astype(BF16)
        cmp_end = NSA_CMP_STRIDE * lax.broadcasted_iota(jnp.int32, (n, tq), 0) + (NSA_CMP_LEN - 1)
        mask = cmp_end <= t_lane
        rel = (cmp_end - q0).astype(F32)
        psum = jnp.zeros((n, tq), F32)
        qt4 = jnp.concatenate([qt_ref[0, hh] for hh in range(NSA_GROUP)], axis=1)
        st4 = _dot(kc_hi, qt4) + _dot(kc_lo, qt4)
        probs = []
        for hh in range(NSA_GROUP):
            st = st4[:, hh * tq:(hh + 1) * tq] + slope_ref[g * NSA_GROUP + hh] * rel
            st = jnp.where(mask, st, -jnp.inf)
            m = jnp.max(st, axis=0, keepdims=True)
            m = jnp.where(m > -jnp.inf, m, 0.0)
            e = jnp.exp2(st - m)
            p = e * (1.0 / jnp.maximum(jnp.sum(e, axis=0, keepdims=True), 1e-30))
            probs.append(p.astype(BF16))
            psum = psum + p
        o4 = _dot(vct, jnp.concatenate(probs, axis=1))
        for hh in range(NSA_GROUP):
            oct_ref[0, hh] = o4[:, hh * tq:(hh + 1) * tq]
        nbk = min(nsb, n * NSA_CMP_STRIDE // NSA_SEL_BLOCK)
        p_hi, p_mid, p_lo = _split3(psum)
        mimpt = mimpt_ref[0:nbk, 0:n]
        imp = _dot(mimpt, p_hi) + _dot(mimpt, p_mid) + _dot(mimpt, p_lo)
        blk = lax.broadcasted_iota(jnp.int32, (nbk, tq), 0)
        jt = t_lane // NSA_SEL_BLOCK
        forced = (blk == 0) | (blk == jt) | (blk == jt - 1)
        imp = jnp.where(forced, NSA_FORCE_SCORE, imp)
        valid = blk * NSA_SEL_BLOCK <= t_lane
        work0 = jnp.where(valid, imp, -jnp.inf)
        big = jnp.int32(2 ** 30)

        def pick(_, work):
            mx = jnp.max(work, axis=0, keepdims=True)
            first = jnp.min(jnp.where(work == mx, blk, big), axis=0, keepdims=True)
            return jnp.where(blk == first, -jnp.inf, work)

        work = lax.fori_loop(0, min(NSA_TOPK, nsb), pick, work0)
        sel = jnp.where(valid & (work == -jnp.inf), 1.0, 0.0)
        selt_ref[0, 0, 0:nbk, :] = sel
        cnt_ref[0, 0, 0, :, 0:nbk] = _dot_nt(jnp.ones((8, tq), BF16), sel.astype(BF16))
        if nbk < nsb:
            selt_ref[0, 0, nbk:nsb, :] = jnp.zeros((nsb - nbk, tq), F32)
            cnt_ref[0, 0, 0, :, nbk:nsb] = jnp.zeros((8, nsb - nbk), F32)

    n_chunks = nr // chunk
    for c in range(n_chunks):
        pl.when(jnp.minimum(qi // tiles_per_chunk, n_chunks - 1) == c)(
            functools.partial(branch, (c + 1) * chunk))


def nsa_cmp_pallas(qvt, cmp_kv, cmp_kvt):
    b, _, dh, s = qvt.shape
    tq = NSA_TQ
    nr = cmp_kv.shape[2]
    nsb = s // NSA_SEL_BLOCK
    ratio = NSA_SEL_BLOCK // NSA_CMP_STRIDE
    front = NSA_CMP_LEN // NSA_CMP_STRIDE - 1
    n_int = ratio + front
    n_idx = np.arange(nr)[None, :]
    j_idx = np.arange(nsb)[:, None]
    mimpt = ((n_idx >= ratio * j_idx - front) & (n_idx <= ratio * j_idx + n_int - 1 - front)
             & (n_idx < nr - 1)).astype(np.float32)
    grid_spec = pltpu.PrefetchScalarGridSpec(
        num_scalar_prefetch=1,
        grid=(b, NSA_KV_HEADS, s // tq),
        in_specs=[pl.BlockSpec((1, NSA_GROUP, dh, tq), lambda i, g, j, sl: (i, TQ_NSA // NSA_GROUP + g, 0, j)),
                  pl.BlockSpec((1, 1, nr, dh), lambda i, g, j, sl: (i, g, 0, 0)),
                  pl.BlockSpec((1, 1, dh, nr), lambda i, g, j, sl: (i, 2 + g, 0, 0)),
                  pl.BlockSpec((nsb, nr), lambda i, g, j, sl: (0, 0))],
        out_specs=[pl.BlockSpec((1, NSA_GROUP, dh, tq), lambda i, g, j, sl: (i, g, 0, j)),
                   pl.BlockSpec((1, 1, nsb, tq), lambda i, g, j, sl: (i, g, 0, j)),
                   pl.BlockSpec((1, 1, 1, 8, nsb), lambda i, g, j, sl: (i, g, j, 0, 0))])
    assert nr % min(NSA_CMP_CHUNK, nr) == 0
    return pl.pallas_call(
        _nsa_cmp_kernel,
        grid_spec=grid_spec,
        out_shape=[jax.ShapeDtypeStruct((b, NSA_HEADS, dh, s), F32),
                   jax.ShapeDtypeStruct((b, NSA_KV_HEADS, nsb, s), F32),
                   jax.ShapeDtypeStruct((b, NSA_KV_HEADS, s // tq, 8, nsb), F32)],
        compiler_params=_cparams(("parallel", "parallel", "parallel")),
        name="nsa_compressed_select",
    )(jnp.asarray(_alibi_slopes2(NSA_HEADS)), qvt, cmp_kv, cmp_kvt, jnp.asarray(mimpt, BF16))


NSA_TK = 256
NSA_SEL_TQ = 256
NSA_LANES = NSA_GROUP * NSA_TQ


def _nsa_qaug(qt_ref, srow_ref):
    return jnp.concatenate(
        [jnp.concatenate([qt_ref[0, hh], srow_ref[hh]], axis=0) for hh in range(NSA_GROUP)], axis=1)


def _nsa_sel_kernel(bits_ref, qt_ref, k_ref, vt_ref, selt_ref, srow_ref, slane_ref, o_ref,
                    m_ref, acc_ref, vaug_ref, sa_ref, ma_ref, sb_ref, mb_ref, list_ref, *, nq, words):
    tq, tk = NSA_SEL_TQ, NSA_TK
    per_tile = tk // NSA_SEL_BLOCK
    qi = pl.program_id(2)
    q0 = qi * tq

    @pl.when(qi == 0)
    def _():
        _fill_vaug(vaug_ref, vt_ref)

    base = ((pl.program_id(0) * NSA_KV_HEADS + pl.program_id(1)) * nq + qi) * words
    list_ref[0] = 0

    def note(j, n):
        list_ref[n] = j
        return n + ((bits_ref[base + j // 32] >> (j % 32)) & 1)

    n_tiles = lax.fori_loop(0, qi, note, 0)

    qaug = _nsa_qaug(qt_ref, srow_ref)
    slane = slane_ref[0:1, :]
    diag = q0 // tk

    def scores(kj):
        st = _dot(_ktile(k_ref, kj, tk), qaug)
        rows = [jnp.broadcast_to(selt_ref[0, 0, pl.ds(kj * per_tile + c, 1), :], (NSA_SEL_BLOCK, tq))
                for c in range(per_tile)]
        bias = (jnp.concatenate(rows, axis=0) - 1.0) * (-NEG)
        return st + jnp.concatenate([bias] * NSA_GROUP, axis=1)

    def meta(kj):
        return slane * (kj * tk - q0).astype(F32), _vtile(vaug_ref, kj, tk)

    def own_tile():
        key = lax.broadcasted_iota(jnp.int32, (tk, tq), 0)
        qry = lax.broadcasted_iota(jnp.int32, (tk, tq), 1)
        causal = jnp.concatenate([key <= qry] * NSA_GROUP, axis=1)
        return jnp.where(causal, scores(diag), NEG)

    _init_state_t(m_ref, acc_ref)
    _flash_pipeline([(own_tile, lambda: meta(diag))], n_tiles,
                    lambda i: scores(list_ref[i]), lambda i: meta(list_ref[i]),
                    ((sa_ref, ma_ref), (sb_ref, mb_ref)), m_ref, acc_ref, n_even=False)
    out = _finish_t(acc_ref)
    for hh in range(NSA_GROUP):
        o_ref[0, hh] = out[:, hh * tq:(hh + 1) * tq]


def _nsa_tables(tq):
    slopes2 = _alibi_slopes2(NSA_HEADS)
    srow = _slope_rows(slopes2, tq)
    slane = np.repeat(slopes2.reshape(NSA_KV_HEADS, NSA_GROUP), tq, axis=1)
    slane8 = np.repeat(slane[:, None, :], 8, axis=1)
    return srow, jnp.asarray(slane8, F32)


def _active_tile_bits(cnt, tq, tk):
    b, g, nq128, _, nsb = cnt.shape
    qper, bper = tq // NSA_TQ, tk // NSA_SEL_BLOCK
    nq, nkv = nq128 // qper, nsb // bper
    act = cnt[:, :, :, 0, :].reshape(b, g, nq, qper, nkv, bper).sum(axis=(3, 5)) > 0.0
    words = -(-nkv // 32)
    act = jnp.pad(act, ((0, 0), (0, 0), (0, 0), (0, words * 32 - nkv))).reshape(b, g, nq, words, 32)
    bits = jnp.sum(act.astype(jnp.uint32) << jnp.arange(32, dtype=jnp.uint32), axis=-1, dtype=jnp.uint32)
    return lax.bitcast_convert_type(bits, jnp.int32).reshape(-1), nq, words


def nsa_sel_pallas(qvt, kaug, selt, cnt):
    b, _, dh, s = qvt.shape
    tq = NSA_SEL_TQ
    assert tq == NSA_TK
    lanes = NSA_GROUP * tq
    nsb = s // NSA_SEL_BLOCK
    srow, slane = _nsa_tables(tq)
    bits, nq, words = _active_tile_bits(cnt, tq, NSA_TK)
    grid_spec = pltpu.PrefetchScalarGridSpec(
        num_scalar_prefetch=1,
        grid=(b, NSA_KV_HEADS, nq),
        in_specs=[pl.BlockSpec((1, NSA_GROUP, dh, tq), lambda i, g, j, bt: (i, TQ_NSA // NSA_GROUP + g, 0, j)),
                  pl.BlockSpec((1, 1, s, KAUG), lambda i, g, j, bt: (i, KA_NSA + g, 0, 0)),
                  pl.BlockSpec((1, 1, dh, s), lambda i, g, j, bt: (i, TV_NSA + g, 0, 0)),
                  pl.BlockSpec((1, 1, nsb, tq), lambda i, g, j, bt: (i, g, 0, j)),
                  pl.BlockSpec((NSA_GROUP, dh, tq), lambda i, g, j, bt: (g, 0, 0)),
                  pl.BlockSpec((None, 8, lanes), lambda i, g, j, bt: (g, 0, 0))],
        out_specs=pl.BlockSpec((1, NSA_GROUP, dh, tq), lambda i, g, j, bt: (i, g, 0, j)),
        scratch_shapes=[pltpu.VMEM((1, lanes), F32), pltpu.VMEM((VROWS, lanes), F32),
                        pltpu.VMEM((VROWS, s), BF16)] + _score_buffers(NSA_TK, lanes)
        + [pltpu.SMEM((max(nq, 8),), jnp.int32)])
    return pl.pallas_call(
        functools.partial(_nsa_sel_kernel, nq=nq, words=words),
        grid_spec=grid_spec,
        out_shape=jax.ShapeDtypeStruct((b, NSA_HEADS, dh, s), F32),
        compiler_params=_cparams(("parallel", "parallel", "arbitrary")),
        name="nsa_selected",
    )(bits, qvt, kaug, qvt, selt, srow, slane)


NSA_WT = 128


def _nsa_win_kernel(qt_ref, k_ref, vt_ref, srow_ref, slane_ref, oc_ref, os_ref, gate_ref, o_ref,
                    m_ref, acc_ref, vaug_ref, sa_ref, ma_ref, sb_ref, mb_ref):
    tq = NSA_TQ
    wt = NSA_WT
    qi = pl.program_id(2)

    @pl.when(qi == 0)
    def _():
        _fill_vaug(vaug_ref, vt_ref)

    qaug = _nsa_qaug(qt_ref, srow_ref)
    slane = slane_ref[0:1, :]
    key = lax.broadcasted_iota(jnp.int32, (wt, tq), 0)
    qry = lax.broadcasted_iota(jnp.int32, (wt, tq), 1)
    span = NSA_WINDOW // wt

    def tile(d, keep):
        kj = jnp.maximum(qi - d, 0)

        def scores():
            st = _dot(_ktile(k_ref, kj, wt), qaug)
            if keep is not None:
                st = jnp.where(jnp.concatenate([keep] * NSA_GROUP, axis=1), st, NEG)
            return st

        def meta():
            base = (kj * wt) // POS_PERIOD * POS_PERIOD - qi * tq
            delta = jnp.where(qi - d >= 0, slane * base.astype(F32), SKIP)
            return delta, _vtile(vaug_ref, kj, wt)

        return scores, meta

    _init_state_t(m_ref, acc_ref)
    tiles = [tile(0, key <= qry)] + [tile(d, None) for d in range(1, span)] + [tile(span, key > qry)]
    _flash_pipeline(tiles, None, None, None, ((sa_ref, ma_ref), (sb_ref, mb_ref)), m_ref, acc_ref, True)

    o_w = _finish_t(acc_ref)
    gt = gate_ref[0].T
    for hh in range(NSA_GROUP):
        c0 = 3 * hh
        mix = (gt[c0:c0 + 1] * oc_ref[0, hh] + gt[c0 + 1:c0 + 2] * os_ref[0, hh]
               + gt[c0 + 2:c0 + 3] * o_w[:, hh * tq:(hh + 1) * tq])
        o_ref[0, hh] = mix.astype(o_ref.dtype)


def nsa_win_pallas(qvt, kaug, o_c, o_s, gates):
    b, _, dh, s = qvt.shape
    tq, wt = NSA_TQ, NSA_WT
    assert tq == wt and tq == LANES
    srow, slane = _nsa_tables(tq)
    head_blk = pl.BlockSpec((1, NSA_GROUP, dh, tq), lambda i, g, j: (i, g, 0, j))
    out = pl.pallas_call(
        _nsa_win_kernel,
        grid=(b, NSA_KV_HEADS, s // tq),
        in_specs=[pl.BlockSpec((1, NSA_GROUP, dh, tq), lambda i, g, j: (i, TQ_NSA // NSA_GROUP + g, 0, j)),
                  pl.BlockSpec((1, 1, s, KAUG), lambda i, g, j: (i, KA_NSA + 2 + g, 0, 0)),
                  pl.BlockSpec((1, 1, dh, s), lambda i, g, j: (i, TV_NSA + 2 + g, 0, 0)),
                  pl.BlockSpec((NSA_GROUP, dh, tq), lambda i, g, j: (g, 0, 0)),
                  pl.BlockSpec((None, 8, NSA_LANES), lambda i, g, j: (g, 0, 0)),
                  head_blk, head_blk,
                  pl.BlockSpec((1, tq, LANES), lambda i, g, j: (i, j, g))],
        out_specs=head_blk,
        out_shape=jax.ShapeDtypeStruct((b, NSA_HEADS, dh, s), BF16),
        scratch_shapes=[pltpu.VMEM((1, NSA_LANES), F32), pltpu.VMEM((VROWS, NSA_LANES), F32),
                        pltpu.VMEM((VROWS, s), BF16)] + _score_buffers(wt, NSA_LANES),
        compiler_params=_cparams(("parallel", "parallel", "arbitrary")),
        name="nsa_window_mix",
    )(qvt, kaug, qvt, srow, slane, o_c, o_s, gates)
    return out.reshape(b, NSA_HEADS * dh, s)


def _mem_attn_kernel(q_ref, k_ref, vt_ref, o_ref):
    for hh in range(MEM_HEADS):
        s = _dot_nt(q_ref[0, hh], k_ref[0, hh])
        m = jnp.max(s, axis=-1, keepdims=True)
        e = jnp.exp2(s - m)
        p = e / jnp.sum(e, axis=-1, keepdims=True)
        o_ref[0, hh] = _dot_nt(vt_ref[0, hh], p.astype(BF16)).astype(o_ref.dtype)


def mem_attn_pallas(nat, mem_k, mem_vt, tq=512):
    b, _, s, dh = nat.shape
    n_mem = mem_k.shape[2]
    out = pl.pallas_call(
        _mem_attn_kernel,
        grid=(b, s // tq),
        in_specs=[pl.BlockSpec((1, MEM_HEADS, tq, dh), lambda i, j: (i, N_MEMQ // MEM_HEADS, j, 0)),
                  pl.BlockSpec((1, MEM_HEADS, n_mem, dh), lambda i, j: (i, 0, 0, 0)),
                  pl.BlockSpec((1, MEM_HEADS, dh, n_mem), lambda i, j: (i, 0, 0, 0))],
        out_specs=pl.BlockSpec((1, MEM_HEADS, dh, tq), lambda i, j: (i, 0, 0, j)),
        out_shape=jax.ShapeDtypeStruct((b, MEM_HEADS, dh, s), BF16),
        compiler_params=_cparams(("parallel", "parallel")),
        name="memory_attention",
    )(nat, mem_k, mem_vt)
    return out.reshape(b, MEM_HEADS * dh, s)


def _pad_key_cols(w):
    d, n = w.shape
    w3 = w.reshape(d, n // HEAD_DIM, HEAD_DIM)
    return jnp.concatenate([w3, jnp.zeros_like(w3)], axis=2).reshape(d, 2 * n)


def _in_proj_weights(w_in):
    hd = HEAD_DIM
    sizes = (3 * MOBA_HEADS * hd, NSA_HEADS * hd, 6 * NSA_KV_HEADS * hd, 3 * NSA_HEADS,
             3 * FOX_HEADS * hd, FOX_HEADS, MEM_HEADS * hd)
    offs = np.concatenate([[0], np.cumsum(sizes)])
    moba, nsa_q, nsa_kv, nsa_g, fox, fox_f, mem_q = (w_in[:, offs[i]:offs[i + 1]] for i in range(7))
    mh, fh, g2 = MOBA_HEADS * hd, FOX_HEADS * hd, NSA_KV_HEADS * hd
    moba_q, moba_k, moba_v = moba[:, :mh], moba[:, mh:2 * mh], moba[:, 2 * mh:]
    fox_q, fox_k, fox_v = fox[:, :fh], fox[:, fh:2 * fh], fox[:, 2 * fh:]
    k_cmp, v_cmp, k_slc, v_slc, k_win, v_win = (nsa_kv[:, i * g2:(i + 1) * g2] for i in range(6))
    w_t = jnp.concatenate([moba_q, nsa_q, fox_q, moba_v, v_slc, v_win, fox_v], axis=1).T.astype(BF16)
    t_scale = np.ones((T_SLOTS * hd,), np.float32)
    t_scale[:TV_MOBA * hd] = Q_SCALE
    w_ka = _pad_key_cols(jnp.concatenate([moba_k, k_slc, k_win], axis=1)).astype(BF16)
    w_kf = _pad_key_cols(fox_k).astype(BF16)
    w_nat = jnp.concatenate([k_cmp, v_cmp, mem_q], axis=1).astype(BF16)
    n_scale = np.ones((N_SLOTS * hd,), np.float32)
    n_scale[N_MEMQ * hd:] = Q_SCALE
    return w_t, jnp.asarray(t_scale), w_ka, w_kf, w_nat, jnp.asarray(n_scale), nsa_g, fox_f


def _mixer(h32, h16, mem16, w_in, b_forget, w_mem_kv, cmp_pe, cmp_w1, cmp_w2):
    b, s, d = h16.shape
    tm = min(1024, s)
    w_t, t_scale, w_ka, w_kf, w_nat, n_scale, w_gate, w_forget = _in_proj_weights(w_in)
    gates, caug = gates_pallas(h32, w_gate, w_forget, b_forget)
    qvt = proj_t_pallas(h16, w_t, t_scale, tm=tm, heads_per_step=13)
    k_alibi = proj_kaug_pallas(h16, w_ka, None, tm=tm, heads_per_step=6)
    k_fox = proj_kaug_pallas(h16, w_kf, caug, tm=tm, heads_per_step=6)
    nat = proj_heads_pallas(h16, w_nat, n_scale, tm=tm, heads_per_step=8)
    n_mem = mem16.shape[1]
    mk = MEM_HEADS * HEAD_DIM
    mem_k = proj_heads_pallas(mem16, w_mem_kv[:, :mk].astype(BF16), jnp.ones((mk,), F32),
                              tm=n_mem, heads_per_step=MEM_HEADS)
    mem_vt = proj_t_pallas(mem16, w_mem_kv[:, mk:].T.astype(BF16), jnp.ones((mk,), F32),
                           tm=n_mem, heads_per_step=MEM_HEADS)
    o_moba = moba_pallas(qvt, k_alibi)
    o_fox = fox_pallas(qvt, k_fox)
    cmp_kv, cmp_kvt = nsa_compress_pallas(nat, cmp_pe, cmp_w1, cmp_w2)
    o_c, selt, cnt = nsa_cmp_pallas(qvt, cmp_kv, cmp_kvt)
    o_s = nsa_sel_pallas(qvt, k_alibi, selt, cnt)
    o_nsa = nsa_win_pallas(qvt, k_alibi, o_c, o_s, gates)
    o_mem = mem_attn_pallas(nat, mem_k, mem_vt)
    return [o_moba, o_nsa, o_fox, o_mem]


def kernel(x, mem, emb_ln_g, emb_ln_b, w_in, b_forget, w_mem_kv, nsa_cmp_pe, nsa_cmp_w1, nsa_cmp_w2,
           w_out, ln1_g, ln1_b, ffn_w_up, ffn_conv_w, ffn_conv_b, ffn_w_down, ln2_g, ln2_b):
    b, s, d = x.shape
    depth = w_in.shape[0]
    dff = ffn_w_down.shape[1]
    mem16 = mem.astype(BF16)
    h32, h16 = layer_norm_pallas(x.reshape(b * s, d), emb_ln_g, emb_ln_b)
    for l in range(depth):
        heads = _mixer(h32.reshape(b, s, d), h16.reshape(b, s, d), mem16, w_in[l], b_forget[l], w_mem_kv[l],
                       nsa_cmp_pe[l], nsa_cmp_w1[l], nsa_cmp_w2[l])
        h32, h16 = out_proj_ln_pallas(heads, w_out[l].astype(BF16), h32.reshape(b, s, d),
                                      ln1_g[l], ln1_b[l], tm=512)
        a = ffn_up_pallas(h16, ffn_w_up[l].astype(BF16), ffn_conv_w[l], ffn_conv_b[l],
                          tm=min(1024, s), tn=512)
        h32, h16 = matmul_ln_resident_pallas(a.reshape(b * s, dff), ffn_w_down[l].astype(BF16),
                                             h32.reshape(b * s, d), ln2_g[l], ln2_b[l], tm=256)
    return h32.reshape(b, s, d)
```

```python
import functools
import math

import jax
import jax.numpy as jnp
import ml_dtypes
import numpy as np
from jax import lax
from jax.experimental import pallas as pl
from jax.experimental.pallas import tpu as pltpu

F32 = jnp.float32
BF16 = jnp.bfloat16

HEAD_DIM = 64
MOBA_HEADS = 8
NSA_HEADS = 8
NSA_KV_HEADS = 2
NSA_GROUP = NSA_HEADS // NSA_KV_HEADS
FOX_HEADS = 12
MEM_HEADS = 4
MOBA_BLOCK = 256
MOBA_TOPK = 3
NSA_CMP_LEN = 32
NSA_CMP_STRIDE = 16
NSA_SEL_BLOCK = 64
NSA_TOPK = 16
NSA_WINDOW = 512
NSA_FORCE_SCORE = 1.0e4
CONV_WIDTH = 3
LN_EPS = 1e-5
DEPTH = 2
DEEPNORM_ALPHA = (2 * DEPTH) ** 0.25

LOG2E = math.log2(math.e)
Q_SCALE = HEAD_DIM ** -0.5 * LOG2E
NEG = -1.0e30
SKIP = -3.0e38
VMEM_LIMIT = 56 * 1024 * 1024
LANES = 128
KAUG = 2 * HEAD_DIM
VROWS = HEAD_DIM + 16
POS_PERIOD = 256

TQ_MOBA, TQ_NSA, TQ_FOX = 0, 8, 16
TV_MOBA, TV_NSA, TV_FOX = 28, 36, 40
T_SLOTS = 52
KA_MOBA, KA_NSA = 0, 8
KA_SLOTS = 12
N_CMP, N_MEMQ = 0, 4
N_SLOTS = 8


def _cparams(sem):
    return pltpu.CompilerParams(dimension_semantics=sem, vmem_limit_bytes=VMEM_LIMIT)


def _split2(x):
    hi = x.astype(BF16)
    return hi, (x - hi.astype(F32)).astype(BF16)


def _split3(x):
    hi = x.astype(BF16)
    r1 = x - hi.astype(F32)
    mid = r1.astype(BF16)
    lo = (r1 - mid.astype(F32)).astype(BF16)
    return hi, mid, lo


def _np_split3(x):
    x = np.asarray(x, np.float32)
    hi = x.astype(ml_dtypes.bfloat16).astype(np.float32)
    r1 = x - hi
    mid = r1.astype(ml_dtypes.bfloat16).astype(np.float32)
    lo = (r1 - mid).astype(ml_dtypes.bfloat16).astype(np.float32)
    return hi, mid, lo


def _dot_nt(a, b):
    return lax.dot_general(a, b, (((1,), (1,)), ((), ())), preferred_element_type=F32)


def _dot_tn(a, b):
    return lax.dot_general(a, b, (((0,), (0,)), ((), ())), preferred_element_type=F32)


def _dot(a, b):
    return jnp.dot(a, b, preferred_element_type=F32)


def _layer_norm_rows(x, g, b):
    mu = jnp.mean(x, axis=-1, keepdims=True)
    xc = x - mu
    var = jnp.mean(xc * xc, axis=-1, keepdims=True)
    return xc * lax.rsqrt(var + LN_EPS) * g + b


def _alibi_slopes2(n):
    return (np.exp2(-8.0 * np.arange(1, n + 1, dtype=np.float64) / n) * LOG2E).astype(np.float32)


def _slope_rows(slopes2, lanes):
    pieces = np.stack(_np_split3(slopes2), axis=1)
    rows = np.zeros((len(slopes2), HEAD_DIM, lanes), np.float32)
    rows[:, :3, :] = pieces[:, :, None]
    return jnp.asarray(rows, BF16)


def _ln_kernel(x_ref, g_ref, b_ref, o32_ref, o16_ref):
    y = _layer_norm_rows(x_ref[...], g_ref[...], b_ref[...])
    o32_ref[...] = y
    o16_ref[...] = y.astype(BF16)


def layer_norm_pallas(x, g, b, tm=512):
    m, d = x.shape
    return pl.pallas_call(
        _ln_kernel,
        grid=(m // tm,),
        in_specs=[pl.BlockSpec((tm, d), lambda i: (i, 0)),
                  pl.BlockSpec((1, d), lambda i: (0, 0)),
                  pl.BlockSpec((1, d), lambda i: (0, 0))],
        out_specs=[pl.BlockSpec((tm, d), lambda i: (i, 0)),
                   pl.BlockSpec((tm, d), lambda i: (i, 0))],
        out_shape=[jax.ShapeDtypeStruct((m, d), F32), jax.ShapeDtypeStruct((m, d), BF16)],
        compiler_params=_cparams(("parallel",)),
        name="layer_norm",
    )(x, g.reshape(1, d), b.reshape(1, d))


GATE_LANES_NSA = 3 * NSA_HEADS


def _gates_kernel(h_ref, w_ref, bf_ref, tri_ref, place_ref, g_ref, caug_ref, carry_ref):
    si = pl.program_id(1)

    @pl.when(si == 0)
    def _():
        carry_ref[...] = jnp.zeros_like(carry_ref)

    h_hi, h_lo = _split2(h_ref[0])
    w_hi, w_lo = _split2(w_ref[...])
    x = _dot(h_hi, w_hi) + _dot(h_lo, w_hi) + _dot(h_hi, w_lo)
    g_ref[0] = 1.0 / (1.0 + jnp.exp(-x))
    x = x + bf_ref[...]
    logf = jnp.minimum(x, 0.0) - jnp.log(1.0 + jnp.exp(-jnp.abs(x)))
    tri = tri_ref[...]
    l_hi, l_mid, l_lo = _split3(logf)
    c = _dot(tri, l_hi) + _dot(tri, l_mid) + _dot(tri, l_lo) + carry_ref[0:1, :]
    carry_ref[...] = jnp.broadcast_to(c[-1:, :], carry_ref.shape)
    n_hi, n_mid, n_lo = _split3(-LOG2E * c)
    caug = _dot(n_hi, place_ref[0]) + _dot(n_mid, place_ref[1]) + _dot(n_lo, place_ref[2])
    caug_ref[0] = caug.astype(BF16)


def gates_pallas(h3, w_gate, w_forget, b_forget, t=512):
    b, s, d = h3.shape
    lo = GATE_LANES_NSA
    w = jnp.zeros((d, LANES), F32).at[:, :lo].set(w_gate).at[:, lo:lo + FOX_HEADS].set(w_forget)
    bf = jnp.zeros((1, LANES), F32).at[0, lo:lo + FOX_HEADS].set(b_forget)
    tri = (np.arange(t)[None, :] <= np.arange(t)[:, None]).astype(np.float32)
    nc = FOX_HEADS * HEAD_DIM
    place = np.zeros((3, LANES, nc), np.float32)
    for piece in range(3):
        for hh in range(FOX_HEADS):
            place[piece, lo + hh, hh * HEAD_DIM + piece] = 1.0
    return pl.pallas_call(
        _gates_kernel,
        grid=(b, s // t),
        in_specs=[pl.BlockSpec((1, t, d), lambda i, j: (i, j, 0)),
                  pl.BlockSpec((d, LANES), lambda i, j: (0, 0)),
                  pl.BlockSpec((1, LANES), lambda i, j: (0, 0)),
                  pl.BlockSpec((t, t), lambda i, j: (0, 0)),
                  pl.BlockSpec((3, LANES, nc), lambda i, j: (0, 0, 0))],
        out_specs=[pl.BlockSpec((1, t, LANES), lambda i, j: (i, j, 0)),
                   pl.BlockSpec((1, t, nc), lambda i, j: (i, j, 0))],
        out_shape=[jax.ShapeDtypeStruct((b, s, LANES), F32),
                   jax.ShapeDtypeStruct((b, s, nc), BF16)],
        scratch_shapes=[pltpu.VMEM((8, LANES), F32)],
        compiler_params=_cparams(("parallel", "arbitrary")),
        name="gates_cumsum",
    )(h3, w, bf, jnp.asarray(tri, BF16), jnp.asarray(place, BF16))


def _proj_heads_kernel(x_ref, w_ref, sc_ref, o_ref, *, heads_per_step):
    acc = _dot(x_ref[0], w_ref[...]) * sc_ref[...]
    for j in range(heads_per_step):
        o_ref[0, j] = acc[:, j * HEAD_DIM:(j + 1) * HEAD_DIM].astype(o_ref.dtype)


def proj_heads_pallas(x3, w, colscale, tm, heads_per_step):
    b, s, d = x3.shape
    n = w.shape[1]
    tn = heads_per_step * HEAD_DIM
    return pl.pallas_call(
        functools.partial(_proj_heads_kernel, heads_per_step=heads_per_step),
        grid=(b, s // tm, n // tn),
        in_specs=[pl.BlockSpec((1, tm, d), lambda i, j, k: (i, j, 0)),
                  pl.BlockSpec((d, tn), lambda i, j, k: (0, k)),
                  pl.BlockSpec((1, tn), lambda i, j, k: (0, k))],
        out_specs=pl.BlockSpec((1, heads_per_step, tm, HEAD_DIM), lambda i, j, k: (i, k, j, 0)),
        out_shape=jax.ShapeDtypeStruct((b, n // HEAD_DIM, s, HEAD_DIM), BF16),
        compiler_params=_cparams(("parallel", "parallel", "arbitrary")),
        name="proj_heads",
    )(x3, w, colscale.reshape(1, n))


def _proj_t_kernel(x_ref, wt_ref, sc_ref, o_ref, *, heads_per_step):
    acc = _dot_nt(wt_ref[...], x_ref[0]) * sc_ref[...]
    o_ref[0] = acc.reshape(heads_per_step, HEAD_DIM, acc.shape[1]).astype(o_ref.dtype)


def proj_t_pallas(x3, wt, rowscale, tm, heads_per_step):
    b, s, d = x3.shape
    n = wt.shape[0]
    tn = heads_per_step * HEAD_DIM
    return pl.pallas_call(
        functools.partial(_proj_t_kernel, heads_per_step=heads_per_step),
        grid=(b, s // tm, n // tn),
        in_specs=[pl.BlockSpec((1, tm, d), lambda i, j, k: (i, j, 0)),
                  pl.BlockSpec((tn, d), lambda i, j, k: (k, 0)),
                  pl.BlockSpec((tn, 1), lambda i, j, k: (k, 0))],
        out_specs=pl.BlockSpec((1, heads_per_step, HEAD_DIM, tm), lambda i, j, k: (i, k, 0, j)),
        out_shape=jax.ShapeDtypeStruct((b, n // HEAD_DIM, HEAD_DIM, s), BF16),
        compiler_params=_cparams(("parallel", "parallel", "arbitrary")),
        name="proj_transposed",
    )(x3, wt, rowscale.reshape(n, 1))


def _proj_kaug_kernel(x_ref, w_ref, *rest, heads_per_step, positional):
    o_ref = rest[-1]
    acc = _dot(x_ref[0], w_ref[...])
    tm = acc.shape[0]
    if positional:
        pos = (pl.program_id(1) * tm + lax.broadcasted_iota(jnp.int32, (tm, HEAD_DIM), 0)) % POS_PERIOD
        lane = lax.broadcasted_iota(jnp.int32, (tm, HEAD_DIM), 1)
        pos_lanes = jnp.where(lane < 3, pos.astype(F32), 0.0).astype(o_ref.dtype)
    for j in range(heads_per_step):
        cols = slice(j * HEAD_DIM, (j + 1) * HEAD_DIM)
        bias = pos_lanes if positional else rest[0][0, :, cols]
        o_ref[0, j] = jnp.concatenate([acc[:, cols].astype(o_ref.dtype), bias], axis=1)


def proj_kaug_pallas(x3, w, aug, tm, heads_per_step):
    b, s, d = x3.shape
    n = w.shape[1]
    tn = heads_per_step * HEAD_DIM
    in_specs = [pl.BlockSpec((1, tm, d), lambda i, j, k: (i, j, 0)),
                pl.BlockSpec((d, tn), lambda i, j, k: (0, k))]
    args = [x3, w]
    if aug is not None:
        in_specs.append(pl.BlockSpec((1, tm, tn), lambda i, j, k: (i, j, k)))
        args.append(aug)
    return pl.pallas_call(
        functools.partial(_proj_kaug_kernel, heads_per_step=heads_per_step, positional=aug is None),
        grid=(b, s // tm, n // tn),
        in_specs=in_specs,
        out_specs=pl.BlockSpec((1, heads_per_step, tm, KAUG), lambda i, j, k: (i, k, j, 0)),
        out_shape=jax.ShapeDtypeStruct((b, n // HEAD_DIM, s, KAUG), BF16),
        compiler_params=_cparams(("parallel", "parallel", "arbitrary")),
        name="proj_keys_aug",
    )(*args)


def _matmul_ln_kernel(x_ref, w_ref, r_ref, g_ref, b_ref, o32_ref, o16_ref, acc_ref):
    k = pl.program_id(1)

    @pl.when(k == 0)
    def _():
        acc_ref[...] = jnp.zeros_like(acc_ref)

    acc_ref[...] += _dot(x_ref[...], w_ref[...])

    @pl.when(k == pl.num_programs(1) - 1)
    def _():
        y = _layer_norm_rows(DEEPNORM_ALPHA * r_ref[...] + acc_ref[...], g_ref[...], b_ref[...])
        o32_ref[...] = y
        o16_ref[...] = y.astype(BF16)


def matmul_ln_pallas(x, w, res, g, b, tm, tk):
    m, kk = x.shape
    d = w.shape[1]
    return pl.pallas_call(
        _matmul_ln_kernel,
        grid=(m // tm, kk // tk),
        in_specs=[pl.BlockSpec((tm, tk), lambda i, k: (i, k)),
                  pl.BlockSpec((tk, d), lambda i, k: (k, 0)),
                  pl.BlockSpec((tm, d), lambda i, k: (i, 0)),
                  pl.BlockSpec((1, d), lambda i, k: (0, 0)),
                  pl.BlockSpec((1, d), lambda i, k: (0, 0))],
        out_specs=[pl.BlockSpec((tm, d), lambda i, k: (i, 0)),
                   pl.BlockSpec((tm, d), lambda i, k: (i, 0))],
        out_shape=[jax.ShapeDtypeStruct((m, d), F32), jax.ShapeDtypeStruct((m, d), BF16)],
        scratch_shapes=[pltpu.VMEM((tm, d), F32)],
        compiler_params=_cparams(("parallel", "arbitrary")),
        name="matmul_ln",
    )(x, w, res, g.reshape(1, d), b.reshape(1, d))


def _matmul_ln_resident_kernel(x_ref, w_ref, r_ref, g_ref, b_ref, o32_ref, o16_ref):
    y = _layer_norm_rows(DEEPNORM_ALPHA * r_ref[...] + _dot(x_ref[...], w_ref[...]), g_ref[...], b_ref[...])
    o32_ref[...] = y
    o16_ref[...] = y.astype(BF16)


def matmul_ln_resident_pallas(x, w, res, g, b, tm):
    m, kk = x.shape
    d = w.shape[1]
    return pl.pallas_call(
        _matmul_ln_resident_kernel,
        grid=(m // tm,),
        in_specs=[pl.BlockSpec((tm, kk), lambda i: (i, 0)),
                  pl.BlockSpec((kk, d), lambda i: (0, 0), pipeline_mode=pl.Buffered(1)),
                  pl.BlockSpec((tm, d), lambda i: (i, 0)),
                  pl.BlockSpec((1, d), lambda i: (0, 0)),
                  pl.BlockSpec((1, d), lambda i: (0, 0))],
        out_specs=[pl.BlockSpec((tm, d), lambda i: (i, 0)),
                   pl.BlockSpec((tm, d), lambda i: (i, 0))],
        out_shape=[jax.ShapeDtypeStruct((m, d), F32), jax.ShapeDtypeStruct((m, d), BF16)],
        compiler_params=_cparams(("parallel",)),
        name="matmul_ln_resident",
    )(x, w, res, g.reshape(1, d), b.reshape(1, d))


def _out_proj_ln_kernel(*refs, widths):
    n = len(widths)
    x_refs, (w_ref, r_ref, g_ref, b_ref, o32_ref, o16_ref) = refs[:n], refs[n:]
    acc = None
    off = 0
    for x_ref, width in zip(x_refs, widths):
        part = _dot_tn(x_ref[0], w_ref[off:off + width, :])
        acc = part if acc is None else acc + part
        off += width
    y = _layer_norm_rows(DEEPNORM_ALPHA * r_ref[0] + acc, g_ref[...], b_ref[...])
    o32_ref[0] = y
    o16_ref[0] = y.astype(BF16)


def out_proj_ln_pallas(xts, w, res3, g, b, tm):
    bsz, s, d = res3.shape
    widths = tuple(x.shape[1] for x in xts)
    in_specs = [pl.BlockSpec((1, wd, tm), lambda i, j: (i, 0, j)) for wd in widths]
    in_specs += [pl.BlockSpec((w.shape[0], d), lambda i, j: (0, 0)),
                 pl.BlockSpec((1, tm, d), lambda i, j: (i, j, 0)),
                 pl.BlockSpec((1, d), lambda i, j: (0, 0)),
                 pl.BlockSpec((1, d), lambda i, j: (0, 0))]
    return pl.pallas_call(
        functools.partial(_out_proj_ln_kernel, widths=widths),
        grid=(bsz, s // tm),
        in_specs=in_specs,
        out_specs=[pl.BlockSpec((1, tm, d), lambda i, j: (i, j, 0)),
                   pl.BlockSpec((1, tm, d), lambda i, j: (i, j, 0))],
        out_shape=[jax.ShapeDtypeStruct((bsz, s, d), F32), jax.ShapeDtypeStruct((bsz, s, d), BF16)],
        compiler_params=_cparams(("parallel", "parallel")),
        name="out_proj_ln",
    )(*xts, w, res3, g.reshape(1, d), b.reshape(1, d))


HALO = 16
FFN_CHUNK = 256
FFN_ROWS = 1024


def _gelu_tanh(x):
    return 0.5 * x * (1.0 + jnp.tanh(math.sqrt(2.0 / math.pi) * (x + 0.044715 * x * x * x)))


def _ffn_up_kernel(x_ref, xh_ref, wu_ref, wg_ref, cw_ref, cb_ref, o_ref):
    j = pl.program_id(1)
    tm, tn = o_ref.shape[1], o_ref.shape[2]
    rows = min(FFN_ROWS, tm)
    first = jnp.where(j > 0, 1.0, 0.0)
    row = lax.broadcasted_iota(jnp.int32, (rows, FFN_CHUNK), 0)
    for c in range(tn // FFN_CHUNK):
        cols = slice(c * FFN_CHUNK, (c + 1) * FFN_CHUNK)
        wu = wu_ref[:, cols]
        wg = wg_ref[:, cols]
        cw = cw_ref[:, cols]
        cb = cb_ref[:, cols]
        gh = _dot(xh_ref[0], wg) * first
        prev1 = gh[HALO - 1:HALO, :]
        prev2 = gh[HALO - 2:HALO - 1, :]
        for r in range(tm // rows):
            x = x_ref[0, r * rows:(r + 1) * rows, :]
            u = _dot(x, wu)
            g = _dot(x, wg)
            g_m1 = jnp.where(row == 0, prev1, pltpu.roll(g, 1, axis=0))
            g_m2 = jnp.where(row == 0, prev2, jnp.where(row == 1, prev1, pltpu.roll(g, 2, axis=0)))
            gc = cb + cw[0:1] * g_m2 + cw[1:2] * g_m1 + cw[2:3] * g
            o_ref[0, r * rows:(r + 1) * rows, cols] = (_gelu_tanh(gc) * u).astype(o_ref.dtype)
            prev1 = g[rows - 1:rows, :]
            prev2 = g[rows - 2:rows - 1, :]


def ffn_up_pallas(x3, w_up, conv_w, conv_b, tm, tn):
    b, s, d = x3.shape
    dff = w_up.shape[1] // 2
    nt = dff // tn
    hb = tm // HALO
    cw = jnp.zeros((8, dff), F32).at[:CONV_WIDTH].set(conv_w)
    return pl.pallas_call(
        _ffn_up_kernel,
        grid=(b, s // tm, nt),
        in_specs=[pl.BlockSpec((1, tm, d), lambda i, j, k: (i, j, 0)),
                  pl.BlockSpec((1, HALO, d), lambda i, j, k: (i, jnp.maximum(j * hb - 1, 0), 0)),
                  pl.BlockSpec((d, tn), lambda i, j, k: (0, k)),
                  pl.BlockSpec((d, tn), lambda i, j, k: (0, k + nt)),
                  pl.BlockSpec((8, tn), lambda i, j, k: (0, k)),
                  pl.BlockSpec((1, tn), lambda i, j, k: (0, k))],
        out_specs=pl.BlockSpec((1, tm, tn), lambda i, j, k: (i, j, k)),
        out_shape=jax.ShapeDtypeStruct((b, s, dff), BF16),
        compiler_params=_cparams(("parallel", "parallel", "arbitrary")),
        name="ffn_up",
    )(x3, x3, w_up, w_up, cw, conv_b.reshape(1, dff))


def _online_step_t(st, delta, vaug, m_ref, acc_ref):
    m_prev = m_ref[...]
    m_cur = jnp.max(st, axis=0, keepdims=True) + delta
    m_new = jnp.maximum(m_prev, m_cur)
    alpha = jnp.exp2(m_prev - m_new)
    pt = jnp.exp2(st - (m_new - delta)).astype(BF16)
    acc_ref[...] = alpha * acc_ref[...] + _dot(vaug, pt)
    m_ref[...] = m_new


def _init_state_t(m_ref, acc_ref):
    m_ref[...] = jnp.full(m_ref.shape, NEG, F32)
    acc_ref[...] = jnp.zeros(acc_ref.shape, F32)


def _finish_t(acc_ref):
    acc = acc_ref[...]
    return acc[:HEAD_DIM] / acc[HEAD_DIM:HEAD_DIM + 1]


def _fill_vaug(vaug_ref, vt_ref):
    s = vaug_ref.shape[1]
    vaug_ref[0:HEAD_DIM, :] = vt_ref[0, 0]
    pad = lax.broadcasted_iota(jnp.int32, (VROWS - HEAD_DIM, s), 0)
    vaug_ref[HEAD_DIM:VROWS, :] = jnp.where(pad == 0, 1.0, 0.0).astype(BF16)


def _ktile(ref, idx, size):
    return ref[0, 0, pl.ds(pl.multiple_of(idx * size, size), size), :]


def _vtile(ref, idx, size):
    return ref[:, pl.ds(pl.multiple_of(idx * size, size), size)]


def _stage_scores(st, s_ref, mc_ref):
    nsub = mc_ref.shape[0]
    s_ref[...] = st
    if nsub == 1:
        mc_ref[...] = jnp.max(st, axis=0, keepdims=True)
    else:
        mc_ref[...] = jnp.max(st.reshape(nsub, st.shape[0] // nsub, st.shape[1]), axis=1)


def _stage_update(s_ref, mc_ref, delta, vaug, m_ref, acc_ref):
    nsub = mc_ref.shape[0]
    m_prev = m_ref[...]
    m_new = jnp.maximum(m_prev, jnp.max(mc_ref[...] + delta, axis=0, keepdims=True))
    alpha = jnp.exp2(m_prev - m_new)
    shift = m_new - delta
    if nsub == 1:
        pt = jnp.exp2(s_ref[...] - shift)
    else:
        tk, lanes = s_ref.shape
        pt = jnp.exp2(s_ref[...].reshape(nsub, tk // nsub, lanes) - shift[:, None, :]).reshape(tk, lanes)
    acc_ref[...] = alpha * acc_ref[...] + _dot(vaug, pt.astype(BF16))
    m_ref[...] = m_new


def _score_buffers(tk, lanes, nsub=1):
    return [pltpu.VMEM((tk, lanes), F32), pltpu.VMEM((nsub, lanes), F32),
            pltpu.VMEM((tk, lanes), F32), pltpu.VMEM((nsub, lanes), F32)]


def _flash_pipeline(lead, n_loop, scores, meta, bufs, m_ref, acc_ref, n_even):
    _stage_scores(lead[0][0](), *bufs[0])
    for i in range(1, len(lead)):
        _stage_scores(lead[i][0](), *bufs[i % 2])
        _stage_update(*bufs[(i - 1) % 2], *lead[i - 1][1](), m_ref, acc_ref)
    cur = (len(lead) - 1) % 2
    nxt = 1 - cur
    if scores is None:
        _stage_update(*bufs[cur], *lead[-1][1](), m_ref, acc_ref)
        return
    _stage_scores(scores(0), *bufs[nxt])
    _stage_update(*bufs[cur], *lead[-1][1](), m_ref, acc_ref)
    last = jnp.maximum(n_loop - 1, 0)

    def body(kp, carry):
        k0 = 2 * kp
        k1 = k0 + 1
        _stage_scores(scores(jnp.minimum(k1, last)), *bufs[cur])
        _stage_update(*bufs[nxt], *meta(k0), m_ref, acc_ref)
        _stage_scores(scores(jnp.minimum(k0 + 2, last)), *bufs[nxt])
        delta1, vaug1 = meta(jnp.minimum(k1, last))
        if not n_even:
            delta1 = jnp.where(k1 < n_loop, delta1, SKIP)
        _stage_update(*bufs[cur], delta1, vaug1, m_ref, acc_ref)
        return carry

    lax.fori_loop(0, (n_loop + 1) // 2, body, 0)


FOX_TQ = 1024
FOX_TK = 512


def _fox_kernel(qt_ref, k_ref, vt_ref, o_ref, m_ref, acc_ref, vaug_ref, sa_ref, ma_ref, sb_ref, mb_ref):
    tq, tk = o_ref.shape[3], FOX_TK
    nd = tq // tk
    qi = pl.program_id(2)

    @pl.when(qi == 0)
    def _():
        _fill_vaug(vaug_ref, vt_ref)

    ones3 = jnp.where(lax.broadcasted_iota(jnp.int32, (HEAD_DIM, tq), 0) < 3, 1.0, 0.0).astype(BF16)
    qaug = jnp.concatenate([qt_ref[0, 0], ones3], axis=0)
    _init_state_t(m_ref, acc_ref)
    key = lax.broadcasted_iota(jnp.int32, (tk, tq), 0)
    qry = lax.broadcasted_iota(jnp.int32, (tk, tq), 1)

    def scores(kj):
        return _dot(_ktile(k_ref, kj, tk), qaug)

    def meta(kj):
        return 0.0, _vtile(vaug_ref, kj, tk)

    lead = [(functools.partial(lambda d: jnp.where(key + d * tk <= qry, scores(qi * nd + d), NEG), d),
             functools.partial(lambda d: meta(qi * nd + d), d)) for d in range(nd)]
    _flash_pipeline(lead, qi * nd, scores, meta, ((sa_ref, ma_ref), (sb_ref, mb_ref)), m_ref, acc_ref,
                    n_even=nd % 2 == 0)
    o_ref[0, 0] = _finish_t(acc_ref).astype(o_ref.dtype)


def fox_pallas(qvt, kaug):
    b, _, dh, s = qvt.shape
    tq = min(FOX_TQ, s)
    assert tq % FOX_TK == 0
    out = pl.pallas_call(
        _fox_kernel,
        grid=(b, FOX_HEADS, s // tq),
        in_specs=[pl.BlockSpec((1, 1, dh, tq), lambda i, h, j: (i, TQ_FOX + h, 0, j)),
                  pl.BlockSpec((1, 1, s, KAUG), lambda i, h, j: (i, h, 0, 0)),
                  pl.BlockSpec((1, 1, dh, s), lambda i, h, j: (i, TV_FOX + h, 0, 0))],
        out_specs=pl.BlockSpec((1, 1, dh, tq), lambda i, h, j: (i, h, 0, j)),
        out_shape=jax.ShapeDtypeStruct((b, FOX_HEADS, dh, s), BF16),
        scratch_shapes=[pltpu.VMEM((1, tq), F32), pltpu.VMEM((VROWS, tq), F32),
                        pltpu.VMEM((VROWS, s), BF16)] + _score_buffers(FOX_TK, tq),
        compiler_params=_cparams(("parallel", "parallel", "arbitrary")),
        name="fox_attention",
    )(qvt, kaug, qvt)
    return out.reshape(b, FOX_HEADS * dh, s)


def _moba_kernel(slope_ref, qt_ref, k_ref, vt_ref, ind_ref, srow_ref, o_ref,
                 m_ref, acc_ref, vaug_ref, km_ref, sel_ref, sa_ref, ma_ref, sb_ref, mb_ref):
    t = MOBA_BLOCK
    tq = o_ref.shape[3]
    nd = tq // t
    h = pl.program_id(1)
    qi = pl.program_id(2)
    slope2 = slope_ref[h]

    @pl.when(qi == 0)
    def _():
        _fill_vaug(vaug_ref, vt_ref)
        km_ref[...] = _dot(ind_ref[...], k_ref[0, 0]) * (1.0 / MOBA_BLOCK)

    qt = qt_ref[0, 0]
    km_hi, km_lo = _split2(km_ref[...])
    q0 = jnp.concatenate([qt, jnp.zeros_like(qt)], axis=0)
    gate = _dot(km_hi, q0) + _dot(km_lo, q0)
    blk = lax.broadcasted_iota(jnp.int32, gate.shape, 0)
    lane_blk = lax.broadcasted_iota(jnp.int32, (1, tq), 1) // t
    valid = blk < qi * nd + lane_blk
    work = jnp.where(valid, gate, -jnp.inf)
    sel = jnp.zeros(gate.shape, F32)
    big = jnp.int32(2 ** 30)
    for _ in range(MOBA_TOPK):
        mx = jnp.max(work, axis=0, keepdims=True)
        first = jnp.min(jnp.where(work == mx, blk, big), axis=0, keepdims=True)
        hit = blk == first
        sel = jnp.where(hit, 1.0, sel)
        work = jnp.where(hit, -jnp.inf, work)
    sel_ref[...] = jnp.where(valid, sel, 0.0)

    qaug = jnp.concatenate([qt, srow_ref[...]], axis=0)
    _init_state_t(m_ref, acc_ref)
    tk = sa_ref.shape[0]
    nsub = tk // t
    ntile = tq // tk
    key = lax.broadcasted_iota(jnp.int32, (tk, tq), 0)
    qry = lax.broadcasted_iota(jnp.int32, (tk, tq), 1)
    sub = lax.broadcasted_iota(jnp.int32, (nsub, tq), 0)

    def tile_constant(kj, d):
        sees = jnp.concatenate([sel_ref[pl.ds(kj * nsub + c, 1), :] for c in range(nsub)], axis=0) > 0.0
        if d is not None:
            sees = sees | (lane_blk == d * nsub + sub)
        offset = ((kj * nsub + sub - qi * nd) * t).astype(F32)
        return jnp.where(sees, slope2 * offset, SKIP)

    def scores(kj):
        return _dot(_ktile(k_ref, kj, tk), qaug)

    def meta(kj):
        return tile_constant(kj, None), _vtile(vaug_ref, kj, tk)

    lead = [(functools.partial(lambda d: jnp.where(key + d * tk <= qry, scores(qi * ntile + d), NEG), d),
             functools.partial(lambda d: (tile_constant(qi * ntile + d, d),
                                          _vtile(vaug_ref, qi * ntile + d, tk)), d))
            for d in range(ntile)]
    _flash_pipeline(lead, qi * ntile, scores, meta, ((sa_ref, ma_ref), (sb_ref, mb_ref)), m_ref, acc_ref,
                    n_even=ntile % 2 == 0)
    o_ref[0, 0] = _finish_t(acc_ref).astype(o_ref.dtype)


MOBA_TQ = 1024
MOBA_TK = 512


def moba_pallas(qvt, kaug):
    b, _, dh, s = qvt.shape
    t = MOBA_BLOCK
    tq = min(MOBA_TQ, s)
    tk = min(MOBA_TK, tq)
    assert POS_PERIOD == t and tq % tk == 0 and tk % t == 0
    nb = s // t
    nbp = max(16, nb)
    ind = np.zeros((nbp, s), np.float32)
    ind[np.arange(s) // t, np.arange(s)] = 1.0
    slopes2 = _alibi_slopes2(MOBA_HEADS)
    grid_spec = pltpu.PrefetchScalarGridSpec(
        num_scalar_prefetch=1,
        grid=(b, MOBA_HEADS, s // tq),
        in_specs=[pl.BlockSpec((1, 1, dh, tq), lambda i, h, j, sl: (i, TQ_MOBA + h, 0, j)),
                  pl.BlockSpec((1, 1, s, KAUG), lambda i, h, j, sl: (i, KA_MOBA + h, 0, 0)),
                  pl.BlockSpec((1, 1, dh, s), lambda i, h, j, sl: (i, TV_MOBA + h, 0, 0)),
                  pl.BlockSpec((nbp, s), lambda i, h, j, sl: (0, 0)),
                  pl.BlockSpec((None, dh, tq), lambda i, h, j, sl: (h, 0, 0))],
        out_specs=pl.BlockSpec((1, 1, dh, tq), lambda i, h, j, sl: (i, h, 0, j)),
        scratch_shapes=[pltpu.VMEM((1, tq), F32), pltpu.VMEM((VROWS, tq), F32),
                        pltpu.VMEM((VROWS, s), BF16), pltpu.VMEM((nbp, KAUG), F32),
                        pltpu.VMEM((nbp, tq), F32)] + _score_buffers(tk, tq, tk // t))
    out = pl.pallas_call(
        _moba_kernel,
        grid_spec=grid_spec,
        out_shape=jax.ShapeDtypeStruct((b, MOBA_HEADS, dh, s), BF16),
        compiler_params=_cparams(("parallel", "parallel", "arbitrary")),
        name="moba_attention",
    )(jnp.asarray(slopes2), qvt, kaug, qvt, jnp.asarray(ind, BF16), _slope_rows(slopes2, tq))
    return out.reshape(b, MOBA_HEADS * dh, s)


def _nsa_compress_kernel(x_ref, w1a_ref, w1b_ref, pe_ref, w1_ref, w2_ref, w2t_ref, o_ref, ot_ref):
    nr = x_ref.shape[2]
    x = x_ref[0, 0]

    def near_f32(xb, w):
        w_hi, w_lo = _split2(w)
        return _dot(xb, w_hi) + _dot(xb, w_lo)

    a = near_f32(x, w1a_ref[0])
    bm = near_f32(x, w1b_ref[0])
    pe_hi, pe_mid, pe_lo = _split3(pe_ref[0])
    w1_hi, w1_lo = _split2(w1_ref[0])
    pe_term = (_dot(pe_hi, w1_hi) + _dot(pe_mid, w1_hi) + _dot(pe_lo, w1_hi)
               + _dot(pe_hi, w1_lo) + _dot(pe_mid, w1_lo))[0:1]
    pre = a + pltpu.roll(bm, nr - 1, axis=0) + pe_term
    hid = _gelu_tanh(pre)
    h_hi, h_mid, h_lo = _split3(hid)
    w2_hi, w2_lo = _split2(w2_ref[0])
    o_ref[0, 0] = (_dot(h_hi, w2_hi) + _dot(h_mid, w2_hi) + _dot(h_lo, w2_hi)
                   + _dot(h_hi, w2_lo) + _dot(h_mid, w2_lo))
    t_hi, t_lo = _split2(w2t_ref[0])
    ot_ref[0, 0] = (_dot_nt(t_hi, h_hi) + _dot_nt(t_hi, h_mid) + _dot_nt(t_hi, h_lo)
                    + _dot_nt(t_lo, h_hi) + _dot_nt(t_lo, h_mid))


def nsa_compress_pallas(nat, pe, w1, w2):
    b, _, s, dh = nat.shape
    nr = s // NSA_CMP_STRIDE
    half = NSA_CMP_STRIDE * dh
    hid = w1.shape[-1]
    x = nat[:, N_CMP:N_CMP + 4].reshape(b, 4, nr, half)
    w1f = w1.reshape(2, NSA_CMP_LEN * dh, hid)
    pef = jnp.zeros((2, 8, NSA_CMP_LEN * dh), F32).at[:, 0].set(pe.reshape(2, NSA_CMP_LEN * dh))
    return pl.pallas_call(
        _nsa_compress_kernel,
        grid=(b, 4),
        in_specs=[pl.BlockSpec((1, 1, nr, half), lambda i, j: (i, j, 0, 0)),
                  pl.BlockSpec((1, half, hid), lambda i, j: (j // 2, 0, 0)),
                  pl.BlockSpec((1, half, hid), lambda i, j: (j // 2, 1, 0)),
                  pl.BlockSpec((1, 8, 2 * half), lambda i, j: (j // 2, 0, 0)),
                  pl.BlockSpec((1, 2 * half, hid), lambda i, j: (j // 2, 0, 0)),
                  pl.BlockSpec((1, hid, dh), lambda i, j: (j // 2, 0, 0)),
                  pl.BlockSpec((1, dh, hid), lambda i, j: (j // 2, 0, 0))],
        out_specs=[pl.BlockSpec((1, 1, nr, dh), lambda i, j: (i, j, 0, 0)),
                   pl.BlockSpec((1, 1, dh, nr), lambda i, j: (i, j, 0, 0))],
        out_shape=[jax.ShapeDtypeStruct((b, 4, nr, dh), F32),
                   jax.ShapeDtypeStruct((b, 4, dh, nr), F32)],
        compiler_params=_cparams(("parallel", "parallel")),
        name="nsa_compress",
    )(x, w1f, w1f, pef, w1f, w2, jnp.swapaxes(w2, 1, 2))


NSA_TQ = 128


NSA_CMP_CHUNK = 256


def _nsa_cmp_kernel(slope_ref, qt_ref, kc_ref, vct_ref, mimpt_ref, oct_ref, selt_ref, cnt_ref):
    tq = NSA_TQ
    g = pl.program_id(1)
    qi = pl.program_id(2)
    q0 = qi * tq
    nr = kc_ref.shape[2]
    nsb = selt_ref.shape[2]
    t_lane = q0 + lax.broadcasted_iota(jnp.int32, (1, tq), 1)
    chunk = min(NSA_CMP_CHUNK, nr)
    tiles_per_chunk = chunk * NSA_CMP_STRIDE // tq

    def branch(n):
        kc_hi, kc_lo = _split2(kc_ref[0, 0, 0:n, :])
        vct = vct_ref[0, 0, :, 0:n].astype(BF16)
        cmp_end = NSA_CMP_STRIDE * lax.broadcasted_iota(jnp.int32, (n, tq), 0) + (NSA_CMP_LEN - 1)
        mask = cmp_end <= t_lane
        rel = (cmp_end - q0).astype(F32)
        psum = jnp.zeros((n, tq), F32)
        qt4 = jnp.concatenate([qt_ref[0, hh] for hh in range(NSA_GROUP)], axis=1)
        st4 = _dot(kc_hi, qt4) + _dot(kc_lo, qt4)
        probs = []
        for hh in range(NSA_GROUP):
            st = st4[:, hh * tq:(hh + 1) * tq] + slope_ref[g * NSA_GROUP + hh] * rel
            st = jnp.where(mask, st, -jnp.inf)
            m = jnp.max(st, axis=0, keepdims=True)
            m = jnp.where(m > -jnp.inf, m, 0.0)
            e = jnp.exp2(st - m)
            p = e * (1.0 / jnp.maximum(jnp.sum(e, axis=0, keepdims=True), 1e-30))
            probs.append(p.astype(BF16))
            psum = psum + p
        o4 = _dot(vct, jnp.concatenate(probs, axis=1))
        for hh in range(NSA_GROUP):
            oct_ref[0, hh] = o4[:, hh * tq:(hh + 1) * tq]
        nbk = min(nsb, n * NSA_CMP_STRIDE // NSA_SEL_BLOCK)
        p_hi, p_mid, p_lo = _split3(psum)
        mimpt = mimpt_ref[0:nbk, 0:n]
        imp = _dot(mimpt, p_hi) + _dot(mimpt, p_mid) + _dot(mimpt, p_lo)
        blk = lax.broadcasted_iota(jnp.int32, (nbk, tq), 0)
        jt = t_lane // NSA_SEL_BLOCK
        forced = (blk == 0) | (blk == jt) | (blk == jt - 1)
        imp = jnp.where(forced, NSA_FORCE_SCORE, imp)
        valid = blk * NSA_SEL_BLOCK <= t_lane
        work0 = jnp.where(valid, imp, -jnp.inf)
        big = jnp.int32(2 ** 30)

        def pick(_, work):
            mx = jnp.max(work, axis=0, keepdims=True)
            first = jnp.min(jnp.where(work == mx, blk, big), axis=0, keepdims=True)
            return jnp.where(blk == first, -jnp.inf, work)

        work = lax.fori_loop(0, min(NSA_TOPK, nsb), pick, work0)
        sel = jnp.where(valid & (work == -jnp.inf), 1.0, 0.0)
        selt_ref[0, 0, 0:nbk, :] = sel
        cnt_ref[0, 0, 0, :, 0:nbk] = _dot_nt(jnp.ones((8, tq), BF16), sel.astype(BF16))
        if nbk < nsb:
            selt_ref[0, 0, nbk:nsb, :] = jnp.zeros((nsb - nbk, tq), F32)
            cnt_ref[0, 0, 0, :, nbk:nsb] = jnp.zeros((8, nsb - nbk), F32)

    n_chunks = nr // chunk
    for c in range(n_chunks):
        pl.when(jnp.minimum(qi // tiles_per_chunk, n_chunks - 1) == c)(
            functools.partial(branch, (c + 1) * chunk))


def nsa_cmp_pallas(qvt, cmp_kv, cmp_kvt):
    b, _, dh, s = qvt.shape
    tq = NSA_TQ
    nr = cmp_kv.shape[2]
    nsb = s // NSA_SEL_BLOCK
    ratio = NSA_SEL_BLOCK // NSA_CMP_STRIDE
    front = NSA_CMP_LEN // NSA_CMP_STRIDE - 1
    n_int = ratio + front
    n_idx = np.arange(nr)[None, :]
    j_idx = np.arange(nsb)[:, None]
    mimpt = ((n_idx >= ratio * j_idx - front) & (n_idx <= ratio * j_idx + n_int - 1 - front)
             & (n_idx < nr - 1)).astype(np.float32)
    grid_spec = pltpu.PrefetchScalarGridSpec(
        num_scalar_prefetch=1,
        grid=(b, NSA_KV_HEADS, s // tq),
        in_specs=[pl.BlockSpec((1, NSA_GROUP, dh, tq), lambda i, g, j, sl: (i, TQ_NSA // NSA_GROUP + g, 0, j)),
                  pl.BlockSpec((1, 1, nr, dh), lambda i, g, j, sl: (i, g, 0, 0)),
                  pl.BlockSpec((1, 1, dh, nr), lambda i, g, j, sl: (i, 2 + g, 0, 0)),
                  pl.BlockSpec((nsb, nr), lambda i, g, j, sl: (0, 0))],
        out_specs=[pl.BlockSpec((1, NSA_GROUP, dh, tq), lambda i, g, j, sl: (i, g, 0, j)),
                   pl.BlockSpec((1, 1, nsb, tq), lambda i, g, j, sl: (i, g, 0, j)),
                   pl.BlockSpec((1, 1, 1, 8, nsb), lambda i, g, j, sl: (i, g, j, 0, 0))])
    assert nr % min(NSA_CMP_CHUNK, nr) == 0
    return pl.pallas_call(
        _nsa_cmp_kernel,
        grid_spec=grid_spec,
        out_shape=[jax.ShapeDtypeStruct((b, NSA_HEADS, dh, s), F32),
                   jax.ShapeDtypeStruct((b, NSA_KV_HEADS, nsb, s), F32),
                   jax.ShapeDtypeStruct((b, NSA_KV_HEADS, s // tq, 8, nsb), F32)],
        compiler_params=_cparams(("parallel", "parallel", "parallel")),
        name="nsa_compressed_select",
    )(jnp.asarray(_alibi_slopes2(NSA_HEADS)), qvt, cmp_kv, cmp_kvt, jnp.asarray(mimpt, BF16))


NSA_TK = 256
NSA_SEL_TQ = 256
NSA_LANES = NSA_GROUP * NSA_TQ


def _nsa_qaug(qt_ref, srow_ref):
    return jnp.concatenate(
        [jnp.concatenate([qt_ref[0, hh], srow_ref[hh]], axis=0) for hh in range(NSA_GROUP)], axis=1)


def _nsa_sel_kernel(bits_ref, qt_ref, k_ref, vt_ref, selt_ref, srow_ref, slane_ref, o_ref,
                    m_ref, acc_ref, vaug_ref, sa_ref, ma_ref, sb_ref, mb_ref, list_ref, *, nq, words):
    tq, tk = NSA_SEL_TQ, NSA_TK
    per_tile = tk // NSA_SEL_BLOCK
    qi = pl.program_id(2)
    q0 = qi * tq

    @pl.when(qi == 0)
    def _():
        _fill_vaug(vaug_ref, vt_ref)

    base = ((pl.program_id(0) * NSA_KV_HEADS + pl.program_id(1)) * nq + qi) * words
    list_ref[0] = 0

    def note(j, n):
        list_ref[n] = j
        return n + ((bits_ref[base + j // 32] >> (j % 32)) & 1)

    n_tiles = lax.fori_loop(0, qi, note, 0)

    qaug = _nsa_qaug(qt_ref, srow_ref)
    slane = slane_ref[0:1, :]
    diag = q0 // tk

    def scores(kj):
        st = _dot(_ktile(k_ref, kj, tk), qaug)
        rows = [jnp.broadcast_to(selt_ref[0, 0, pl.ds(kj * per_tile + c, 1), :], (NSA_SEL_BLOCK, tq))
                for c in range(per_tile)]
        bias = (jnp.concatenate(rows, axis=0) - 1.0) * (-NEG)
        return st + jnp.concatenate([bias] * NSA_GROUP, axis=1)

    def meta(kj):
        return slane * (kj * tk - q0).astype(F32), _vtile(vaug_ref, kj, tk)

    def own_tile():
        key = lax.broadcasted_iota(jnp.int32, (tk, tq), 0)
        qry = lax.broadcasted_iota(jnp.int32, (tk, tq), 1)
        causal = jnp.concatenate([key <= qry] * NSA_GROUP, axis=1)
        return jnp.where(causal, scores(diag), NEG)

    _init_state_t(m_ref, acc_ref)
    _flash_pipeline([(own_tile, lambda: meta(diag))], n_tiles,
                    lambda i: scores(list_ref[i]), lambda i: meta(list_ref[i]),
                    ((sa_ref, ma_ref), (sb_ref, mb_ref)), m_ref, acc_ref, n_even=False)
    out = _finish_t(acc_ref)
    for hh in range(NSA_GROUP):
        o_ref[0, hh] = out[:, hh * tq:(hh + 1) * tq]


def _nsa_tables(tq):
    slopes2 = _alibi_slopes2(NSA_HEADS)
    srow = _slope_rows(slopes2, tq)
    slane = np.repeat(slopes2.reshape(NSA_KV_HEADS, NSA_GROUP), tq, axis=1)
    slane8 = np.repeat(slane[:, None, :], 8, axis=1)
    return srow, jnp.asarray(slane8, F32)


def _active_tile_bits(cnt, tq, tk):
    b, g, nq128, _, nsb = cnt.shape
    qper, bper = tq // NSA_TQ, tk // NSA_SEL_BLOCK
    nq, nkv = nq128 // qper, nsb // bper
    act = cnt[:, :, :, 0, :].reshape(b, g, nq, qper, nkv, bper).sum(axis=(3, 5)) > 0.0
    words = -(-nkv // 32)
    act = jnp.pad(act, ((0, 0), (0, 0), (0, 0), (0, words * 32 - nkv))).reshape(b, g, nq, words, 32)
    bits = jnp.sum(act.astype(jnp.uint32) << jnp.arange(32, dtype=jnp.uint32), axis=-1, dtype=jnp.uint32)
    return lax.bitcast_convert_type(bits, jnp.int32).reshape(-1), nq, words


def nsa_sel_pallas(qvt, kaug, selt, cnt):
    b, _, dh, s = qvt.shape
    tq = NSA_SEL_TQ
    assert tq == NSA_TK
    lanes = NSA_GROUP * tq
    nsb = s // NSA_SEL_BLOCK
    srow, slane = _nsa_tables(tq)
    bits, nq, words = _active_tile_bits(cnt, tq, NSA_TK)
    grid_spec = pltpu.PrefetchScalarGridSpec(
        num_scalar_prefetch=1,
        grid=(b, NSA_KV_HEADS, nq),
        in_specs=[pl.BlockSpec((1, NSA_GROUP, dh, tq), lambda i, g, j, bt: (i, TQ_NSA // NSA_GROUP + g, 0, j)),
                  pl.BlockSpec((1, 1, s, KAUG), lambda i, g, j, bt: (i, KA_NSA + g, 0, 0)),
                  pl.BlockSpec((1, 1, dh, s), lambda i, g, j, bt: (i, TV_NSA + g, 0, 0)),
                  pl.BlockSpec((1, 1, nsb, tq), lambda i, g, j, bt: (i, g, 0, j)),
                  pl.BlockSpec((NSA_GROUP, dh, tq), lambda i, g, j, bt: (g, 0, 0)),
                  pl.BlockSpec((None, 8, lanes), lambda i, g, j, bt: (g, 0, 0))],
        out_specs=pl.BlockSpec((1, NSA_GROUP, dh, tq), lambda i, g, j, bt: (i, g, 0, j)),
        scratch_shapes=[pltpu.VMEM((1, lanes), F32), pltpu.VMEM((VROWS, lanes), F32),
                        pltpu.VMEM((VROWS, s), BF16)] + _score_buffers(NSA_TK, lanes)
        + [pltpu.SMEM((max(nq, 8),), jnp.int32)])
    return pl.pallas_call(
        functools.partial(_nsa_sel_kernel, nq=nq, words=words),
        grid_spec=grid_spec,
        out_shape=jax.ShapeDtypeStruct((b, NSA_HEADS, dh, s), F32),
        compiler_params=_cparams(("parallel", "parallel", "arbitrary")),
        name="nsa_selected",
    )(bits, qvt, kaug, qvt, selt, srow, slane)


NSA_WT = 128


def _nsa_win_kernel(qt_ref, k_ref, vt_ref, srow_ref, slane_ref, oc_ref, os_ref, gate_ref, o_ref,
                    m_ref, acc_ref, vaug_ref, sa_ref, ma_ref, sb_ref, mb_ref, gt_ref):
    tq = NSA_TQ
    wt = NSA_WT
    qi = pl.program_id(2)

    @pl.when(qi == 0)
    def _():
        _fill_vaug(vaug_ref, vt_ref)

    qaug = _nsa_qaug(qt_ref, srow_ref)
    slane = slane_ref[0:1, :]
    key = lax.broadcasted_iota(jnp.int32, (wt, tq), 0)
    qry = lax.broadcasted_iota(jnp.int32, (wt, tq), 1)
    span = NSA_WINDOW // wt

    def tile(d, keep):
        kj = jnp.maximum(qi - d, 0)

        def scores():
            st = _dot(_ktile(k_ref, kj, wt), qaug)
            if keep is not None:
                st = jnp.where(jnp.concatenate([keep] * NSA_GROUP, axis=1), st, NEG)
            return st

        def meta():
            base = (kj * wt) // POS_PERIOD * POS_PERIOD - qi * tq
            delta = jnp.where(qi - d >= 0, slane * base.astype(F32), SKIP)
            return delta, _vtile(vaug_ref, kj, wt)

        return scores, meta

    _init_state_t(m_ref, acc_ref)
    tiles = [tile(0, key <= qry)] + [tile(d, None) for d in range(1, span)] + [tile(span, key > qry)]
    _flash_pipeline(tiles, None, None, None, ((sa_ref, ma_ref), (sb_ref, mb_ref)), m_ref, acc_ref, True)

    o_w = _finish_t(acc_ref)
    gt_ref[...] = gate_ref[0].T
    for hh in range(NSA_GROUP):
        c0 = 3 * (pl.program_id(1) * NSA_GROUP + hh)
        mix = (gt_ref[pl.ds(c0, 1), :] * oc_ref[0, hh] + gt_ref[pl.ds(c0 + 1, 1), :] * os_ref[0, hh]
               + gt_ref[pl.ds(c0 + 2, 1), :] * o_w[:, hh * tq:(hh + 1) * tq])
        o_ref[0, hh] = mix.astype(o_ref.dtype)


def nsa_win_pallas(qvt, kaug, o_c, o_s, gates):
    b, _, dh, s = qvt.shape
    tq, wt = NSA_TQ, NSA_WT
    assert tq == wt and tq == LANES
    srow, slane = _nsa_tables(tq)
    head_blk = pl.BlockSpec((1, NSA_GROUP, dh, tq), lambda i, g, j: (i, g, 0, j))
    out = pl.pallas_call(
        _nsa_win_kernel,
        grid=(b, NSA_KV_HEADS, s // tq),
        in_specs=[pl.BlockSpec((1, NSA_GROUP, dh, tq), lambda i, g, j: (i, TQ_NSA // NSA_GROUP + g, 0, j)),
                  pl.BlockSpec((1, 1, s, KAUG), lambda i, g, j: (i, KA_NSA + 2 + g, 0, 0)),
                  pl.BlockSpec((1, 1, dh, s), lambda i, g, j: (i, TV_NSA + 2 + g, 0, 0)),
                  pl.BlockSpec((NSA_GROUP, dh, tq), lambda i, g, j: (g, 0, 0)),
                  pl.BlockSpec((None, 8, NSA_LANES), lambda i, g, j: (g, 0, 0)),
                  head_blk, head_blk,
                  pl.BlockSpec((1, tq, LANES), lambda i, g, j: (i, j, 0))],
        out_specs=head_blk,
        out_shape=jax.ShapeDtypeStruct((b, NSA_HEADS, dh, s), BF16),
        scratch_shapes=[pltpu.VMEM((1, NSA_LANES), F32), pltpu.VMEM((VROWS, NSA_LANES), F32),
                        pltpu.VMEM((VROWS, s), BF16)] + _score_buffers(wt, NSA_LANES)
        + [pltpu.VMEM((LANES, tq), F32)],
        compiler_params=_cparams(("parallel", "parallel", "arbitrary")),
        name="nsa_window_mix",
    )(qvt, kaug, qvt, srow, slane, o_c, o_s, gates)
    return out.reshape(b, NSA_HEADS * dh, s)


def _mem_attn_kernel(q_ref, k_ref, vt_ref, o_ref):
    for hh in range(MEM_HEADS):
        s = _dot_nt(q_ref[0, hh], k_ref[0, hh])
        m = jnp.max(s, axis=-1, keepdims=True)
        e = jnp.exp2(s - m)
        p = e / jnp.sum(e, axis=-1, keepdims=True)
        o_ref[0, hh] = _dot_nt(vt_ref[0, hh], p.astype(BF16)).astype(o_ref.dtype)


def mem_attn_pallas(nat, mem_k, mem_vt, tq=512):
    b, _, s, dh = nat.shape
    n_mem = mem_k.shape[2]
    out = pl.pallas_call(
        _mem_attn_kernel,
        grid=(b, s // tq),
        in_specs=[pl.BlockSpec((1, MEM_HEADS, tq, dh), lambda i, j: (i, N_MEMQ // MEM_HEADS, j, 0)),
                  pl.BlockSpec((1, MEM_HEADS, n_mem, dh), lambda i, j: (i, 0, 0, 0)),
                  pl.BlockSpec((1, MEM_HEADS, dh, n_mem), lambda i, j: (i, 0, 0, 0))],
        out_specs=pl.BlockSpec((1, MEM_HEADS, dh, tq), lambda i, j: (i, 0, 0, j)),
        out_shape=jax.ShapeDtypeStruct((b, MEM_HEADS, dh, s), BF16),
        compiler_params=_cparams(("parallel", "parallel")),
        name="memory_attention",
    )(nat, mem_k, mem_vt)
    return out.reshape(b, MEM_HEADS * dh, s)


def _in_proj_weights(w_in):
    hd = HEAD_DIM
    sizes = (3 * MOBA_HEADS * hd, NSA_HEADS * hd, 6 * NSA_KV_HEADS * hd, 3 * NSA_HEADS,
             3 * FOX_HEADS * hd, FOX_HEADS, MEM_HEADS * hd)
    offs = np.concatenate([[0], np.cumsum(sizes)])
    moba, nsa_q, nsa_kv, nsa_g, fox, fox_f, mem_q = (w_in[:, offs[i]:offs[i + 1]] for i in range(7))
    mh, fh, g2 = MOBA_HEADS * hd, FOX_HEADS * hd, NSA_KV_HEADS * hd
    moba_q, moba_k, moba_v = moba[:, :mh], moba[:, mh:2 * mh], moba[:, 2 * mh:]
    fox_q, fox_k, fox_v = fox[:, :fh], fox[:, fh:2 * fh], fox[:, 2 * fh:]
    k_cmp, v_cmp, k_slc, v_slc, k_win, v_win = (nsa_kv[:, i * g2:(i + 1) * g2] for i in range(6))
    w_t = jnp.concatenate([moba_q, nsa_q, fox_q, moba_v, v_slc, v_win, fox_v], axis=1).T.astype(BF16)
    t_scale = np.ones((T_SLOTS * hd,), np.float32)
    t_scale[:TV_MOBA * hd] = Q_SCALE
    w_ka = jnp.concatenate([moba_k, k_slc, k_win], axis=1).astype(BF16)
    w_kf = fox_k.astype(BF16)
    w_nat = jnp.concatenate([k_cmp, v_cmp, mem_q], axis=1).astype(BF16)
    n_scale = np.ones((N_SLOTS * hd,), np.float32)
    n_scale[N_MEMQ * hd:] = Q_SCALE
    return w_t, jnp.asarray(t_scale), w_ka, w_kf, w_nat, jnp.asarray(n_scale), nsa_g, fox_f


def _mixer(h32, h16, mem16, w_in, b_forget, w_mem_kv, cmp_pe, cmp_w1, cmp_w2):
    b, s, d = h16.shape
    tm = min(1024, s)
    w_t, t_scale, w_ka, w_kf, w_nat, n_scale, w_gate, w_forget = _in_proj_weights(w_in)
    gates, caug = gates_pallas(h32, w_gate, w_forget, b_forget)
    qvt = proj_t_pallas(h16, w_t, t_scale, tm=tm, heads_per_step=13)
    k_alibi = proj_kaug_pallas(h16, w_ka, None, tm=tm, heads_per_step=KA_SLOTS)
    k_fox = proj_kaug_pallas(h16, w_kf, caug, tm=tm, heads_per_step=FOX_HEADS)
    nat = proj_heads_pallas(h16, w_nat, n_scale, tm=tm, heads_per_step=8)
    n_mem = mem16.shape[1]
    mk = MEM_HEADS * HEAD_DIM
    mem_k = proj_heads_pallas(mem16, w_mem_kv[:, :mk].astype(BF16), jnp.ones((mk,), F32),
                              tm=n_mem, heads_per_step=MEM_HEADS)
    mem_vt = proj_t_pallas(mem16, w_mem_kv[:, mk:].T.astype(BF16), jnp.ones((mk,), F32),
                           tm=n_mem, heads_per_step=MEM_HEADS)
    o_moba = moba_pallas(qvt, k_alibi)
    o_fox = fox_pallas(qvt, k_fox)
    cmp_kv, cmp_kvt = nsa_compress_pallas(nat, cmp_pe, cmp_w1, cmp_w2)
    o_c, selt, cnt = nsa_cmp_pallas(qvt, cmp_kv, cmp_kvt)
    o_s = nsa_sel_pallas(qvt, k_alibi, selt, cnt)
    o_nsa = nsa_win_pallas(qvt, k_alibi, o_c, o_s, gates)
    o_mem = mem_attn_pallas(nat, mem_k, mem_vt)
    return [o_moba, o_nsa, o_fox, o_mem]


def kernel(x, mem, emb_ln_g, emb_ln_b, w_in, b_forget, w_mem_kv, nsa_cmp_pe, nsa_cmp_w1, nsa_cmp_w2,
           w_out, ln1_g, ln1_b, ffn_w_up, ffn_conv_w, ffn_conv_b, ffn_w_down, ln2_g, ln2_b):
    b, s, d = x.shape
    depth = w_in.shape[0]
    dff = ffn_w_down.shape[1]
    mem16 = mem.astype(BF16)
    h32, h16 = layer_norm_pallas(x.reshape(b * s, d), emb_ln_g, emb_ln_b)
    for l in range(depth):
        heads = _mixer(h32.reshape(b, s, d), h16.reshape(b, s, d), mem16, w_in[l], b_forget[l], w_mem_kv[l],
                       nsa_cmp_pe[l], nsa_cmp_w1[l], nsa_cmp_w2[l])
        h32, h16 = out_proj_ln_pallas(heads, w_out[l].astype(BF16), h32.reshape(b, s, d),
                                      ln1_g[l], ln1_b[l], tm=512)
        a = ffn_up_pallas(h16, ffn_w_up[l].astype(BF16), ffn_conv_w[l], ffn_conv_b[l],
                          tm=min(1024, s), tn=512)
        h32, h16 = matmul_ln_resident_pallas(a.reshape(b * s, dff), ffn_w_down[l].astype(BF16),
                                             h32.reshape(b * s, d), ln2_g[l], ln2_b[l], tm=256)
    return h32.reshape(b, s, d)
```

```python
import functools
import math

import jax
import jax.numpy as jnp
import ml_dtypes
import numpy as np
from jax import lax
from jax.experimental import pallas as pl
from jax.experimental.pallas import tpu as pltpu

F32 = jnp.float32
BF16 = jnp.bfloat16

HEAD_DIM = 64
MOBA_HEADS = 8
NSA_HEADS = 8
NSA_KV_HEADS = 2
NSA_GROUP = NSA_HEADS // NSA_KV_HEADS
FOX_HEADS = 12
MEM_HEADS = 4
MOBA_BLOCK = 256
MOBA_TOPK = 3
NSA_CMP_LEN = 32
NSA_CMP_STRIDE = 16
NSA_SEL_BLOCK = 64
NSA_TOPK = 16
NSA_WINDOW = 512
NSA_FORCE_SCORE = 1.0e4
CONV_WIDTH = 3
LN_EPS = 1e-5
DEPTH = 2
DEEPNORM_ALPHA = (2 * DEPTH) ** 0.25

LOG2E = math.log2(math.e)
Q_SCALE = HEAD_DIM ** -0.5 * LOG2E
NEG = -1.0e30
SKIP = -3.0e38
VMEM_LIMIT = 56 * 1024 * 1024
LANES = 128
KAUG = 2 * HEAD_DIM
VROWS = HEAD_DIM + 16
POS_PERIOD = 256

TQ_MOBA, TQ_NSA, TQ_FOX = 0, 8, 16
TV_MOBA, TV_NSA, TV_FOX = 28, 36, 40
T_SLOTS = 52
KA_MOBA, KA_NSA = 0, 8
KA_SLOTS = 12
N_CMP, N_MEMQ = 0, 4
N_SLOTS = 8


def _cparams(sem):
    return pltpu.CompilerParams(dimension_semantics=sem, vmem_limit_bytes=VMEM_LIMIT)


def _split2(x):
    hi = x.astype(BF16)
    return hi, (x - hi.astype(F32)).astype(BF16)


def _split3(x):
    hi = x.astype(BF16)
    r1 = x - hi.astype(F32)
    mid = r1.astype(BF16)
    lo = (r1 - mid.astype(F32)).astype(BF16)
    return hi, mid, lo


def _np_split3(x):
    x = np.asarray(x, np.float32)
    hi = x.astype(ml_dtypes.bfloat16).astype(np.float32)
    r1 = x - hi
    mid = r1.astype(ml_dtypes.bfloat16).astype(np.float32)
    lo = (r1 - mid).astype(ml_dtypes.bfloat16).astype(np.float32)
    return hi, mid, lo


def _dot_nt(a, b):
    return lax.dot_general(a, b, (((1,), (1,)), ((), ())), preferred_element_type=F32)


def _dot_tn(a, b):
    return lax.dot_general(a, b, (((0,), (0,)), ((), ())), preferred_element_type=F32)


def _dot(a, b):
    return jnp.dot(a, b, preferred_element_type=F32)


def _layer_norm_rows(x, g, b):
    mu = jnp.mean(x, axis=-1, keepdims=True)
    xc = x - mu
    var = jnp.mean(xc * xc, axis=-1, keepdims=True)
    return xc * lax.rsqrt(var + LN_EPS) * g + b


def _alibi_slopes2(n):
    return (np.exp2(-8.0 * np.arange(1, n + 1, dtype=np.float64) / n) * LOG2E).astype(np.float32)


def _slope_rows(slopes2, lanes):
    pieces = np.stack(_np_split3(slopes2), axis=1)
    rows = np.zeros((len(slopes2), HEAD_DIM, lanes), np.float32)
    rows[:, :3, :] = pieces[:, :, None]
    return jnp.asarray(rows, BF16)


def _ln_kernel(x_ref, g_ref, b_ref, o32_ref, o16_ref):
    y = _layer_norm_rows(x_ref[...], g_ref[...], b_ref[...])
    o32_ref[...] = y
    o16_ref[...] = y.astype(BF16)


def layer_norm_pallas(x, g, b, tm=512):
    m, d = x.shape
    return pl.pallas_call(
        _ln_kernel,
        grid=(m // tm,),
        in_specs=[pl.BlockSpec((tm, d), lambda i: (i, 0)),
                  pl.BlockSpec((1, d), lambda i: (0, 0)),
                  pl.BlockSpec((1, d), lambda i: (0, 0))],
        out_specs=[pl.BlockSpec((tm, d), lambda i: (i, 0)),
                   pl.BlockSpec((tm, d), lambda i: (i, 0))],
        out_shape=[jax.ShapeDtypeStruct((m, d), F32), jax.ShapeDtypeStruct((m, d), BF16)],
        compiler_params=_cparams(("parallel",)),
        name="layer_norm",
    )(x, g.reshape(1, d), b.reshape(1, d))


GATE_LANES_NSA = 3 * NSA_HEADS


def _gates_kernel(h_ref, w_ref, bf_ref, tri_ref, place_ref, g_ref, caug_ref, carry_ref):
    si = pl.program_id(1)

    @pl.when(si == 0)
    def _():
        carry_ref[...] = jnp.zeros_like(carry_ref)

    h_hi, h_lo = _split2(h_ref[0])
    w_hi, w_lo = _split2(w_ref[...])
    x = _dot(h_hi, w_hi) + _dot(h_lo, w_hi) + _dot(h_hi, w_lo)
    g_ref[0] = 1.0 / (1.0 + jnp.exp(-x))
    x = x + bf_ref[...]
    logf = jnp.minimum(x, 0.0) - jnp.log(1.0 + jnp.exp(-jnp.abs(x)))
    tri = tri_ref[...]
    l_hi, l_mid, l_lo = _split3(logf)
    c = _dot(tri, l_hi) + _dot(tri, l_mid) + _dot(tri, l_lo) + carry_ref[0:1, :]
    carry_ref[...] = jnp.broadcast_to(c[-1:, :], carry_ref.shape)
    n_hi, n_mid, n_lo = _split3(-LOG2E * c)
    caug = _dot(n_hi, place_ref[0]) + _dot(n_mid, place_ref[1]) + _dot(n_lo, place_ref[2])
    caug_ref[0] = caug.astype(BF16)


def gates_pallas(h3, w_gate, w_forget, b_forget, t=512):
    b, s, d = h3.shape
    lo = GATE_LANES_NSA
    w = jnp.zeros((d, LANES), F32).at[:, :lo].set(w_gate).at[:, lo:lo + FOX_HEADS].set(w_forget)
    bf = jnp.zeros((1, LANES), F32).at[0, lo:lo + FOX_HEADS].set(b_forget)
    tri = (np.arange(t)[None, :] <= np.arange(t)[:, None]).astype(np.float32)
    nc = FOX_HEADS * HEAD_DIM
    place = np.zeros((3, LANES, nc), np.float32)
    for piece in range(3):
        for hh in range(FOX_HEADS):
            place[piece, lo + hh, hh * HEAD_DIM + piece] = 1.0
    return pl.pallas_call(
        _gates_kernel,
        grid=(b, s // t),
        in_specs=[pl.BlockSpec((1, t, d), lambda i, j: (i, j, 0)),
                  pl.BlockSpec((d, LANES), lambda i, j: (0, 0)),
                  pl.BlockSpec((1, LANES), lambda i, j: (0, 0)),
                  pl.BlockSpec((t, t), lambda i, j: (0, 0)),
                  pl.BlockSpec((3, LANES, nc), lambda i, j: (0, 0, 0))],
        out_specs=[pl.BlockSpec((1, t, LANES), lambda i, j: (i, j, 0)),
                   pl.BlockSpec((1, t, nc), lambda i, j: (i, j, 0))],
        out_shape=[jax.ShapeDtypeStruct((b, s, LANES), F32),
                   jax.ShapeDtypeStruct((b, s, nc), BF16)],
        scratch_shapes=[pltpu.VMEM((8, LANES), F32)],
        compiler_params=_cparams(("parallel", "arbitrary")),
        name="gates_cumsum",
    )(h3, w, bf, jnp.asarray(tri, BF16), jnp.asarray(place, BF16))


def _proj_heads_kernel(x_ref, w_ref, sc_ref, o_ref, *, heads_per_step):
    acc = _dot(x_ref[0], w_ref[...]) * sc_ref[...]
    for j in range(heads_per_step):
        o_ref[0, j] = acc[:, j * HEAD_DIM:(j + 1) * HEAD_DIM].astype(o_ref.dtype)


def proj_heads_pallas(x3, w, colscale, tm, heads_per_step):
    b, s, d = x3.shape
    n = w.shape[1]
    tn = heads_per_step * HEAD_DIM
    return pl.pallas_call(
        functools.partial(_proj_heads_kernel, heads_per_step=heads_per_step),
        grid=(b, s // tm, n // tn),
        in_specs=[pl.BlockSpec((1, tm, d), lambda i, j, k: (i, j, 0)),
                  pl.BlockSpec((d, tn), lambda i, j, k: (0, k)),
                  pl.BlockSpec((1, tn), lambda i, j, k: (0, k))],
        out_specs=pl.BlockSpec((1, heads_per_step, tm, HEAD_DIM), lambda i, j, k: (i, k, j, 0)),
        out_shape=jax.ShapeDtypeStruct((b, n // HEAD_DIM, s, HEAD_DIM), BF16),
        compiler_params=_cparams(("parallel", "parallel", "arbitrary")),
        name="proj_heads",
    )(x3, w, colscale.reshape(1, n))


def _proj_t_kernel(x_ref, wt_ref, sc_ref, o_ref, *, heads_per_step):
    acc = _dot_nt(wt_ref[...], x_ref[0]) * sc_ref[...]
    o_ref[0] = acc.reshape(heads_per_step, HEAD_DIM, acc.shape[1]).astype(o_ref.dtype)


def proj_t_pallas(x3, wt, rowscale, tm, heads_per_step):
    b, s, d = x3.shape
    n = wt.shape[0]
    tn = heads_per_step * HEAD_DIM
    return pl.pallas_call(
        functools.partial(_proj_t_kernel, heads_per_step=heads_per_step),
        grid=(b, s // tm, n // tn),
        in_specs=[pl.BlockSpec((1, tm, d), lambda i, j, k: (i, j, 0)),
                  pl.BlockSpec((tn, d), lambda i, j, k: (k, 0)),
                  pl.BlockSpec((tn, 1), lambda i, j, k: (k, 0))],
        out_specs=pl.BlockSpec((1, heads_per_step, HEAD_DIM, tm), lambda i, j, k: (i, k, 0, j)),
        out_shape=jax.ShapeDtypeStruct((b, n // HEAD_DIM, HEAD_DIM, s), BF16),
        compiler_params=_cparams(("parallel", "parallel", "arbitrary")),
        name="proj_transposed",
    )(x3, wt, rowscale.reshape(n, 1))


def _proj_kaug_kernel(x_ref, w_ref, *rest, heads_per_step, positional):
    o_ref = rest[-1]
    acc = _dot(x_ref[0], w_ref[...])
    tm = acc.shape[0]
    if positional:
        pos = (pl.program_id(1) * tm + lax.broadcasted_iota(jnp.int32, (tm, HEAD_DIM), 0)) % POS_PERIOD
        lane = lax.broadcasted_iota(jnp.int32, (tm, HEAD_DIM), 1)
        pos_lanes = jnp.where(lane < 3, pos.astype(F32), 0.0).astype(o_ref.dtype)
    for j in range(heads_per_step):
        cols = slice(j * HEAD_DIM, (j + 1) * HEAD_DIM)
        bias = pos_lanes if positional else rest[0][0, :, cols]
        o_ref[0, j] = jnp.concatenate([acc[:, cols].astype(o_ref.dtype), bias], axis=1)


def proj_kaug_pallas(x3, w, aug, tm, heads_per_step):
    b, s, d = x3.shape
    n = w.shape[1]
    tn = heads_per_step * HEAD_DIM
    in_specs = [pl.BlockSpec((1, tm, d), lambda i, j, k: (i, j, 0)),
                pl.BlockSpec((d, tn), lambda i, j, k: (0, k))]
    args = [x3, w]
    if aug is not None:
        in_specs.append(pl.BlockSpec((1, tm, tn), lambda i, j, k: (i, j, k)))
        args.append(aug)
    return pl.pallas_call(
        functools.partial(_proj_kaug_kernel, heads_per_step=heads_per_step, positional=aug is None),
        grid=(b, s // tm, n // tn),
        in_specs=in_specs,
        out_specs=pl.BlockSpec((1, heads_per_step, tm, KAUG), lambda i, j, k: (i, k, j, 0)),
        out_shape=jax.ShapeDtypeStruct((b, n // HEAD_DIM, s, KAUG), BF16),
        compiler_params=_cparams(("parallel", "parallel", "arbitrary")),
        name="proj_keys_aug",
    )(*args)


def _matmul_ln_resident_kernel(x_ref, w_ref, r_ref, g_ref, b_ref, o32_ref, o16_ref):
    y = _layer_norm_rows(DEEPNORM_ALPHA * r_ref[...] + _dot(x_ref[...], w_ref[...]), g_ref[...], b_ref[...])
    o32_ref[...] = y
    o16_ref[...] = y.astype(BF16)


def matmul_ln_resident_pallas(x, w, res, g, b, tm):
    m, kk = x.shape
    d = w.shape[1]
    return pl.pallas_call(
        _matmul_ln_resident_kernel,
        grid=(m // tm,),
        in_specs=[pl.BlockSpec((tm, kk), lambda i: (i, 0)),
                  pl.BlockSpec((kk, d), lambda i: (0, 0), pipeline_mode=pl.Buffered(1)),
                  pl.BlockSpec((tm, d), lambda i: (i, 0)),
                  pl.BlockSpec((1, d), lambda i: (0, 0)),
                  pl.BlockSpec((1, d), lambda i: (0, 0))],
        out_specs=[pl.BlockSpec((tm, d), lambda i: (i, 0)),
                   pl.BlockSpec((tm, d), lambda i: (i, 0))],
        out_shape=[jax.ShapeDtypeStruct((m, d), F32), jax.ShapeDtypeStruct((m, d), BF16)],
        compiler_params=_cparams(("parallel",)),
        name="matmul_ln_resident",
    )(x, w, res, g.reshape(1, d), b.reshape(1, d))


def _out_proj_ln_kernel(*refs, widths):
    n = len(widths)
    x_refs, (w_ref, r_ref, g_ref, b_ref, o32_ref, o16_ref) = refs[:n], refs[n:]
    acc = None
    off = 0
    for x_ref, width in zip(x_refs, widths):
        part = _dot_tn(x_ref[0], w_ref[off:off + width, :])
        acc = part if acc is None else acc + part
        off += width
    y = _layer_norm_rows(DEEPNORM_ALPHA * r_ref[0] + acc, g_ref[...], b_ref[...])
    o32_ref[0] = y
    o16_ref[0] = y.astype(BF16)


def out_proj_ln_pallas(xts, w, res3, g, b, tm):
    bsz, s, d = res3.shape
    widths = tuple(x.shape[1] for x in xts)
    in_specs = [pl.BlockSpec((1, wd, tm), lambda i, j: (i, 0, j)) for wd in widths]
    in_specs += [pl.BlockSpec((w.shape[0], d), lambda i, j: (0, 0)),
                 pl.BlockSpec((1, tm, d), lambda i, j: (i, j, 0)),
                 pl.BlockSpec((1, d), lambda i, j: (0, 0)),
                 pl.BlockSpec((1, d), lambda i, j: (0, 0))]
    return pl.pallas_call(
        functools.partial(_out_proj_ln_kernel, widths=widths),
        grid=(bsz, s // tm),
        in_specs=in_specs,
        out_specs=[pl.BlockSpec((1, tm, d), lambda i, j: (i, j, 0)),
                   pl.BlockSpec((1, tm, d), lambda i, j: (i, j, 0))],
        out_shape=[jax.ShapeDtypeStruct((bsz, s, d), F32), jax.ShapeDtypeStruct((bsz, s, d), BF16)],
        compiler_params=_cparams(("parallel", "parallel")),
        name="out_proj_ln",
    )(*xts, w, res3, g.reshape(1, d), b.reshape(1, d))


HALO = 16
FFN_CHUNK = 256


def _gelu_tanh(x):
    return 0.5 * x * (1.0 + jnp.tanh(math.sqrt(2.0 / math.pi) * (x + 0.044715 * x * x * x)))


def _ffn_up_kernel(x_ref, xh_ref, wu_ref, wg_ref, cw_ref, cb_ref, o_ref):
    j = pl.program_id(1)
    x = x_ref[0]
    xh = xh_ref[0]
    first = jnp.where(j > 0, 1.0, 0.0)
    tn = o_ref.shape[2]
    row = lax.broadcasted_iota(jnp.int32, (x.shape[0], FFN_CHUNK), 0)
    for c in range(tn // FFN_CHUNK):
        cols = slice(c * FFN_CHUNK, (c + 1) * FFN_CHUNK)
        u = _dot(x, wu_ref[:, cols])
        g = _dot(x, wg_ref[:, cols])
        gh = _dot(xh, wg_ref[:, cols]) * first
        prev1 = gh[HALO - 1:HALO, :]
        prev2 = gh[HALO - 2:HALO - 1, :]
        g_m1 = jnp.where(row == 0, prev1, pltpu.roll(g, 1, axis=0))
        g_m2 = jnp.where(row == 0, prev2, jnp.where(row == 1, prev1, pltpu.roll(g, 2, axis=0)))
        cw = cw_ref[:, cols]
        gc = cb_ref[:, cols] + cw[0:1] * g_m2 + cw[1:2] * g_m1 + cw[2:3] * g
        o_ref[0, :, cols] = (_gelu_tanh(gc) * u).astype(o_ref.dtype)


def ffn_up_pallas(x3, w_up, conv_w, conv_b, tm, tn):
    b, s, d = x3.shape
    dff = w_up.shape[1] // 2
    nt = dff // tn
    hb = tm // HALO
    cw = jnp.zeros((8, dff), F32).at[:CONV_WIDTH].set(conv_w)
    return pl.pallas_call(
        _ffn_up_kernel,
        grid=(b, s // tm, nt),
        in_specs=[pl.BlockSpec((1, tm, d), lambda i, j, k: (i, j, 0)),
                  pl.BlockSpec((1, HALO, d), lambda i, j, k: (i, jnp.maximum(j * hb - 1, 0), 0)),
                  pl.BlockSpec((d, tn), lambda i, j, k: (0, k)),
                  pl.BlockSpec((d, tn), lambda i, j, k: (0, k + nt)),
                  pl.BlockSpec((8, tn), lambda i, j, k: (0, k)),
                  pl.BlockSpec((1, tn), lambda i, j, k: (0, k))],
        out_specs=pl.BlockSpec((1, tm, tn), lambda i, j, k: (i, j, k)),
        out_shape=jax.ShapeDtypeStruct((b, s, dff), BF16),
        compiler_params=_cparams(("parallel", "parallel", "arbitrary")),
        name="ffn_up",
    )(x3, x3, w_up, w_up, cw, conv_b.reshape(1, dff))


def _init_state_t(m_ref, acc_ref):
    m_ref[...] = jnp.full(m_ref.shape, NEG, F32)
    acc_ref[...] = jnp.zeros(acc_ref.shape, F32)


def _finish_t(acc_ref):
    acc = acc_ref[...]
    return acc[:HEAD_DIM] / acc[HEAD_DIM:HEAD_DIM + 1]


def _fill_vaug(vaug_ref, vt_ref):
    s = vaug_ref.shape[1]
    vaug_ref[0:HEAD_DIM, :] = vt_ref[0, 0]
    pad = lax.broadcasted_iota(jnp.int32, (VROWS - HEAD_DIM, s), 0)
    vaug_ref[HEAD_DIM:VROWS, :] = jnp.where(pad == 0, 1.0, 0.0).astype(BF16)


def _ktile(ref, idx, size):
    return ref[0, 0, pl.ds(pl.multiple_of(idx * size, size), size), :]


def _vtile(ref, idx, size):
    return ref[:, pl.ds(pl.multiple_of(idx * size, size), size)]


def _stage_scores(st, s_ref, mc_ref):
    nsub = mc_ref.shape[0]
    s_ref[...] = st
    if nsub == 1:
        mc_ref[...] = jnp.max(st, axis=0, keepdims=True)
    else:
        mc_ref[...] = jnp.max(st.reshape(nsub, st.shape[0] // nsub, st.shape[1]), axis=1)


def _stage_update(s_ref, mc_ref, delta, vaug, m_ref, acc_ref):
    nsub = mc_ref.shape[0]
    m_prev = m_ref[...]
    m_new = jnp.maximum(m_prev, jnp.max(mc_ref[...] + delta, axis=0, keepdims=True))
    alpha = jnp.exp2(m_prev - m_new)
    shift = m_new - delta
    if nsub == 1:
        pt = jnp.exp2(s_ref[...] - shift)
    else:
        tk, lanes = s_ref.shape
        pt = jnp.exp2(s_ref[...].reshape(nsub, tk // nsub, lanes) - shift[:, None, :]).reshape(tk, lanes)
    acc_ref[...] = alpha * acc_ref[...] + _dot(vaug, pt.astype(BF16))
    m_ref[...] = m_new


def _score_buffers(tk, lanes, nsub=1):
    return [pltpu.VMEM((tk, lanes), F32), pltpu.VMEM((nsub, lanes), F32),
            pltpu.VMEM((tk, lanes), F32), pltpu.VMEM((nsub, lanes), F32)]


def _flash_pipeline(lead, n_loop, scores, meta, bufs, m_ref, acc_ref, n_even):
    _stage_scores(lead[0][0](), *bufs[0])
    for i in range(1, len(lead)):
        _stage_scores(lead[i][0](), *bufs[i % 2])
        _stage_update(*bufs[(i - 1) % 2], *lead[i - 1][1](), m_ref, acc_ref)
    cur = (len(lead) - 1) % 2
    nxt = 1 - cur
    if scores is None:
        _stage_update(*bufs[cur], *lead[-1][1](), m_ref, acc_ref)
        return
    _stage_scores(scores(0), *bufs[nxt])
    _stage_update(*bufs[cur], *lead[-1][1](), m_ref, acc_ref)
    last = jnp.maximum(n_loop - 1, 0)

    def body(kp, carry):
        k0 = 2 * kp
        k1 = k0 + 1
        _stage_scores(scores(jnp.minimum(k1, last)), *bufs[cur])
        _stage_update(*bufs[nxt], *meta(k0), m_ref, acc_ref)
        _stage_scores(scores(jnp.minimum(k0 + 2, last)), *bufs[nxt])
        delta1, vaug1 = meta(jnp.minimum(k1, last))
        if not n_even:
            delta1 = jnp.where(k1 < n_loop, delta1, SKIP)
        _stage_update(*bufs[cur], delta1, vaug1, m_ref, acc_ref)
        return carry

    lax.fori_loop(0, (n_loop + 1) // 2, body, 0)


FOX_TQ = 1024
FOX_TK = 512


def _fox_kernel(qt_ref, k_ref, vt_ref, o_ref, m_ref, acc_ref, vaug_ref, sa_ref, ma_ref, sb_ref, mb_ref):
    tq, tk = o_ref.shape[3], FOX_TK
    nd = tq // tk
    qi = pl.program_id(2)

    @pl.when(qi == 0)
    def _():
        _fill_vaug(vaug_ref, vt_ref)

    ones3 = jnp.where(lax.broadcasted_iota(jnp.int32, (HEAD_DIM, tq), 0) < 3, 1.0, 0.0).astype(BF16)
    qaug = jnp.concatenate([qt_ref[0, 0], ones3], axis=0)
    _init_state_t(m_ref, acc_ref)
    key = lax.broadcasted_iota(jnp.int32, (tk, tq), 0)
    qry = lax.broadcasted_iota(jnp.int32, (tk, tq), 1)

    def scores(kj):
        return _dot(_ktile(k_ref, kj, tk), qaug)

    def meta(kj):
        return 0.0, _vtile(vaug_ref, kj, tk)

    lead = [(functools.partial(lambda d: jnp.where(key + d * tk <= qry, scores(qi * nd + d), NEG), d),
             functools.partial(lambda d: meta(qi * nd + d), d)) for d in range(nd)]
    _flash_pipeline(lead, qi * nd, scores, meta, ((sa_ref, ma_ref), (sb_ref, mb_ref)), m_ref, acc_ref,
                    n_even=nd % 2 == 0)
    o_ref[0, 0] = _finish_t(acc_ref).astype(o_ref.dtype)


def fox_pallas(qvt, kaug):
    b, _, dh, s = qvt.shape
    tq = min(FOX_TQ, s)
    assert tq % FOX_TK == 0
    out = pl.pallas_call(
        _fox_kernel,
        grid=(b, FOX_HEADS, s // tq),
        in_specs=[pl.BlockSpec((1, 1, dh, tq), lambda i, h, j: (i, TQ_FOX + h, 0, j)),
                  pl.BlockSpec((1, 1, s, KAUG), lambda i, h, j: (i, h, 0, 0)),
                  pl.BlockSpec((1, 1, dh, s), lambda i, h, j: (i, TV_FOX + h, 0, 0))],
        out_specs=pl.BlockSpec((1, 1, dh, tq), lambda i, h, j: (i, h, 0, j)),
        out_shape=jax.ShapeDtypeStruct((b, FOX_HEADS, dh, s), BF16),
        scratch_shapes=[pltpu.VMEM((1, tq), F32), pltpu.VMEM((VROWS, tq), F32),
                        pltpu.VMEM((VROWS, s), BF16)] + _score_buffers(FOX_TK, tq),
        compiler_params=_cparams(("parallel", "parallel", "arbitrary")),
        name="fox_attention",
    )(qvt, kaug, qvt)
    return out.reshape(b, FOX_HEADS * dh, s)


def _moba_kernel(slope_ref, qt_ref, k_ref, vt_ref, ind_ref, srow_ref, o_ref,
                 m_ref, acc_ref, vaug_ref, km_ref, sel_ref, sa_ref, ma_ref, sb_ref, mb_ref):
    t = MOBA_BLOCK
    tq = o_ref.shape[3]
    nd = tq // t
    h = pl.program_id(1)
    qi = pl.program_id(2)
    slope2 = slope_ref[h]

    @pl.when(qi == 0)
    def _():
        _fill_vaug(vaug_ref, vt_ref)
        km_ref[...] = _dot(ind_ref[...], k_ref[0, 0]) * (1.0 / MOBA_BLOCK)

    qt = qt_ref[0, 0]
    km_hi, km_lo = _split2(km_ref[...])
    q0 = jnp.concatenate([qt, jnp.zeros_like(qt)], axis=0)
    gate = _dot(km_hi, q0) + _dot(km_lo, q0)
    blk = lax.broadcasted_iota(jnp.int32, gate.shape, 0)
    lane_blk = lax.broadcasted_iota(jnp.int32, (1, tq), 1) // t
    valid = blk < qi * nd + lane_blk
    work = jnp.where(valid, gate, -jnp.inf)
    sel = jnp.zeros(gate.shape, F32)
    big = jnp.int32(2 ** 30)
    for _ in range(MOBA_TOPK):
        mx = jnp.max(work, axis=0, keepdims=True)
        first = jnp.min(jnp.where(work == mx, blk, big), axis=0, keepdims=True)
        hit = blk == first
        sel = jnp.where(hit, 1.0, sel)
        work = jnp.where(hit, -jnp.inf, work)
    sel_ref[...] = jnp.where(valid, sel, 0.0)

    qaug = jnp.concatenate([qt, srow_ref[...]], axis=0)
    _init_state_t(m_ref, acc_ref)
    tk = sa_ref.shape[0]
    nsub = tk // t
    ntile = tq // tk
    key = lax.broadcasted_iota(jnp.int32, (tk, tq), 0)
    qry = lax.broadcasted_iota(jnp.int32, (tk, tq), 1)
    sub = lax.broadcasted_iota(jnp.int32, (nsub, tq), 0)

    def tile_constant(kj, d):
        sees = jnp.concatenate([sel_ref[pl.ds(kj * nsub + c, 1), :] for c in range(nsub)], axis=0) > 0.0
        if d is not None:
            sees = sees | (lane_blk == d * nsub + sub)
        offset = ((kj * nsub + sub - qi * nd) * t).astype(F32)
        return jnp.where(sees, slope2 * offset, SKIP)

    def scores(kj):
        return _dot(_ktile(k_ref, kj, tk), qaug)

    def meta(kj):
        return tile_constant(kj, None), _vtile(vaug_ref, kj, tk)

    lead = [(functools.partial(lambda d: jnp.where(key + d * tk <= qry, scores(qi * ntile + d), NEG), d),
             functools.partial(lambda d: (tile_constant(qi * ntile + d, d),
                                          _vtile(vaug_ref, qi * ntile + d, tk)), d))
            for d in range(ntile)]
    _flash_pipeline(lead, qi * ntile, scores, meta, ((sa_ref, ma_ref), (sb_ref, mb_ref)), m_ref, acc_ref,
                    n_even=ntile % 2 == 0)
    o_ref[0, 0] = _finish_t(acc_ref).astype(o_ref.dtype)


MOBA_TQ = 1024
MOBA_TK = 512


def moba_pallas(qvt, kaug):
    b, _, dh, s = qvt.shape
    t = MOBA_BLOCK
    tq = min(MOBA_TQ, s)
    tk = min(MOBA_TK, tq)
    assert POS_PERIOD == t and tq % tk == 0 and tk % t == 0
    nb = s // t
    nbp = max(16, nb)
    ind = np.zeros((nbp, s), np.float32)
    ind[np.arange(s) // t, np.arange(s)] = 1.0
    slopes2 = _alibi_slopes2(MOBA_HEADS)
    grid_spec = pltpu.PrefetchScalarGridSpec(
        num_scalar_prefetch=1,
        grid=(b, MOBA_HEADS, s // tq),
        in_specs=[pl.BlockSpec((1, 1, dh, tq), lambda i, h, j, sl: (i, TQ_MOBA + h, 0, j)),
                  pl.BlockSpec((1, 1, s, KAUG), lambda i, h, j, sl: (i, KA_MOBA + h, 0, 0)),
                  pl.BlockSpec((1, 1, dh, s), lambda i, h, j, sl: (i, TV_MOBA + h, 0, 0)),
                  pl.BlockSpec((nbp, s), lambda i, h, j, sl: (0, 0)),
                  pl.BlockSpec((None, dh, tq), lambda i, h, j, sl: (h, 0, 0))],
        out_specs=pl.BlockSpec((1, 1, dh, tq), lambda i, h, j, sl: (i, h, 0, j)),
        scratch_shapes=[pltpu.VMEM((1, tq), F32), pltpu.VMEM((VROWS, tq), F32),
                        pltpu.VMEM((VROWS, s), BF16), pltpu.VMEM((nbp, KAUG), F32),
                        pltpu.VMEM((nbp, tq), F32)] + _score_buffers(tk, tq, tk // t))
    out = pl.pallas_call(
        _moba_kernel,
        grid_spec=grid_spec,
        out_shape=jax.ShapeDtypeStruct((b, MOBA_HEADS, dh, s), BF16),
        compiler_params=_cparams(("parallel", "parallel", "arbitrary")),
        name="moba_attention",
    )(jnp.asarray(slopes2), qvt, kaug, qvt, jnp.asarray(ind, BF16), _slope_rows(slopes2, tq))
    return out.reshape(b, MOBA_HEADS * dh, s)


def _nsa_compress_kernel(x_ref, w1a_ref, w1b_ref, pe_ref, w1_ref, w2_ref, w2t_ref, o_ref, ot_ref):
    nr = x_ref.shape[2]
    x = x_ref[0, 0]

    def near_f32(xb, w):
        w_hi, w_lo = _split2(w)
        return _dot(xb, w_hi) + _dot(xb, w_lo)

    a = near_f32(x, w1a_ref[0])
    bm = near_f32(x, w1b_ref[0])
    pe_hi, pe_mid, pe_lo = _split3(pe_ref[0])
    w1_hi, w1_lo = _split2(w1_ref[0])
    pe_term = (_dot(pe_hi, w1_hi) + _dot(pe_mid, w1_hi) + _dot(pe_lo, w1_hi)
               + _dot(pe_hi, w1_lo) + _dot(pe_mid, w1_lo))[0:1]
    pre = a + pltpu.roll(bm, nr - 1, axis=0) + pe_term
    hid = _gelu_tanh(pre)
    h_hi, h_mid, h_lo = _split3(hid)
    w2_hi, w2_lo = _split2(w2_ref[0])
    o_ref[0, 0] = (_dot(h_hi, w2_hi) + _dot(h_mid, w2_hi) + _dot(h_lo, w2_hi)
                   + _dot(h_hi, w2_lo) + _dot(h_mid, w2_lo))
    t_hi, t_lo = _split2(w2t_ref[0])
    ot_ref[0, 0] = (_dot_nt(t_hi, h_hi) + _dot_nt(t_hi, h_mid) + _dot_nt(t_hi, h_lo)
                    + _dot_nt(t_lo, h_hi) + _dot_nt(t_lo, h_mid))


def nsa_compress_pallas(nat, pe, w1, w2):
    b, _, s, dh = nat.shape
    nr = s // NSA_CMP_STRIDE
    half = NSA_CMP_STRIDE * dh
    hid = w1.shape[-1]
    x = nat[:, N_CMP:N_CMP + 4].reshape(b, 4, nr, half)
    w1f = w1.reshape(2, NSA_CMP_LEN * dh, hid)
    pef = jnp.zeros((2, 8, NSA_CMP_LEN * dh), F32).at[:, 0].set(pe.reshape(2, NSA_CMP_LEN * dh))
    return pl.pallas_call(
        _nsa_compress_kernel,
        grid=(b, 4),
        in_specs=[pl.BlockSpec((1, 1, nr, half), lambda i, j: (i, j, 0, 0)),
                  pl.BlockSpec((1, half, hid), lambda i, j: (j // 2, 0, 0)),
                  pl.BlockSpec((1, half, hid), lambda i, j: (j // 2, 1, 0)),
                  pl.BlockSpec((1, 8, 2 * half), lambda i, j: (j // 2, 0, 0)),
                  pl.BlockSpec((1, 2 * half, hid), lambda i, j: (j // 2, 0, 0)),
                  pl.BlockSpec((1, hid, dh), lambda i, j: (j // 2, 0, 0)),
                  pl.BlockSpec((1, dh, hid), lambda i, j: (j // 2, 0, 0))],
        out_specs=[pl.BlockSpec((1, 1, nr, dh), lambda i, j: (i, j, 0, 0)),
                   pl.BlockSpec((1, 1, dh, nr), lambda i, j: (i, j, 0, 0))],
        out_shape=[jax.ShapeDtypeStruct((b, 4, nr, dh), F32),
                   jax.ShapeDtypeStruct((b, 4, dh, nr), F32)],
        compiler_params=_cparams(("parallel", "parallel")),
        name="nsa_compress",
    )(x, w1f, w1f, pef, w1f, w2, jnp.swapaxes(w2, 1, 2))


NSA_TQ = 128


NSA_CMP_CHUNK = 256


def _nsa_cmp_kernel(slope_ref, qt_ref, kc_ref, vct_ref, mimpt_ref, oct_ref, selt_ref, cnt_ref):
    tq = NSA_TQ
    g = pl.program_id(1)
    qi = pl.program_id(2)
    q0 = qi * tq
    nr = kc_ref.shape[2]
    nsb = selt_ref.shape[2]
    t_lane = q0 + lax.broadcasted_iota(jnp.int32, (1, tq), 1)
    chunk = min(NSA_CMP_CHUNK, nr)
    tiles_per_chunk = chunk * NSA_CMP_STRIDE // tq

    def branch(n):
        kc_hi, kc_lo = _split2(kc_ref[0, 0, 0:n, :])
        vct = vct_ref[0, 0, :, 0:n].astype(BF16)
        cmp_end = NSA_CMP_STRIDE * lax.broadcasted_iota(jnp.int32, (n, tq), 0) + (NSA_CMP_LEN - 1)
        mask = cmp_end <= t_lane
        rel = (cmp_end - q0).astype(F32)
        psum = jnp.zeros((n, tq), F32)
        qt4 = jnp.concatenate([qt_ref[0, hh] for hh in range(NSA_GROUP)], axis=1)
        st4 = _dot(kc_hi, qt4) + _dot(kc_lo, qt4)
        probs = []
        for hh in range(NSA_GROUP):
            st = st4[:, hh * tq:(hh + 1) * tq] + slope_ref[g * NSA_GROUP + hh] * rel
            st = jnp.where(mask, st, -jnp.inf)
            m = jnp.max(st, axis=0, keepdims=True)
            m = jnp.where(m > -jnp.inf, m, 0.0)
            e = jnp.exp2(st - m)
            p = e * (1.0 / jnp.maximum(jnp.sum(e, axis=0, keepdims=True), 1e-30))
            probs.append(p.astype(BF16))
            psum = psum + p
        o4 = _dot(vct, jnp.concatenate(probs, axis=1))
        for hh in range(NSA_GROUP):
            oct_ref[0, hh] = o4[:, hh * tq:(hh + 1) * tq]
        nbk = min(nsb, n * NSA_CMP_STRIDE // NSA_SEL_BLOCK)
        p_hi, p_mid, p_lo = _split3(psum)
        mimpt = mimpt_ref[0:nbk, 0:n]
        imp = _dot(mimpt, p_hi) + _dot(mimpt, p_mid) + _dot(mimpt, p_lo)
        blk = lax.broadcasted_iota(jnp.int32, (nbk, tq), 0)
        jt = t_lane // NSA_SEL_BLOCK
        forced = (blk == 0) | (blk == jt) | (blk == jt - 1)
        imp = jnp.where(forced, NSA_FORCE_SCORE, imp)
        valid = blk * NSA_SEL_BLOCK <= t_lane
        work0 = jnp.where(valid, imp, -jnp.inf)
        big = jnp.int32(2 ** 30)

        def pick(_, work):
            mx = jnp.max(work, axis=0, keepdims=True)
            first = jnp.min(jnp.where(work == mx, blk, big), axis=0, keepdims=True)
            return jnp.where(blk == first, -jnp.inf, work)

        work = lax.fori_loop(0, min(NSA_TOPK, nsb), pick, work0)
        sel = jnp.where(valid & (work == -jnp.inf), 1.0, 0.0)
        selt_ref[0, 0, 0:nbk, :] = sel
        cnt_ref[0, 0, 0, :, 0:nbk] = _dot_nt(jnp.ones((8, tq), BF16), sel.astype(BF16))
        if nbk < nsb:
            selt_ref[0, 0, nbk:nsb, :] = jnp.zeros((nsb - nbk, tq), F32)
            cnt_ref[0, 0, 0, :, nbk:nsb] = jnp.zeros((8, nsb - nbk), F32)

    n_chunks = nr // chunk
    for c in range(n_chunks):
        pl.when(jnp.minimum(qi // tiles_per_chunk, n_chunks - 1) == c)(
            functools.partial(branch, (c + 1) * chunk))


def nsa_cmp_pallas(qvt, cmp_kv, cmp_kvt):
    b, _, dh, s = qvt.shape
    tq = NSA_TQ
    nr = cmp_kv.shape[2]
    nsb = s // NSA_SEL_BLOCK
    ratio = NSA_SEL_BLOCK // NSA_CMP_STRIDE
    front = NSA_CMP_LEN // NSA_CMP_STRIDE - 1
    n_int = ratio + front
    n_idx = np.arange(nr)[None, :]
    j_idx = np.arange(nsb)[:, None]
    mimpt = ((n_idx >= ratio * j_idx - front) & (n_idx <= ratio * j_idx + n_int - 1 - front)
             & (n_idx < nr - 1)).astype(np.float32)
    grid_spec = pltpu.PrefetchScalarGridSpec(
        num_scalar_prefetch=1,
        grid=(b, NSA_KV_HEADS, s // tq),
        in_specs=[pl.BlockSpec((1, NSA_GROUP, dh, tq), lambda i, g, j, sl: (i, TQ_NSA // NSA_GROUP + g, 0, j)),
                  pl.BlockSpec((1, 1, nr, dh), lambda i, g, j, sl: (i, g, 0, 0)),
                  pl.BlockSpec((1, 1, dh, nr), lambda i, g, j, sl: (i, 2 + g, 0, 0)),
                  pl.BlockSpec((nsb, nr), lambda i, g, j, sl: (0, 0))],
        out_specs=[pl.BlockSpec((1, NSA_GROUP, dh, tq), lambda i, g, j, sl: (i, g, 0, j)),
                   pl.BlockSpec((1, 1, nsb, tq), lambda i, g, j, sl: (i, g, 0, j)),
                   pl.BlockSpec((1, 1, 1, 8, nsb), lambda i, g, j, sl: (i, g, j, 0, 0))])
    assert nr % min(NSA_CMP_CHUNK, nr) == 0
    return pl.pallas_call(
        _nsa_cmp_kernel,
        grid_spec=grid_spec,
        out_shape=[jax.ShapeDtypeStruct((b, NSA_HEADS, dh, s), F32),
                   jax.ShapeDtypeStruct((b, NSA_KV_HEADS, nsb, s), F32),
                   jax.ShapeDtypeStruct((b, NSA_KV_HEADS, s // tq, 8, nsb), F32)],
        compiler_params=_cparams(("parallel", "parallel", "parallel")),
        name="nsa_compressed_select",
    )(jnp.asarray(_alibi_slopes2(NSA_HEADS)), qvt, cmp_kv, cmp_kvt, jnp.asarray(mimpt, BF16))


NSA_TK = 256
NSA_SEL_TQ = 256


def _nsa_qaug(qt_ref, srow_ref):
    return jnp.concatenate(
        [jnp.concatenate([qt_ref[0, hh], srow_ref[hh]], axis=0) for hh in range(NSA_GROUP)], axis=1)


def _nsa_sel_kernel(bits_ref, qt_ref, k_ref, vt_ref, selt_ref, srow_ref, slane_ref, o_ref,
                    m_ref, acc_ref, vaug_ref, sa_ref, ma_ref, sb_ref, mb_ref, list_ref, *, nq, words):
    tq, tk = NSA_SEL_TQ, NSA_TK
    per_tile = tk // NSA_SEL_BLOCK
    qi = pl.program_id(2)
    q0 = qi * tq

    @pl.when(qi == 0)
    def _():
        _fill_vaug(vaug_ref, vt_ref)

    base = ((pl.program_id(0) * NSA_KV_HEADS + pl.program_id(1)) * nq + qi) * words
    list_ref[0] = 0

    def note(j, n):
        list_ref[n] = j
        return n + ((bits_ref[base + j // 32] >> (j % 32)) & 1)

    n_tiles = lax.fori_loop(0, qi, note, 0)

    qaug = _nsa_qaug(qt_ref, srow_ref)
    slane = slane_ref[0:1, :]
    diag = q0 // tk

    def scores(kj):
        st = _dot(_ktile(k_ref, kj, tk), qaug)
        rows = [jnp.broadcast_to(selt_ref[0, 0, pl.ds(kj * per_tile + c, 1), :], (NSA_SEL_BLOCK, tq))
                for c in range(per_tile)]
        bias = (jnp.concatenate(rows, axis=0) - 1.0) * (-NEG)
        return st + jnp.concatenate([bias] * NSA_GROUP, axis=1)

    def meta(kj):
        return slane * (kj * tk - q0).astype(F32), _vtile(vaug_ref, kj, tk)

    def own_tile():
        key = lax.broadcasted_iota(jnp.int32, (tk, tq), 0)
        qry = lax.broadcasted_iota(jnp.int32, (tk, tq), 1)
        causal = jnp.concatenate([key <= qry] * NSA_GROUP, axis=1)
        return jnp.where(causal, scores(diag), NEG)

    _init_state_t(m_ref, acc_ref)
    _flash_pipeline([(own_tile, lambda: meta(diag))], n_tiles,
                    lambda i: scores(list_ref[i]), lambda i: meta(list_ref[i]),
                    ((sa_ref, ma_ref), (sb_ref, mb_ref)), m_ref, acc_ref, n_even=False)
    out = _finish_t(acc_ref)
    for hh in range(NSA_GROUP):
        o_ref[0, hh] = out[:, hh * tq:(hh + 1) * tq]


def _nsa_tables(tq):
    slopes2 = _alibi_slopes2(NSA_HEADS)
    srow = _slope_rows(slopes2, tq)
    slane = np.repeat(slopes2.reshape(NSA_KV_HEADS, NSA_GROUP), tq, axis=1)
    slane8 = np.repeat(slane[:, None, :], 8, axis=1)
    return srow, jnp.asarray(slane8, F32)


def _active_tile_bits(cnt, tq, tk):
    b, g, nq128, _, nsb = cnt.shape
    qper, bper = tq // NSA_TQ, tk // NSA_SEL_BLOCK
    nq, nkv = nq128 // qper, nsb // bper
    act = cnt[:, :, :, 0, :].reshape(b, g, nq, qper, nkv, bper).sum(axis=(3, 5)) > 0.0
    words = -(-nkv // 32)
    act = jnp.pad(act, ((0, 0), (0, 0), (0, 0), (0, words * 32 - nkv))).reshape(b, g, nq, words, 32)
    bits = jnp.sum(act.astype(jnp.uint32) << jnp.arange(32, dtype=jnp.uint32), axis=-1, dtype=jnp.uint32)
    return lax.bitcast_convert_type(bits, jnp.int32).reshape(-1), nq, words


def nsa_sel_pallas(qvt, kaug, selt, cnt):
    b, _, dh, s = qvt.shape
    tq = NSA_SEL_TQ
    assert tq == NSA_TK
    lanes = NSA_GROUP * tq
    nsb = s // NSA_SEL_BLOCK
    srow, slane = _nsa_tables(tq)
    bits, nq, words = _active_tile_bits(cnt, tq, NSA_TK)
    grid_spec = pltpu.PrefetchScalarGridSpec(
        num_scalar_prefetch=1,
        grid=(b, NSA_KV_HEADS, nq),
        in_specs=[pl.BlockSpec((1, NSA_GROUP, dh, tq), lambda i, g, j, bt: (i, TQ_NSA // NSA_GROUP + g, 0, j)),
                  pl.BlockSpec((1, 1, s, KAUG), lambda i, g, j, bt: (i, KA_NSA + g, 0, 0)),
                  pl.BlockSpec((1, 1, dh, s), lambda i, g, j, bt: (i, TV_NSA + g, 0, 0)),
                  pl.BlockSpec((1, 1, nsb, tq), lambda i, g, j, bt: (i, g, 0, j)),
                  pl.BlockSpec((NSA_GROUP, dh, tq), lambda i, g, j, bt: (g, 0, 0)),
                  pl.BlockSpec((None, 8, lanes), lambda i, g, j, bt: (g, 0, 0))],
        out_specs=pl.BlockSpec((1, NSA_GROUP, dh, tq), lambda i, g, j, bt: (i, g, 0, j)),
        scratch_shapes=[pltpu.VMEM((1, lanes), F32), pltpu.VMEM((VROWS, lanes), F32),
                        pltpu.VMEM((VROWS, s), BF16)] + _score_buffers(NSA_TK, lanes)
        + [pltpu.SMEM((max(nq, 8),), jnp.int32)])
    return pl.pallas_call(
        functools.partial(_nsa_sel_kernel, nq=nq, words=words),
        grid_spec=grid_spec,
        out_shape=jax.ShapeDtypeStruct((b, NSA_HEADS, dh, s), F32),
        compiler_params=_cparams(("parallel", "parallel", "arbitrary")),
        name="nsa_selected",
    )(bits, qvt, kaug, qvt, selt, srow, slane)


NSA_WT = 256


def _nsa_win_kernel(qt_ref, k_ref, vt_ref, srow_ref, slane_ref, oc_ref, os_ref, gate_ref, o_ref,
                    m_ref, acc_ref, vaug_ref, sa_ref, ma_ref, sb_ref, mb_ref, gt_ref):
    tq = NSA_WT
    wt = NSA_WT
    qi = pl.program_id(2)

    @pl.when(qi == 0)
    def _():
        _fill_vaug(vaug_ref, vt_ref)

    qaug = _nsa_qaug(qt_ref, srow_ref)
    slane = slane_ref[0:1, :]
    key = lax.broadcasted_iota(jnp.int32, (wt, tq), 0)
    qry = lax.broadcasted_iota(jnp.int32, (wt, tq), 1)
    span = NSA_WINDOW // wt

    def tile(d, keep):
        kj = jnp.maximum(qi - d, 0)

        def scores():
            st = _dot(_ktile(k_ref, kj, wt), qaug)
            if keep is not None:
                st = jnp.where(jnp.concatenate([keep] * NSA_GROUP, axis=1), st, NEG)
            return st

        def meta():
            base = (kj * wt) // POS_PERIOD * POS_PERIOD - qi * tq
            delta = jnp.where(qi - d >= 0, slane * base.astype(F32), SKIP)
            return delta, _vtile(vaug_ref, kj, wt)

        return scores, meta

    _init_state_t(m_ref, acc_ref)
    tiles = [tile(0, key <= qry)] + [tile(d, None) for d in range(1, span)] + [tile(span, key > qry)]
    _flash_pipeline(tiles, None, None, None, ((sa_ref, ma_ref), (sb_ref, mb_ref)), m_ref, acc_ref, True)

    o_w = _finish_t(acc_ref)
    gt_ref[...] = gate_ref[0].T
    for hh in range(NSA_GROUP):
        c0 = 3 * (pl.program_id(1) * NSA_GROUP + hh)
        mix = (gt_ref[pl.ds(c0, 1), :] * oc_ref[0, hh] + gt_ref[pl.ds(c0 + 1, 1), :] * os_ref[0, hh]
               + gt_ref[pl.ds(c0 + 2, 1), :] * o_w[:, hh * tq:(hh + 1) * tq])
        o_ref[0, hh] = mix.astype(o_ref.dtype)


def nsa_win_pallas(qvt, kaug, o_c, o_s, gates):
    b, _, dh, s = qvt.shape
    tq = wt = NSA_WT
    assert NSA_WINDOW % wt == 0 and POS_PERIOD % wt == 0
    lanes = NSA_GROUP * tq
    srow, slane = _nsa_tables(tq)
    head_blk = pl.BlockSpec((1, NSA_GROUP, dh, tq), lambda i, g, j: (i, g, 0, j))
    out = pl.pallas_call(
        _nsa_win_kernel,
        grid=(b, NSA_KV_HEADS, s // tq),
        in_specs=[pl.BlockSpec((1, NSA_GROUP, dh, tq), lambda i, g, j: (i, TQ_NSA // NSA_GROUP + g, 0, j)),
                  pl.BlockSpec((1, 1, s, KAUG), lambda i, g, j: (i, KA_NSA + 2 + g, 0, 0)),
                  pl.BlockSpec((1, 1, dh, s), lambda i, g, j: (i, TV_NSA + 2 + g, 0, 0)),
                  pl.BlockSpec((NSA_GROUP, dh, tq), lambda i, g, j: (g, 0, 0)),
                  pl.BlockSpec((None, 8, lanes), lambda i, g, j: (g, 0, 0)),
                  head_blk, head_blk,
                  pl.BlockSpec((1, tq, LANES), lambda i, g, j: (i, j, 0))],
        out_specs=head_blk,
        out_shape=jax.ShapeDtypeStruct((b, NSA_HEADS, dh, s), BF16),
        scratch_shapes=[pltpu.VMEM((1, lanes), F32), pltpu.VMEM((VROWS, lanes), F32),
                        pltpu.VMEM((VROWS, s), BF16)] + _score_buffers(wt, lanes)
        + [pltpu.VMEM((LANES, tq), F32)],
        compiler_params=_cparams(("parallel", "parallel", "arbitrary")),
        name="nsa_window_mix",
    )(qvt, kaug, qvt, srow, slane, o_c, o_s, gates)
    return out.reshape(b, NSA_HEADS * dh, s)


def _mem_attn_kernel(q_ref, k_ref, vt_ref, o_ref):
    for hh in range(MEM_HEADS):
        s = _dot_nt(q_ref[0, hh], k_ref[0, hh])
        m = jnp.max(s, axis=-1, keepdims=True)
        e = jnp.exp2(s - m)
        p = e / jnp.sum(e, axis=-1, keepdims=True)
        o_ref[0, hh] = _dot_nt(vt_ref[0, hh], p.astype(BF16)).astype(o_ref.dtype)


def mem_attn_pallas(nat, mem_k, mem_vt, tq=512):
    b, _, s, dh = nat.shape
    n_mem = mem_k.shape[2]
    out = pl.pallas_call(
        _mem_attn_kernel,
        grid=(b, s // tq),
        in_specs=[pl.BlockSpec((1, MEM_HEADS, tq, dh), lambda i, j: (i, N_MEMQ // MEM_HEADS, j, 0)),
                  pl.BlockSpec((1, MEM_HEADS, n_mem, dh), lambda i, j: (i, 0, 0, 0)),
                  pl.BlockSpec((1, MEM_HEADS, dh, n_mem), lambda i, j: (i, 0, 0, 0))],
        out_specs=pl.BlockSpec((1, MEM_HEADS, dh, tq), lambda i, j: (i, 0, 0, j)),
        out_shape=jax.ShapeDtypeStruct((b, MEM_HEADS, dh, s), BF16),
        compiler_params=_cparams(("parallel", "parallel")),
        name="memory_attention",
    )(nat, mem_k, mem_vt)
    return out.reshape(b, MEM_HEADS * dh, s)


def _in_proj_weights(w_in):
    hd = HEAD_DIM
    sizes = (3 * MOBA_HEADS * hd, NSA_HEADS * hd, 6 * NSA_KV_HEADS * hd, 3 * NSA_HEADS,
             3 * FOX_HEADS * hd, FOX_HEADS, MEM_HEADS * hd)
    offs = np.concatenate([[0], np.cumsum(sizes)])
    moba, nsa_q, nsa_kv, nsa_g, fox, fox_f, mem_q = (w_in[:, offs[i]:offs[i + 1]] for i in range(7))
    mh, fh, g2 = MOBA_HEADS * hd, FOX_HEADS * hd, NSA_KV_HEADS * hd
    moba_q, moba_k, moba_v = moba[:, :mh], moba[:, mh:2 * mh], moba[:, 2 * mh:]
    fox_q, fox_k, fox_v = fox[:, :fh], fox[:, fh:2 * fh], fox[:, 2 * fh:]
    k_cmp, v_cmp, k_slc, v_slc, k_win, v_win = (nsa_kv[:, i * g2:(i + 1) * g2] for i in range(6))
    w_t = jnp.concatenate([moba_q, nsa_q, fox_q, moba_v, v_slc, v_win, fox_v], axis=1).T.astype(BF16)
    t_scale = np.ones((T_SLOTS * hd,), np.float32)
    t_scale[:TV_MOBA * hd] = Q_SCALE
    w_ka = jnp.concatenate([moba_k, k_slc, k_win], axis=1).astype(BF16)
    w_kf = fox_k.astype(BF16)
    w_nat = jnp.concatenate([k_cmp, v_cmp, mem_q], axis=1).astype(BF16)
    n_scale = np.ones((N_SLOTS * hd,), np.float32)
    n_scale[N_MEMQ * hd:] = Q_SCALE
    return w_t, jnp.asarray(t_scale), w_ka, w_kf, w_nat, jnp.asarray(n_scale), nsa_g, fox_f


def _mixer(h32, h16, mem16, w_in, b_forget, w_mem_kv, cmp_pe, cmp_w1, cmp_w2):
    b, s, d = h16.shape
    tm = min(1024, s)
    w_t, t_scale, w_ka, w_kf, w_nat, n_scale, w_gate, w_forget = _in_proj_weights(w_in)
    gates, caug = gates_pallas(h32, w_gate, w_forget, b_forget)
    qvt = proj_t_pallas(h16, w_t, t_scale, tm=tm, heads_per_step=13)
    k_alibi = proj_kaug_pallas(h16, w_ka, None, tm=tm, heads_per_step=KA_SLOTS)
    k_fox = proj_kaug_pallas(h16, w_kf, caug, tm=tm, heads_per_step=FOX_HEADS)
    nat = proj_heads_pallas(h16, w_nat, n_scale, tm=tm, heads_per_step=8)
    n_mem = mem16.shape[1]
    mk = MEM_HEADS * HEAD_DIM
    mem_k = proj_heads_pallas(mem16, w_mem_kv[:, :mk].astype(BF16), jnp.ones((mk,), F32),
                              tm=n_mem, heads_per_step=MEM_HEADS)
    mem_vt = proj_t_pallas(mem16, w_mem_kv[:, mk:].T.astype(BF16), jnp.ones((mk,), F32),
                           tm=n_mem, heads_per_step=MEM_HEADS)
    o_moba = moba_pallas(qvt, k_alibi)
    o_fox = fox_pallas(qvt, k_fox)
    cmp_kv, cmp_kvt = nsa_compress_pallas(nat, cmp_pe, cmp_w1, cmp_w2)
    o_c, selt, cnt = nsa_cmp_pallas(qvt, cmp_kv, cmp_kvt)
    o_s = nsa_sel_pallas(qvt, k_alibi, selt, cnt)
    o_nsa = nsa_win_pallas(qvt, k_alibi, o_c, o_s, gates)
    o_mem = mem_attn_pallas(nat, mem_k, mem_vt)
    return [o_moba, o_nsa, o_fox, o_mem]


def kernel(x, mem, emb_ln_g, emb_ln_b, w_in, b_forget, w_mem_kv, nsa_cmp_pe, nsa_cmp_w1, nsa_cmp_w2,
           w_out, ln1_g, ln1_b, ffn_w_up, ffn_conv_w, ffn_conv_b, ffn_w_down, ln2_g, ln2_b):
    b, s, d = x.shape
    depth = w_in.shape[0]
    dff = ffn_w_down.shape[1]
    mem16 = mem.astype(BF16)
    h32, h16 = layer_norm_pallas(x.reshape(b * s, d), emb_ln_g, emb_ln_b)
    for l in range(depth):
        heads = _mixer(h32.reshape(b, s, d), h16.reshape(b, s, d), mem16, w_in[l], b_forget[l], w_mem_kv[l],
                       nsa_cmp_pe[l], nsa_cmp_w1[l], nsa_cmp_w2[l])
        h32, h16 = out_proj_ln_pallas(heads, w_out[l].astype(BF16), h32.reshape(b, s, d),
                                      ln1_g[l], ln1_b[l], tm=512)
        a = ffn_up_pallas(h16, ffn_w_up[l].astype(BF16), ffn_conv_w[l], ffn_conv_b[l],
                          tm=min(1024, s), tn=512)
        h32, h16 = matmul_ln_resident_pallas(a.reshape(b * s, dff), ffn_w_down[l].astype(BF16),
                                             h32.reshape(b * s, d), ln2_g[l], ln2_b[l], tm=256)
    return h32.reshape(b, s, d)
```

```python
import functools
import math

import jax
import jax.numpy as jnp
import ml_dtypes
import numpy as np
from jax import lax
from jax.experimental import pallas as pl
from jax.experimental.pallas import tpu as pltpu

F32 = jnp.float32
BF16 = jnp.bfloat16

HEAD_DIM = 64
MOBA_HEADS = 8
NSA_HEADS = 8
NSA_KV_HEADS = 2
NSA_GROUP = NSA_HEADS // NSA_KV_HEADS
FOX_HEADS = 12
MEM_HEADS = 4
MOBA_BLOCK = 256
MOBA_TOPK = 3
NSA_CMP_LEN = 32
NSA_CMP_STRIDE = 16
NSA_SEL_BLOCK = 64
NSA_TOPK = 16
NSA_WINDOW = 512
NSA_FORCE_SCORE = 1.0e4
CONV_WIDTH = 3
LN_EPS = 1e-5
DEPTH = 2
DEEPNORM_ALPHA = (2 * DEPTH) ** 0.25

LOG2E = math.log2(math.e)
Q_SCALE = HEAD_DIM ** -0.5 * LOG2E
NEG = -1.0e30
SKIP = -3.0e38
VMEM_LIMIT = 56 * 1024 * 1024
LANES = 128
KAUG = 2 * HEAD_DIM
VROWS = HEAD_DIM + 16
POS_PERIOD = 256

TQ_MOBA, TQ_NSA, TQ_FOX = 0, 8, 16
TV_MOBA, TV_NSA, TV_FOX = 28, 36, 40
T_SLOTS = 52
KA_MOBA, KA_NSA = 0, 8
KA_SLOTS = 12
N_CMP, N_MEMQ = 0, 4
N_SLOTS = 8


def _cparams(sem):
    return pltpu.CompilerParams(dimension_semantics=sem, vmem_limit_bytes=VMEM_LIMIT)


def _split2(x):
    hi = x.astype(BF16)
    return hi, (x - hi.astype(F32)).astype(BF16)


def _split3(x):
    hi = x.astype(BF16)
    r1 = x - hi.astype(F32)
    mid = r1.astype(BF16)
    lo = (r1 - mid.astype(F32)).astype(BF16)
    return hi, mid, lo


def _np_split3(x):
    x = np.asarray(x, np.float32)
    hi = x.astype(ml_dtypes.bfloat16).astype(np.float32)
    r1 = x - hi
    mid = r1.astype(ml_dtypes.bfloat16).astype(np.float32)
    lo = (r1 - mid).astype(ml_dtypes.bfloat16).astype(np.float32)
    return hi, mid, lo


def _dot_nt(a, b):
    return lax.dot_general(a, b, (((1,), (1,)), ((), ())), preferred_element_type=F32)


def _dot_tn(a, b):
    return lax.dot_general(a, b, (((0,), (0,)), ((), ())), preferred_element_type=F32)


def _dot(a, b):
    return jnp.dot(a, b, preferred_element_type=F32)


def _layer_norm_rows(x, g, b):
    mu = jnp.mean(x, axis=-1, keepdims=True)
    xc = x - mu
    var = jnp.mean(xc * xc, axis=-1, keepdims=True)
    return xc * lax.rsqrt(var + LN_EPS) * g + b


def _alibi_slopes2(n):
    return (np.exp2(-8.0 * np.arange(1, n + 1, dtype=np.float64) / n) * LOG2E).astype(np.float32)


def _slope_rows(slopes2, lanes):
    pieces = np.stack(_np_split3(slopes2), axis=1)
    rows = np.zeros((len(slopes2), HEAD_DIM, lanes), np.float32)
    rows[:, :3, :] = pieces[:, :, None]
    return jnp.asarray(rows, BF16)


def _ln_kernel(x_ref, g_ref, b_ref, o32_ref, o16_ref):
    y = _layer_norm_rows(x_ref[...], g_ref[...], b_ref[...])
    o32_ref[...] = y
    o16_ref[...] = y.astype(BF16)


def layer_norm_pallas(x, g, b, tm=512):
    m, d = x.shape
    return pl.pallas_call(
        _ln_kernel,
        grid=(m // tm,),
        in_specs=[pl.BlockSpec((tm, d), lambda i: (i, 0)),
                  pl.BlockSpec((1, d), lambda i: (0, 0)),
                  pl.BlockSpec((1, d), lambda i: (0, 0))],
        out_specs=[pl.BlockSpec((tm, d), lambda i: (i, 0)),
                   pl.BlockSpec((tm, d), lambda i: (i, 0))],
        out_shape=[jax.ShapeDtypeStruct((m, d), F32), jax.ShapeDtypeStruct((m, d), BF16)],
        compiler_params=_cparams(("parallel",)),
        name="layer_norm",
    )(x, g.reshape(1, d), b.reshape(1, d))


GATE_LANES_NSA = 3 * NSA_HEADS


def _gates_kernel(h_ref, w_ref, bf_ref, tri_ref, place_ref, g_ref, caug_ref, carry_ref):
    si = pl.program_id(1)

    @pl.when(si == 0)
    def _():
        carry_ref[...] = jnp.zeros_like(carry_ref)

    h_hi, h_lo = _split2(h_ref[0])
    w_hi, w_lo = _split2(w_ref[...])
    x = _dot(h_hi, w_hi) + _dot(h_lo, w_hi) + _dot(h_hi, w_lo)
    g_ref[0] = 1.0 / (1.0 + jnp.exp(-x))
    x = x + bf_ref[...]
    logf = jnp.minimum(x, 0.0) - jnp.log(1.0 + jnp.exp(-jnp.abs(x)))
    tri = tri_ref[...]
    l_hi, l_mid, l_lo = _split3(logf)
    c = _dot(tri, l_hi) + _dot(tri, l_mid) + _dot(tri, l_lo) + carry_ref[0:1, :]
    carry_ref[...] = jnp.broadcast_to(c[-1:, :], carry_ref.shape)
    n_hi, n_mid, n_lo = _split3(-LOG2E * c)
    caug = _dot(n_hi, place_ref[0]) + _dot(n_mid, place_ref[1]) + _dot(n_lo, place_ref[2])
    caug_ref[0] = caug.astype(BF16)


def gates_pallas(h3, w_gate, w_forget, b_forget, t=512):
    b, s, d = h3.shape
    lo = GATE_LANES_NSA
    w = jnp.zeros((d, LANES), F32).at[:, :lo].set(w_gate).at[:, lo:lo + FOX_HEADS].set(w_forget)
    bf = jnp.zeros((1, LANES), F32).at[0, lo:lo + FOX_HEADS].set(b_forget)
    tri = (np.arange(t)[None, :] <= np.arange(t)[:, None]).astype(np.float32)
    nc = FOX_HEADS * HEAD_DIM
    place = np.zeros((3, LANES, nc), np.float32)
    for piece in range(3):
        for hh in range(FOX_HEADS):
            place[piece, lo + hh, hh * HEAD_DIM + piece] = 1.0
    return pl.pallas_call(
        _gates_kernel,
        grid=(b, s // t),
        in_specs=[pl.BlockSpec((1, t, d), lambda i, j: (i, j, 0)),
                  pl.BlockSpec((d, LANES), lambda i, j: (0, 0)),
                  pl.BlockSpec((1, LANES), lambda i, j: (0, 0)),
                  pl.BlockSpec((t, t), lambda i, j: (0, 0)),
                  pl.BlockSpec((3, LANES, nc), lambda i, j: (0, 0, 0))],
        out_specs=[pl.BlockSpec((1, t, LANES), lambda i, j: (i, j, 0)),
                   pl.BlockSpec((1, t, nc), lambda i, j: (i, j, 0))],
        out_shape=[jax.ShapeDtypeStruct((b, s, LANES), F32),
                   jax.ShapeDtypeStruct((b, s, nc), BF16)],
        scratch_shapes=[pltpu.VMEM((8, LANES), F32)],
        compiler_params=_cparams(("parallel", "arbitrary")),
        name="gates_cumsum",
    )(h3, w, bf, jnp.asarray(tri, BF16), jnp.asarray(place, BF16))


def _proj_heads_kernel(x_ref, w_ref, sc_ref, o_ref, *, heads_per_step):
    acc = _dot(x_ref[0], w_ref[...]) * sc_ref[...]
    for j in range(heads_per_step):
        o_ref[0, j] = acc[:, j * HEAD_DIM:(j + 1) * HEAD_DIM].astype(o_ref.dtype)


def proj_heads_pallas(x3, w, colscale, tm, heads_per_step):
    b, s, d = x3.shape
    n = w.shape[1]
    tn = heads_per_step * HEAD_DIM
    return pl.pallas_call(
        functools.partial(_proj_heads_kernel, heads_per_step=heads_per_step),
        grid=(b, s // tm, n // tn),
        in_specs=[pl.BlockSpec((1, tm, d), lambda i, j, k: (i, j, 0)),
                  pl.BlockSpec((d, tn), lambda i, j, k: (0, k)),
                  pl.BlockSpec((1, tn), lambda i, j, k: (0, k))],
        out_specs=pl.BlockSpec((1, heads_per_step, tm, HEAD_DIM), lambda i, j, k: (i, k, j, 0)),
        out_shape=jax.ShapeDtypeStruct((b, n // HEAD_DIM, s, HEAD_DIM), BF16),
        compiler_params=_cparams(("parallel", "parallel", "arbitrary")),
        name="proj_heads",
    )(x3, w, colscale.reshape(1, n))


def _proj_t_kernel(x_ref, wt_ref, sc_ref, o_ref, *, heads_per_step):
    acc = _dot_nt(wt_ref[...], x_ref[0]) * sc_ref[...]
    o_ref[0] = acc.reshape(heads_per_step, HEAD_DIM, acc.shape[1]).astype(o_ref.dtype)


def proj_t_pallas(x3, wt, rowscale, tm, heads_per_step):
    b, s, d = x3.shape
    n = wt.shape[0]
    tn = heads_per_step * HEAD_DIM
    return pl.pallas_call(
        functools.partial(_proj_t_kernel, heads_per_step=heads_per_step),
        grid=(b, s // tm, n // tn),
        in_specs=[pl.BlockSpec((1, tm, d), lambda i, j, k: (i, j, 0)),
                  pl.BlockSpec((tn, d), lambda i, j, k: (k, 0)),
                  pl.BlockSpec((tn, 1), lambda i, j, k: (k, 0))],
        out_specs=pl.BlockSpec((1, heads_per_step, HEAD_DIM, tm), lambda i, j, k: (i, k, 0, j)),
        out_shape=jax.ShapeDtypeStruct((b, n // HEAD_DIM, HEAD_DIM, s), BF16),
        compiler_params=_cparams(("parallel", "parallel", "arbitrary")),
        name="proj_transposed",
    )(x3, wt, rowscale.reshape(n, 1))


def _proj_kaug_kernel(x_ref, w_ref, *rest, heads_per_step, positional):
    o_ref = rest[-1]
    acc = _dot(x_ref[0], w_ref[...])
    tm = acc.shape[0]
    if positional:
        pos = (pl.program_id(1) * tm + lax.broadcasted_iota(jnp.int32, (tm, HEAD_DIM), 0)) % POS_PERIOD
        lane = lax.broadcasted_iota(jnp.int32, (tm, HEAD_DIM), 1)
        pos_lanes = jnp.where(lane < 3, pos.astype(F32), 0.0).astype(o_ref.dtype)
    for j in range(heads_per_step):
        cols = slice(j * HEAD_DIM, (j + 1) * HEAD_DIM)
        bias = pos_lanes if positional else rest[0][0, :, cols]
        o_ref[0, j] = jnp.concatenate([acc[:, cols].astype(o_ref.dtype), bias], axis=1)


def proj_kaug_pallas(x3, w, aug, tm, heads_per_step):
    b, s, d = x3.shape
    n = w.shape[1]
    tn = heads_per_step * HEAD_DIM
    in_specs = [pl.BlockSpec((1, tm, d), lambda i, j, k: (i, j, 0)),
                pl.BlockSpec((d, tn), lambda i, j, k: (0, k))]
    args = [x3, w]
    if aug is not None:
        in_specs.append(pl.BlockSpec((1, tm, tn), lambda i, j, k: (i, j, k)))
        args.append(aug)
    return pl.pallas_call(
        functools.partial(_proj_kaug_kernel, heads_per_step=heads_per_step, positional=aug is None),
        grid=(b, s // tm, n // tn),
        in_specs=in_specs,
        out_specs=pl.BlockSpec((1, heads_per_step, tm, KAUG), lambda i, j, k: (i, k, j, 0)),
        out_shape=jax.ShapeDtypeStruct((b, n // HEAD_DIM, s, KAUG), BF16),
        compiler_params=_cparams(("parallel", "parallel", "arbitrary")),
        name="proj_keys_aug",
    )(*args)


def _matmul_ln_resident_kernel(x_ref, w_ref, r_ref, g_ref, b_ref, o32_ref, o16_ref):
    y = _layer_norm_rows(DEEPNORM_ALPHA * r_ref[...] + _dot(x_ref[...], w_ref[...]), g_ref[...], b_ref[...])
    o32_ref[...] = y
    o16_ref[...] = y.astype(BF16)


def matmul_ln_resident_pallas(x, w, res, g, b, tm):
    m, kk = x.shape
    d = w.shape[1]
    return pl.pallas_call(
        _matmul_ln_resident_kernel,
        grid=(m // tm,),
        in_specs=[pl.BlockSpec((tm, kk), lambda i: (i, 0)),
                  pl.BlockSpec((kk, d), lambda i: (0, 0), pipeline_mode=pl.Buffered(1)),
                  pl.BlockSpec((tm, d), lambda i: (i, 0)),
                  pl.BlockSpec((1, d), lambda i: (0, 0)),
                  pl.BlockSpec((1, d), lambda i: (0, 0))],
        out_specs=[pl.BlockSpec((tm, d), lambda i: (i, 0)),
                   pl.BlockSpec((tm, d), lambda i: (i, 0))],
        out_shape=[jax.ShapeDtypeStruct((m, d), F32), jax.ShapeDtypeStruct((m, d), BF16)],
        compiler_params=_cparams(("parallel",)),
        name="matmul_ln_resident",
    )(x, w, res, g.reshape(1, d), b.reshape(1, d))


def _out_proj_ln_kernel(*refs, widths):
    n = len(widths)
    x_refs, (w_ref, r_ref, g_ref, b_ref, o32_ref, o16_ref) = refs[:n], refs[n:]
    acc = None
    off = 0
    for x_ref, width in zip(x_refs, widths):
        part = _dot_tn(x_ref[0], w_ref[off:off + width, :])
        acc = part if acc is None else acc + part
        off += width
    y = _layer_norm_rows(DEEPNORM_ALPHA * r_ref[0] + acc, g_ref[...], b_ref[...])
    o32_ref[0] = y
    o16_ref[0] = y.astype(BF16)


def out_proj_ln_pallas(xts, w, res3, g, b, tm):
    bsz, s, d = res3.shape
    widths = tuple(x.shape[1] for x in xts)
    in_specs = [pl.BlockSpec((1, wd, tm), lambda i, j: (i, 0, j)) for wd in widths]
    in_specs += [pl.BlockSpec((w.shape[0], d), lambda i, j: (0, 0)),
                 pl.BlockSpec((1, tm, d), lambda i, j: (i, j, 0)),
                 pl.BlockSpec((1, d), lambda i, j: (0, 0)),
                 pl.BlockSpec((1, d), lambda i, j: (0, 0))]
    return pl.pallas_call(
        functools.partial(_out_proj_ln_kernel, widths=widths),
        grid=(bsz, s // tm),
        in_specs=in_specs,
        out_specs=[pl.BlockSpec((1, tm, d), lambda i, j: (i, j, 0)),
                   pl.BlockSpec((1, tm, d), lambda i, j: (i, j, 0))],
        out_shape=[jax.ShapeDtypeStruct((bsz, s, d), F32), jax.ShapeDtypeStruct((bsz, s, d), BF16)],
        compiler_params=_cparams(("parallel", "parallel")),
        name="out_proj_ln",
    )(*xts, w, res3, g.reshape(1, d), b.reshape(1, d))


HALO = 16
FFN_CHUNK = 256


def _gelu_tanh(x):
    return 0.5 * x * (1.0 + jnp.tanh(math.sqrt(2.0 / math.pi) * (x + 0.044715 * x * x * x)))


def _ffn_up_kernel(x_ref, xh_ref, wu_ref, wg_ref, cw_ref, cb_ref, o_ref):
    j = pl.program_id(1)
    x = x_ref[0]
    xh = xh_ref[0]
    first = jnp.where(j > 0, 1.0, 0.0)
    tn = o_ref.shape[2]
    row = lax.broadcasted_iota(jnp.int32, (x.shape[0], FFN_CHUNK), 0)
    for c in range(tn // FFN_CHUNK):
        cols = slice(c * FFN_CHUNK, (c + 1) * FFN_CHUNK)
        u = _dot(x, wu_ref[:, cols])
        g = _dot(x, wg_ref[:, cols])
        gh = _dot(xh, wg_ref[:, cols]) * first
        prev1 = gh[HALO - 1:HALO, :]
        prev2 = gh[HALO - 2:HALO - 1, :]
        g_m1 = jnp.where(row == 0, prev1, pltpu.roll(g, 1, axis=0))
        g_m2 = jnp.where(row == 0, prev2, jnp.where(row == 1, prev1, pltpu.roll(g, 2, axis=0)))
        cw = cw_ref[:, cols]
        gc = cb_ref[:, cols] + cw[0:1] * g_m2 + cw[1:2] * g_m1 + cw[2:3] * g
        o_ref[0, :, cols] = (_gelu_tanh(gc) * u).astype(o_ref.dtype)


def ffn_up_pallas(x3, w_up, conv_w, conv_b, tm, tn):
    b, s, d = x3.shape
    dff = w_up.shape[1] // 2
    nt = dff // tn
    hb = tm // HALO
    cw = jnp.zeros((8, dff), F32).at[:CONV_WIDTH].set(conv_w)
    return pl.pallas_call(
        _ffn_up_kernel,
        grid=(b, s // tm, nt),
        in_specs=[pl.BlockSpec((1, tm, d), lambda i, j, k: (i, j, 0)),
                  pl.BlockSpec((1, HALO, d), lambda i, j, k: (i, jnp.maximum(j * hb - 1, 0), 0)),
                  pl.BlockSpec((d, tn), lambda i, j, k: (0, k)),
                  pl.BlockSpec((d, tn), lambda i, j, k: (0, k + nt)),
                  pl.BlockSpec((8, tn), lambda i, j, k: (0, k)),
                  pl.BlockSpec((1, tn), lambda i, j, k: (0, k))],
        out_specs=pl.BlockSpec((1, tm, tn), lambda i, j, k: (i, j, k)),
        out_shape=jax.ShapeDtypeStruct((b, s, dff), BF16),
        compiler_params=_cparams(("parallel", "parallel", "arbitrary")),
        name="ffn_up",
    )(x3, x3, w_up, w_up, cw, conv_b.reshape(1, dff))


def _init_state_t(m_ref, acc_ref):
    m_ref[...] = jnp.full(m_ref.shape, NEG, F32)
    acc_ref[...] = jnp.zeros(acc_ref.shape, F32)


def _finish_t(acc_ref):
    acc = acc_ref[...]
    return acc[:HEAD_DIM] / acc[HEAD_DIM:HEAD_DIM + 1]


def _fill_vaug(vaug_ref, vt_ref):
    s = vaug_ref.shape[1]
    vaug_ref[0:HEAD_DIM, :] = vt_ref[0, 0]
    pad = lax.broadcasted_iota(jnp.int32, (VROWS - HEAD_DIM, s), 0)
    vaug_ref[HEAD_DIM:VROWS, :] = jnp.where(pad == 0, 1.0, 0.0).astype(BF16)


def _ktile(ref, idx, size):
    return ref[0, 0, pl.ds(pl.multiple_of(idx * size, size), size), :]


def _vtile(ref, idx, size):
    return ref[:, pl.ds(pl.multiple_of(idx * size, size), size)]


def _stage_scores(st, s_ref, mc_ref):
    nsub = mc_ref.shape[0]
    s_ref[...] = st
    if nsub == 1:
        mc_ref[...] = jnp.max(st, axis=0, keepdims=True)
    else:
        mc_ref[...] = jnp.max(st.reshape(nsub, st.shape[0] // nsub, st.shape[1]), axis=1)


def _stage_update(s_ref, mc_ref, delta, vaug, m_ref, acc_ref):
    nsub = mc_ref.shape[0]
    m_prev = m_ref[...]
    m_new = jnp.maximum(m_prev, jnp.max(mc_ref[...] + delta, axis=0, keepdims=True))
    alpha = jnp.exp2(m_prev - m_new)
    shift = m_new - delta
    if nsub == 1:
        pt = jnp.exp2(s_ref[...] - shift)
    else:
        tk, lanes = s_ref.shape
        pt = jnp.exp2(s_ref[...].reshape(nsub, tk // nsub, lanes) - shift[:, None, :]).reshape(tk, lanes)
    acc_ref[...] = alpha * acc_ref[...] + _dot(vaug, pt.astype(BF16))
    m_ref[...] = m_new


def _score_buffers(tk, lanes, nsub=1):
    return [pltpu.VMEM((tk, lanes), F32), pltpu.VMEM((nsub, lanes), F32),
            pltpu.VMEM((tk, lanes), F32), pltpu.VMEM((nsub, lanes), F32)]


def _flash_pipeline(lead, n_loop, scores, meta, bufs, m_ref, acc_ref, n_even):
    _stage_scores(lead[0][0](), *bufs[0])
    for i in range(1, len(lead)):
        _stage_scores(lead[i][0](), *bufs[i % 2])
        _stage_update(*bufs[(i - 1) % 2], *lead[i - 1][1](), m_ref, acc_ref)
    cur = (len(lead) - 1) % 2
    nxt = 1 - cur
    if scores is None:
        _stage_update(*bufs[cur], *lead[-1][1](), m_ref, acc_ref)
        return
    _stage_scores(scores(0), *bufs[nxt])
    _stage_update(*bufs[cur], *lead[-1][1](), m_ref, acc_ref)

    def pair(k0, stage_next, both):
        if both:
            _stage_scores(scores(k0 + 1), *bufs[cur])
        _stage_update(*bufs[nxt], *meta(k0), m_ref, acc_ref)
        if stage_next:
            _stage_scores(scores(k0 + 2), *bufs[nxt])
        if both:
            _stage_update(*bufs[cur], *meta(k0 + 1), m_ref, acc_ref)

    full = jnp.maximum(n_loop - 1, 0) // 2

    def body(kp, carry):
        pair(2 * kp, True, True)
        return carry

    lax.fori_loop(0, full, body, 0)
    rest = n_loop - 2 * full
    pl.when(rest == 2)(lambda: pair(2 * full, False, True))
    if not n_even:
        pl.when(rest == 1)(lambda: pair(2 * full, False, False))


FOX_TQ = 1024
FOX_TK = 512


def _fox_kernel(qt_ref, k_ref, vt_ref, o_ref, m_ref, acc_ref, vaug_ref, sa_ref, ma_ref, sb_ref, mb_ref):
    tq, tk = o_ref.shape[3], FOX_TK
    nd = tq // tk
    qi = pl.program_id(2)

    @pl.when(qi == 0)
    def _():
        _fill_vaug(vaug_ref, vt_ref)

    ones3 = jnp.where(lax.broadcasted_iota(jnp.int32, (HEAD_DIM, tq), 0) < 3, 1.0, 0.0).astype(BF16)
    qaug = jnp.concatenate([qt_ref[0, 0], ones3], axis=0)
    _init_state_t(m_ref, acc_ref)
    key = lax.broadcasted_iota(jnp.int32, (tk, tq), 0)
    qry = lax.broadcasted_iota(jnp.int32, (tk, tq), 1)

    def scores(kj):
        return _dot(_ktile(k_ref, kj, tk), qaug)

    def meta(kj):
        return 0.0, _vtile(vaug_ref, kj, tk)

    lead = [(functools.partial(lambda d: jnp.where(key + d * tk <= qry, scores(qi * nd + d), NEG), d),
             functools.partial(lambda d: meta(qi * nd + d), d)) for d in range(nd)]
    _flash_pipeline(lead, qi * nd, scores, meta, ((sa_ref, ma_ref), (sb_ref, mb_ref)), m_ref, acc_ref,
                    n_even=nd % 2 == 0)
    o_ref[0, 0] = _finish_t(acc_ref).astype(o_ref.dtype)


def fox_pallas(qvt, kaug):
    b, _, dh, s = qvt.shape
    tq = min(FOX_TQ, s)
    assert tq % FOX_TK == 0
    out = pl.pallas_call(
        _fox_kernel,
        grid=(b, FOX_HEADS, s // tq),
        in_specs=[pl.BlockSpec((1, 1, dh, tq), lambda i, h, j: (i, TQ_FOX + h, 0, j)),
                  pl.BlockSpec((1, 1, s, KAUG), lambda i, h, j: (i, h, 0, 0)),
                  pl.BlockSpec((1, 1, dh, s), lambda i, h, j: (i, TV_FOX + h, 0, 0))],
        out_specs=pl.BlockSpec((1, 1, dh, tq), lambda i, h, j: (i, h, 0, j)),
        out_shape=jax.ShapeDtypeStruct((b, FOX_HEADS, dh, s), BF16),
        scratch_shapes=[pltpu.VMEM((1, tq), F32), pltpu.VMEM((VROWS, tq), F32),
                        pltpu.VMEM((VROWS, s), BF16)] + _score_buffers(FOX_TK, tq),
        compiler_params=_cparams(("parallel", "parallel", "arbitrary")),
        name="fox_attention",
    )(qvt, kaug, qvt)
    return out.reshape(b, FOX_HEADS * dh, s)


def _moba_kernel(slope_ref, qt_ref, k_ref, vt_ref, ind_ref, srow_ref, o_ref,
                 m_ref, acc_ref, vaug_ref, km_ref, sel_ref, sa_ref, ma_ref, sb_ref, mb_ref):
    t = MOBA_BLOCK
    tq = o_ref.shape[3]
    nd = tq // t
    h = pl.program_id(1)
    qi = pl.program_id(2)
    slope2 = slope_ref[h]

    @pl.when(qi == 0)
    def _():
        _fill_vaug(vaug_ref, vt_ref)
        km_ref[...] = _dot(ind_ref[...], k_ref[0, 0]) * (1.0 / MOBA_BLOCK)

    qt = qt_ref[0, 0]
    km_hi, km_lo = _split2(km_ref[...])
    q0 = jnp.concatenate([qt, jnp.zeros_like(qt)], axis=0)
    gate = _dot(km_hi, q0) + _dot(km_lo, q0)
    blk = lax.broadcasted_iota(jnp.int32, gate.shape, 0)
    lane_blk = lax.broadcasted_iota(jnp.int32, (1, tq), 1) // t
    valid = blk < qi * nd + lane_blk
    work = jnp.where(valid, gate, -jnp.inf)
    big = jnp.int32(2 ** 30)
    for _ in range(MOBA_TOPK):
        mx = jnp.max(work, axis=0, keepdims=True)
        first = jnp.min(jnp.where(work == mx, blk, big), axis=0, keepdims=True)
        work = jnp.where(blk == first, -jnp.inf, work)
    sel_ref[...] = jnp.where(valid & (work == -jnp.inf), 1.0, 0.0)

    qaug = jnp.concatenate([qt, srow_ref[...]], axis=0)
    _init_state_t(m_ref, acc_ref)
    tk = sa_ref.shape[0]
    nsub = tk // t
    ntile = tq // tk
    key = lax.broadcasted_iota(jnp.int32, (tk, tq), 0)
    qry = lax.broadcasted_iota(jnp.int32, (tk, tq), 1)
    sub = lax.broadcasted_iota(jnp.int32, (nsub, tq), 0)

    def tile_constant(kj, d):
        sees = jnp.concatenate([sel_ref[pl.ds(kj * nsub + c, 1), :] for c in range(nsub)], axis=0) > 0.0
        if d is not None:
            sees = sees | (lane_blk == d * nsub + sub)
        offset = ((kj * nsub + sub - qi * nd) * t).astype(F32)
        return jnp.where(sees, slope2 * offset, SKIP)

    def scores(kj):
        return _dot(_ktile(k_ref, kj, tk), qaug)

    def meta(kj):
        return tile_constant(kj, None), _vtile(vaug_ref, kj, tk)

    lead = [(functools.partial(lambda d: jnp.where(key + d * tk <= qry, scores(qi * ntile + d), NEG), d),
             functools.partial(lambda d: (tile_constant(qi * ntile + d, d),
                                          _vtile(vaug_ref, qi * ntile + d, tk)), d))
            for d in range(ntile)]
    _flash_pipeline(lead, qi * ntile, scores, meta, ((sa_ref, ma_ref), (sb_ref, mb_ref)), m_ref, acc_ref,
                    n_even=ntile % 2 == 0)
    o_ref[0, 0] = _finish_t(acc_ref).astype(o_ref.dtype)


MOBA_TQ = 1024
MOBA_TK = 512


def moba_pallas(qvt, kaug):
    b, _, dh, s = qvt.shape
    t = MOBA_BLOCK
    tq = min(MOBA_TQ, s)
    tk = min(MOBA_TK, tq)
    assert POS_PERIOD == t and tq % tk == 0 and tk % t == 0
    nb = s // t
    nbp = max(16, nb)
    ind = np.zeros((nbp, s), np.float32)
    ind[np.arange(s) // t, np.arange(s)] = 1.0
    slopes2 = _alibi_slopes2(MOBA_HEADS)
    grid_spec = pltpu.PrefetchScalarGridSpec(
        num_scalar_prefetch=1,
        grid=(b, MOBA_HEADS, s // tq),
        in_specs=[pl.BlockSpec((1, 1, dh, tq), lambda i, h, j, sl: (i, TQ_MOBA + h, 0, j)),
                  pl.BlockSpec((1, 1, s, KAUG), lambda i, h, j, sl: (i, KA_MOBA + h, 0, 0)),
                  pl.BlockSpec((1, 1, dh, s), lambda i, h, j, sl: (i, TV_MOBA + h, 0, 0)),
                  pl.BlockSpec((nbp, s), lambda i, h, j, sl: (0, 0)),
                  pl.BlockSpec((None, dh, tq), lambda i, h, j, sl: (h, 0, 0))],
        out_specs=pl.BlockSpec((1, 1, dh, tq), lambda i, h, j, sl: (i, h, 0, j)),
        scratch_shapes=[pltpu.VMEM((1, tq), F32), pltpu.VMEM((VROWS, tq), F32),
                        pltpu.VMEM((VROWS, s), BF16), pltpu.VMEM((nbp, KAUG), F32),
                        pltpu.VMEM((nbp, tq), F32)] + _score_buffers(tk, tq, tk // t))
    out = pl.pallas_call(
        _moba_kernel,
        grid_spec=grid_spec,
        out_shape=jax.ShapeDtypeStruct((b, MOBA_HEADS, dh, s), BF16),
        compiler_params=_cparams(("parallel", "parallel", "arbitrary")),
        name="moba_attention",
    )(jnp.asarray(slopes2), qvt, kaug, qvt, jnp.asarray(ind, BF16), _slope_rows(slopes2, tq))
    return out.reshape(b, MOBA_HEADS * dh, s)


def _nsa_compress_kernel(x_ref, w1a_ref, w1b_ref, pe_ref, w1_ref, w2_ref, w2t_ref, o_ref, ot_ref):
    nr = x_ref.shape[2]
    x = x_ref[0, 0]

    def near_f32(xb, w):
        w_hi, w_lo = _split2(w)
        return _dot(xb, w_hi) + _dot(xb, w_lo)

    a = near_f32(x, w1a_ref[0])
    bm = near_f32(x, w1b_ref[0])
    pe_hi, pe_mid, pe_lo = _split3(pe_ref[0])
    w1_hi, w1_lo = _split2(w1_ref[0])
    pe_term = (_dot(pe_hi, w1_hi) + _dot(pe_mid, w1_hi) + _dot(pe_lo, w1_hi)
               + _dot(pe_hi, w1_lo) + _dot(pe_mid, w1_lo))[0:1]
    pre = a + pltpu.roll(bm, nr - 1, axis=0) + pe_term
    hid = _gelu_tanh(pre)
    h_hi, h_mid, h_lo = _split3(hid)
    w2_hi, w2_lo = _split2(w2_ref[0])
    o_ref[0, 0] = (_dot(h_hi, w2_hi) + _dot(h_mid, w2_hi) + _dot(h_lo, w2_hi)
                   + _dot(h_hi, w2_lo) + _dot(h_mid, w2_lo))
    t_hi, t_lo = _split2(w2t_ref[0])
    ot_ref[0, 0] = (_dot_nt(t_hi, h_hi) + _dot_nt(t_hi, h_mid) + _dot_nt(t_hi, h_lo)
                    + _dot_nt(t_lo, h_hi) + _dot_nt(t_lo, h_mid))


def nsa_compress_pallas(nat, pe, w1, w2):
    b, _, s, dh = nat.shape
    nr = s // NSA_CMP_STRIDE
    half = NSA_CMP_STRIDE * dh
    hid = w1.shape[-1]
    x = nat[:, N_CMP:N_CMP + 4].reshape(b, 4, nr, half)
    w1f = w1.reshape(2, NSA_CMP_LEN * dh, hid)
    pef = jnp.zeros((2, 8, NSA_CMP_LEN * dh), F32).at[:, 0].set(pe.reshape(2, NSA_CMP_LEN * dh))
    return pl.pallas_call(
        _nsa_compress_kernel,
        grid=(b, 4),
        in_specs=[pl.BlockSpec((1, 1, nr, half), lambda i, j: (i, j, 0, 0)),
                  pl.BlockSpec((1, half, hid), lambda i, j: (j // 2, 0, 0)),
                  pl.BlockSpec((1, half, hid), lambda i, j: (j // 2, 1, 0)),
                  pl.BlockSpec((1, 8, 2 * half), lambda i, j: (j // 2, 0, 0)),
                  pl.BlockSpec((1, 2 * half, hid), lambda i, j: (j // 2, 0, 0)),
                  pl.BlockSpec((1, hid, dh), lambda i, j: (j // 2, 0, 0)),
                  pl.BlockSpec((1, dh, hid), lambda i, j: (j // 2, 0, 0))],
        out_specs=[pl.BlockSpec((1, 1, nr, dh), lambda i, j: (i, j, 0, 0)),
                   pl.BlockSpec((1, 1, dh, nr), lambda i, j: (i, j, 0, 0))],
        out_shape=[jax.ShapeDtypeStruct((b, 4, nr, dh), F32),
                   jax.ShapeDtypeStruct((b, 4, dh, nr), F32)],
        compiler_params=_cparams(("parallel", "parallel")),
        name="nsa_compress",
    )(x, w1f, w1f, pef, w1f, w2, jnp.swapaxes(w2, 1, 2))


NSA_TQ = 128


NSA_CMP_CHUNK = 256


def _nsa_cmp_kernel(slope_ref, qt_ref, kc_ref, vct_ref, mimpt_ref, oct_ref, selt_ref, cnt_ref):
    tq = NSA_TQ
    g = pl.program_id(1)
    qi = pl.program_id(2)
    q0 = qi * tq
    nr = kc_ref.shape[2]
    nsb = selt_ref.shape[2]
    t_lane = q0 + lax.broadcasted_iota(jnp.int32, (1, tq), 1)
    chunk = min(NSA_CMP_CHUNK, nr)
    tiles_per_chunk = chunk * NSA_CMP_STRIDE // tq

    def branch(n):
        kc_hi, kc_lo = _split2(kc_ref[0, 0, 0:n, :])
        vct = vct_ref[0, 0, :, 0:n].astype(BF16)
        cmp_end = NSA_CMP_STRIDE * lax.broadcasted_iota(jnp.int32, (n, tq), 0) + (NSA_CMP_LEN - 1)
        mask = cmp_end <= t_lane
        rel = (cmp_end - q0).astype(F32)
        psum = jnp.zeros((n, tq), F32)
        qt4 = jnp.concatenate([qt_ref[0, hh] for hh in range(NSA_GROUP)], axis=1)
        st4 = _dot(kc_hi, qt4) + _dot(kc_lo, qt4)
        probs = []
        for hh in range(NSA_GROUP):
            st = st4[:, hh * tq:(hh + 1) * tq] + slope_ref[g * NSA_GROUP + hh] * rel
            st = jnp.where(mask, st, -jnp.inf)
            m = jnp.max(st, axis=0, keepdims=True)
            m = jnp.where(m > -jnp.inf, m, 0.0)
            e = jnp.exp2(st - m)
            p = e * (1.0 / jnp.maximum(jnp.sum(e, axis=0, keepdims=True), 1e-30))
            probs.append(p.astype(BF16))
            psum = psum + p
        o4 = _dot(vct, jnp.concatenate(probs, axis=1))
        for hh in range(NSA_GROUP):
            oct_ref[0, hh] = o4[:, hh * tq:(hh + 1) * tq]
        nbk = min(nsb, n * NSA_CMP_STRIDE // NSA_SEL_BLOCK)
        p_hi, p_mid, p_lo = _split3(psum)
        mimpt = mimpt_ref[0:nbk, 0:n]
        imp = _dot(mimpt, p_hi) + _dot(mimpt, p_mid) + _dot(mimpt, p_lo)
        blk = lax.broadcasted_iota(jnp.int32, (nbk, tq), 0)
        jt = t_lane // NSA_SEL_BLOCK
        forced = (blk == 0) | (blk == jt) | (blk == jt - 1)
        imp = jnp.where(forced, NSA_FORCE_SCORE, imp)
        valid = blk * NSA_SEL_BLOCK <= t_lane
        work0 = jnp.where(valid, imp, -jnp.inf)
        big = jnp.int32(2 ** 30)

        def pick(_, work):
            mx = jnp.max(work, axis=0, keepdims=True)
            first = jnp.min(jnp.where(work == mx, blk, big), axis=0, keepdims=True)
            return jnp.where(blk == first, -jnp.inf, work)

        work = lax.fori_loop(0, min(NSA_TOPK, nsb), pick, work0)
        sel = jnp.where(valid & (work == -jnp.inf), 1.0, 0.0)
        selt_ref[0, 0, 0:nbk, :] = sel
        cnt_ref[0, 0, 0, :, 0:nbk] = _dot_nt(jnp.ones((8, tq), BF16), sel.astype(BF16))
        if nbk < nsb:
            selt_ref[0, 0, nbk:nsb, :] = jnp.zeros((nsb - nbk, tq), F32)
            cnt_ref[0, 0, 0, :, nbk:nsb] = jnp.zeros((8, nsb - nbk), F32)

    n_chunks = nr // chunk
    for c in range(n_chunks):
        pl.when(jnp.minimum(qi // tiles_per_chunk, n_chunks - 1) == c)(
            functools.partial(branch, (c + 1) * chunk))


def nsa_cmp_pallas(qvt, cmp_kv, cmp_kvt):
    b, _, dh, s = qvt.shape
    tq = NSA_TQ
    nr = cmp_kv.shape[2]
    nsb = s // NSA_SEL_BLOCK
    ratio = NSA_SEL_BLOCK // NSA_CMP_STRIDE
    front = NSA_CMP_LEN // NSA_CMP_STRIDE - 1
    n_int = ratio + front
    n_idx = np.arange(nr)[None, :]
    j_idx = np.arange(nsb)[:, None]
    mimpt = ((n_idx >= ratio * j_idx - front) & (n_idx <= ratio * j_idx + n_int - 1 - front)
             & (n_idx < nr - 1)).astype(np.float32)
    grid_spec = pltpu.PrefetchScalarGridSpec(
        num_scalar_prefetch=1,
        grid=(b, NSA_KV_HEADS, s // tq),
        in_specs=[pl.BlockSpec((1, NSA_GROUP, dh, tq), lambda i, g, j, sl: (i, TQ_NSA // NSA_GROUP + g, 0, j)),
                  pl.BlockSpec((1, 1, nr, dh), lambda i, g, j, sl: (i, g, 0, 0)),
                  pl.BlockSpec((1, 1, dh, nr), lambda i, g, j, sl: (i, 2 + g, 0, 0)),
                  pl.BlockSpec((nsb, nr), lambda i, g, j, sl: (0, 0))],
        out_specs=[pl.BlockSpec((1, NSA_GROUP, dh, tq), lambda i, g, j, sl: (i, g, 0, j)),
                   pl.BlockSpec((1, 1, nsb, tq), lambda i, g, j, sl: (i, g, 0, j)),
                   pl.BlockSpec((1, 1, 1, 8, nsb), lambda i, g, j, sl: (i, g, j, 0, 0))])
    assert nr % min(NSA_CMP_CHUNK, nr) == 0
    return pl.pallas_call(
        _nsa_cmp_kernel,
        grid_spec=grid_spec,
        out_shape=[jax.ShapeDtypeStruct((b, NSA_HEADS, dh, s), F32),
                   jax.ShapeDtypeStruct((b, NSA_KV_HEADS, nsb, s), F32),
                   jax.ShapeDtypeStruct((b, NSA_KV_HEADS, s // tq, 8, nsb), F32)],
        compiler_params=_cparams(("parallel", "parallel", "parallel")),
        name="nsa_compressed_select",
    )(jnp.asarray(_alibi_slopes2(NSA_HEADS)), qvt, cmp_kv, cmp_kvt, jnp.asarray(mimpt, BF16))


NSA_TK = 256
NSA_SEL_TQ = 256


def _nsa_qaug(qt_ref, srow_ref):
    return jnp.concatenate(
        [jnp.concatenate([qt_ref[0, hh], srow_ref[hh]], axis=0) for hh in range(NSA_GROUP)], axis=1)


def _nsa_sel_kernel(bits_ref, qt_ref, k_ref, vt_ref, selt_ref, srow_ref, slane_ref, o_ref,
                    m_ref, acc_ref, vaug_ref, sa_ref, ma_ref, sb_ref, mb_ref, list_ref, *, nq, words):
    tq, tk = NSA_SEL_TQ, NSA_TK
    per_tile = tk // NSA_SEL_BLOCK
    qi = pl.program_id(2)
    q0 = qi * tq

    @pl.when(qi == 0)
    def _():
        _fill_vaug(vaug_ref, vt_ref)

    base = ((pl.program_id(0) * NSA_KV_HEADS + pl.program_id(1)) * nq + qi) * words
    list_ref[0] = 0

    def note(j, n):
        list_ref[n] = j
        return n + ((bits_ref[base + j // 32] >> (j % 32)) & 1)

    n_tiles = lax.fori_loop(0, qi, note, 0)

    qaug = _nsa_qaug(qt_ref, srow_ref)
    slane = slane_ref[0:1, :]
    diag = q0 // tk

    def scores(kj):
        st = _dot(_ktile(k_ref, kj, tk), qaug)
        rows = [jnp.broadcast_to(selt_ref[0, 0, pl.ds(kj * per_tile + c, 1), :], (NSA_SEL_BLOCK, tq))
                for c in range(per_tile)]
        bias = (jnp.concatenate(rows, axis=0) - 1.0) * (-NEG)
        return st + jnp.concatenate([bias] * NSA_GROUP, axis=1)

    def meta(kj):
        return slane * (kj * tk - q0).astype(F32), _vtile(vaug_ref, kj, tk)

    def own_tile():
        key = lax.broadcasted_iota(jnp.int32, (tk, tq), 0)
        qry = lax.broadcasted_iota(jnp.int32, (tk, tq), 1)
        causal = jnp.concatenate([key <= qry] * NSA_GROUP, axis=1)
        return jnp.where(causal, scores(diag), NEG)

    _init_state_t(m_ref, acc_ref)
    _flash_pipeline([(own_tile, lambda: meta(diag))], n_tiles,
                    lambda i: scores(list_ref[i]), lambda i: meta(list_ref[i]),
                    ((sa_ref, ma_ref), (sb_ref, mb_ref)), m_ref, acc_ref, n_even=False)
    out = _finish_t(acc_ref)
    for hh in range(NSA_GROUP):
        o_ref[0, hh] = out[:, hh * tq:(hh + 1) * tq]


def _nsa_tables(tq):
    slopes2 = _alibi_slopes2(NSA_HEADS)
    srow = _slope_rows(slopes2, tq)
    slane = np.repeat(slopes2.reshape(NSA_KV_HEADS, NSA_GROUP), tq, axis=1)
    slane8 = np.repeat(slane[:, None, :], 8, axis=1)
    return srow, jnp.asarray(slane8, F32)


def _active_tile_bits(cnt, tq, tk):
    b, g, nq128, _, nsb = cnt.shape
    qper, bper = tq // NSA_TQ, tk // NSA_SEL_BLOCK
    nq, nkv = nq128 // qper, nsb // bper
    act = cnt[:, :, :, 0, :].reshape(b, g, nq, qper, nkv, bper).sum(axis=(3, 5)) > 0.0
    words = -(-nkv // 32)
    act = jnp.pad(act, ((0, 0), (0, 0), (0, 0), (0, words * 32 - nkv))).reshape(b, g, nq, words, 32)
    bits = jnp.sum(act.astype(jnp.uint32) << jnp.arange(32, dtype=jnp.uint32), axis=-1, dtype=jnp.uint32)
    return lax.bitcast_convert_type(bits, jnp.int32).reshape(-1), nq, words


def nsa_sel_pallas(qvt, kaug, selt, cnt):
    b, _, dh, s = qvt.shape
    tq = NSA_SEL_TQ
    assert tq == NSA_TK
    lanes = NSA_GROUP * tq
    nsb = s // NSA_SEL_BLOCK
    srow, slane = _nsa_tables(tq)
    bits, nq, words = _active_tile_bits(cnt, tq, NSA_TK)
    grid_spec = pltpu.PrefetchScalarGridSpec(
        num_scalar_prefetch=1,
        grid=(b, NSA_KV_HEADS, nq),
        in_specs=[pl.BlockSpec((1, NSA_GROUP, dh, tq), lambda i, g, j, bt: (i, TQ_NSA // NSA_GROUP + g, 0, j)),
                  pl.BlockSpec((1, 1, s, KAUG), lambda i, g, j, bt: (i, KA_NSA + g, 0, 0)),
                  pl.BlockSpec((1, 1, dh, s), lambda i, g, j, bt: (i, TV_NSA + g, 0, 0)),
                  pl.BlockSpec((1, 1, nsb, tq), lambda i, g, j, bt: (i, g, 0, j)),
                  pl.BlockSpec((NSA_GROUP, dh, tq), lambda i, g, j, bt: (g, 0, 0)),
                  pl.BlockSpec((None, 8, lanes), lambda i, g, j, bt: (g, 0, 0))],
        out_specs=pl.BlockSpec((1, NSA_GROUP, dh, tq), lambda i, g, j, bt: (i, g, 0, j)),
        scratch_shapes=[pltpu.VMEM((1, lanes), F32), pltpu.VMEM((VROWS, lanes), F32),
                        pltpu.VMEM((VROWS, s), BF16)] + _score_buffers(NSA_TK, lanes)
        + [pltpu.SMEM((max(nq, 8),), jnp.int32)])
    return pl.pallas_call(
        functools.partial(_nsa_sel_kernel, nq=nq, words=words),
        grid_spec=grid_spec,
        out_shape=jax.ShapeDtypeStruct((b, NSA_HEADS, dh, s), F32),
        compiler_params=_cparams(("parallel", "parallel", "arbitrary")),
        name="nsa_selected",
    )(bits, qvt, kaug, qvt, selt, srow, slane)


NSA_WT = 256


def _nsa_win_kernel(qt_ref, k_ref, vt_ref, srow_ref, slane_ref, oc_ref, os_ref, gate_ref, o_ref,
                    m_ref, acc_ref, vaug_ref, sa_ref, ma_ref, sb_ref, mb_ref, gt_ref):
    tq = NSA_WT
    wt = NSA_WT
    qi = pl.program_id(2)

    @pl.when(qi == 0)
    def _():
        _fill_vaug(vaug_ref, vt_ref)

    qaug = _nsa_qaug(qt_ref, srow_ref)
    slane = slane_ref[0:1, :]
    key = lax.broadcasted_iota(jnp.int32, (wt, tq), 0)
    qry = lax.broadcasted_iota(jnp.int32, (wt, tq), 1)
    span = NSA_WINDOW // wt

    def tile(d, keep):
        kj = jnp.maximum(qi - d, 0)

        def scores():
            st = _dot(_ktile(k_ref, kj, wt), qaug)
            if keep is not None:
                st = jnp.where(jnp.concatenate([keep] * NSA_GROUP, axis=1), st, NEG)
            return st

        def meta():
            base = (kj * wt) // POS_PERIOD * POS_PERIOD - qi * tq
            delta = jnp.where(qi - d >= 0, slane * base.astype(F32), SKIP)
            return delta, _vtile(vaug_ref, kj, wt)

        return scores, meta

    _init_state_t(m_ref, acc_ref)
    tiles = [tile(0, key <= qry)] + [tile(d, None) for d in range(1, span)] + [tile(span, key > qry)]
    _flash_pipeline(tiles, None, None, None, ((sa_ref, ma_ref), (sb_ref, mb_ref)), m_ref, acc_ref, True)

    o_w = _finish_t(acc_ref)
    gt_ref[...] = gate_ref[0].T
    for hh in range(NSA_GROUP):
        c0 = 3 * (pl.program_id(1) * NSA_GROUP + hh)
        mix = (gt_ref[pl.ds(c0, 1), :] * oc_ref[0, hh] + gt_ref[pl.ds(c0 + 1, 1), :] * os_ref[0, hh]
               + gt_ref[pl.ds(c0 + 2, 1), :] * o_w[:, hh * tq:(hh + 1) * tq])
        o_ref[0, hh] = mix.astype(o_ref.dtype)


def nsa_win_pallas(qvt, kaug, o_c, o_s, gates):
    b, _, dh, s = qvt.shape
    tq = wt = NSA_WT
    assert NSA_WINDOW % wt == 0 and POS_PERIOD % wt == 0
    lanes = NSA_GROUP * tq
    srow, slane = _nsa_tables(tq)
    head_blk = pl.BlockSpec((1, NSA_GROUP, dh, tq), lambda i, g, j: (i, g, 0, j))
    out = pl.pallas_call(
        _nsa_win_kernel,
        grid=(b, NSA_KV_HEADS, s // tq),
        in_specs=[pl.BlockSpec((1, NSA_GROUP, dh, tq), lambda i, g, j: (i, TQ_NSA // NSA_GROUP + g, 0, j)),
                  pl.BlockSpec((1, 1, s, KAUG), lambda i, g, j: (i, KA_NSA + 2 + g, 0, 0)),
                  pl.BlockSpec((1, 1, dh, s), lambda i, g, j: (i, TV_NSA + 2 + g, 0, 0)),
                  pl.BlockSpec((NSA_GROUP, dh, tq), lambda i, g, j: (g, 0, 0)),
                  pl.BlockSpec((None, 8, lanes), lambda i, g, j: (g, 0, 0)),
                  head_blk, head_blk,
                  pl.BlockSpec((1, tq, LANES), lambda i, g, j: (i, j, 0))],
        out_specs=head_blk,
        out_shape=jax.ShapeDtypeStruct((b, NSA_HEADS, dh, s), BF16),
        scratch_shapes=[pltpu.VMEM((1, lanes), F32), pltpu.VMEM((VROWS, lanes), F32),
                        pltpu.VMEM((VROWS, s), BF16)] + _score_buffers(wt, lanes)
        + [pltpu.VMEM((LANES, tq), F32)],
        compiler_params=_cparams(("parallel", "parallel", "arbitrary")),
        name="nsa_window_mix",
    )(qvt, kaug, qvt, srow, slane, o_c, o_s, gates)
    return out.reshape(b, NSA_HEADS * dh, s)


def _mem_attn_kernel(q_ref, k_ref, vt_ref, o_ref):
    for hh in range(MEM_HEADS):
        s = _dot_nt(q_ref[0, hh], k_ref[0, hh])
        m = jnp.max(s, axis=-1, keepdims=True)
        e = jnp.exp2(s - m)
        p = e / jnp.sum(e, axis=-1, keepdims=True)
        o_ref[0, hh] = _dot_nt(vt_ref[0, hh], p.astype(BF16)).astype(o_ref.dtype)


def mem_attn_pallas(nat, mem_k, mem_vt, tq=512):
    b, _, s, dh = nat.shape
    n_mem = mem_k.shape[2]
    out = pl.pallas_call(
        _mem_attn_kernel,
        grid=(b, s // tq),
        in_specs=[pl.BlockSpec((1, MEM_HEADS, tq, dh), lambda i, j: (i, N_MEMQ // MEM_HEADS, j, 0)),
                  pl.BlockSpec((1, MEM_HEADS, n_mem, dh), lambda i, j: (i, 0, 0, 0)),
                  pl.BlockSpec((1, MEM_HEADS, dh, n_mem), lambda i, j: (i, 0, 0, 0))],
        out_specs=pl.BlockSpec((1, MEM_HEADS, dh, tq), lambda i, j: (i, 0, 0, j)),
        out_shape=jax.ShapeDtypeStruct((b, MEM_HEADS, dh, s), BF16),
        compiler_params=_cparams(("parallel", "parallel")),
        name="memory_attention",
    )(nat, mem_k, mem_vt)
    return out.reshape(b, MEM_HEADS * dh, s)


def _in_proj_weights(w_in):
    hd = HEAD_DIM
    sizes = (3 * MOBA_HEADS * hd, NSA_HEADS * hd, 6 * NSA_KV_HEADS * hd, 3 * NSA_HEADS,
             3 * FOX_HEADS * hd, FOX_HEADS, MEM_HEADS * hd)
    offs = np.concatenate([[0], np.cumsum(sizes)])
    moba, nsa_q, nsa_kv, nsa_g, fox, fox_f, mem_q = (w_in[:, offs[i]:offs[i + 1]] for i in range(7))
    mh, fh, g2 = MOBA_HEADS * hd, FOX_HEADS * hd, NSA_KV_HEADS * hd
    moba_q, moba_k, moba_v = moba[:, :mh], moba[:, mh:2 * mh], moba[:, 2 * mh:]
    fox_q, fox_k, fox_v = fox[:, :fh], fox[:, fh:2 * fh], fox[:, 2 * fh:]
    k_cmp, v_cmp, k_slc, v_slc, k_win, v_win = (nsa_kv[:, i * g2:(i + 1) * g2] for i in range(6))
    w_t = jnp.concatenate([moba_q, nsa_q, fox_q, moba_v, v_slc, v_win, fox_v], axis=1).T.astype(BF16)
    t_scale = np.ones((T_SLOTS * hd,), np.float32)
    t_scale[:TV_MOBA * hd] = Q_SCALE
    w_ka = jnp.concatenate([moba_k, k_slc, k_win], axis=1).astype(BF16)
    w_kf = fox_k.astype(BF16)
    w_nat = jnp.concatenate([k_cmp, v_cmp, mem_q], axis=1).astype(BF16)
    n_scale = np.ones((N_SLOTS * hd,), np.float32)
    n_scale[N_MEMQ * hd:] = Q_SCALE
    return w_t, jnp.asarray(t_scale), w_ka, w_kf, w_nat, jnp.asarray(n_scale), nsa_g, fox_f


def _mixer(h32, h16, mem16, w_in, b_forget, w_mem_kv, cmp_pe, cmp_w1, cmp_w2):
    b, s, d = h16.shape
    tm = min(1024, s)
    w_t, t_scale, w_ka, w_kf, w_nat, n_scale, w_gate, w_forget = _in_proj_weights(w_in)
    gates, caug = gates_pallas(h32, w_gate, w_forget, b_forget)
    qvt = proj_t_pallas(h16, w_t, t_scale, tm=tm, heads_per_step=13)
    k_alibi = proj_kaug_pallas(h16, w_ka, None, tm=tm, heads_per_step=KA_SLOTS)
    k_fox = proj_kaug_pallas(h16, w_kf, caug, tm=tm, heads_per_step=FOX_HEADS)
    nat = proj_heads_pallas(h16, w_nat, n_scale, tm=tm, heads_per_step=8)
    n_mem = mem16.shape[1]
    mk = MEM_HEADS * HEAD_DIM
    mem_k = proj_heads_pallas(mem16, w_mem_kv[:, :mk].astype(BF16), jnp.ones((mk,), F32),
                              tm=n_mem, heads_per_step=MEM_HEADS)
    mem_vt = proj_t_pallas(mem16, w_mem_kv[:, mk:].T.astype(BF16), jnp.ones((mk,), F32),
                           tm=n_mem, heads_per_step=MEM_HEADS)
    o_moba = moba_pallas(qvt, k_alibi)
    o_fox = fox_pallas(qvt, k_fox)
    cmp_kv, cmp_kvt = nsa_compress_pallas(nat, cmp_pe, cmp_w1, cmp_w2)
    o_c, selt, cnt = nsa_cmp_pallas(qvt, cmp_kv, cmp_kvt)
    o_s = nsa_sel_pallas(qvt, k_alibi, selt, cnt)
    o_nsa = nsa_win_pallas(qvt, k_alibi, o_c, o_s, gates)
    o_mem = mem_attn_pallas(nat, mem_k, mem_vt)
    return [o_moba, o_nsa, o_fox, o_mem]


def kernel(x, mem, emb_ln_g, emb_ln_b, w_in, b_forget, w_mem_kv, nsa_cmp_pe, nsa_cmp_w1, nsa_cmp_w2,
           w_out, ln1_g, ln1_b, ffn_w_up, ffn_conv_w, ffn_conv_b, ffn_w_down, ln2_g, ln2_b):
    b, s, d = x.shape
    depth = w_in.shape[0]
    dff = ffn_w_down.shape[1]
    mem16 = mem.astype(BF16)
    h32, h16 = layer_norm_pallas(x.reshape(b * s, d), emb_ln_g, emb_ln_b)
    for l in range(depth):
        heads = _mixer(h32.reshape(b, s, d), h16.reshape(b, s, d), mem16, w_in[l], b_forget[l], w_mem_kv[l],
                       nsa_cmp_pe[l], nsa_cmp_w1[l], nsa_cmp_w2[l])
        h32, h16 = out_proj_ln_pallas(heads, w_out[l].astype(BF16), h32.reshape(b, s, d),
                                      ln1_g[l], ln1_b[l], tm=512)
        a = ffn_up_pallas(h16, ffn_w_up[l].astype(BF16), ffn_conv_w[l], ffn_conv_b[l],
                          tm=min(1024, s), tn=512)
        h32, h16 = matmul_ln_resident_pallas(a.reshape(b * s, dff), ffn_w_down[l].astype(BF16),
                                             h32.reshape(b * s, d), ln2_g[l], ln2_b[l], tm=256)
    return h32.reshape(b, s, d)
```

```python
import functools
import math

import jax
import jax.numpy as jnp
import ml_dtypes
import numpy as np
from jax import lax
from jax.experimental import pallas as pl
from jax.experimental.pallas import tpu as pltpu

F32 = jnp.float32
BF16 = jnp.bfloat16

HEAD_DIM = 64
MOBA_HEADS = 8
NSA_HEADS = 8
NSA_KV_HEADS = 2
NSA_GROUP = NSA_HEADS // NSA_KV_HEADS
FOX_HEADS = 12
MEM_HEADS = 4
MOBA_BLOCK = 256
MOBA_TOPK = 3
NSA_CMP_LEN = 32
NSA_CMP_STRIDE = 16
NSA_SEL_BLOCK = 64
NSA_TOPK = 16
NSA_WINDOW = 512
NSA_FORCE_SCORE = 1.0e4
CONV_WIDTH = 3
LN_EPS = 1e-5
DEPTH = 2
DEEPNORM_ALPHA = (2 * DEPTH) ** 0.25

LOG2E = math.log2(math.e)
Q_SCALE = HEAD_DIM ** -0.5 * LOG2E
NEG = -1.0e30
SKIP = -3.0e38
VMEM_LIMIT = 56 * 1024 * 1024
LANES = 128
KAUG = 2 * HEAD_DIM
VROWS = HEAD_DIM + 16
POS_PERIOD = 256

TQ_MOBA, TQ_NSA, TQ_FOX = 0, 8, 16
TV_MOBA, TV_NSA, TV_FOX = 28, 36, 40
T_SLOTS = 52
KA_MOBA, KA_NSA = 0, 8
KA_SLOTS = 12
N_CMP, N_MEMQ = 0, 4
N_SLOTS = 8


def _cparams(sem):
    return pltpu.CompilerParams(dimension_semantics=sem, vmem_limit_bytes=VMEM_LIMIT)


def _split2(x):
    hi = x.astype(BF16)
    return hi, (x - hi.astype(F32)).astype(BF16)


def _split3(x):
    hi = x.astype(BF16)
    r1 = x - hi.astype(F32)
    mid = r1.astype(BF16)
    lo = (r1 - mid.astype(F32)).astype(BF16)
    return hi, mid, lo


def _np_split3(x):
    x = np.asarray(x, np.float32)
    hi = x.astype(ml_dtypes.bfloat16).astype(np.float32)
    r1 = x - hi
    mid = r1.astype(ml_dtypes.bfloat16).astype(np.float32)
    lo = (r1 - mid).astype(ml_dtypes.bfloat16).astype(np.float32)
    return hi, mid, lo


def _dot_nt(a, b):
    return lax.dot_general(a, b, (((1,), (1,)), ((), ())), preferred_element_type=F32)


def _dot_tn(a, b):
    return lax.dot_general(a, b, (((0,), (0,)), ((), ())), preferred_element_type=F32)


def _dot(a, b):
    return jnp.dot(a, b, preferred_element_type=F32)


def _layer_norm_rows(x, g, b):
    mu = jnp.mean(x, axis=-1, keepdims=True)
    xc = x - mu
    var = jnp.mean(xc * xc, axis=-1, keepdims=True)
    return xc * lax.rsqrt(var + LN_EPS) * g + b


def _alibi_slopes2(n):
    return (np.exp2(-8.0 * np.arange(1, n + 1, dtype=np.float64) / n) * LOG2E).astype(np.float32)


def _slope_rows(slopes2, lanes):
    pieces = np.stack(_np_split3(slopes2), axis=1)
    rows = np.zeros((len(slopes2), HEAD_DIM, lanes), np.float32)
    rows[:, :3, :] = pieces[:, :, None]
    return jnp.asarray(rows, BF16)


def _ln_kernel(x_ref, g_ref, b_ref, o32_ref, o16_ref):
    y = _layer_norm_rows(x_ref[...], g_ref[...], b_ref[...])
    o32_ref[...] = y
    o16_ref[...] = y.astype(BF16)


def layer_norm_pallas(x, g, b, tm=512):
    m, d = x.shape
    return pl.pallas_call(
        _ln_kernel,
        grid=(m // tm,),
        in_specs=[pl.BlockSpec((tm, d), lambda i: (i, 0)),
                  pl.BlockSpec((1, d), lambda i: (0, 0)),
                  pl.BlockSpec((1, d), lambda i: (0, 0))],
        out_specs=[pl.BlockSpec((tm, d), lambda i: (i, 0)),
                   pl.BlockSpec((tm, d), lambda i: (i, 0))],
        out_shape=[jax.ShapeDtypeStruct((m, d), F32), jax.ShapeDtypeStruct((m, d), BF16)],
        compiler_params=_cparams(("parallel",)),
        name="layer_norm",
    )(x, g.reshape(1, d), b.reshape(1, d))


GATE_LANES_NSA = 3 * NSA_HEADS


def _gates_kernel(h_ref, w_ref, bf_ref, tri_ref, place_ref, g_ref, caug_ref, carry_ref):
    si = pl.program_id(1)

    @pl.when(si == 0)
    def _():
        carry_ref[...] = jnp.zeros_like(carry_ref)

    h_hi, h_lo = _split2(h_ref[0])
    w_hi, w_lo = _split2(w_ref[...])
    x = _dot(h_hi, w_hi) + _dot(h_lo, w_hi) + _dot(h_hi, w_lo)
    g_ref[0] = 1.0 / (1.0 + jnp.exp(-x))
    x = x + bf_ref[...]
    logf = jnp.minimum(x, 0.0) - jnp.log(1.0 + jnp.exp(-jnp.abs(x)))
    tri = tri_ref[...]
    l_hi, l_mid, l_lo = _split3(logf)
    c = _dot(tri, l_hi) + _dot(tri, l_mid) + _dot(tri, l_lo) + carry_ref[0:1, :]
    carry_ref[...] = jnp.broadcast_to(c[-1:, :], carry_ref.shape)
    n_hi, n_mid, n_lo = _split3(-LOG2E * c)
    caug = _dot(n_hi, place_ref[0]) + _dot(n_mid, place_ref[1]) + _dot(n_lo, place_ref[2])
    caug_ref[0] = caug.astype(BF16)


def gates_pallas(h3, w_gate, w_forget, b_forget, t=512):
    b, s, d = h3.shape
    lo = GATE_LANES_NSA
    w = jnp.zeros((d, LANES), F32).at[:, :lo].set(w_gate).at[:, lo:lo + FOX_HEADS].set(w_forget)
    bf = jnp.zeros((1, LANES), F32).at[0, lo:lo + FOX_HEADS].set(b_forget)
    tri = (np.arange(t)[None, :] <= np.arange(t)[:, None]).astype(np.float32)
    nc = FOX_HEADS * HEAD_DIM
    place = np.zeros((3, LANES, nc), np.float32)
    for piece in range(3):
        for hh in range(FOX_HEADS):
            place[piece, lo + hh, hh * HEAD_DIM + piece] = 1.0
    return pl.pallas_call(
        _gates_kernel,
        grid=(b, s // t),
        in_specs=[pl.BlockSpec((1, t, d), lambda i, j: (i, j, 0)),
                  pl.BlockSpec((d, LANES), lambda i, j: (0, 0)),
                  pl.BlockSpec((1, LANES), lambda i, j: (0, 0)),
                  pl.BlockSpec((t, t), lambda i, j: (0, 0)),
                  pl.BlockSpec((3, LANES, nc), lambda i, j: (0, 0, 0))],
        out_specs=[pl.BlockSpec((1, t, LANES), lambda i, j: (i, j, 0)),
                   pl.BlockSpec((1, t, nc), lambda i, j: (i, j, 0))],
        out_shape=[jax.ShapeDtypeStruct((b, s, LANES), F32),
                   jax.ShapeDtypeStruct((b, s, nc), BF16)],
        scratch_shapes=[pltpu.VMEM((8, LANES), F32)],
        compiler_params=_cparams(("parallel", "arbitrary")),
        name="gates_cumsum",
    )(h3, w, bf, jnp.asarray(tri, BF16), jnp.asarray(place, BF16))


def _proj_heads_kernel(x_ref, w_ref, sc_ref, o_ref, *, heads_per_step):
    acc = _dot(x_ref[0], w_ref[...]) * sc_ref[...]
    for j in range(heads_per_step):
        o_ref[0, j] = acc[:, j * HEAD_DIM:(j + 1) * HEAD_DIM].astype(o_ref.dtype)


def proj_heads_pallas(x3, w, colscale, tm, heads_per_step):
    b, s, d = x3.shape
    n = w.shape[1]
    tn = heads_per_step * HEAD_DIM
    return pl.pallas_call(
        functools.partial(_proj_heads_kernel, heads_per_step=heads_per_step),
        grid=(b, s // tm, n // tn),
        in_specs=[pl.BlockSpec((1, tm, d), lambda i, j, k: (i, j, 0)),
                  pl.BlockSpec((d, tn), lambda i, j, k: (0, k)),
                  pl.BlockSpec((1, tn), lambda i, j, k: (0, k))],
        out_specs=pl.BlockSpec((1, heads_per_step, tm, HEAD_DIM), lambda i, j, k: (i, k, j, 0)),
        out_shape=jax.ShapeDtypeStruct((b, n // HEAD_DIM, s, HEAD_DIM), BF16),
        compiler_params=_cparams(("parallel", "parallel", "arbitrary")),
        name="proj_heads",
    )(x3, w, colscale.reshape(1, n))


def _proj_t_kernel(x_ref, wt_ref, sc_ref, o_ref, *, heads_per_step):
    acc = _dot_nt(wt_ref[...], x_ref[0]) * sc_ref[...]
    o_ref[0] = acc.reshape(heads_per_step, HEAD_DIM, acc.shape[1]).astype(o_ref.dtype)


def proj_t_pallas(x3, wt, rowscale, tm, heads_per_step):
    b, s, d = x3.shape
    n = wt.shape[0]
    tn = heads_per_step * HEAD_DIM
    return pl.pallas_call(
        functools.partial(_proj_t_kernel, heads_per_step=heads_per_step),
        grid=(b, s // tm, n // tn),
        in_specs=[pl.BlockSpec((1, tm, d), lambda i, j, k: (i, j, 0)),
                  pl.BlockSpec((tn, d), lambda i, j, k: (k, 0)),
                  pl.BlockSpec((tn, 1), lambda i, j, k: (k, 0))],
        out_specs=pl.BlockSpec((1, heads_per_step, HEAD_DIM, tm), lambda i, j, k: (i, k, 0, j)),
        out_shape=jax.ShapeDtypeStruct((b, n // HEAD_DIM, HEAD_DIM, s), BF16),
        compiler_params=_cparams(("parallel", "parallel", "arbitrary")),
        name="proj_transposed",
    )(x3, wt, rowscale.reshape(n, 1))


def _proj_kaug_kernel(x_ref, w_ref, *rest, heads_per_step, positional):
    o_ref = rest[-1]
    acc = _dot(x_ref[0], w_ref[...])
    tm = acc.shape[0]
    if positional:
        pos = (pl.program_id(1) * tm + lax.broadcasted_iota(jnp.int32, (tm, HEAD_DIM), 0)) % POS_PERIOD
        lane = lax.broadcasted_iota(jnp.int32, (tm, HEAD_DIM), 1)
        pos_lanes = jnp.where(lane < 3, pos.astype(F32), 0.0).astype(o_ref.dtype)
    for j in range(heads_per_step):
        cols = slice(j * HEAD_DIM, (j + 1) * HEAD_DIM)
        bias = pos_lanes if positional else rest[0][0, :, cols]
        o_ref[0, j] = jnp.concatenate([acc[:, cols].astype(o_ref.dtype), bias], axis=1)


def proj_kaug_pallas(x3, w, aug, tm, heads_per_step):
    b, s, d = x3.shape
    n = w.shape[1]
    tn = heads_per_step * HEAD_DIM
    in_specs = [pl.BlockSpec((1, tm, d), lambda i, j, k: (i, j, 0)),
                pl.BlockSpec((d, tn), lambda i, j, k: (0, k))]
    args = [x3, w]
    if aug is not None:
        in_specs.append(pl.BlockSpec((1, tm, tn), lambda i, j, k: (i, j, k)))
        args.append(aug)
    return pl.pallas_call(
        functools.partial(_proj_kaug_kernel, heads_per_step=heads_per_step, positional=aug is None),
        grid=(b, s // tm, n // tn),
        in_specs=in_specs,
        out_specs=pl.BlockSpec((1, heads_per_step, tm, KAUG), lambda i, j, k: (i, k, j, 0)),
        out_shape=jax.ShapeDtypeStruct((b, n // HEAD_DIM, s, KAUG), BF16),
        compiler_params=_cparams(("parallel", "parallel", "arbitrary")),
        name="proj_keys_aug",
    )(*args)


def _matmul_ln_resident_kernel(x_ref, w_ref, r_ref, g_ref, b_ref, o32_ref, o16_ref):
    y = _layer_norm_rows(DEEPNORM_ALPHA * r_ref[...] + _dot(x_ref[...], w_ref[...]), g_ref[...], b_ref[...])
    o32_ref[...] = y
    o16_ref[...] = y.astype(BF16)


def matmul_ln_resident_pallas(x, w, res, g, b, tm):
    m, kk = x.shape
    d = w.shape[1]
    return pl.pallas_call(
        _matmul_ln_resident_kernel,
        grid=(m // tm,),
        in_specs=[pl.BlockSpec((tm, kk), lambda i: (i, 0)),
                  pl.BlockSpec((kk, d), lambda i: (0, 0), pipeline_mode=pl.Buffered(1)),
                  pl.BlockSpec((tm, d), lambda i: (i, 0)),
                  pl.BlockSpec((1, d), lambda i: (0, 0)),
                  pl.BlockSpec((1, d), lambda i: (0, 0))],
        out_specs=[pl.BlockSpec((tm, d), lambda i: (i, 0)),
                   pl.BlockSpec((tm, d), lambda i: (i, 0))],
        out_shape=[jax.ShapeDtypeStruct((m, d), F32), jax.ShapeDtypeStruct((m, d), BF16)],
        compiler_params=_cparams(("parallel",)),
        name="matmul_ln_resident",
    )(x, w, res, g.reshape(1, d), b.reshape(1, d))


def _out_proj_ln_kernel(*refs, widths):
    n = len(widths)
    x_refs, (w_ref, r_ref, g_ref, b_ref, o32_ref, o16_ref) = refs[:n], refs[n:]
    acc = None
    off = 0
    for x_ref, width in zip(x_refs, widths):
        part = _dot_tn(x_ref[0], w_ref[off:off + width, :])
        acc = part if acc is None else acc + part
        off += width
    y = _layer_norm_rows(DEEPNORM_ALPHA * r_ref[0] + acc, g_ref[...], b_ref[...])
    o32_ref[0] = y
    o16_ref[0] = y.astype(BF16)


def out_proj_ln_pallas(xts, w, res3, g, b, tm):
    bsz, s, d = res3.shape
    widths = tuple(x.shape[1] for x in xts)
    in_specs = [pl.BlockSpec((1, wd, tm), lambda i, j: (i, 0, j)) for wd in widths]
    in_specs += [pl.BlockSpec((w.shape[0], d), lambda i, j: (0, 0)),
                 pl.BlockSpec((1, tm, d), lambda i, j: (i, j, 0)),
                 pl.BlockSpec((1, d), lambda i, j: (0, 0)),
                 pl.BlockSpec((1, d), lambda i, j: (0, 0))]
    return pl.pallas_call(
        functools.partial(_out_proj_ln_kernel, widths=widths),
        grid=(bsz, s // tm),
        in_specs=in_specs,
        out_specs=[pl.BlockSpec((1, tm, d), lambda i, j: (i, j, 0)),
                   pl.BlockSpec((1, tm, d), lambda i, j: (i, j, 0))],
        out_shape=[jax.ShapeDtypeStruct((bsz, s, d), F32), jax.ShapeDtypeStruct((bsz, s, d), BF16)],
        compiler_params=_cparams(("parallel", "parallel")),
        name="out_proj_ln",
    )(*xts, w, res3, g.reshape(1, d), b.reshape(1, d))


HALO = 16
FFN_CHUNK = 256


def _gelu_tanh(x):
    return 0.5 * x * (1.0 + jnp.tanh(math.sqrt(2.0 / math.pi) * (x + 0.044715 * x * x * x)))


def _ffn_up_kernel(x_ref, xh_ref, wu_ref, wg_ref, cw_ref, cb_ref, o_ref):
    j = pl.program_id(1)
    x = x_ref[0]
    xh = xh_ref[0]
    first = jnp.where(j > 0, 1.0, 0.0)
    tn = o_ref.shape[2]
    row = lax.broadcasted_iota(jnp.int32, (x.shape[0], FFN_CHUNK), 0)
    for c in range(tn // FFN_CHUNK):
        cols = slice(c * FFN_CHUNK, (c + 1) * FFN_CHUNK)
        u = _dot(x, wu_ref[:, cols])
        g = _dot(x, wg_ref[:, cols])
        gh = _dot(xh, wg_ref[:, cols]) * first
        prev1 = gh[HALO - 1:HALO, :]
        prev2 = gh[HALO - 2:HALO - 1, :]
        g_m1 = jnp.where(row == 0, prev1, pltpu.roll(g, 1, axis=0))
        g_m2 = jnp.where(row == 0, prev2, jnp.where(row == 1, prev1, pltpu.roll(g, 2, axis=0)))
        cw = cw_ref[:, cols]
        gc = cb_ref[:, cols] + cw[0:1] * g_m2 + cw[1:2] * g_m1 + cw[2:3] * g
        o_ref[0, :, cols] = (_gelu_tanh(gc) * u).astype(o_ref.dtype)


def ffn_up_pallas(x3, w_up, conv_w, conv_b, tm, tn):
    b, s, d = x3.shape
    dff = w_up.shape[1] // 2
    nt = dff // tn
    hb = tm // HALO
    cw = jnp.zeros((8, dff), F32).at[:CONV_WIDTH].set(conv_w)
    return pl.pallas_call(
        _ffn_up_kernel,
        grid=(b, s // tm, nt),
        in_specs=[pl.BlockSpec((1, tm, d), lambda i, j, k: (i, j, 0)),
                  pl.BlockSpec((1, HALO, d), lambda i, j, k: (i, jnp.maximum(j * hb - 1, 0), 0)),
                  pl.BlockSpec((d, tn), lambda i, j, k: (0, k)),
                  pl.BlockSpec((d, tn), lambda i, j, k: (0, k + nt)),
                  pl.BlockSpec((8, tn), lambda i, j, k: (0, k)),
                  pl.BlockSpec((1, tn), lambda i, j, k: (0, k))],
        out_specs=pl.BlockSpec((1, tm, tn), lambda i, j, k: (i, j, k)),
        out_shape=jax.ShapeDtypeStruct((b, s, dff), BF16),
        compiler_params=_cparams(("parallel", "parallel", "arbitrary")),
        name="ffn_up",
    )(x3, x3, w_up, w_up, cw, conv_b.reshape(1, dff))


def _init_state_t(m_ref, acc_ref):
    m_ref[...] = jnp.full(m_ref.shape, NEG, F32)
    acc_ref[...] = jnp.zeros(acc_ref.shape, F32)


def _finish_t(acc_ref):
    acc = acc_ref[...]
    return acc[:HEAD_DIM] / acc[HEAD_DIM:HEAD_DIM + 1]


def _fill_vaug(vaug_ref, vt_ref):
    s = vaug_ref.shape[1]
    vaug_ref[0:HEAD_DIM, :] = vt_ref[0, 0]
    pad = lax.broadcasted_iota(jnp.int32, (VROWS - HEAD_DIM, s), 0)
    vaug_ref[HEAD_DIM:VROWS, :] = jnp.where(pad == 0, 1.0, 0.0).astype(BF16)


def _ktile(ref, idx, size):
    return ref[0, 0, pl.ds(pl.multiple_of(idx * size, size), size), :]


def _vtile(ref, idx, size):
    return ref[:, pl.ds(pl.multiple_of(idx * size, size), size)]


def _stage_scores(st, s_ref, mc_ref):
    nsub = mc_ref.shape[0]
    s_ref[...] = st
    if nsub == 1:
        mc_ref[...] = jnp.max(st, axis=0, keepdims=True)
    else:
        mc_ref[...] = jnp.max(st.reshape(nsub, st.shape[0] // nsub, st.shape[1]), axis=1)


def _stage_update(s_ref, mc_ref, delta, vaug, m_ref, acc_ref):
    nsub = mc_ref.shape[0]
    m_prev = m_ref[...]
    m_new = jnp.maximum(m_prev, jnp.max(mc_ref[...] + delta, axis=0, keepdims=True))
    alpha = jnp.exp2(m_prev - m_new)
    shift = m_new - delta
    if nsub == 1:
        pt = jnp.exp2(s_ref[...] - shift)
    else:
        tk, lanes = s_ref.shape
        pt = jnp.exp2(s_ref[...].reshape(nsub, tk // nsub, lanes) - shift[:, None, :]).reshape(tk, lanes)
    acc_ref[...] = alpha * acc_ref[...] + _dot(vaug, pt.astype(BF16))
    m_ref[...] = m_new


def _score_buffers(tk, lanes, nsub=1):
    return [pltpu.VMEM((tk, lanes), F32), pltpu.VMEM((nsub, lanes), F32),
            pltpu.VMEM((tk, lanes), F32), pltpu.VMEM((nsub, lanes), F32)]


def _flash_pipeline(lead, n_loop, scores, meta, bufs, m_ref, acc_ref, n_even):
    _stage_scores(lead[0][0](), *bufs[0])
    for i in range(1, len(lead)):
        _stage_scores(lead[i][0](), *bufs[i % 2])
        _stage_update(*bufs[(i - 1) % 2], *lead[i - 1][1](), m_ref, acc_ref)
    cur = (len(lead) - 1) % 2
    nxt = 1 - cur
    if scores is None:
        _stage_update(*bufs[cur], *lead[-1][1](), m_ref, acc_ref)
        return
    _stage_scores(scores(0), *bufs[nxt])
    _stage_update(*bufs[cur], *lead[-1][1](), m_ref, acc_ref)

    def pair(k0, stage_next, both):
        if both:
            _stage_scores(scores(k0 + 1), *bufs[cur])
        _stage_update(*bufs[nxt], *meta(k0), m_ref, acc_ref)
        if stage_next:
            _stage_scores(scores(k0 + 2), *bufs[nxt])
        if both:
            _stage_update(*bufs[cur], *meta(k0 + 1), m_ref, acc_ref)

    full = jnp.maximum(n_loop - 1, 0) // 2

    def body(kp, carry):
        pair(2 * kp, True, True)
        return carry

    lax.fori_loop(0, full, body, 0)
    rest = n_loop - 2 * full
    pl.when(rest == 2)(lambda: pair(2 * full, False, True))
    if not n_even:
        pl.when(rest == 1)(lambda: pair(2 * full, False, False))


FOX_TQ = 1024
FOX_TK = 512


def _fox_kernel(qt_ref, k_ref, vt_ref, o_ref, m_ref, acc_ref, vaug_ref, sa_ref, ma_ref, sb_ref, mb_ref):
    tq, tk = o_ref.shape[3], FOX_TK
    nd = tq // tk
    qi = pl.program_id(2)

    @pl.when(qi == 0)
    def _():
        _fill_vaug(vaug_ref, vt_ref)

    ones3 = jnp.where(lax.broadcasted_iota(jnp.int32, (HEAD_DIM, tq), 0) < 3, 1.0, 0.0).astype(BF16)
    qaug = jnp.concatenate([qt_ref[0, 0], ones3], axis=0)
    _init_state_t(m_ref, acc_ref)
    key = lax.broadcasted_iota(jnp.int32, (tk, tq), 0)
    qry = lax.broadcasted_iota(jnp.int32, (tk, tq), 1)

    def scores(kj):
        return _dot(_ktile(k_ref, kj, tk), qaug)

    def meta(kj):
        return 0.0, _vtile(vaug_ref, kj, tk)

    lead = [(functools.partial(lambda d: jnp.where(key + d * tk <= qry, scores(qi * nd + d), NEG), d),
             functools.partial(lambda d: meta(qi * nd + d), d)) for d in range(nd)]
    _flash_pipeline(lead, qi * nd, scores, meta, ((sa_ref, ma_ref), (sb_ref, mb_ref)), m_ref, acc_ref,
                    n_even=nd % 2 == 0)
    o_ref[0, 0] = _finish_t(acc_ref).astype(o_ref.dtype)


def fox_pallas(qvt, kaug):
    b, _, dh, s = qvt.shape
    tq = min(FOX_TQ, s)
    assert tq % FOX_TK == 0
    out = pl.pallas_call(
        _fox_kernel,
        grid=(b, FOX_HEADS, s // tq),
        in_specs=[pl.BlockSpec((1, 1, dh, tq), lambda i, h, j: (i, TQ_FOX + h, 0, j)),
                  pl.BlockSpec((1, 1, s, KAUG), lambda i, h, j: (i, h, 0, 0)),
                  pl.BlockSpec((1, 1, dh, s), lambda i, h, j: (i, TV_FOX + h, 0, 0))],
        out_specs=pl.BlockSpec((1, 1, dh, tq), lambda i, h, j: (i, h, 0, j)),
        out_shape=jax.ShapeDtypeStruct((b, FOX_HEADS, dh, s), BF16),
        scratch_shapes=[pltpu.VMEM((1, tq), F32), pltpu.VMEM((VROWS, tq), F32),
                        pltpu.VMEM((VROWS, s), BF16)] + _score_buffers(FOX_TK, tq),
        compiler_params=_cparams(("parallel", "parallel", "arbitrary")),
        name="fox_attention",
    )(qvt, kaug, qvt)
    return out.reshape(b, FOX_HEADS * dh, s)


def _moba_kernel(slope_ref, qt_ref, k_ref, vt_ref, ind_ref, srow_ref, o_ref,
                 m_ref, acc_ref, vaug_ref, km_ref, sel_ref, sa_ref, ma_ref, sb_ref, mb_ref):
    t = MOBA_BLOCK
    tq = o_ref.shape[3]
    nd = tq // t
    h = pl.program_id(1)
    qi = pl.program_id(2)
    slope2 = slope_ref[h]

    @pl.when(qi == 0)
    def _():
        _fill_vaug(vaug_ref, vt_ref)
        km_ref[...] = _dot(ind_ref[...], k_ref[0, 0]) * (1.0 / MOBA_BLOCK)

    qt = qt_ref[0, 0]
    km_hi, km_lo = _split2(km_ref[...])
    q0 = jnp.concatenate([qt, jnp.zeros_like(qt)], axis=0)
    gate = _dot(km_hi, q0) + _dot(km_lo, q0)
    blk = lax.broadcasted_iota(jnp.int32, gate.shape, 0)
    lane_blk = lax.broadcasted_iota(jnp.int32, (1, tq), 1) // t
    valid = blk < qi * nd + lane_blk
    work = jnp.where(valid, gate, -jnp.inf)
    big = jnp.int32(2 ** 30)
    for _ in range(MOBA_TOPK):
        mx = jnp.max(work, axis=0, keepdims=True)
        first = jnp.min(jnp.where(work == mx, blk, big), axis=0, keepdims=True)
        work = jnp.where(blk == first, -jnp.inf, work)
    sel_ref[...] = jnp.where(valid & (work == -jnp.inf), 1.0, 0.0)

    qaug = jnp.concatenate([qt, srow_ref[...]], axis=0)
    _init_state_t(m_ref, acc_ref)
    tk = sa_ref.shape[0]
    nsub = tk // t
    ntile = tq // tk
    key = lax.broadcasted_iota(jnp.int32, (tk, tq), 0)
    qry = lax.broadcasted_iota(jnp.int32, (tk, tq), 1)
    sub = lax.broadcasted_iota(jnp.int32, (nsub, tq), 0)

    def tile_constant(kj, d):
        sees = jnp.concatenate([sel_ref[pl.ds(kj * nsub + c, 1), :] for c in range(nsub)], axis=0) > 0.0
        if d is not None:
            sees = sees | (lane_blk == d * nsub + sub)
        offset = ((kj * nsub + sub - qi * nd) * t).astype(F32)
        return jnp.where(sees, slope2 * offset, SKIP)

    def scores(kj):
        return _dot(_ktile(k_ref, kj, tk), qaug)

    def meta(kj):
        return tile_constant(kj, None), _vtile(vaug_ref, kj, tk)

    lead = [(functools.partial(lambda d: jnp.where(key + d * tk <= qry, scores(qi * ntile + d), NEG), d),
             functools.partial(lambda d: (tile_constant(qi * ntile + d, d),
                                          _vtile(vaug_ref, qi * ntile + d, tk)), d))
            for d in range(ntile)]
    _flash_pipeline(lead, qi * ntile, scores, meta, ((sa_ref, ma_ref), (sb_ref, mb_ref)), m_ref, acc_ref,
                    n_even=ntile % 2 == 0)
    o_ref[0, 0] = _finish_t(acc_ref).astype(o_ref.dtype)


MOBA_TQ = 1024
MOBA_TK = 512


def moba_pallas(qvt, kaug):
    b, _, dh, s = qvt.shape
    t = MOBA_BLOCK
    tq = min(MOBA_TQ, s)
    tk = min(MOBA_TK, tq)
    assert POS_PERIOD == t and tq % tk == 0 and tk % t == 0
    nb = s // t
    nbp = max(16, nb)
    ind = np.zeros((nbp, s), np.float32)
    ind[np.arange(s) // t, np.arange(s)] = 1.0
    slopes2 = _alibi_slopes2(MOBA_HEADS)
    grid_spec = pltpu.PrefetchScalarGridSpec(
        num_scalar_prefetch=1,
        grid=(b, MOBA_HEADS, s // tq),
        in_specs=[pl.BlockSpec((1, 1, dh, tq), lambda i, h, j, sl: (i, TQ_MOBA + h, 0, j)),
                  pl.BlockSpec((1, 1, s, KAUG), lambda i, h, j, sl: (i, KA_MOBA + h, 0, 0)),
                  pl.BlockSpec((1, 1, dh, s), lambda i, h, j, sl: (i, TV_MOBA + h, 0, 0)),
                  pl.BlockSpec((nbp, s), lambda i, h, j, sl: (0, 0)),
                  pl.BlockSpec((None, dh, tq), lambda i, h, j, sl: (h, 0, 0))],
        out_specs=pl.BlockSpec((1, 1, dh, tq), lambda i, h, j, sl: (i, h, 0, j)),
        scratch_shapes=[pltpu.VMEM((1, tq), F32), pltpu.VMEM((VROWS, tq), F32),
                        pltpu.VMEM((VROWS, s), BF16), pltpu.VMEM((nbp, KAUG), F32),
                        pltpu.VMEM((nbp, tq), F32)] + _score_buffers(tk, tq, tk // t))
    out = pl.pallas_call(
        _moba_kernel,
        grid_spec=grid_spec,
        out_shape=jax.ShapeDtypeStruct((b, MOBA_HEADS, dh, s), BF16),
        compiler_params=_cparams(("parallel", "parallel", "arbitrary")),
        name="moba_attention",
    )(jnp.asarray(slopes2), qvt, kaug, qvt, jnp.asarray(ind, BF16), _slope_rows(slopes2, tq))
    return out.reshape(b, MOBA_HEADS * dh, s)


def _nsa_compress_kernel(x_ref, w1a_ref, w1b_ref, pe_ref, w1_ref, w2_ref, w2t_ref, o_ref, ot_ref):
    nr = x_ref.shape[2]
    x = x_ref[0, 0]

    def near_f32(xb, w):
        w_hi, w_lo = _split2(w)
        return _dot(xb, w_hi) + _dot(xb, w_lo)

    a = near_f32(x, w1a_ref[0])
    bm = near_f32(x, w1b_ref[0])
    pe_hi, pe_mid, pe_lo = _split3(pe_ref[0])
    w1_hi, w1_lo = _split2(w1_ref[0])
    pe_term = (_dot(pe_hi, w1_hi) + _dot(pe_mid, w1_hi) + _dot(pe_lo, w1_hi)
               + _dot(pe_hi, w1_lo) + _dot(pe_mid, w1_lo))[0:1]
    pre = a + pltpu.roll(bm, nr - 1, axis=0) + pe_term
    hid = _gelu_tanh(pre)
    h_hi, h_mid, h_lo = _split3(hid)
    w2_hi, w2_lo = _split2(w2_ref[0])
    o_ref[0, 0] = (_dot(h_hi, w2_hi) + _dot(h_mid, w2_hi) + _dot(h_lo, w2_hi)
                   + _dot(h_hi, w2_lo) + _dot(h_mid, w2_lo))
    t_hi, t_lo = _split2(w2t_ref[0])
    ot_ref[0, 0] = (_dot_nt(t_hi, h_hi) + _dot_nt(t_hi, h_mid) + _dot_nt(t_hi, h_lo)
                    + _dot_nt(t_lo, h_hi) + _dot_nt(t_lo, h_mid))


def nsa_compress_pallas(nat, pe, w1, w2):
    b, _, s, dh = nat.shape
    nr = s // NSA_CMP_STRIDE
    half = NSA_CMP_STRIDE * dh
    hid = w1.shape[-1]
    x = nat[:, N_CMP:N_CMP + 4].reshape(b, 4, nr, half)
    w1f = w1.reshape(2, NSA_CMP_LEN * dh, hid)
    pef = jnp.zeros((2, 8, NSA_CMP_LEN * dh), F32).at[:, 0].set(pe.reshape(2, NSA_CMP_LEN * dh))
    return pl.pallas_call(
        _nsa_compress_kernel,
        grid=(b, 4),
        in_specs=[pl.BlockSpec((1, 1, nr, half), lambda i, j: (i, j, 0, 0)),
                  pl.BlockSpec((1, half, hid), lambda i, j: (j // 2, 0, 0)),
                  pl.BlockSpec((1, half, hid), lambda i, j: (j // 2, 1, 0)),
                  pl.BlockSpec((1, 8, 2 * half), lambda i, j: (j // 2, 0, 0)),
                  pl.BlockSpec((1, 2 * half, hid), lambda i, j: (j // 2, 0, 0)),
                  pl.BlockSpec((1, hid, dh), lambda i, j: (j // 2, 0, 0)),
                  pl.BlockSpec((1, dh, hid), lambda i, j: (j // 2, 0, 0))],
        out_specs=[pl.BlockSpec((1, 1, nr, dh), lambda i, j: (i, j, 0, 0)),
                   pl.BlockSpec((1, 1, dh, nr), lambda i, j: (i, j, 0, 0))],
        out_shape=[jax.ShapeDtypeStruct((b, 4, nr, dh), F32),
                   jax.ShapeDtypeStruct((b, 4, dh, nr), F32)],
        compiler_params=_cparams(("parallel", "parallel")),
        name="nsa_compress",
    )(x, w1f, w1f, pef, w1f, w2, jnp.swapaxes(w2, 1, 2))


NSA_TQ = 128


NSA_CMP_CHUNK = 128


def _nsa_cmp_kernel(slope_ref, qt_ref, kc_ref, vct_ref, mimpt_ref, oct_ref, selt_ref, cnt_ref):
    tq = NSA_TQ
    g = pl.program_id(1)
    qi = pl.program_id(2)
    q0 = qi * tq
    nr = kc_ref.shape[2]
    nsb = selt_ref.shape[2]
    t_lane = q0 + lax.broadcasted_iota(jnp.int32, (1, tq), 1)
    chunk = min(NSA_CMP_CHUNK, nr)
    tiles_per_chunk = chunk * NSA_CMP_STRIDE // tq

    def branch(n):
        kc_hi, kc_lo = _split2(kc_ref[0, 0, 0:n, :])
        vct = vct_ref[0, 0, :, 0:n].astype(BF16)
        cmp_end = NSA_CMP_STRIDE * lax.broadcasted_iota(jnp.int32, (n, tq), 0) + (NSA_CMP_LEN - 1)
        mask = cmp_end <= t_lane
        rel = (cmp_end - q0).astype(F32)
        psum = jnp.zeros((n, tq), F32)
        qt4 = jnp.concatenate([qt_ref[0, hh] for hh in range(NSA_GROUP)], axis=1)
        st4 = _dot(kc_hi, qt4) + _dot(kc_lo, qt4)
        probs = []
        for hh in range(NSA_GROUP):
            st = st4[:, hh * tq:(hh + 1) * tq] + slope_ref[g * NSA_GROUP + hh] * rel
            st = jnp.where(mask, st, -jnp.inf)
            m = jnp.max(st, axis=0, keepdims=True)
            m = jnp.where(m > -jnp.inf, m, 0.0)
            e = jnp.exp2(st - m)
            p = e * (1.0 / jnp.maximum(jnp.sum(e, axis=0, keepdims=True), 1e-30))
            probs.append(p.astype(BF16))
            psum = psum + p
        o4 = _dot(vct, jnp.concatenate(probs, axis=1))
        for hh in range(NSA_GROUP):
            oct_ref[0, hh] = o4[:, hh * tq:(hh + 1) * tq]
        nbk = min(nsb, n * NSA_CMP_STRIDE // NSA_SEL_BLOCK)
        p_hi, p_mid, p_lo = _split3(psum)
        mimpt = mimpt_ref[0:nbk, 0:n]
        imp = _dot(mimpt, p_hi) + _dot(mimpt, p_mid) + _dot(mimpt, p_lo)
        blk = lax.broadcasted_iota(jnp.int32, (nbk, tq), 0)
        jt = t_lane // NSA_SEL_BLOCK
        forced = (blk == 0) | (blk == jt) | (blk == jt - 1)
        imp = jnp.where(forced, NSA_FORCE_SCORE, imp)
        valid = blk * NSA_SEL_BLOCK <= t_lane
        work0 = jnp.where(valid, imp, -jnp.inf)
        big = jnp.int32(2 ** 30)

        def pick(_, work):
            mx = jnp.max(work, axis=0, keepdims=True)
            first = jnp.min(jnp.where(work == mx, blk, big), axis=0, keepdims=True)
            return jnp.where(blk == first, -jnp.inf, work)

        work = lax.fori_loop(0, min(NSA_TOPK, nsb), pick, work0)
        sel = jnp.where(valid & (work == -jnp.inf), 1.0, 0.0)
        selt_ref[0, 0, 0:nbk, :] = sel
        cnt_ref[0, 0, 0, :, 0:nbk] = _dot_nt(jnp.ones((8, tq), BF16), sel.astype(BF16))
        if nbk < nsb:
            selt_ref[0, 0, nbk:nsb, :] = jnp.zeros((nsb - nbk, tq), F32)
            cnt_ref[0, 0, 0, :, nbk:nsb] = jnp.zeros((8, nsb - nbk), F32)

    n_chunks = nr // chunk
    for c in range(n_chunks):
        pl.when(jnp.minimum(qi // tiles_per_chunk, n_chunks - 1) == c)(
            functools.partial(branch, (c + 1) * chunk))


def nsa_cmp_pallas(qvt, cmp_kv, cmp_kvt):
    b, _, dh, s = qvt.shape
    tq = NSA_TQ
    nr = cmp_kv.shape[2]
    nsb = s // NSA_SEL_BLOCK
    ratio = NSA_SEL_BLOCK // NSA_CMP_STRIDE
    front = NSA_CMP_LEN // NSA_CMP_STRIDE - 1
    n_int = ratio + front
    n_idx = np.arange(nr)[None, :]
    j_idx = np.arange(nsb)[:, None]
    mimpt = ((n_idx >= ratio * j_idx - front) & (n_idx <= ratio * j_idx + n_int - 1 - front)
             & (n_idx < nr - 1)).astype(np.float32)
    grid_spec = pltpu.PrefetchScalarGridSpec(
        num_scalar_prefetch=1,
        grid=(b, NSA_KV_HEADS, s // tq),
        in_specs=[pl.BlockSpec((1, NSA_GROUP, dh, tq), lambda i, g, j, sl: (i, TQ_NSA // NSA_GROUP + g, 0, j)),
                  pl.BlockSpec((1, 1, nr, dh), lambda i, g, j, sl: (i, g, 0, 0)),
                  pl.BlockSpec((1, 1, dh, nr), lambda i, g, j, sl: (i, 2 + g, 0, 0)),
                  pl.BlockSpec((nsb, nr), lambda i, g, j, sl: (0, 0))],
        out_specs=[pl.BlockSpec((1, NSA_GROUP, dh, tq), lambda i, g, j, sl: (i, g, 0, j)),
                   pl.BlockSpec((1, 1, nsb, tq), lambda i, g, j, sl: (i, g, 0, j)),
                   pl.BlockSpec((1, 1, 1, 8, nsb), lambda i, g, j, sl: (i, g, j, 0, 0))])
    assert nr % min(NSA_CMP_CHUNK, nr) == 0
    return pl.pallas_call(
        _nsa_cmp_kernel,
        grid_spec=grid_spec,
        out_shape=[jax.ShapeDtypeStruct((b, NSA_HEADS, dh, s), F32),
                   jax.ShapeDtypeStruct((b, NSA_KV_HEADS, nsb, s), F32),
                   jax.ShapeDtypeStruct((b, NSA_KV_HEADS, s // tq, 8, nsb), F32)],
        compiler_params=_cparams(("parallel", "parallel", "parallel")),
        name="nsa_compressed_select",
    )(jnp.asarray(_alibi_slopes2(NSA_HEADS)), qvt, cmp_kv, cmp_kvt, jnp.asarray(mimpt, BF16))


NSA_TK = 256
NSA_SEL_TQ = 256


def _nsa_qaug(qt_ref, srow_ref):
    return jnp.concatenate(
        [jnp.concatenate([qt_ref[0, hh], srow_ref[hh]], axis=0) for hh in range(NSA_GROUP)], axis=1)


def _nsa_sel_kernel(bits_ref, qt_ref, k_ref, vt_ref, selt_ref, srow_ref, slane_ref, o_ref,
                    m_ref, acc_ref, vaug_ref, sa_ref, ma_ref, sb_ref, mb_ref, list_ref, *, nq, words):
    tq, tk = NSA_SEL_TQ, NSA_TK
    per_tile = tk // NSA_SEL_BLOCK
    qi = pl.program_id(2)
    q0 = qi * tq

    @pl.when(qi == 0)
    def _():
        _fill_vaug(vaug_ref, vt_ref)

    base = ((pl.program_id(0) * NSA_KV_HEADS + pl.program_id(1)) * nq + qi) * words
    list_ref[0] = 0

    def note(j, n):
        list_ref[n] = j
        return n + ((bits_ref[base + j // 32] >> (j % 32)) & 1)

    n_tiles = lax.fori_loop(0, qi, note, 0)

    qaug = _nsa_qaug(qt_ref, srow_ref)
    slane = slane_ref[0:1, :]
    diag = q0 // tk

    def scores(kj):
        st = _dot(_ktile(k_ref, kj, tk), qaug)
        rows = [jnp.broadcast_to(selt_ref[0, 0, pl.ds(kj * per_tile + c, 1), :], (NSA_SEL_BLOCK, tq))
                for c in range(per_tile)]
        bias = (jnp.concatenate(rows, axis=0) - 1.0) * (-NEG)
        return st + jnp.concatenate([bias] * NSA_GROUP, axis=1)

    def meta(kj):
        return slane * (kj * tk - q0).astype(F32), _vtile(vaug_ref, kj, tk)

    def own_tile():
        key = lax.broadcasted_iota(jnp.int32, (tk, tq), 0)
        qry = lax.broadcasted_iota(jnp.int32, (tk, tq), 1)
        causal = jnp.concatenate([key <= qry] * NSA_GROUP, axis=1)
        return jnp.where(causal, scores(diag), NEG)

    _init_state_t(m_ref, acc_ref)
    _flash_pipeline([(own_tile, lambda: meta(diag))], n_tiles,
                    lambda i: scores(list_ref[i]), lambda i: meta(list_ref[i]),
                    ((sa_ref, ma_ref), (sb_ref, mb_ref)), m_ref, acc_ref, n_even=False)
    out = _finish_t(acc_ref)
    for hh in range(NSA_GROUP):
        o_ref[0, hh] = out[:, hh * tq:(hh + 1) * tq]


def _nsa_tables(tq):
    slopes2 = _alibi_slopes2(NSA_HEADS)
    srow = _slope_rows(slopes2, tq)
    slane = np.repeat(slopes2.reshape(NSA_KV_HEADS, NSA_GROUP), tq, axis=1)
    slane8 = np.repeat(slane[:, None, :], 8, axis=1)
    return srow, jnp.asarray(slane8, F32)


def _active_tile_bits(cnt, tq, tk):
    b, g, nq128, _, nsb = cnt.shape
    qper, bper = tq // NSA_TQ, tk // NSA_SEL_BLOCK
    nq, nkv = nq128 // qper, nsb // bper
    act = cnt[:, :, :, 0, :].reshape(b, g, nq, qper, nkv, bper).sum(axis=(3, 5)) > 0.0
    words = -(-nkv // 32)
    act = jnp.pad(act, ((0, 0), (0, 0), (0, 0), (0, words * 32 - nkv))).reshape(b, g, nq, words, 32)
    bits = jnp.sum(act.astype(jnp.uint32) << jnp.arange(32, dtype=jnp.uint32), axis=-1, dtype=jnp.uint32)
    return lax.bitcast_convert_type(bits, jnp.int32).reshape(-1), nq, words


def nsa_sel_pallas(qvt, kaug, selt, cnt):
    b, _, dh, s = qvt.shape
    tq = NSA_SEL_TQ
    assert tq == NSA_TK
    lanes = NSA_GROUP * tq
    nsb = s // NSA_SEL_BLOCK
    srow, slane = _nsa_tables(tq)
    bits, nq, words = _active_tile_bits(cnt, tq, NSA_TK)
    grid_spec = pltpu.PrefetchScalarGridSpec(
        num_scalar_prefetch=1,
        grid=(b, NSA_KV_HEADS, nq),
        in_specs=[pl.BlockSpec((1, NSA_GROUP, dh, tq), lambda i, g, j, bt: (i, TQ_NSA // NSA_GROUP + g, 0, j)),
                  pl.BlockSpec((1, 1, s, KAUG), lambda i, g, j, bt: (i, KA_NSA + g, 0, 0)),
                  pl.BlockSpec((1, 1, dh, s), lambda i, g, j, bt: (i, TV_NSA + g, 0, 0)),
                  pl.BlockSpec((1, 1, nsb, tq), lambda i, g, j, bt: (i, g, 0, j)),
                  pl.BlockSpec((NSA_GROUP, dh, tq), lambda i, g, j, bt: (g, 0, 0)),
                  pl.BlockSpec((None, 8, lanes), lambda i, g, j, bt: (g, 0, 0))],
        out_specs=pl.BlockSpec((1, NSA_GROUP, dh, tq), lambda i, g, j, bt: (i, g, 0, j)),
        scratch_shapes=[pltpu.VMEM((1, lanes), F32), pltpu.VMEM((VROWS, lanes), F32),
                        pltpu.VMEM((VROWS, s), BF16)] + _score_buffers(NSA_TK, lanes)
        + [pltpu.SMEM((max(nq, 8),), jnp.int32)])
    return pl.pallas_call(
        functools.partial(_nsa_sel_kernel, nq=nq, words=words),
        grid_spec=grid_spec,
        out_shape=jax.ShapeDtypeStruct((b, NSA_HEADS, dh, s), F32),
        compiler_params=_cparams(("parallel", "parallel", "arbitrary")),
        name="nsa_selected",
    )(bits, qvt, kaug, qvt, selt, srow, slane)


NSA_WT = 256


def _nsa_win_kernel(qt_ref, k_ref, vt_ref, srow_ref, slane_ref, oc_ref, os_ref, gate_ref, o_ref,
                    m_ref, acc_ref, vaug_ref, sa_ref, ma_ref, sb_ref, mb_ref, gt_ref):
    tq = NSA_WT
    wt = NSA_WT
    qi = pl.program_id(2)

    @pl.when(qi == 0)
    def _():
        _fill_vaug(vaug_ref, vt_ref)

    qaug = _nsa_qaug(qt_ref, srow_ref)
    slane = slane_ref[0:1, :]
    key = lax.broadcasted_iota(jnp.int32, (wt, tq), 0)
    qry = lax.broadcasted_iota(jnp.int32, (wt, tq), 1)
    span = NSA_WINDOW // wt

    def tile(d, keep):
        kj = jnp.maximum(qi - d, 0)

        def scores():
            st = _dot(_ktile(k_ref, kj, wt), qaug)
            if keep is not None:
                st = jnp.where(jnp.concatenate([keep] * NSA_GROUP, axis=1), st, NEG)
            return st

        def meta():
            base = (kj * wt) // POS_PERIOD * POS_PERIOD - qi * tq
            delta = jnp.where(qi - d >= 0, slane * base.astype(F32), SKIP)
            return delta, _vtile(vaug_ref, kj, wt)

        return scores, meta

    _init_state_t(m_ref, acc_ref)
    tiles = [tile(0, key <= qry)] + [tile(d, None) for d in range(1, span)] + [tile(span, key > qry)]
    _flash_pipeline(tiles, None, None, None, ((sa_ref, ma_ref), (sb_ref, mb_ref)), m_ref, acc_ref, True)

    o_w = _finish_t(acc_ref)
    gt_ref[...] = gate_ref[0].T
    for hh in range(NSA_GROUP):
        c0 = 3 * (pl.program_id(1) * NSA_GROUP + hh)
        mix = (gt_ref[pl.ds(c0, 1), :] * oc_ref[0, hh] + gt_ref[pl.ds(c0 + 1, 1), :] * os_ref[0, hh]
               + gt_ref[pl.ds(c0 + 2, 1), :] * o_w[:, hh * tq:(hh + 1) * tq])
        o_ref[0, hh] = mix.astype(o_ref.dtype)


def nsa_win_pallas(qvt, kaug, o_c, o_s, gates):
    b, _, dh, s = qvt.shape
    tq = wt = NSA_WT
    assert NSA_WINDOW % wt == 0 and POS_PERIOD % wt == 0
    lanes = NSA_GROUP * tq
    srow, slane = _nsa_tables(tq)
    head_blk = pl.BlockSpec((1, NSA_GROUP, dh, tq), lambda i, g, j: (i, g, 0, j))
    out = pl.pallas_call(
        _nsa_win_kernel,
        grid=(b, NSA_KV_HEADS, s // tq),
        in_specs=[pl.BlockSpec((1, NSA_GROUP, dh, tq), lambda i, g, j: (i, TQ_NSA // NSA_GROUP + g, 0, j)),
                  pl.BlockSpec((1, 1, s, KAUG), lambda i, g, j: (i, KA_NSA + 2 + g, 0, 0)),
                  pl.BlockSpec((1, 1, dh, s), lambda i, g, j: (i, TV_NSA + 2 + g, 0, 0)),
                  pl.BlockSpec((NSA_GROUP, dh, tq), lambda i, g, j: (g, 0, 0)),
                  pl.BlockSpec((None, 8, lanes), lambda i, g, j: (g, 0, 0)),
                  head_blk, head_blk,
                  pl.BlockSpec((1, tq, LANES), lambda i, g, j: (i, j, 0))],
        out_specs=head_blk,
        out_shape=jax.ShapeDtypeStruct((b, NSA_HEADS, dh, s), BF16),
        scratch_shapes=[pltpu.VMEM((1, lanes), F32), pltpu.VMEM((VROWS, lanes), F32),
                        pltpu.VMEM((VROWS, s), BF16)] + _score_buffers(wt, lanes)
        + [pltpu.VMEM((LANES, tq), F32)],
        compiler_params=_cparams(("parallel", "parallel", "arbitrary")),
        name="nsa_window_mix",
    )(qvt, kaug, qvt, srow, slane, o_c, o_s, gates)
    return out.reshape(b, NSA_HEADS * dh, s)


def _mem_attn_kernel(q_ref, k_ref, vt_ref, o_ref):
    for hh in range(MEM_HEADS):
        s = _dot_nt(q_ref[0, hh], k_ref[0, hh])
        m = jnp.max(s, axis=-1, keepdims=True)
        e = jnp.exp2(s - m)
        p = e / jnp.sum(e, axis=-1, keepdims=True)
        o_ref[0, hh] = _dot_nt(vt_ref[0, hh], p.astype(BF16)).astype(o_ref.dtype)


def mem_attn_pallas(nat, mem_k, mem_vt, tq=512):
    b, _, s, dh = nat.shape
    n_mem = mem_k.shape[2]
    out = pl.pallas_call(
        _mem_attn_kernel,
        grid=(b, s // tq),
        in_specs=[pl.BlockSpec((1, MEM_HEADS, tq, dh), lambda i, j: (i, N_MEMQ // MEM_HEADS, j, 0)),
                  pl.BlockSpec((1, MEM_HEADS, n_mem, dh), lambda i, j: (i, 0, 0, 0)),
                  pl.BlockSpec((1, MEM_HEADS, dh, n_mem), lambda i, j: (i, 0, 0, 0))],
        out_specs=pl.BlockSpec((1, MEM_HEADS, dh, tq), lambda i, j: (i, 0, 0, j)),
        out_shape=jax.ShapeDtypeStruct((b, MEM_HEADS, dh, s), BF16),
        compiler_params=_cparams(("parallel", "parallel")),
        name="memory_attention",
    )(nat, mem_k, mem_vt)
    return out.reshape(b, MEM_HEADS * dh, s)


def _in_proj_weights(w_in):
    hd = HEAD_DIM
    sizes = (3 * MOBA_HEADS * hd, NSA_HEADS * hd, 6 * NSA_KV_HEADS * hd, 3 * NSA_HEADS,
             3 * FOX_HEADS * hd, FOX_HEADS, MEM_HEADS * hd)
    offs = np.concatenate([[0], np.cumsum(sizes)])
    moba, nsa_q, nsa_kv, nsa_g, fox, fox_f, mem_q = (w_in[:, offs[i]:offs[i + 1]] for i in range(7))
    mh, fh, g2 = MOBA_HEADS * hd, FOX_HEADS * hd, NSA_KV_HEADS * hd
    moba_q, moba_k, moba_v = moba[:, :mh], moba[:, mh:2 * mh], moba[:, 2 * mh:]
    fox_q, fox_k, fox_v = fox[:, :fh], fox[:, fh:2 * fh], fox[:, 2 * fh:]
    k_cmp, v_cmp, k_slc, v_slc, k_win, v_win = (nsa_kv[:, i * g2:(i + 1) * g2] for i in range(6))
    w_t = jnp.concatenate([moba_q, nsa_q, fox_q, moba_v, v_slc, v_win, fox_v], axis=1).T.astype(BF16)
    t_scale = np.ones((T_SLOTS * hd,), np.float32)
    t_scale[:TV_MOBA * hd] = Q_SCALE
    w_ka = jnp.concatenate([moba_k, k_slc, k_win], axis=1).astype(BF16)
    w_kf = fox_k.astype(BF16)
    w_nat = jnp.concatenate([k_cmp, v_cmp, mem_q], axis=1).astype(BF16)
    n_scale = np.ones((N_SLOTS * hd,), np.float32)
    n_scale[N_MEMQ * hd:] = Q_SCALE
    return w_t, jnp.asarray(t_scale), w_ka, w_kf, w_nat, jnp.asarray(n_scale), nsa_g, fox_f


def _mixer(h32, h16, mem16, w_in, b_forget, w_mem_kv, cmp_pe, cmp_w1, cmp_w2):
    b, s, d = h16.shape
    tm = min(1024, s)
    w_t, t_scale, w_ka, w_kf, w_nat, n_scale, w_gate, w_forget = _in_proj_weights(w_in)
    gates, caug = gates_pallas(h32, w_gate, w_forget, b_forget)
    qvt = proj_t_pallas(h16, w_t, t_scale, tm=tm, heads_per_step=13)
    k_alibi = proj_kaug_pallas(h16, w_ka, None, tm=tm, heads_per_step=KA_SLOTS)
    k_fox = proj_kaug_pallas(h16, w_kf, caug, tm=tm, heads_per_step=FOX_HEADS)
    nat = proj_heads_pallas(h16, w_nat, n_scale, tm=tm, heads_per_step=8)
    n_mem = mem16.shape[1]
    mk = MEM_HEADS * HEAD_DIM
    mem_k = proj_heads_pallas(mem16, w_mem_kv[:, :mk].astype(BF16), jnp.ones((mk,), F32),
                              tm=n_mem, heads_per_step=MEM_HEADS)
    mem_vt = proj_t_pallas(mem16, w_mem_kv[:, mk:].T.astype(BF16), jnp.ones((mk,), F32),
                           tm=n_mem, heads_per_step=MEM_HEADS)
    o_moba = moba_pallas(qvt, k_alibi)
    o_fox = fox_pallas(qvt, k_fox)
    cmp_kv, cmp_kvt = nsa_compress_pallas(nat, cmp_pe, cmp_w1, cmp_w2)
    o_c, selt, cnt = nsa_cmp_pallas(qvt, cmp_kv, cmp_kvt)
    o_s = nsa_sel_pallas(qvt, k_alibi, selt, cnt)
    o_nsa = nsa_win_pallas(qvt, k_alibi, o_c, o_s, gates)
    o_mem = mem_attn_pallas(nat, mem_k, mem_vt)
    return [o_moba, o_nsa, o_fox, o_mem]


def kernel(x, mem, emb_ln_g, emb_ln_b, w_in, b_forget, w_mem_kv, nsa_cmp_pe, nsa_cmp_w1, nsa_cmp_w2,
           w_out, ln1_g, ln1_b, ffn_w_up, ffn_conv_w, ffn_conv_b, ffn_w_down, ln2_g, ln2_b):
    b, s, d = x.shape
    depth = w_in.shape[0]
    dff = ffn_w_down.shape[1]
    mem16 = mem.astype(BF16)
    h32, h16 = layer_norm_pallas(x.reshape(b * s, d), emb_ln_g, emb_ln_b)
    for l in range(depth):
        heads = _mixer(h32.reshape(b, s, d), h16.reshape(b, s, d), mem16, w_in[l], b_forget[l], w_mem_kv[l],
                       nsa_cmp_pe[l], nsa_cmp_w1[l], nsa_cmp_w2[l])
        h32, h16 = out_proj_ln_pallas(heads, w_out[l].astype(BF16), h32.reshape(b, s, d),
                                      ln1_g[l], ln1_b[l], tm=512)
        a = ffn_up_pallas(h16, ffn_w_up[l].astype(BF16), ffn_conv_w[l], ffn_conv_b[l],
                          tm=min(1024, s), tn=512)
        h32, h16 = matmul_ln_resident_pallas(a.reshape(b * s, dff), ffn_w_down[l].astype(BF16),
                                             h32.reshape(b * s, d), ln2_g[l], ln2_b[l], tm=256)
    return h32.reshape(b, s, d)
```

```python
import functools
import math

import jax
import jax.numpy as jnp
import ml_dtypes
import numpy as np
from jax import lax
from jax.experimental import pallas as pl
from jax.experimental.pallas import tpu as pltpu

F32 = jnp.float32
BF16 = jnp.bfloat16

HEAD_DIM = 64
MOBA_HEADS = 8
NSA_HEADS = 8
NSA_KV_HEADS = 2
NSA_GROUP = NSA_HEADS // NSA_KV_HEADS
FOX_HEADS = 12
MEM_HEADS = 4
MOBA_BLOCK = 256
MOBA_TOPK = 3
NSA_CMP_LEN = 32
NSA_CMP_STRIDE = 16
NSA_SEL_BLOCK = 64
NSA_TOPK = 16
NSA_WINDOW = 512
NSA_FORCE_SCORE = 1.0e4
CONV_WIDTH = 3
LN_EPS = 1e-5
DEPTH = 2
DEEPNORM_ALPHA = (2 * DEPTH) ** 0.25

LOG2E = math.log2(math.e)
Q_SCALE = HEAD_DIM ** -0.5 * LOG2E
NEG = -1.0e30
SKIP = -3.0e38
VMEM_LIMIT = 56 * 1024 * 1024
LANES = 128
KAUG = 2 * HEAD_DIM
VROWS = HEAD_DIM + 16
POS_PERIOD = 256

TQ_MOBA, TQ_NSA, TQ_FOX = 0, 8, 16
TV_MOBA, TV_NSA, TV_FOX = 28, 36, 40
T_SLOTS = 52
KA_MOBA, KA_NSA = 0, 8
KA_SLOTS = 12
N_CMP, N_MEMQ = 0, 4
N_SLOTS = 8


def _cparams(sem):
    return pltpu.CompilerParams(dimension_semantics=sem, vmem_limit_bytes=VMEM_LIMIT)


def _split2(x):
    hi = x.astype(BF16)
    return hi, (x - hi.astype(F32)).astype(BF16)


def _split3(x):
    hi = x.astype(BF16)
    r1 = x - hi.astype(F32)
    mid = r1.astype(BF16)
    lo = (r1 - mid.astype(F32)).astype(BF16)
    return hi, mid, lo


def _np_split3(x):
    x = np.asarray(x, np.float32)
    hi = x.astype(ml_dtypes.bfloat16).astype(np.float32)
    r1 = x - hi
    mid = r1.astype(ml_dtypes.bfloat16).astype(np.float32)
    lo = (r1 - mid).astype(ml_dtypes.bfloat16).astype(np.float32)
    return hi, mid, lo


def _dot_nt(a, b):
    return lax.dot_general(a, b, (((1,), (1,)), ((), ())), preferred_element_type=F32)


def _dot_tn(a, b):
    return lax.dot_general(a, b, (((0,), (0,)), ((), ())), preferred_element_type=F32)


def _dot(a, b):
    return jnp.dot(a, b, preferred_element_type=F32)


def _layer_norm_rows(x, g, b):
    mu = jnp.mean(x, axis=-1, keepdims=True)
    xc = x - mu
    var = jnp.mean(xc * xc, axis=-1, keepdims=True)
    return xc * lax.rsqrt(var + LN_EPS) * g + b


def _alibi_slopes2(n):
    return (np.exp2(-8.0 * np.arange(1, n + 1, dtype=np.float64) / n) * LOG2E).astype(np.float32)


def _slope_rows(slopes2, lanes):
    pieces = np.stack(_np_split3(slopes2), axis=1)
    rows = np.zeros((len(slopes2), HEAD_DIM, lanes), np.float32)
    rows[:, :3, :] = pieces[:, :, None]
    return jnp.asarray(rows, BF16)


def _ln_kernel(x_ref, g_ref, b_ref, o32_ref, o16_ref):
    y = _layer_norm_rows(x_ref[...], g_ref[...], b_ref[...])
    o32_ref[...] = y
    o16_ref[...] = y.astype(BF16)


def layer_norm_pallas(x, g, b, tm=512):
    m, d = x.shape
    return pl.pallas_call(
        _ln_kernel,
        grid=(m // tm,),
        in_specs=[pl.BlockSpec((tm, d), lambda i: (i, 0)),
                  pl.BlockSpec((1, d), lambda i: (0, 0)),
                  pl.BlockSpec((1, d), lambda i: (0, 0))],
        out_specs=[pl.BlockSpec((tm, d), lambda i: (i, 0)),
                   pl.BlockSpec((tm, d), lambda i: (i, 0))],
        out_shape=[jax.ShapeDtypeStruct((m, d), F32), jax.ShapeDtypeStruct((m, d), BF16)],
        compiler_params=_cparams(("parallel",)),
        name="layer_norm",
    )(x, g.reshape(1, d), b.reshape(1, d))


GATE_LANES_NSA = 3 * NSA_HEADS


def _gates_kernel(h_ref, w_ref, bf_ref, tri_ref, place_ref, g_ref, caug_ref, carry_ref):
    si = pl.program_id(1)

    @pl.when(si == 0)
    def _():
        carry_ref[...] = jnp.zeros_like(carry_ref)

    h_hi, h_lo = _split2(h_ref[0])
    w_hi, w_lo = _split2(w_ref[...])
    x = _dot(h_hi, w_hi) + _dot(h_lo, w_hi) + _dot(h_hi, w_lo)
    g_ref[0] = 1.0 / (1.0 + jnp.exp(-x))
    x = x + bf_ref[...]
    logf = jnp.minimum(x, 0.0) - jnp.log(1.0 + jnp.exp(-jnp.abs(x)))
    tri = tri_ref[...]
    l_hi, l_mid, l_lo = _split3(logf)
    c = _dot(tri, l_hi) + _dot(tri, l_mid) + _dot(tri, l_lo) + carry_ref[0:1, :]
    carry_ref[...] = jnp.broadcast_to(c[-1:, :], carry_ref.shape)
    n_hi, n_mid, n_lo = _split3(-LOG2E * c)
    caug = _dot(n_hi, place_ref[0]) + _dot(n_mid, place_ref[1]) + _dot(n_lo, place_ref[2])
    caug_ref[0] = caug.astype(BF16)


def gates_pallas(h3, w_gate, w_forget, b_forget, t=512):
    b, s, d = h3.shape
    lo = GATE_LANES_NSA
    w = jnp.zeros((d, LANES), F32).at[:, :lo].set(w_gate).at[:, lo:lo + FOX_HEADS].set(w_forget)
    bf = jnp.zeros((1, LANES), F32).at[0, lo:lo + FOX_HEADS].set(b_forget)
    tri = (np.arange(t)[None, :] <= np.arange(t)[:, None]).astype(np.float32)
    nc = FOX_HEADS * HEAD_DIM
    place = np.zeros((3, LANES, nc), np.float32)
    for piece in range(3):
        for hh in range(FOX_HEADS):
            place[piece, lo + hh, hh * HEAD_DIM + piece] = 1.0
    return pl.pallas_call(
        _gates_kernel,
        grid=(b, s // t),
        in_specs=[pl.BlockSpec((1, t, d), lambda i, j: (i, j, 0)),
                  pl.BlockSpec((d, LANES), lambda i, j: (0, 0)),
                  pl.BlockSpec((1, LANES), lambda i, j: (0, 0)),
                  pl.BlockSpec((t, t), lambda i, j: (0, 0)),
                  pl.BlockSpec((3, LANES, nc), lambda i, j: (0, 0, 0))],
        out_specs=[pl.BlockSpec((1, t, LANES), lambda i, j: (i, j, 0)),
                   pl.BlockSpec((1, t, nc), lambda i, j: (i, j, 0))],
        out_shape=[jax.ShapeDtypeStruct((b, s, LANES), F32),
                   jax.ShapeDtypeStruct((b, s, nc), BF16)],
        scratch_shapes=[pltpu.VMEM((8, LANES), F32)],
        compiler_params=_cparams(("parallel", "arbitrary")),
        name="gates_cumsum",
    )(h3, w, bf, jnp.asarray(tri, BF16), jnp.asarray(place, BF16))


def _proj_heads_kernel(x_ref, w_ref, sc_ref, o_ref, *, heads_per_step):
    acc = _dot(x_ref[0], w_ref[...]) * sc_ref[...]
    for j in range(heads_per_step):
        o_ref[0, j] = acc[:, j * HEAD_DIM:(j + 1) * HEAD_DIM].astype(o_ref.dtype)


def proj_heads_pallas(x3, w, colscale, tm, heads_per_step):
    b, s, d = x3.shape
    n = w.shape[1]
    tn = heads_per_step * HEAD_DIM
    return pl.pallas_call(
        functools.partial(_proj_heads_kernel, heads_per_step=heads_per_step),
        grid=(b, s // tm, n // tn),
        in_specs=[pl.BlockSpec((1, tm, d), lambda i, j, k: (i, j, 0)),
                  pl.BlockSpec((d, tn), lambda i, j, k: (0, k)),
                  pl.BlockSpec((1, tn), lambda i, j, k: (0, k))],
        out_specs=pl.BlockSpec((1, heads_per_step, tm, HEAD_DIM), lambda i, j, k: (i, k, j, 0)),
        out_shape=jax.ShapeDtypeStruct((b, n // HEAD_DIM, s, HEAD_DIM), BF16),
        compiler_params=_cparams(("parallel", "parallel", "arbitrary")),
        name="proj_heads",
    )(x3, w, colscale.reshape(1, n))


def _proj_t_kernel(x_ref, wt_ref, sc_ref, o_ref, *, heads_per_step):
    acc = _dot_nt(wt_ref[...], x_ref[0]) * sc_ref[...]
    o_ref[0] = acc.reshape(heads_per_step, HEAD_DIM, acc.shape[1]).astype(o_ref.dtype)


def proj_t_pallas(x3, wt, rowscale, tm, heads_per_step):
    b, s, d = x3.shape
    n = wt.shape[0]
    tn = heads_per_step * HEAD_DIM
    return pl.pallas_call(
        functools.partial(_proj_t_kernel, heads_per_step=heads_per_step),
        grid=(b, s // tm, n // tn),
        in_specs=[pl.BlockSpec((1, tm, d), lambda i, j, k: (i, j, 0)),
                  pl.BlockSpec((tn, d), lambda i, j, k: (k, 0)),
                  pl.BlockSpec((tn, 1), lambda i, j, k: (k, 0))],
        out_specs=pl.BlockSpec((1, heads_per_step, HEAD_DIM, tm), lambda i, j, k: (i, k, 0, j)),
        out_shape=jax.ShapeDtypeStruct((b, n // HEAD_DIM, HEAD_DIM, s), BF16),
        compiler_params=_cparams(("parallel", "parallel", "arbitrary")),
        name="proj_transposed",
    )(x3, wt, rowscale.reshape(n, 1))


def _proj_kaug_kernel(x_ref, w_ref, *rest, heads_per_step, positional):
    o_ref = rest[-1]
    acc = _dot(x_ref[0], w_ref[...])
    tm = acc.shape[0]
    if positional:
        pos = (pl.program_id(1) * tm + lax.broadcasted_iota(jnp.int32, (tm, HEAD_DIM), 0)) % POS_PERIOD
        lane = lax.broadcasted_iota(jnp.int32, (tm, HEAD_DIM), 1)
        pos_lanes = jnp.where(lane < 3, pos.astype(F32), 0.0).astype(o_ref.dtype)
    for j in range(heads_per_step):
        cols = slice(j * HEAD_DIM, (j + 1) * HEAD_DIM)
        bias = pos_lanes if positional else rest[0][0, :, cols]
        o_ref[0, j] = jnp.concatenate([acc[:, cols].astype(o_ref.dtype), bias], axis=1)


def proj_kaug_pallas(x3, w, aug, tm, heads_per_step):
    b, s, d = x3.shape
    n = w.shape[1]
    tn = heads_per_step * HEAD_DIM
    in_specs = [pl.BlockSpec((1, tm, d), lambda i, j, k: (i, j, 0)),
                pl.BlockSpec((d, tn), lambda i, j, k: (0, k))]
    args = [x3, w]
    if aug is not None:
        in_specs.append(pl.BlockSpec((1, tm, tn), lambda i, j, k: (i, j, k)))
        args.append(aug)
    return pl.pallas_call(
        functools.partial(_proj_kaug_kernel, heads_per_step=heads_per_step, positional=aug is None),
        grid=(b, s // tm, n // tn),
        in_specs=in_specs,
        out_specs=pl.BlockSpec((1, heads_per_step, tm, KAUG), lambda i, j, k: (i, k, j, 0)),
        out_shape=jax.ShapeDtypeStruct((b, n // HEAD_DIM, s, KAUG), BF16),
        compiler_params=_cparams(("parallel", "parallel", "arbitrary")),
        name="proj_keys_aug",
    )(*args)


def _matmul_ln_resident_kernel(x_ref, w_ref, r_ref, g_ref, b_ref, o32_ref, o16_ref):
    y = _layer_norm_rows(DEEPNORM_ALPHA * r_ref[...] + _dot(x_ref[...], w_ref[...]), g_ref[...], b_ref[...])
    o32_ref[...] = y
    o16_ref[...] = y.astype(BF16)


def matmul_ln_resident_pallas(x, w, res, g, b, tm):
    m, kk = x.shape
    d = w.shape[1]
    return pl.pallas_call(
        _matmul_ln_resident_kernel,
        grid=(m // tm,),
        in_specs=[pl.BlockSpec((tm, kk), lambda i: (i, 0)),
                  pl.BlockSpec((kk, d), lambda i: (0, 0), pipeline_mode=pl.Buffered(1)),
                  pl.BlockSpec((tm, d), lambda i: (i, 0)),
                  pl.BlockSpec((1, d), lambda i: (0, 0)),
                  pl.BlockSpec((1, d), lambda i: (0, 0))],
        out_specs=[pl.BlockSpec((tm, d), lambda i: (i, 0)),
                   pl.BlockSpec((tm, d), lambda i: (i, 0))],
        out_shape=[jax.ShapeDtypeStruct((m, d), F32), jax.ShapeDtypeStruct((m, d), BF16)],
        compiler_params=_cparams(("parallel",)),
        name="matmul_ln_resident",
    )(x, w, res, g.reshape(1, d), b.reshape(1, d))


def _out_proj_ln_kernel(*refs, widths):
    n = len(widths)
    x_refs, (w_ref, r_ref, g_ref, b_ref, o32_ref, o16_ref) = refs[:n], refs[n:]
    acc = None
    off = 0
    for x_ref, width in zip(x_refs, widths):
        part = _dot_tn(x_ref[0], w_ref[off:off + width, :])
        acc = part if acc is None else acc + part
        off += width
    y = _layer_norm_rows(DEEPNORM_ALPHA * r_ref[0] + acc, g_ref[...], b_ref[...])
    o32_ref[0] = y
    o16_ref[0] = y.astype(BF16)


def out_proj_ln_pallas(xts, w, res3, g, b, tm):
    bsz, s, d = res3.shape
    widths = tuple(x.shape[1] for x in xts)
    in_specs = [pl.BlockSpec((1, wd, tm), lambda i, j: (i, 0, j)) for wd in widths]
    in_specs += [pl.BlockSpec((w.shape[0], d), lambda i, j: (0, 0)),
                 pl.BlockSpec((1, tm, d), lambda i, j: (i, j, 0)),
                 pl.BlockSpec((1, d), lambda i, j: (0, 0)),
                 pl.BlockSpec((1, d), lambda i, j: (0, 0))]
    return pl.pallas_call(
        functools.partial(_out_proj_ln_kernel, widths=widths),
        grid=(bsz, s // tm),
        in_specs=in_specs,
        out_specs=[pl.BlockSpec((1, tm, d), lambda i, j: (i, j, 0)),
                   pl.BlockSpec((1, tm, d), lambda i, j: (i, j, 0))],
        out_shape=[jax.ShapeDtypeStruct((bsz, s, d), F32), jax.ShapeDtypeStruct((bsz, s, d), BF16)],
        compiler_params=_cparams(("parallel", "parallel")),
        name="out_proj_ln",
    )(*xts, w, res3, g.reshape(1, d), b.reshape(1, d))


HALO = 16
FFN_CHUNK = 256


def _gelu_tanh(x):
    return 0.5 * x * (1.0 + jnp.tanh(math.sqrt(2.0 / math.pi) * (x + 0.044715 * x * x * x)))


def _ffn_up_kernel(x_ref, xh_ref, wu_ref, wg_ref, cw_ref, cb_ref, o_ref):
    j = pl.program_id(1)
    x = x_ref[0]
    xh = xh_ref[0]
    first = jnp.where(j > 0, 1.0, 0.0)
    tn = o_ref.shape[2]
    row = lax.broadcasted_iota(jnp.int32, (x.shape[0], FFN_CHUNK), 0)
    for c in range(tn // FFN_CHUNK):
        cols = slice(c * FFN_CHUNK, (c + 1) * FFN_CHUNK)
        u = _dot(x, wu_ref[:, cols])
        g = _dot(x, wg_ref[:, cols])
        gh = _dot(xh, wg_ref[:, cols]) * first
        prev1 = gh[HALO - 1:HALO, :]
        prev2 = gh[HALO - 2:HALO - 1, :]
        g_m1 = jnp.where(row == 0, prev1, pltpu.roll(g, 1, axis=0))
        g_m2 = jnp.where(row == 0, prev2, jnp.where(row == 1, prev1, pltpu.roll(g, 2, axis=0)))
        cw = cw_ref[:, cols]
        gc = cb_ref[:, cols] + cw[0:1] * g_m2 + cw[1:2] * g_m1 + cw[2:3] * g
        o_ref[0, :, cols] = (_gelu_tanh(gc) * u).astype(o_ref.dtype)


def ffn_up_pallas(x3, w_up, conv_w, conv_b, tm, tn):
    b, s, d = x3.shape
    dff = w_up.shape[1] // 2
    nt = dff // tn
    hb = tm // HALO
    cw = jnp.zeros((8, dff), F32).at[:CONV_WIDTH].set(conv_w)
    return pl.pallas_call(
        _ffn_up_kernel,
        grid=(b, s // tm, nt),
        in_specs=[pl.BlockSpec((1, tm, d), lambda i, j, k: (i, j, 0)),
                  pl.BlockSpec((1, HALO, d), lambda i, j, k: (i, jnp.maximum(j * hb - 1, 0), 0)),
                  pl.BlockSpec((d, tn), lambda i, j, k: (0, k)),
                  pl.BlockSpec((d, tn), lambda i, j, k: (0, k + nt)),
                  pl.BlockSpec((8, tn), lambda i, j, k: (0, k)),
                  pl.BlockSpec((1, tn), lambda i, j, k: (0, k))],
        out_specs=pl.BlockSpec((1, tm, tn), lambda i, j, k: (i, j, k)),
        out_shape=jax.ShapeDtypeStruct((b, s, dff), BF16),
        compiler_params=_cparams(("parallel", "parallel", "arbitrary")),
        name="ffn_up",
    )(x3, x3, w_up, w_up, cw, conv_b.reshape(1, dff))


def _init_state_t(m_ref, acc_ref):
    m_ref[...] = jnp.full(m_ref.shape, NEG, F32)
    acc_ref[...] = jnp.zeros(acc_ref.shape, F32)


def _finish_t(acc_ref):
    acc = acc_ref[...]
    return acc[:HEAD_DIM] / acc[HEAD_DIM:HEAD_DIM + 1]


def _fill_vaug(vaug_ref, vt_ref):
    s = vaug_ref.shape[1]
    vaug_ref[0:HEAD_DIM, :] = vt_ref[0, 0]
    pad = lax.broadcasted_iota(jnp.int32, (VROWS - HEAD_DIM, s), 0)
    vaug_ref[HEAD_DIM:VROWS, :] = jnp.where(pad == 0, 1.0, 0.0).astype(BF16)


def _ktile(ref, idx, size):
    return ref[0, 0, pl.ds(pl.multiple_of(idx * size, size), size), :]


def _vtile(ref, idx, size):
    return ref[:, pl.ds(pl.multiple_of(idx * size, size), size)]


def _stage_scores(st, s_ref, mc_ref):
    nsub = mc_ref.shape[0]
    s_ref[...] = st
    if nsub == 1:
        mc_ref[...] = jnp.max(st, axis=0, keepdims=True)
    else:
        mc_ref[...] = jnp.max(st.reshape(nsub, st.shape[0] // nsub, st.shape[1]), axis=1)


def _stage_update(s_ref, mc_ref, delta, vaug, m_ref, acc_ref):
    nsub = mc_ref.shape[0]
    m_prev = m_ref[...]
    m_new = jnp.maximum(m_prev, jnp.max(mc_ref[...] + delta, axis=0, keepdims=True))
    alpha = jnp.exp2(m_prev - m_new)
    shift = m_new - delta
    if nsub == 1:
        pt = jnp.exp2(s_ref[...] - shift)
    else:
        tk, lanes = s_ref.shape
        pt = jnp.exp2(s_ref[...].reshape(nsub, tk // nsub, lanes) - shift[:, None, :]).reshape(tk, lanes)
    acc_ref[...] = alpha * acc_ref[...] + _dot(vaug, pt.astype(BF16))
    m_ref[...] = m_new


def _score_buffers(tk, lanes, nsub=1):
    return [pltpu.VMEM((tk, lanes), F32), pltpu.VMEM((nsub, lanes), F32),
            pltpu.VMEM((tk, lanes), F32), pltpu.VMEM((nsub, lanes), F32)]


def _flash_pipeline(lead, n_loop, scores, meta, bufs, m_ref, acc_ref, n_even):
    _stage_scores(lead[0][0](), *bufs[0])
    for i in range(1, len(lead)):
        _stage_scores(lead[i][0](), *bufs[i % 2])
        _stage_update(*bufs[(i - 1) % 2], *lead[i - 1][1](), m_ref, acc_ref)
    cur = (len(lead) - 1) % 2
    nxt = 1 - cur
    if scores is None:
        _stage_update(*bufs[cur], *lead[-1][1](), m_ref, acc_ref)
        return
    _stage_scores(scores(0), *bufs[nxt])
    _stage_update(*bufs[cur], *lead[-1][1](), m_ref, acc_ref)

    def pair(k0, stage_next, both):
        if both:
            _stage_scores(scores(k0 + 1), *bufs[cur])
        _stage_update(*bufs[nxt], *meta(k0), m_ref, acc_ref)
        if stage_next:
            _stage_scores(scores(k0 + 2), *bufs[nxt])
        if both:
            _stage_update(*bufs[cur], *meta(k0 + 1), m_ref, acc_ref)

    full = jnp.maximum(n_loop - 1, 0) // 2

    def body(kp, carry):
        pair(4 * kp, True, True)
        pair(4 * kp + 2, True, True)
        return carry

    lax.fori_loop(0, full // 2, body, 0)
    pl.when(full % 2 == 1)(lambda: pair(2 * (full - 1), True, True))
    rest = n_loop - 2 * full
    pl.when(rest == 2)(lambda: pair(2 * full, False, True))
    if not n_even:
        pl.when(rest == 1)(lambda: pair(2 * full, False, False))


FOX_TQ = 1024
FOX_TK = 512


def _fox_kernel(qt_ref, k_ref, vt_ref, o_ref, m_ref, acc_ref, vaug_ref, sa_ref, ma_ref, sb_ref, mb_ref):
    tq, tk = o_ref.shape[3], FOX_TK
    nd = tq // tk
    qi = pl.program_id(2)

    @pl.when(qi == 0)
    def _():
        _fill_vaug(vaug_ref, vt_ref)

    ones3 = jnp.where(lax.broadcasted_iota(jnp.int32, (HEAD_DIM, tq), 0) < 3, 1.0, 0.0).astype(BF16)
    qaug = jnp.concatenate([qt_ref[0, 0], ones3], axis=0)
    _init_state_t(m_ref, acc_ref)
    key = lax.broadcasted_iota(jnp.int32, (tk, tq), 0)
    qry = lax.broadcasted_iota(jnp.int32, (tk, tq), 1)

    def scores(kj):
        return _dot(_ktile(k_ref, kj, tk), qaug)

    def meta(kj):
        return 0.0, _vtile(vaug_ref, kj, tk)

    lead = [(functools.partial(lambda d: jnp.where(key + d * tk <= qry, scores(qi * nd + d), NEG), d),
             functools.partial(lambda d: meta(qi * nd + d), d)) for d in range(nd)]
    _flash_pipeline(lead, qi * nd, scores, meta, ((sa_ref, ma_ref), (sb_ref, mb_ref)), m_ref, acc_ref,
                    n_even=nd % 2 == 0)
    o_ref[0, 0] = _finish_t(acc_ref).astype(o_ref.dtype)


def fox_pallas(qvt, kaug):
    b, _, dh, s = qvt.shape
    tq = min(FOX_TQ, s)
    assert tq % FOX_TK == 0
    out = pl.pallas_call(
        _fox_kernel,
        grid=(b, FOX_HEADS, s // tq),
        in_specs=[pl.BlockSpec((1, 1, dh, tq), lambda i, h, j: (i, TQ_FOX + h, 0, j)),
                  pl.BlockSpec((1, 1, s, KAUG), lambda i, h, j: (i, h, 0, 0)),
                  pl.BlockSpec((1, 1, dh, s), lambda i, h, j: (i, TV_FOX + h, 0, 0))],
        out_specs=pl.BlockSpec((1, 1, dh, tq), lambda i, h, j: (i, h, 0, j)),
        out_shape=jax.ShapeDtypeStruct((b, FOX_HEADS, dh, s), BF16),
        scratch_shapes=[pltpu.VMEM((1, tq), F32), pltpu.VMEM((VROWS, tq), F32),
                        pltpu.VMEM((VROWS, s), BF16)] + _score_buffers(FOX_TK, tq),
        compiler_params=_cparams(("parallel", "parallel", "arbitrary")),
        name="fox_attention",
    )(qvt, kaug, qvt)
    return out.reshape(b, FOX_HEADS * dh, s)


def _moba_kernel(slope_ref, qt_ref, k_ref, vt_ref, ind_ref, srow_ref, o_ref,
                 m_ref, acc_ref, vaug_ref, km_ref, sel_ref, sa_ref, ma_ref, sb_ref, mb_ref):
    t = MOBA_BLOCK
    tq = o_ref.shape[3]
    nd = tq // t
    h = pl.program_id(1)
    qi = pl.program_id(2)
    slope2 = slope_ref[h]

    @pl.when(qi == 0)
    def _():
        _fill_vaug(vaug_ref, vt_ref)
        km_ref[...] = _dot(ind_ref[...], k_ref[0, 0]) * (1.0 / MOBA_BLOCK)

    qt = qt_ref[0, 0]
    km_hi, km_lo = _split2(km_ref[...])
    q0 = jnp.concatenate([qt, jnp.zeros_like(qt)], axis=0)
    gate = _dot(km_hi, q0) + _dot(km_lo, q0)
    blk = lax.broadcasted_iota(jnp.int32, gate.shape, 0)
    lane_blk = lax.broadcasted_iota(jnp.int32, (1, tq), 1) // t
    valid = blk < qi * nd + lane_blk
    work = jnp.where(valid, gate, -jnp.inf)
    big = jnp.int32(2 ** 30)
    for _ in range(MOBA_TOPK):
        mx = jnp.max(work, axis=0, keepdims=True)
        first = jnp.min(jnp.where(work == mx, blk, big), axis=0, keepdims=True)
        work = jnp.where(blk == first, -jnp.inf, work)
    sel_ref[...] = jnp.where(valid & (work == -jnp.inf), 1.0, 0.0)

    qaug = jnp.concatenate([qt, srow_ref[...]], axis=0)
    _init_state_t(m_ref, acc_ref)
    tk = sa_ref.shape[0]
    nsub = tk // t
    ntile = tq // tk
    key = lax.broadcasted_iota(jnp.int32, (tk, tq), 0)
    qry = lax.broadcasted_iota(jnp.int32, (tk, tq), 1)
    sub = lax.broadcasted_iota(jnp.int32, (nsub, tq), 0)

    def tile_constant(kj, d):
        sees = jnp.concatenate([sel_ref[pl.ds(kj * nsub + c, 1), :] for c in range(nsub)], axis=0) > 0.0
        if d is not None:
            sees = sees | (lane_blk == d * nsub + sub)
        offset = ((kj * nsub + sub - qi * nd) * t).astype(F32)
        return jnp.where(sees, slope2 * offset, SKIP)

    def scores(kj):
        return _dot(_ktile(k_ref, kj, tk), qaug)

    def meta(kj):
        return tile_constant(kj, None), _vtile(vaug_ref, kj, tk)

    lead = [(functools.partial(lambda d: jnp.where(key + d * tk <= qry, scores(qi * ntile + d), NEG), d),
             functools.partial(lambda d: (tile_constant(qi * ntile + d, d),
                                          _vtile(vaug_ref, qi * ntile + d, tk)), d))
            for d in range(ntile)]
    _flash_pipeline(lead, qi * ntile, scores, meta, ((sa_ref, ma_ref), (sb_ref, mb_ref)), m_ref, acc_ref,
                    n_even=ntile % 2 == 0)
    o_ref[0, 0] = _finish_t(acc_ref).astype(o_ref.dtype)


MOBA_TQ = 1024
MOBA_TK = 512


def moba_pallas(qvt, kaug):
    b, _, dh, s = qvt.shape
    t = MOBA_BLOCK
    tq = min(MOBA_TQ, s)
    tk = min(MOBA_TK, tq)
    assert POS_PERIOD == t and tq % tk == 0 and tk % t == 0
    nb = s // t
    nbp = max(16, nb)
    ind = np.zeros((nbp, s), np.float32)
    ind[np.arange(s) // t, np.arange(s)] = 1.0
    slopes2 = _alibi_slopes2(MOBA_HEADS)
    grid_spec = pltpu.PrefetchScalarGridSpec(
        num_scalar_prefetch=1,
        grid=(b, MOBA_HEADS, s // tq),
        in_specs=[pl.BlockSpec((1, 1, dh, tq), lambda i, h, j, sl: (i, TQ_MOBA + h, 0, j)),
                  pl.BlockSpec((1, 1, s, KAUG), lambda i, h, j, sl: (i, KA_MOBA + h, 0, 0)),
                  pl.BlockSpec((1, 1, dh, s), lambda i, h, j, sl: (i, TV_MOBA + h, 0, 0)),
                  pl.BlockSpec((nbp, s), lambda i, h, j, sl: (0, 0)),
                  pl.BlockSpec((None, dh, tq), lambda i, h, j, sl: (h, 0, 0))],
        out_specs=pl.BlockSpec((1, 1, dh, tq), lambda i, h, j, sl: (i, h, 0, j)),
        scratch_shapes=[pltpu.VMEM((1, tq), F32), pltpu.VMEM((VROWS, tq), F32),
                        pltpu.VMEM((VROWS, s), BF16), pltpu.VMEM((nbp, KAUG), F32),
                        pltpu.VMEM((nbp, tq), F32)] + _score_buffers(tk, tq, tk // t))
    out = pl.pallas_call(
        _moba_kernel,
        grid_spec=grid_spec,
        out_shape=jax.ShapeDtypeStruct((b, MOBA_HEADS, dh, s), BF16),
        compiler_params=_cparams(("parallel", "parallel", "arbitrary")),
        name="moba_attention",
    )(jnp.asarray(slopes2), qvt, kaug, qvt, jnp.asarray(ind, BF16), _slope_rows(slopes2, tq))
    return out.reshape(b, MOBA_HEADS * dh, s)


def _nsa_compress_kernel(x_ref, w1a_ref, w1b_ref, pe_ref, w1_ref, w2_ref, w2t_ref, o_ref, ot_ref):
    nr = x_ref.shape[2]
    x = x_ref[0, 0]

    def near_f32(xb, w):
        w_hi, w_lo = _split2(w)
        return _dot(xb, w_hi) + _dot(xb, w_lo)

    a = near_f32(x, w1a_ref[0])
    bm = near_f32(x, w1b_ref[0])
    pe_hi, pe_mid, pe_lo = _split3(pe_ref[0])
    w1_hi, w1_lo = _split2(w1_ref[0])
    pe_term = (_dot(pe_hi, w1_hi) + _dot(pe_mid, w1_hi) + _dot(pe_lo, w1_hi)
               + _dot(pe_hi, w1_lo) + _dot(pe_mid, w1_lo))[0:1]
    pre = a + pltpu.roll(bm, nr - 1, axis=0) + pe_term
    hid = _gelu_tanh(pre)
    h_hi, h_mid, h_lo = _split3(hid)
    w2_hi, w2_lo = _split2(w2_ref[0])
    o_ref[0, 0] = (_dot(h_hi, w2_hi) + _dot(h_mid, w2_hi) + _dot(h_lo, w2_hi)
                   + _dot(h_hi, w2_lo) + _dot(h_mid, w2_lo))
    t_hi, t_lo = _split2(w2t_ref[0])
    ot_ref[0, 0] = (_dot_nt(t_hi, h_hi) + _dot_nt(t_hi, h_mid) + _dot_nt(t_hi, h_lo)
                    + _dot_nt(t_lo, h_hi) + _dot_nt(t_lo, h_mid))


def nsa_compress_pallas(nat, pe, w1, w2):
    b, _, s, dh = nat.shape
    nr = s // NSA_CMP_STRIDE
    half = NSA_CMP_STRIDE * dh
    hid = w1.shape[-1]
    x = nat[:, N_CMP:N_CMP + 4].reshape(b, 4, nr, half)
    w1f = w1.reshape(2, NSA_CMP_LEN * dh, hid)
    pef = jnp.zeros((2, 8, NSA_CMP_LEN * dh), F32).at[:, 0].set(pe.reshape(2, NSA_CMP_LEN * dh))
    return pl.pallas_call(
        _nsa_compress_kernel,
        grid=(b, 4),
        in_specs=[pl.BlockSpec((1, 1, nr, half), lambda i, j: (i, j, 0, 0)),
                  pl.BlockSpec((1, half, hid), lambda i, j: (j // 2, 0, 0)),
                  pl.BlockSpec((1, half, hid), lambda i, j: (j // 2, 1, 0)),
                  pl.BlockSpec((1, 8, 2 * half), lambda i, j: (j // 2, 0, 0)),
                  pl.BlockSpec((1, 2 * half, hid), lambda i, j: (j // 2, 0, 0)),
                  pl.BlockSpec((1, hid, dh), lambda i, j: (j // 2, 0, 0)),
                  pl.BlockSpec((1, dh, hid), lambda i, j: (j // 2, 0, 0))],
        out_specs=[pl.BlockSpec((1, 1, nr, dh), lambda i, j: (i, j, 0, 0)),
                   pl.BlockSpec((1, 1, dh, nr), lambda i, j: (i, j, 0, 0))],
        out_shape=[jax.ShapeDtypeStruct((b, 4, nr, dh), F32),
                   jax.ShapeDtypeStruct((b, 4, dh, nr), F32)],
        compiler_params=_cparams(("parallel", "parallel")),
        name="nsa_compress",
    )(x, w1f, w1f, pef, w1f, w2, jnp.swapaxes(w2, 1, 2))


NSA_TQ = 128


NSA_CMP_CHUNK = 128


def _nsa_cmp_kernel(slope_ref, qt_ref, kc_ref, vct_ref, mimpt_ref, oct_ref, selt_ref, cnt_ref):
    tq = NSA_TQ
    g = pl.program_id(1)
    qi = pl.program_id(2)
    q0 = qi * tq
    nr = kc_ref.shape[2]
    nsb = selt_ref.shape[2]
    t_lane = q0 + lax.broadcasted_iota(jnp.int32, (1, tq), 1)
    chunk = min(NSA_CMP_CHUNK, nr)
    tiles_per_chunk = chunk * NSA_CMP_STRIDE // tq

    def branch(n):
        kc_hi, kc_lo = _split2(kc_ref[0, 0, 0:n, :])
        vct = vct_ref[0, 0, :, 0:n].astype(BF16)
        cmp_end = NSA_CMP_STRIDE * lax.broadcasted_iota(jnp.int32, (n, tq), 0) + (NSA_CMP_LEN - 1)
        mask = cmp_end <= t_lane
        rel = (cmp_end - q0).astype(F32)
        psum = jnp.zeros((n, tq), F32)
        qt4 = jnp.concatenate([qt_ref[0, hh] for hh in range(NSA_GROUP)], axis=1)
        st4 = _dot(kc_hi, qt4) + _dot(kc_lo, qt4)
        probs = []
        for hh in range(NSA_GROUP):
            st = st4[:, hh * tq:(hh + 1) * tq] + slope_ref[g * NSA_GROUP + hh] * rel
            st = jnp.where(mask, st, -jnp.inf)
            m = jnp.max(st, axis=0, keepdims=True)
            m = jnp.where(m > -jnp.inf, m, 0.0)
            e = jnp.exp2(st - m)
            p = e * (1.0 / jnp.maximum(jnp.sum(e, axis=0, keepdims=True), 1e-30))
            probs.append(p.astype(BF16))
            psum = psum + p
        o4 = _dot(vct, jnp.concatenate(probs, axis=1))
        for hh in range(NSA_GROUP):
            oct_ref[0, hh] = o4[:, hh * tq:(hh + 1) * tq]
        nbk = min(nsb, n * NSA_CMP_STRIDE // NSA_SEL_BLOCK)
        p_hi, p_mid, p_lo = _split3(psum)
        mimpt = mimpt_ref[0:nbk, 0:n]
        imp = _dot(mimpt, p_hi) + _dot(mimpt, p_mid) + _dot(mimpt, p_lo)
        blk = lax.broadcasted_iota(jnp.int32, (nbk, tq), 0)
        jt = t_lane // NSA_SEL_BLOCK
        forced = (blk == 0) | (blk == jt) | (blk == jt - 1)
        imp = jnp.where(forced, NSA_FORCE_SCORE, imp)
        valid = blk * NSA_SEL_BLOCK <= t_lane
        work0 = jnp.where(valid, imp, -jnp.inf)
        big = jnp.int32(2 ** 30)

        def pick(_, work):
            mx = jnp.max(work, axis=0, keepdims=True)
            first = jnp.min(jnp.where(work == mx, blk, big), axis=0, keepdims=True)
            return jnp.where(blk == first, -jnp.inf, work)

        work = lax.fori_loop(0, min(NSA_TOPK, nsb), pick, work0)
        sel = jnp.where(valid & (work == -jnp.inf), 1.0, 0.0)
        selt_ref[0, 0, 0:nbk, :] = sel
        cnt_ref[0, 0, 0, :, 0:nbk] = _dot_nt(jnp.ones((8, tq), BF16), sel.astype(BF16))
        if nbk < nsb:
            selt_ref[0, 0, nbk:nsb, :] = jnp.zeros((nsb - nbk, tq), F32)
            cnt_ref[0, 0, 0, :, nbk:nsb] = jnp.zeros((8, nsb - nbk), F32)

    n_chunks = nr // chunk
    for c in range(n_chunks):
        pl.when(jnp.minimum(qi // tiles_per_chunk, n_chunks - 1) == c)(
            functools.partial(branch, (c + 1) * chunk))


def nsa_cmp_pallas(qvt, cmp_kv, cmp_kvt):
    b, _, dh, s = qvt.shape
    tq = NSA_TQ
    nr = cmp_kv.shape[2]
    nsb = s // NSA_SEL_BLOCK
    ratio = NSA_SEL_BLOCK // NSA_CMP_STRIDE
    front = NSA_CMP_LEN // NSA_CMP_STRIDE - 1
    n_int = ratio + front
    n_idx = np.arange(nr)[None, :]
    j_idx = np.arange(nsb)[:, None]
    mimpt = ((n_idx >= ratio * j_idx - front) & (n_idx <= ratio * j_idx + n_int - 1 - front)
             & (n_idx < nr - 1)).astype(np.float32)
    grid_spec = pltpu.PrefetchScalarGridSpec(
        num_scalar_prefetch=1,
        grid=(b, NSA_KV_HEADS, s // tq),
        in_specs=[pl.BlockSpec((1, NSA_GROUP, dh, tq), lambda i, g, j, sl: (i, TQ_NSA // NSA_GROUP + g, 0, j)),
                  pl.BlockSpec((1, 1, nr, dh), lambda i, g, j, sl: (i, g, 0, 0)),
                  pl.BlockSpec((1, 1, dh, nr), lambda i, g, j, sl: (i, 2 + g, 0, 0)),
                  pl.BlockSpec((nsb, nr), lambda i, g, j, sl: (0, 0))],
        out_specs=[pl.BlockSpec((1, NSA_GROUP, dh, tq), lambda i, g, j, sl: (i, g, 0, j)),
                   pl.BlockSpec((1, 1, nsb, tq), lambda i, g, j, sl: (i, g, 0, j)),
                   pl.BlockSpec((1, 1, 1, 8, nsb), lambda i, g, j, sl: (i, g, j, 0, 0))])
    assert nr % min(NSA_CMP_CHUNK, nr) == 0
    return pl.pallas_call(
        _nsa_cmp_kernel,
        grid_spec=grid_spec,
        out_shape=[jax.ShapeDtypeStruct((b, NSA_HEADS, dh, s), F32),
                   jax.ShapeDtypeStruct((b, NSA_KV_HEADS, nsb, s), F32),
                   jax.ShapeDtypeStruct((b, NSA_KV_HEADS, s // tq, 8, nsb), F32)],
        compiler_params=_cparams(("parallel", "parallel", "parallel")),
        name="nsa_compressed_select",
    )(jnp.asarray(_alibi_slopes2(NSA_HEADS)), qvt, cmp_kv, cmp_kvt, jnp.asarray(mimpt, BF16))


NSA_TK = 256
NSA_SEL_TQ = 256


def _nsa_qaug(qt_ref, srow_ref):
    return jnp.concatenate(
        [jnp.concatenate([qt_ref[0, hh], srow_ref[hh]], axis=0) for hh in range(NSA_GROUP)], axis=1)


def _nsa_sel_kernel(bits_ref, qt_ref, k_ref, vt_ref, selt_ref, srow_ref, slane_ref, o_ref,
                    m_ref, acc_ref, vaug_ref, sa_ref, ma_ref, sb_ref, mb_ref, list_ref, *, nq, words):
    tq, tk = NSA_SEL_TQ, NSA_TK
    per_tile = tk // NSA_SEL_BLOCK
    qi = pl.program_id(2)
    q0 = qi * tq

    @pl.when(qi == 0)
    def _():
        _fill_vaug(vaug_ref, vt_ref)

    base = ((pl.program_id(0) * NSA_KV_HEADS + pl.program_id(1)) * nq + qi) * words
    list_ref[0] = 0

    def note(j, n):
        list_ref[n] = j
        return n + ((bits_ref[base + j // 32] >> (j % 32)) & 1)

    n_tiles = lax.fori_loop(0, qi, note, 0)

    qaug = _nsa_qaug(qt_ref, srow_ref)
    slane = slane_ref[0:1, :]
    diag = q0 // tk

    def scores(kj):
        st = _dot(_ktile(k_ref, kj, tk), qaug)
        rows = [jnp.broadcast_to(selt_ref[0, 0, pl.ds(kj * per_tile + c, 1), :], (NSA_SEL_BLOCK, tq))
                for c in range(per_tile)]
        bias = (jnp.concatenate(rows, axis=0) - 1.0) * (-NEG)
        return st + jnp.concatenate([bias] * NSA_GROUP, axis=1)

    def meta(kj):
        return slane * (kj * tk - q0).astype(F32), _vtile(vaug_ref, kj, tk)

    def own_tile():
        key = lax.broadcasted_iota(jnp.int32, (tk, tq), 0)
        qry = lax.broadcasted_iota(jnp.int32, (tk, tq), 1)
        causal = jnp.concatenate([key <= qry] * NSA_GROUP, axis=1)
        return jnp.where(causal, scores(diag), NEG)

    _init_state_t(m_ref, acc_ref)
    _flash_pipeline([(own_tile, lambda: meta(diag))], n_tiles,
                    lambda i: scores(list_ref[i]), lambda i: meta(list_ref[i]),
                    ((sa_ref, ma_ref), (sb_ref, mb_ref)), m_ref, acc_ref, n_even=False)
    out = _finish_t(acc_ref)
    for hh in range(NSA_GROUP):
        o_ref[0, hh] = out[:, hh * tq:(hh + 1) * tq]


def _nsa_tables(tq):
    slopes2 = _alibi_slopes2(NSA_HEADS)
    srow = _slope_rows(slopes2, tq)
    slane = np.repeat(slopes2.reshape(NSA_KV_HEADS, NSA_GROUP), tq, axis=1)
    slane8 = np.repeat(slane[:, None, :], 8, axis=1)
    return srow, jnp.asarray(slane8, F32)


def _active_tile_bits(cnt, tq, tk):
    b, g, nq128, _, nsb = cnt.shape
    qper, bper = tq // NSA_TQ, tk // NSA_SEL_BLOCK
    nq, nkv = nq128 // qper, nsb // bper
    act = cnt[:, :, :, 0, :].reshape(b, g, nq, qper, nkv, bper).sum(axis=(3, 5)) > 0.0
    words = -(-nkv // 32)
    act = jnp.pad(act, ((0, 0), (0, 0), (0, 0), (0, words * 32 - nkv))).reshape(b, g, nq, words, 32)
    bits = jnp.sum(act.astype(jnp.uint32) << jnp.arange(32, dtype=jnp.uint32), axis=-1, dtype=jnp.uint32)
    return lax.bitcast_convert_type(bits, jnp.int32).reshape(-1), nq, words


def nsa_sel_pallas(qvt, kaug, selt, cnt):
    b, _, dh, s = qvt.shape
    tq = NSA_SEL_TQ
    assert tq == NSA_TK
    lanes = NSA_GROUP * tq
    nsb = s // NSA_SEL_BLOCK
    srow, slane = _nsa_tables(tq)
    bits, nq, words = _active_tile_bits(cnt, tq, NSA_TK)
    grid_spec = pltpu.PrefetchScalarGridSpec(
        num_scalar_prefetch=1,
        grid=(b, NSA_KV_HEADS, nq),
        in_specs=[pl.BlockSpec((1, NSA_GROUP, dh, tq), lambda i, g, j, bt: (i, TQ_NSA // NSA_GROUP + g, 0, j)),
                  pl.BlockSpec((1, 1, s, KAUG), lambda i, g, j, bt: (i, KA_NSA + g, 0, 0)),
                  pl.BlockSpec((1, 1, dh, s), lambda i, g, j, bt: (i, TV_NSA + g, 0, 0)),
                  pl.BlockSpec((1, 1, nsb, tq), lambda i, g, j, bt: (i, g, 0, j)),
                  pl.BlockSpec((NSA_GROUP, dh, tq), lambda i, g, j, bt: (g, 0, 0)),
                  pl.BlockSpec((None, 8, lanes), lambda i, g, j, bt: (g, 0, 0))],
        out_specs=pl.BlockSpec((1, NSA_GROUP, dh, tq), lambda i, g, j, bt: (i, g, 0, j)),
        scratch_shapes=[pltpu.VMEM((1, lanes), F32), pltpu.VMEM((VROWS, lanes), F32),
                        pltpu.VMEM((VROWS, s), BF16)] + _score_buffers(NSA_TK, lanes)
        + [pltpu.SMEM((max(nq, 8),), jnp.int32)])
    return pl.pallas_call(
        functools.partial(_nsa_sel_kernel, nq=nq, words=words),
        grid_spec=grid_spec,
        out_shape=jax.ShapeDtypeStruct((b, NSA_HEADS, dh, s), F32),
        compiler_params=_cparams(("parallel", "parallel", "arbitrary")),
        name="nsa_selected",
    )(bits, qvt, kaug, qvt, selt, srow, slane)


NSA_WT = 256


def _nsa_win_kernel(qt_ref, k_ref, vt_ref, srow_ref, slane_ref, oc_ref, os_ref, gate_ref, o_ref,
                    m_ref, acc_ref, vaug_ref, sa_ref, ma_ref, sb_ref, mb_ref, gt_ref):
    tq = NSA_WT
    wt = NSA_WT
    qi = pl.program_id(2)

    @pl.when(qi == 0)
    def _():
        _fill_vaug(vaug_ref, vt_ref)

    qaug = _nsa_qaug(qt_ref, srow_ref)
    slane = slane_ref[0:1, :]
    key = lax.broadcasted_iota(jnp.int32, (wt, tq), 0)
    qry = lax.broadcasted_iota(jnp.int32, (wt, tq), 1)
    span = NSA_WINDOW // wt

    def tile(d, keep):
        kj = jnp.maximum(qi - d, 0)

        def scores():
            st = _dot(_ktile(k_ref, kj, wt), qaug)
            if keep is not None:
                st = jnp.where(jnp.concatenate([keep] * NSA_GROUP, axis=1), st, NEG)
            return st

        def meta():
            base = (kj * wt) // POS_PERIOD * POS_PERIOD - qi * tq
            delta = jnp.where(qi - d >= 0, slane * base.astype(F32), SKIP)
            return delta, _vtile(vaug_ref, kj, wt)

        return scores, meta

    _init_state_t(m_ref, acc_ref)
    tiles = [tile(0, key <= qry)] + [tile(d, None) for d in range(1, span)] + [tile(span, key > qry)]
    _flash_pipeline(tiles, None, None, None, ((sa_ref, ma_ref), (sb_ref, mb_ref)), m_ref, acc_ref, True)

    o_w = _finish_t(acc_ref)
    gt_ref[...] = gate_ref[0].T
    for hh in range(NSA_GROUP):
        c0 = 3 * (pl.program_id(1) * NSA_GROUP + hh)
        mix = (gt_ref[pl.ds(c0, 1), :] * oc_ref[0, hh] + gt_ref[pl.ds(c0 + 1, 1), :] * os_ref[0, hh]
               + gt_ref[pl.ds(c0 + 2, 1), :] * o_w[:, hh * tq:(hh + 1) * tq])
        o_ref[0, hh] = mix.astype(o_ref.dtype)


def nsa_win_pallas(qvt, kaug, o_c, o_s, gates):
    b, _, dh, s = qvt.shape
    tq = wt = NSA_WT
    assert NSA_WINDOW % wt == 0 and POS_PERIOD % wt == 0
    lanes = NSA_GROUP * tq
    srow, slane = _nsa_tables(tq)
    head_blk = pl.BlockSpec((1, NSA_GROUP, dh, tq), lambda i, g, j: (i, g, 0, j))
    out = pl.pallas_call(
        _nsa_win_kernel,
        grid=(b, NSA_KV_HEADS, s // tq),
        in_specs=[pl.BlockSpec((1, NSA_GROUP, dh, tq), lambda i, g, j: (i, TQ_NSA // NSA_GROUP + g, 0, j)),
                  pl.BlockSpec((1, 1, s, KAUG), lambda i, g, j: (i, KA_NSA + 2 + g, 0, 0)),
                  pl.BlockSpec((1, 1, dh, s), lambda i, g, j: (i, TV_NSA + 2 + g, 0, 0)),
                  pl.BlockSpec((NSA_GROUP, dh, tq), lambda i, g, j: (g, 0, 0)),
                  pl.BlockSpec((None, 8, lanes), lambda i, g, j: (g, 0, 0)),
                  head_blk, head_blk,
                  pl.BlockSpec((1, tq, LANES), lambda i, g, j: (i, j, 0))],
        out_specs=head_blk,
        out_shape=jax.ShapeDtypeStruct((b, NSA_HEADS, dh, s), BF16),
        scratch_shapes=[pltpu.VMEM((1, lanes), F32), pltpu.VMEM((VROWS, lanes), F32),
                        pltpu.VMEM((VROWS, s), BF16)] + _score_buffers(wt, lanes)
        + [pltpu.VMEM((LANES, tq), F32)],
        compiler_params=_cparams(("parallel", "parallel", "arbitrary")),
        name="nsa_window_mix",
    )(qvt, kaug, qvt, srow, slane, o_c, o_s, gates)
    return out.reshape(b, NSA_HEADS * dh, s)


def _mem_attn_kernel(q_ref, k_ref, vt_ref, o_ref):
    for hh in range(MEM_HEADS):
        s = _dot_nt(q_ref[0, hh], k_ref[0, hh])
        m = jnp.max(s, axis=-1, keepdims=True)
        e = jnp.exp2(s - m)
        p = e / jnp.sum(e, axis=-1, keepdims=True)
        o_ref[0, hh] = _dot_nt(vt_ref[0, hh], p.astype(BF16)).astype(o_ref.dtype)


def mem_attn_pallas(nat, mem_k, mem_vt, tq=512):
    b, _, s, dh = nat.shape
    n_mem = mem_k.shape[2]
    out = pl.pallas_call(
        _mem_attn_kernel,
        grid=(b, s // tq),
        in_specs=[pl.BlockSpec((1, MEM_HEADS, tq, dh), lambda i, j: (i, N_MEMQ // MEM_HEADS, j, 0)),
                  pl.BlockSpec((1, MEM_HEADS, n_mem, dh), lambda i, j: (i, 0, 0, 0)),
                  pl.BlockSpec((1, MEM_HEADS, dh, n_mem), lambda i, j: (i, 0, 0, 0))],
        out_specs=pl.BlockSpec((1, MEM_HEADS, dh, tq), lambda i, j: (i, 0, 0, j)),
        out_shape=jax.ShapeDtypeStruct((b, MEM_HEADS, dh, s), BF16),
        compiler_params=_cparams(("parallel", "parallel")),
        name="memory_attention",
    )(nat, mem_k, mem_vt)
    return out.reshape(b, MEM_HEADS * dh, s)


def _in_proj_weights(w_in):
    hd = HEAD_DIM
    sizes = (3 * MOBA_HEADS * hd, NSA_HEADS * hd, 6 * NSA_KV_HEADS * hd, 3 * NSA_HEADS,
             3 * FOX_HEADS * hd, FOX_HEADS, MEM_HEADS * hd)
    offs = np.concatenate([[0], np.cumsum(sizes)])
    moba, nsa_q, nsa_kv, nsa_g, fox, fox_f, mem_q = (w_in[:, offs[i]:offs[i + 1]] for i in range(7))
    mh, fh, g2 = MOBA_HEADS * hd, FOX_HEADS * hd, NSA_KV_HEADS * hd
    moba_q, moba_k, moba_v = moba[:, :mh], moba[:, mh:2 * mh], moba[:, 2 * mh:]
    fox_q, fox_k, fox_v = fox[:, :fh], fox[:, fh:2 * fh], fox[:, 2 * fh:]
    k_cmp, v_cmp, k_slc, v_slc, k_win, v_win = (nsa_kv[:, i * g2:(i + 1) * g2] for i in range(6))
    w_t = jnp.concatenate([moba_q, nsa_q, fox_q, moba_v, v_slc, v_win, fox_v], axis=1).T.astype(BF16)
    t_scale = np.ones((T_SLOTS * hd,), np.float32)
    t_scale[:TV_MOBA * hd] = Q_SCALE
    w_ka = jnp.concatenate([moba_k, k_slc, k_win], axis=1).astype(BF16)
    w_kf = fox_k.astype(BF16)
    w_nat = jnp.concatenate([k_cmp, v_cmp, mem_q], axis=1).astype(BF16)
    n_scale = np.ones((N_SLOTS * hd,), np.float32)
    n_scale[N_MEMQ * hd:] = Q_SCALE
    return w_t, jnp.asarray(t_scale), w_ka, w_kf, w_nat, jnp.asarray(n_scale), nsa_g, fox_f


def _mixer(h32, h16, mem16, w_in, b_forget, w_mem_kv, cmp_pe, cmp_w1, cmp_w2):
    b, s, d = h16.shape
    tm = min(1024, s)
    w_t, t_scale, w_ka, w_kf, w_nat, n_scale, w_gate, w_forget = _in_proj_weights(w_in)
    gates, caug = gates_pallas(h32, w_gate, w_forget, b_forget)
    qvt = proj_t_pallas(h16, w_t, t_scale, tm=tm, heads_per_step=13)
    k_alibi = proj_kaug_pallas(h16, w_ka, None, tm=tm, heads_per_step=KA_SLOTS)
    k_fox = proj_kaug_pallas(h16, w_kf, caug, tm=tm, heads_per_step=FOX_HEADS)
    nat = proj_heads_pallas(h16, w_nat, n_scale, tm=tm, heads_per_step=8)
    n_mem = mem16.shape[1]
    mk = MEM_HEADS * HEAD_DIM
    mem_k = proj_heads_pallas(mem16, w_mem_kv[:, :mk].astype(BF16), jnp.ones((mk,), F32),
                              tm=n_mem, heads_per_step=MEM_HEADS)
    mem_vt = proj_t_pallas(mem16, w_mem_kv[:, mk:].T.astype(BF16), jnp.ones((mk,), F32),
                           tm=n_mem, heads_per_step=MEM_HEADS)
    o_moba = moba_pallas(qvt, k_alibi)
    o_fox = fox_pallas(qvt, k_fox)
    cmp_kv, cmp_kvt = nsa_compress_pallas(nat, cmp_pe, cmp_w1, cmp_w2)
    o_c, selt, cnt = nsa_cmp_pallas(qvt, cmp_kv, cmp_kvt)
    o_s = nsa_sel_pallas(qvt, k_alibi, selt, cnt)
    o_nsa = nsa_win_pallas(qvt, k_alibi, o_c, o_s, gates)
    o_mem = mem_attn_pallas(nat, mem_k, mem_vt)
    return [o_moba, o_nsa, o_fox, o_mem]


def kernel(x, mem, emb_ln_g, emb_ln_b, w_in, b_forget, w_mem_kv, nsa_cmp_pe, nsa_cmp_w1, nsa_cmp_w2,
           w_out, ln1_g, ln1_b, ffn_w_up, ffn_conv_w, ffn_conv_b, ffn_w_down, ln2_g, ln2_b):
    b, s, d = x.shape
    depth = w_in.shape[0]
    dff = ffn_w_down.shape[1]
    mem16 = mem.astype(BF16)
    h32, h16 = layer_norm_pallas(x.reshape(b * s, d), emb_ln_g, emb_ln_b)
    for l in range(depth):
        heads = _mixer(h32.reshape(b, s, d), h16.reshape(b, s, d), mem16, w_in[l], b_forget[l], w_mem_kv[l],
                       nsa_cmp_pe[l], nsa_cmp_w1[l], nsa_cmp_w2[l])
        h32, h16 = out_proj_ln_pallas(heads, w_out[l].astype(BF16), h32.reshape(b, s, d),
                                      ln1_g[l], ln1_b[l], tm=512)
        a = ffn_up_pallas(h16, ffn_w_up[l].astype(BF16), ffn_conv_w[l], ffn_conv_b[l],
                          tm=min(1024, s), tn=512)
        h32, h16 = matmul_ln_resident_pallas(a.reshape(b * s, dff), ffn_w_down[l].astype(BF16),
                                             h32.reshape(b * s, d), ln2_g[l], ln2_b[l], tm=256)
    return h32.reshape(b, s, d)
```

```python
import functools
import math

import jax
import jax.numpy as jnp
import ml_dtypes
import numpy as np
from jax import lax
from jax.experimental import pallas as pl
from jax.experimental.pallas import tpu as pltpu

F32 = jnp.float32
BF16 = jnp.bfloat16

HEAD_DIM = 64
MOBA_HEADS = 8
NSA_HEADS = 8
NSA_KV_HEADS = 2
NSA_GROUP = NSA_HEADS // NSA_KV_HEADS
FOX_HEADS = 12
MEM_HEADS = 4
MOBA_BLOCK = 256
MOBA_TOPK = 3
NSA_CMP_LEN = 32
NSA_CMP_STRIDE = 16
NSA_SEL_BLOCK = 64
NSA_TOPK = 16
NSA_WINDOW = 512
NSA_FORCE_SCORE = 1.0e4
CONV_WIDTH = 3
LN_EPS = 1e-5
DEPTH = 2
DEEPNORM_ALPHA = (2 * DEPTH) ** 0.25

LOG2E = math.log2(math.e)
Q_SCALE = HEAD_DIM ** -0.5 * LOG2E
NEG = -1.0e30
SKIP = -3.0e38
VMEM_LIMIT = 56 * 1024 * 1024
LANES = 128
KAUG = 2 * HEAD_DIM
VROWS = HEAD_DIM + 16
POS_PERIOD = 256

TQ_MOBA, TQ_NSA, TQ_FOX = 0, 8, 16
TV_MOBA, TV_NSA, TV_FOX = 28, 36, 40
T_SLOTS = 52
KA_MOBA, KA_NSA = 0, 8
KA_SLOTS = 12
N_CMP, N_MEMQ = 0, 4
N_SLOTS = 8


def _cparams(sem):
    return pltpu.CompilerParams(dimension_semantics=sem, vmem_limit_bytes=VMEM_LIMIT)


def _split2(x):
    hi = x.astype(BF16)
    return hi, (x - hi.astype(F32)).astype(BF16)


def _split3(x):
    hi = x.astype(BF16)
    r1 = x - hi.astype(F32)
    mid = r1.astype(BF16)
    lo = (r1 - mid.astype(F32)).astype(BF16)
    return hi, mid, lo


def _np_split3(x):
    x = np.asarray(x, np.float32)
    hi = x.astype(ml_dtypes.bfloat16).astype(np.float32)
    r1 = x - hi
    mid = r1.astype(ml_dtypes.bfloat16).astype(np.float32)
    lo = (r1 - mid).astype(ml_dtypes.bfloat16).astype(np.float32)
    return hi, mid, lo


def _dot_nt(a, b):
    return lax.dot_general(a, b, (((1,), (1,)), ((), ())), preferred_element_type=F32)


def _dot_tn(a, b):
    return lax.dot_general(a, b, (((0,), (0,)), ((), ())), preferred_element_type=F32)


def _dot(a, b):
    return jnp.dot(a, b, preferred_element_type=F32)


def _layer_norm_rows(x, g, b):
    mu = jnp.mean(x, axis=-1, keepdims=True)
    xc = x - mu
    var = jnp.mean(xc * xc, axis=-1, keepdims=True)
    return xc * lax.rsqrt(var + LN_EPS) * g + b


def _alibi_slopes2(n):
    return (np.exp2(-8.0 * np.arange(1, n + 1, dtype=np.float64) / n) * LOG2E).astype(np.float32)


def _slope_rows(slopes2, lanes):
    pieces = np.stack(_np_split3(slopes2), axis=1)
    rows = np.zeros((len(slopes2), HEAD_DIM, lanes), np.float32)
    rows[:, :3, :] = pieces[:, :, None]
    return jnp.asarray(rows, BF16)


def _ln_kernel(x_ref, g_ref, b_ref, o32_ref, o16_ref):
    y = _layer_norm_rows(x_ref[...], g_ref[...], b_ref[...])
    o32_ref[...] = y
    o16_ref[...] = y.astype(BF16)


def layer_norm_pallas(x, g, b, tm=512):
    m, d = x.shape
    return pl.pallas_call(
        _ln_kernel,
        grid=(m // tm,),
        in_specs=[pl.BlockSpec((tm, d), lambda i: (i, 0)),
                  pl.BlockSpec((1, d), lambda i: (0, 0)),
                  pl.BlockSpec((1, d), lambda i: (0, 0))],
        out_specs=[pl.BlockSpec((tm, d), lambda i: (i, 0)),
                   pl.BlockSpec((tm, d), lambda i: (i, 0))],
        out_shape=[jax.ShapeDtypeStruct((m, d), F32), jax.ShapeDtypeStruct((m, d), BF16)],
        compiler_params=_cparams(("parallel",)),
        name="layer_norm",
    )(x, g.reshape(1, d), b.reshape(1, d))


GATE_LANES_NSA = 3 * NSA_HEADS


def _gates_kernel(h_ref, w_ref, bf_ref, tri_ref, place_ref, g_ref, caug_ref, carry_ref):
    si = pl.program_id(1)

    @pl.when(si == 0)
    def _():
        carry_ref[...] = jnp.zeros_like(carry_ref)

    h_hi, h_lo = _split2(h_ref[0])
    w_hi, w_lo = _split2(w_ref[...])
    x = _dot(h_hi, w_hi) + _dot(h_lo, w_hi) + _dot(h_hi, w_lo)
    g_ref[0] = 1.0 / (1.0 + jnp.exp(-x))
    x = x + bf_ref[...]
    logf = jnp.minimum(x, 0.0) - jnp.log(1.0 + jnp.exp(-jnp.abs(x)))
    tri = tri_ref[...]
    l_hi, l_mid, l_lo = _split3(logf)
    c = _dot(tri, l_hi) + _dot(tri, l_mid) + _dot(tri, l_lo) + carry_ref[0:1, :]
    carry_ref[...] = jnp.broadcast_to(c[-1:, :], carry_ref.shape)
    n_hi, n_mid, n_lo = _split3(-LOG2E * c)
    caug = _dot(n_hi, place_ref[0]) + _dot(n_mid, place_ref[1]) + _dot(n_lo, place_ref[2])
    caug_ref[0] = caug.astype(BF16)


def gates_pallas(h3, w_gate, w_forget, b_forget, t=512):
    b, s, d = h3.shape
    lo = GATE_LANES_NSA
    w = jnp.zeros((d, LANES), F32).at[:, :lo].set(w_gate).at[:, lo:lo + FOX_HEADS].set(w_forget)
    bf = jnp.zeros((1, LANES), F32).at[0, lo:lo + FOX_HEADS].set(b_forget)
    tri = (np.arange(t)[None, :] <= np.arange(t)[:, None]).astype(np.float32)
    nc = FOX_HEADS * HEAD_DIM
    place = np.zeros((3, LANES, nc), np.float32)
    for piece in range(3):
        for hh in range(FOX_HEADS):
            place[piece, lo + hh, hh * HEAD_DIM + piece] = 1.0
    return pl.pallas_call(
        _gates_kernel,
        grid=(b, s // t),
        in_specs=[pl.BlockSpec((1, t, d), lambda i, j: (i, j, 0)),
                  pl.BlockSpec((d, LANES), lambda i, j: (0, 0)),
                  pl.BlockSpec((1, LANES), lambda i, j: (0, 0)),
                  pl.BlockSpec((t, t), lambda i, j: (0, 0)),
                  pl.BlockSpec((3, LANES, nc), lambda i, j: (0, 0, 0))],
        out_specs=[pl.BlockSpec((1, t, LANES), lambda i, j: (i, j, 0)),
                   pl.BlockSpec((1, t, nc), lambda i, j: (i, j, 0))],
        out_shape=[jax.ShapeDtypeStruct((b, s, LANES), F32),
                   jax.ShapeDtypeStruct((b, s, nc), BF16)],
        scratch_shapes=[pltpu.VMEM((8, LANES), F32)],
        compiler_params=_cparams(("parallel", "arbitrary")),
        name="gates_cumsum",
    )(h3, w, bf, jnp.asarray(tri, BF16), jnp.asarray(place, BF16))


def _proj_heads_kernel(x_ref, w_ref, sc_ref, o_ref, *, heads_per_step):
    acc = _dot(x_ref[0], w_ref[...]) * sc_ref[...]
    for j in range(heads_per_step):
        o_ref[0, j] = acc[:, j * HEAD_DIM:(j + 1) * HEAD_DIM].astype(o_ref.dtype)


def proj_heads_pallas(x3, w, colscale, tm, heads_per_step):
    b, s, d = x3.shape
    n = w.shape[1]
    tn = heads_per_step * HEAD_DIM
    return pl.pallas_call(
        functools.partial(_proj_heads_kernel, heads_per_step=heads_per_step),
        grid=(b, s // tm, n // tn),
        in_specs=[pl.BlockSpec((1, tm, d), lambda i, j, k: (i, j, 0)),
                  pl.BlockSpec((d, tn), lambda i, j, k: (0, k)),
                  pl.BlockSpec((1, tn), lambda i, j, k: (0, k))],
        out_specs=pl.BlockSpec((1, heads_per_step, tm, HEAD_DIM), lambda i, j, k: (i, k, j, 0)),
        out_shape=jax.ShapeDtypeStruct((b, n // HEAD_DIM, s, HEAD_DIM), BF16),
        compiler_params=_cparams(("parallel", "parallel", "arbitrary")),
        name="proj_heads",
    )(x3, w, colscale.reshape(1, n))


def _proj_t_kernel(x_ref, wt_ref, sc_ref, o_ref, *, heads_per_step):
    acc = _dot_nt(wt_ref[...], x_ref[0]) * sc_ref[...]
    o_ref[0] = acc.reshape(heads_per_step, HEAD_DIM, acc.shape[1]).astype(o_ref.dtype)


def proj_t_pallas(x3, wt, rowscale, tm, heads_per_step):
    b, s, d = x3.shape
    n = wt.shape[0]
    tn = heads_per_step * HEAD_DIM
    return pl.pallas_call(
        functools.partial(_proj_t_kernel, heads_per_step=heads_per_step),
        grid=(b, s // tm, n // tn),
        in_specs=[pl.BlockSpec((1, tm, d), lambda i, j, k: (i, j, 0)),
                  pl.BlockSpec((tn, d), lambda i, j, k: (k, 0)),
                  pl.BlockSpec((tn, 1), lambda i, j, k: (k, 0))],
        out_specs=pl.BlockSpec((1, heads_per_step, HEAD_DIM, tm), lambda i, j, k: (i, k, 0, j)),
        out_shape=jax.ShapeDtypeStruct((b, n // HEAD_DIM, HEAD_DIM, s), BF16),
        compiler_params=_cparams(("parallel", "parallel", "arbitrary")),
        name="proj_transposed",
    )(x3, wt, rowscale.reshape(n, 1))


def _proj_kaug_kernel(x_ref, w_ref, *rest, heads_per_step, positional):
    o_ref = rest[-1]
    acc = _dot(x_ref[0], w_ref[...])
    tm = acc.shape[0]
    if positional:
        pos = (pl.program_id(1) * tm + lax.broadcasted_iota(jnp.int32, (tm, HEAD_DIM), 0)) % POS_PERIOD
        lane = lax.broadcasted_iota(jnp.int32, (tm, HEAD_DIM), 1)
        pos_lanes = jnp.where(lane < 3, pos.astype(F32), 0.0).astype(o_ref.dtype)
    for j in range(heads_per_step):
        cols = slice(j * HEAD_DIM, (j + 1) * HEAD_DIM)
        bias = pos_lanes if positional else rest[0][0, :, cols]
        o_ref[0, j] = jnp.concatenate([acc[:, cols].astype(o_ref.dtype), bias], axis=1)


def proj_kaug_pallas(x3, w, aug, tm, heads_per_step):
    b, s, d = x3.shape
    n = w.shape[1]
    tn = heads_per_step * HEAD_DIM
    in_specs = [pl.BlockSpec((1, tm, d), lambda i, j, k: (i, j, 0)),
                pl.BlockSpec((d, tn), lambda i, j, k: (0, k))]
    args = [x3, w]
    if aug is not None:
        in_specs.append(pl.BlockSpec((1, tm, tn), lambda i, j, k: (i, j, k)))
        args.append(aug)
    return pl.pallas_call(
        functools.partial(_proj_kaug_kernel, heads_per_step=heads_per_step, positional=aug is None),
        grid=(b, s // tm, n // tn),
        in_specs=in_specs,
        out_specs=pl.BlockSpec((1, heads_per_step, tm, KAUG), lambda i, j, k: (i, k, j, 0)),
        out_shape=jax.ShapeDtypeStruct((b, n // HEAD_DIM, s, KAUG), BF16),
        compiler_params=_cparams(("parallel", "parallel", "arbitrary")),
        name="proj_keys_aug",
    )(*args)


def _matmul_ln_resident_kernel(x_ref, w_ref, r_ref, g_ref, b_ref, o32_ref, o16_ref):
    y = _layer_norm_rows(DEEPNORM_ALPHA * r_ref[...] + _dot(x_ref[...], w_ref[...]), g_ref[...], b_ref[...])
    o32_ref[...] = y
    o16_ref[...] = y.astype(BF16)


def matmul_ln_resident_pallas(x, w, res, g, b, tm):
    m, kk = x.shape
    d = w.shape[1]
    return pl.pallas_call(
        _matmul_ln_resident_kernel,
        grid=(m // tm,),
        in_specs=[pl.BlockSpec((tm, kk), lambda i: (i, 0)),
                  pl.BlockSpec((kk, d), lambda i: (0, 0), pipeline_mode=pl.Buffered(1)),
                  pl.BlockSpec((tm, d), lambda i: (i, 0)),
                  pl.BlockSpec((1, d), lambda i: (0, 0)),
                  pl.BlockSpec((1, d), lambda i: (0, 0))],
        out_specs=[pl.BlockSpec((tm, d), lambda i: (i, 0)),
                   pl.BlockSpec((tm, d), lambda i: (i, 0))],
        out_shape=[jax.ShapeDtypeStruct((m, d), F32), jax.ShapeDtypeStruct((m, d), BF16)],
        compiler_params=_cparams(("parallel",)),
        name="matmul_ln_resident",
    )(x, w, res, g.reshape(1, d), b.reshape(1, d))


def _out_proj_ln_kernel(*refs, widths):
    n = len(widths)
    x_refs, (w_ref, r_ref, g_ref, b_ref, o32_ref, o16_ref) = refs[:n], refs[n:]
    acc = None
    off = 0
    for x_ref, width in zip(x_refs, widths):
        part = _dot_tn(x_ref[0], w_ref[off:off + width, :])
        acc = part if acc is None else acc + part
        off += width
    y = _layer_norm_rows(DEEPNORM_ALPHA * r_ref[0] + acc, g_ref[...], b_ref[...])
    o32_ref[0] = y
    o16_ref[0] = y.astype(BF16)


def out_proj_ln_pallas(xts, w, res3, g, b, tm):
    bsz, s, d = res3.shape
    widths = tuple(x.shape[1] for x in xts)
    in_specs = [pl.BlockSpec((1, wd, tm), lambda i, j: (i, 0, j)) for wd in widths]
    in_specs += [pl.BlockSpec((w.shape[0], d), lambda i, j: (0, 0)),
                 pl.BlockSpec((1, tm, d), lambda i, j: (i, j, 0)),
                 pl.BlockSpec((1, d), lambda i, j: (0, 0)),
                 pl.BlockSpec((1, d), lambda i, j: (0, 0))]
    return pl.pallas_call(
        functools.partial(_out_proj_ln_kernel, widths=widths),
        grid=(bsz, s // tm),
        in_specs=in_specs,
        out_specs=[pl.BlockSpec((1, tm, d), lambda i, j: (i, j, 0)),
                   pl.BlockSpec((1, tm, d), lambda i, j: (i, j, 0))],
        out_shape=[jax.ShapeDtypeStruct((bsz, s, d), F32), jax.ShapeDtypeStruct((bsz, s, d), BF16)],
        compiler_params=_cparams(("parallel", "parallel")),
        name="out_proj_ln",
    )(*xts, w, res3, g.reshape(1, d), b.reshape(1, d))


HALO = 16
FFN_CHUNK = 256


def _gelu_tanh(x):
    return 0.5 * x * (1.0 + jnp.tanh(math.sqrt(2.0 / math.pi) * (x + 0.044715 * x * x * x)))


def _ffn_up_kernel(x_ref, xh_ref, wu_ref, wg_ref, cw_ref, cb_ref, o_ref):
    j = pl.program_id(1)
    x = x_ref[0]
    xh = xh_ref[0]
    first = jnp.where(j > 0, 1.0, 0.0)
    tn = o_ref.shape[2]
    row = lax.broadcasted_iota(jnp.int32, (x.shape[0], FFN_CHUNK), 0)
    for c in range(tn // FFN_CHUNK):
        cols = slice(c * FFN_CHUNK, (c + 1) * FFN_CHUNK)
        g = _dot(x, wg_ref[:, cols])
        gh = _dot(xh, wg_ref[:, cols]) * first
        prev1 = gh[HALO - 1:HALO, :]
        prev2 = gh[HALO - 2:HALO - 1, :]
        g_m1 = jnp.where(row == 0, prev1, pltpu.roll(g, 1, axis=0))
        g_m2 = jnp.where(row == 0, prev2, jnp.where(row == 1, prev1, pltpu.roll(g, 2, axis=0)))
        cw = cw_ref[:, cols]
        gc = cb_ref[:, cols] + cw[0:1] * g_m2 + cw[1:2] * g_m1 + cw[2:3] * g
        act = _gelu_tanh(gc)
        u = _dot(x, wu_ref[:, cols])
        o_ref[0, :, cols] = (act * u).astype(o_ref.dtype)


def ffn_up_pallas(x3, w_up, conv_w, conv_b, tm, tn):
    b, s, d = x3.shape
    dff = w_up.shape[1] // 2
    nt = dff // tn
    hb = tm // HALO
    cw = jnp.zeros((8, dff), F32).at[:CONV_WIDTH].set(conv_w)
    return pl.pallas_call(
        _ffn_up_kernel,
        grid=(b, s // tm, nt),
        in_specs=[pl.BlockSpec((1, tm, d), lambda i, j, k: (i, j, 0)),
                  pl.BlockSpec((1, HALO, d), lambda i, j, k: (i, jnp.maximum(j * hb - 1, 0), 0)),
                  pl.BlockSpec((d, tn), lambda i, j, k: (0, k)),
                  pl.BlockSpec((d, tn), lambda i, j, k: (0, k + nt)),
                  pl.BlockSpec((8, tn), lambda i, j, k: (0, k)),
                  pl.BlockSpec((1, tn), lambda i, j, k: (0, k))],
        out_specs=pl.BlockSpec((1, tm, tn), lambda i, j, k: (i, j, k)),
        out_shape=jax.ShapeDtypeStruct((b, s, dff), BF16),
        compiler_params=_cparams(("parallel", "parallel", "arbitrary")),
        name="ffn_up",
    )(x3, x3, w_up, w_up, cw, conv_b.reshape(1, dff))


def _init_state_t(m_ref, acc_ref):
    m_ref[...] = jnp.full(m_ref.shape, NEG, F32)
    acc_ref[...] = jnp.zeros(acc_ref.shape, F32)


def _finish_t(acc_ref):
    acc = acc_ref[...]
    return acc[:HEAD_DIM] / acc[HEAD_DIM:HEAD_DIM + 1]


def _fill_vaug(vaug_ref, vt_ref):
    s = vaug_ref.shape[1]
    vaug_ref[0:HEAD_DIM, :] = vt_ref[0, 0]
    pad = lax.broadcasted_iota(jnp.int32, (VROWS - HEAD_DIM, s), 0)
    vaug_ref[HEAD_DIM:VROWS, :] = jnp.where(pad == 0, 1.0, 0.0).astype(BF16)


def _ktile(ref, idx, size):
    return ref[0, 0, pl.ds(pl.multiple_of(idx * size, size), size), :]


def _vtile(ref, idx, size):
    return ref[:, pl.ds(pl.multiple_of(idx * size, size), size)]


def _stage_scores(st, s_ref, mc_ref):
    nsub = mc_ref.shape[0]
    s_ref[...] = st
    if nsub == 1:
        mc_ref[...] = jnp.max(st, axis=0, keepdims=True)
    else:
        mc_ref[...] = jnp.max(st.reshape(nsub, st.shape[0] // nsub, st.shape[1]), axis=1)


def _stage_update(s_ref, mc_ref, delta, vaug, m_ref, acc_ref):
    nsub = mc_ref.shape[0]
    m_prev = m_ref[...]
    m_new = jnp.maximum(m_prev, jnp.max(mc_ref[...] + delta, axis=0, keepdims=True))
    alpha = jnp.exp2(m_prev - m_new)
    shift = m_new - delta
    if nsub == 1:
        pt = jnp.exp2(s_ref[...] - shift)
    else:
        tk, lanes = s_ref.shape
        pt = jnp.exp2(s_ref[...].reshape(nsub, tk // nsub, lanes) - shift[:, None, :]).reshape(tk, lanes)
    acc_ref[...] = alpha * acc_ref[...] + _dot(vaug, pt.astype(BF16))
    m_ref[...] = m_new


def _score_buffers(tk, lanes, nsub=1):
    return [pltpu.VMEM((tk, lanes), F32), pltpu.VMEM((nsub, lanes), F32),
            pltpu.VMEM((tk, lanes), F32), pltpu.VMEM((nsub, lanes), F32)]


def _flash_pipeline(lead, n_loop, scores, meta, bufs, m_ref, acc_ref, n_even):
    _stage_scores(lead[0][0](), *bufs[0])
    for i in range(1, len(lead)):
        _stage_scores(lead[i][0](), *bufs[i % 2])
        _stage_update(*bufs[(i - 1) % 2], *lead[i - 1][1](), m_ref, acc_ref)
    cur = (len(lead) - 1) % 2
    nxt = 1 - cur
    if scores is None:
        _stage_update(*bufs[cur], *lead[-1][1](), m_ref, acc_ref)
        return
    _stage_scores(scores(0), *bufs[nxt])
    _stage_update(*bufs[cur], *lead[-1][1](), m_ref, acc_ref)

    def pair(k0, stage_next, both):
        if both:
            _stage_scores(scores(k0 + 1), *bufs[cur])
        _stage_update(*bufs[nxt], *meta(k0), m_ref, acc_ref)
        if stage_next:
            _stage_scores(scores(k0 + 2), *bufs[nxt])
        if both:
            _stage_update(*bufs[cur], *meta(k0 + 1), m_ref, acc_ref)

    full = jnp.maximum(n_loop - 1, 0) // 2

    def pairs(first, count):
        for u in range(count):
            pair(2 * (first + u), True, True)

    def body(kp, carry):
        pairs(4 * kp, 4)
        return carry

    lax.fori_loop(0, full // 4, body, 0)
    done = full // 4 * 4
    pl.when(full - done >= 2)(lambda: pairs(done, 2))
    pl.when(full % 2 == 1)(lambda: pairs(full - 1, 1))
    rest = n_loop - 2 * full
    pl.when(rest == 2)(lambda: pair(2 * full, False, True))
    if not n_even:
        pl.when(rest == 1)(lambda: pair(2 * full, False, False))


FOX_TQ = 1024
FOX_TK = 512


def _fox_kernel(qt_ref, k_ref, vt_ref, o_ref, m_ref, acc_ref, vaug_ref, sa_ref, ma_ref, sb_ref, mb_ref):
    tq, tk = o_ref.shape[3], FOX_TK
    nd = tq // tk
    qi = pl.program_id(2)

    @pl.when(qi == 0)
    def _():
        _fill_vaug(vaug_ref, vt_ref)

    ones3 = jnp.where(lax.broadcasted_iota(jnp.int32, (HEAD_DIM, tq), 0) < 3, 1.0, 0.0).astype(BF16)
    qaug = jnp.concatenate([qt_ref[0, 0], ones3], axis=0)
    _init_state_t(m_ref, acc_ref)
    key = lax.broadcasted_iota(jnp.int32, (tk, tq), 0)
    qry = lax.broadcasted_iota(jnp.int32, (tk, tq), 1)

    def scores(kj):
        return _dot(_ktile(k_ref, kj, tk), qaug)

    def meta(kj):
        return 0.0, _vtile(vaug_ref, kj, tk)

    lead = [(functools.partial(lambda d: jnp.where(key + d * tk <= qry, scores(qi * nd + d), NEG), d),
             functools.partial(lambda d: meta(qi * nd + d), d)) for d in range(nd)]
    _flash_pipeline(lead, qi * nd, scores, meta, ((sa_ref, ma_ref), (sb_ref, mb_ref)), m_ref, acc_ref,
                    n_even=nd % 2 == 0)
    o_ref[0, 0] = _finish_t(acc_ref).astype(o_ref.dtype)


def fox_pallas(qvt, kaug):
    b, _, dh, s = qvt.shape
    tq = min(FOX_TQ, s)
    assert tq % FOX_TK == 0
    out = pl.pallas_call(
        _fox_kernel,
        grid=(b, FOX_HEADS, s // tq),
        in_specs=[pl.BlockSpec((1, 1, dh, tq), lambda i, h, j: (i, TQ_FOX + h, 0, j)),
                  pl.BlockSpec((1, 1, s, KAUG), lambda i, h, j: (i, h, 0, 0)),
                  pl.BlockSpec((1, 1, dh, s), lambda i, h, j: (i, TV_FOX + h, 0, 0))],
        out_specs=pl.BlockSpec((1, 1, dh, tq), lambda i, h, j: (i, h, 0, j)),
        out_shape=jax.ShapeDtypeStruct((b, FOX_HEADS, dh, s), BF16),
        scratch_shapes=[pltpu.VMEM((1, tq), F32), pltpu.VMEM((VROWS, tq), F32),
                        pltpu.VMEM((VROWS, s), BF16)] + _score_buffers(FOX_TK, tq),
        compiler_params=_cparams(("parallel", "parallel", "arbitrary")),
        name="fox_attention",
    )(qvt, kaug, qvt)
    return out.reshape(b, FOX_HEADS * dh, s)


def _moba_kernel(slope_ref, qt_ref, k_ref, vt_ref, ind_ref, srow_ref, o_ref,
                 m_ref, acc_ref, vaug_ref, km_ref, sel_ref, sa_ref, ma_ref, sb_ref, mb_ref):
    t = MOBA_BLOCK
    tq = o_ref.shape[3]
    nd = tq // t
    h = pl.program_id(1)
    qi = pl.program_id(2)
    slope2 = slope_ref[h]

    @pl.when(qi == 0)
    def _():
        _fill_vaug(vaug_ref, vt_ref)
        km_ref[...] = _dot(ind_ref[...], k_ref[0, 0]) * (1.0 / MOBA_BLOCK)

    qt = qt_ref[0, 0]
    km_hi, km_lo = _split2(km_ref[...])
    q0 = jnp.concatenate([qt, jnp.zeros_like(qt)], axis=0)
    gate = _dot(km_hi, q0) + _dot(km_lo, q0)
    blk = lax.broadcasted_iota(jnp.int32, gate.shape, 0)
    lane_blk = lax.broadcasted_iota(jnp.int32, (1, tq), 1) // t
    valid = blk < qi * nd + lane_blk
    work = jnp.where(valid, gate, -jnp.inf)
    big = jnp.int32(2 ** 30)
    for _ in range(MOBA_TOPK):
        mx = jnp.max(work, axis=0, keepdims=True)
        first = jnp.min(jnp.where(work == mx, blk, big), axis=0, keepdims=True)
        work = jnp.where(blk == first, -jnp.inf, work)
    sel_ref[...] = jnp.where(valid & (work == -jnp.inf), 1.0, 0.0)

    qaug = jnp.concatenate([qt, srow_ref[...]], axis=0)
    _init_state_t(m_ref, acc_ref)
    tk = sa_ref.shape[0]
    nsub = tk // t
    ntile = tq // tk
    key = lax.broadcasted_iota(jnp.int32, (tk, tq), 0)
    qry = lax.broadcasted_iota(jnp.int32, (tk, tq), 1)
    sub = lax.broadcasted_iota(jnp.int32, (nsub, tq), 0)

    def tile_constant(kj, d):
        sees = jnp.concatenate([sel_ref[pl.ds(kj * nsub + c, 1), :] for c in range(nsub)], axis=0) > 0.0
        if d is not None:
            sees = sees | (lane_blk == d * nsub + sub)
        offset = ((kj * nsub + sub - qi * nd) * t).astype(F32)
        return jnp.where(sees, slope2 * offset, SKIP)

    def scores(kj):
        return _dot(_ktile(k_ref, kj, tk), qaug)

    def meta(kj):
        return tile_constant(kj, None), _vtile(vaug_ref, kj, tk)

    lead = [(functools.partial(lambda d: jnp.where(key + d * tk <= qry, scores(qi * ntile + d), NEG), d),
             functools.partial(lambda d: (tile_constant(qi * ntile + d, d),
                                          _vtile(vaug_ref, qi * ntile + d, tk)), d))
            for d in range(ntile)]
    _flash_pipeline(lead, qi * ntile, scores, meta, ((sa_ref, ma_ref), (sb_ref, mb_ref)), m_ref, acc_ref,
                    n_even=ntile % 2 == 0)
    o_ref[0, 0] = _finish_t(acc_ref).astype(o_ref.dtype)


MOBA_TQ = 1024
MOBA_TK = 512


def moba_pallas(qvt, kaug):
    b, _, dh, s = qvt.shape
    t = MOBA_BLOCK
    tq = min(MOBA_TQ, s)
    tk = min(MOBA_TK, tq)
    assert POS_PERIOD == t and tq % tk == 0 and tk % t == 0
    nb = s // t
    nbp = max(16, nb)
    ind = np.zeros((nbp, s), np.float32)
    ind[np.arange(s) // t, np.arange(s)] = 1.0
    slopes2 = _alibi_slopes2(MOBA_HEADS)
    grid_spec = pltpu.PrefetchScalarGridSpec(
        num_scalar_prefetch=1,
        grid=(b, MOBA_HEADS, s // tq),
        in_specs=[pl.BlockSpec((1, 1, dh, tq), lambda i, h, j, sl: (i, TQ_MOBA + h, 0, j)),
                  pl.BlockSpec((1, 1, s, KAUG), lambda i, h, j, sl: (i, KA_MOBA + h, 0, 0)),
                  pl.BlockSpec((1, 1, dh, s), lambda i, h, j, sl: (i, TV_MOBA + h, 0, 0)),
                  pl.BlockSpec((nbp, s), lambda i, h, j, sl: (0, 0)),
                  pl.BlockSpec((None, dh, tq), lambda i, h, j, sl: (h, 0, 0))],
        out_specs=pl.BlockSpec((1, 1, dh, tq), lambda i, h, j, sl: (i, h, 0, j)),
        scratch_shapes=[pltpu.VMEM((1, tq), F32), pltpu.VMEM((VROWS, tq), F32),
                        pltpu.VMEM((VROWS, s), BF16), pltpu.VMEM((nbp, KAUG), F32),
                        pltpu.VMEM((nbp, tq), F32)] + _score_buffers(tk, tq, tk // t))
    out = pl.pallas_call(
        _moba_kernel,
        grid_spec=grid_spec,
        out_shape=jax.ShapeDtypeStruct((b, MOBA_HEADS, dh, s), BF16),
        compiler_params=_cparams(("parallel", "parallel", "arbitrary")),
        name="moba_attention",
    )(jnp.asarray(slopes2), qvt, kaug, qvt, jnp.asarray(ind, BF16), _slope_rows(slopes2, tq))
    return out.reshape(b, MOBA_HEADS * dh, s)


def _nsa_compress_kernel(x_ref, w1a_ref, w1b_ref, pe_ref, w1_ref, w2_ref, w2t_ref, o_ref, ot_ref):
    nr = x_ref.shape[2]
    x = x_ref[0, 0]

    def near_f32(xb, w):
        w_hi, w_lo = _split2(w)
        return _dot(xb, w_hi) + _dot(xb, w_lo)

    a = near_f32(x, w1a_ref[0])
    bm = near_f32(x, w1b_ref[0])
    pe_hi, pe_mid, pe_lo = _split3(pe_ref[0])
    w1_hi, w1_lo = _split2(w1_ref[0])
    pe_term = (_dot(pe_hi, w1_hi) + _dot(pe_mid, w1_hi) + _dot(pe_lo, w1_hi)
               + _dot(pe_hi, w1_lo) + _dot(pe_mid, w1_lo))[0:1]
    pre = a + pltpu.roll(bm, nr - 1, axis=0) + pe_term
    hid = _gelu_tanh(pre)
    h_hi, h_mid, h_lo = _split3(hid)
    w2_hi, w2_lo = _split2(w2_ref[0])
    o_ref[0, 0] = (_dot(h_hi, w2_hi) + _dot(h_mid, w2_hi) + _dot(h_lo, w2_hi)
                   + _dot(h_hi, w2_lo) + _dot(h_mid, w2_lo))
    t_hi, t_lo = _split2(w2t_ref[0])
    ot_ref[0, 0] = (_dot_nt(t_hi, h_hi) + _dot_nt(t_hi, h_mid) + _dot_nt(t_hi, h_lo)
                    + _dot_nt(t_lo, h_hi) + _dot_nt(t_lo, h_mid))


def nsa_compress_pallas(nat, pe, w1, w2):
    b, _, s, dh = nat.shape
    nr = s // NSA_CMP_STRIDE
    half = NSA_CMP_STRIDE * dh
    hid = w1.shape[-1]
    x = nat[:, N_CMP:N_CMP + 4].reshape(b, 4, nr, half)
    w1f = w1.reshape(2, NSA_CMP_LEN * dh, hid)
    pef = jnp.zeros((2, 8, NSA_CMP_LEN * dh), F32).at[:, 0].set(pe.reshape(2, NSA_CMP_LEN * dh))
    return pl.pallas_call(
        _nsa_compress_kernel,
        grid=(b, 4),
        in_specs=[pl.BlockSpec((1, 1, nr, half), lambda i, j: (i, j, 0, 0)),
                  pl.BlockSpec((1, half, hid), lambda i, j: (j // 2, 0, 0)),
                  pl.BlockSpec((1, half, hid), lambda i, j: (j // 2, 1, 0)),
                  pl.BlockSpec((1, 8, 2 * half), lambda i, j: (j // 2, 0, 0)),
                  pl.BlockSpec((1, 2 * half, hid), lambda i, j: (j // 2, 0, 0)),
                  pl.BlockSpec((1, hid, dh), lambda i, j: (j // 2, 0, 0)),
                  pl.BlockSpec((1, dh, hid), lambda i, j: (j // 2, 0, 0))],
        out_specs=[pl.BlockSpec((1, 1, nr, dh), lambda i, j: (i, j, 0, 0)),
                   pl.BlockSpec((1, 1, dh, nr), lambda i, j: (i, j, 0, 0))],
        out_shape=[jax.ShapeDtypeStruct((b, 4, nr, dh), F32),
                   jax.ShapeDtypeStruct((b, 4, dh, nr), F32)],
        compiler_params=_cparams(("parallel", "parallel")),
        name="nsa_compress",
    )(x, w1f, w1f, pef, w1f, w2, jnp.swapaxes(w2, 1, 2))


NSA_TQ = 128


NSA_CMP_CHUNK = 128


def _nsa_cmp_kernel(slope_ref, qt_ref, kc_ref, vct_ref, mimpt_ref, oct_ref, selt_ref, cnt_ref):
    tq = NSA_TQ
    g = pl.program_id(1)
    qi = pl.program_id(2)
    q0 = qi * tq
    nr = kc_ref.shape[2]
    nsb = selt_ref.shape[2]
    t_lane = q0 + lax.broadcasted_iota(jnp.int32, (1, tq), 1)
    chunk = min(NSA_CMP_CHUNK, nr)
    tiles_per_chunk = chunk * NSA_CMP_STRIDE // tq

    def branch(n):
        kc_hi, kc_lo = _split2(kc_ref[0, 0, 0:n, :])
        vct = vct_ref[0, 0, :, 0:n].astype(BF16)
        cmp_end = NSA_CMP_STRIDE * lax.broadcasted_iota(jnp.int32, (n, tq), 0) + (NSA_CMP_LEN - 1)
        mask = cmp_end <= t_lane
        rel = (cmp_end - q0).astype(F32)
        psum = jnp.zeros((n, tq), F32)
        qt4 = jnp.concatenate([qt_ref[0, hh] for hh in range(NSA_GROUP)], axis=1)
        st4 = _dot(kc_hi, qt4) + _dot(kc_lo, qt4)
        probs = []
        for hh in range(NSA_GROUP):
            st = st4[:, hh * tq:(hh + 1) * tq] + slope_ref[g * NSA_GROUP + hh] * rel
            st = jnp.where(mask, st, -jnp.inf)
            m = jnp.max(st, axis=0, keepdims=True)
            m = jnp.where(m > -jnp.inf, m, 0.0)
            e = jnp.exp2(st - m)
            p = e * (1.0 / jnp.maximum(jnp.sum(e, axis=0, keepdims=True), 1e-30))
            probs.append(p.astype(BF16))
            psum = psum + p
        o4 = _dot(vct, jnp.concatenate(probs, axis=1))
        for hh in range(NSA_GROUP):
            oct_ref[0, hh] = o4[:, hh * tq:(hh + 1) * tq]
        nbk = min(nsb, n * NSA_CMP_STRIDE // NSA_SEL_BLOCK)
        p_hi, p_mid, p_lo = _split3(psum)
        mimpt = mimpt_ref[0:nbk, 0:n]
        imp = _dot(mimpt, p_hi) + _dot(mimpt, p_mid) + _dot(mimpt, p_lo)
        blk = lax.broadcasted_iota(jnp.int32, (nbk, tq), 0)
        jt = t_lane // NSA_SEL_BLOCK
        forced = (blk == 0) | (blk == jt) | (blk == jt - 1)
        imp = jnp.where(forced, NSA_FORCE_SCORE, imp)
        valid = blk * NSA_SEL_BLOCK <= t_lane
        work0 = jnp.where(valid, imp, -jnp.inf)
        big = jnp.int32(2 ** 30)

        def pick(_, work):
            mx = jnp.max(work, axis=0, keepdims=True)
            first = jnp.min(jnp.where(work == mx, blk, big), axis=0, keepdims=True)
            return jnp.where(blk == first, -jnp.inf, work)

        work = lax.fori_loop(0, min(NSA_TOPK, nsb), pick, work0)
        sel = jnp.where(valid & (work == -jnp.inf), 1.0, 0.0)
        selt_ref[0, 0, 0:nbk, :] = sel
        cnt_ref[0, 0, 0, :, 0:nbk] = _dot_nt(jnp.ones((8, tq), BF16), sel.astype(BF16))
        if nbk < nsb:
            selt_ref[0, 0, nbk:nsb, :] = jnp.zeros((nsb - nbk, tq), F32)
            cnt_ref[0, 0, 0, :, nbk:nsb] = jnp.zeros((8, nsb - nbk), F32)

    n_chunks = nr // chunk
    for c in range(n_chunks):
        pl.when(jnp.minimum(qi // tiles_per_chunk, n_chunks - 1) == c)(
            functools.partial(branch, (c + 1) * chunk))


def nsa_cmp_pallas(qvt, cmp_kv, cmp_kvt):
    b, _, dh, s = qvt.shape
    tq = NSA_TQ
    nr = cmp_kv.shape[2]
    nsb = s // NSA_SEL_BLOCK
    ratio = NSA_SEL_BLOCK // NSA_CMP_STRIDE
    front = NSA_CMP_LEN // NSA_CMP_STRIDE - 1
    n_int = ratio + front
    n_idx = np.arange(nr)[None, :]
    j_idx = np.arange(nsb)[:, None]
    mimpt = ((n_idx >= ratio * j_idx - front) & (n_idx <= ratio * j_idx + n_int - 1 - front)
             & (n_idx < nr - 1)).astype(np.float32)
    grid_spec = pltpu.PrefetchScalarGridSpec(
        num_scalar_prefetch=1,
        grid=(b, NSA_KV_HEADS, s // tq),
        in_specs=[pl.BlockSpec((1, NSA_GROUP, dh, tq), lambda i, g, j, sl: (i, TQ_NSA // NSA_GROUP + g, 0, j)),
                  pl.BlockSpec((1, 1, nr, dh), lambda i, g, j, sl: (i, g, 0, 0)),
                  pl.BlockSpec((1, 1, dh, nr), lambda i, g, j, sl: (i, 2 + g, 0, 0)),
                  pl.BlockSpec((nsb, nr), lambda i, g, j, sl: (0, 0))],
        out_specs=[pl.BlockSpec((1, NSA_GROUP, dh, tq), lambda i, g, j, sl: (i, g, 0, j)),
                   pl.BlockSpec((1, 1, nsb, tq), lambda i, g, j, sl: (i, g, 0, j)),
                   pl.BlockSpec((1, 1, 1, 8, nsb), lambda i, g, j, sl: (i, g, j, 0, 0))])
    assert nr % min(NSA_CMP_CHUNK, nr) == 0
    return pl.pallas_call(
        _nsa_cmp_kernel,
        grid_spec=grid_spec,
        out_shape=[jax.ShapeDtypeStruct((b, NSA_HEADS, dh, s), F32),
                   jax.ShapeDtypeStruct((b, NSA_KV_HEADS, nsb, s), F32),
                   jax.ShapeDtypeStruct((b, NSA_KV_HEADS, s // tq, 8, nsb), F32)],
        compiler_params=_cparams(("parallel", "parallel", "parallel")),
        name="nsa_compressed_select",
    )(jnp.asarray(_alibi_slopes2(NSA_HEADS)), qvt, cmp_kv, cmp_kvt, jnp.asarray(mimpt, BF16))


NSA_TK = 256
NSA_SEL_TQ = 256


def _nsa_qaug(qt_ref, srow_ref):
    return jnp.concatenate(
        [jnp.concatenate([qt_ref[0, hh], srow_ref[hh]], axis=0) for hh in range(NSA_GROUP)], axis=1)


def _nsa_sel_kernel(bits_ref, qt_ref, k_ref, vt_ref, selt_ref, srow_ref, slane_ref, o_ref,
                    m_ref, acc_ref, vaug_ref, sa_ref, ma_ref, sb_ref, mb_ref, list_ref, *, nq, words):
    tq, tk = NSA_SEL_TQ, NSA_TK
    per_tile = tk // NSA_SEL_BLOCK
    qi = pl.program_id(2)
    q0 = qi * tq

    @pl.when(qi == 0)
    def _():
        _fill_vaug(vaug_ref, vt_ref)

    base = ((pl.program_id(0) * NSA_KV_HEADS + pl.program_id(1)) * nq + qi) * words
    list_ref[0] = 0

    def note(j, n):
        list_ref[n] = j
        return n + ((bits_ref[base + j // 32] >> (j % 32)) & 1)

    n_tiles = lax.fori_loop(0, qi, note, 0)

    qaug = _nsa_qaug(qt_ref, srow_ref)
    slane = slane_ref[0:1, :]
    diag = q0 // tk

    def scores(kj):
        st = _dot(_ktile(k_ref, kj, tk), qaug)
        rows = [jnp.broadcast_to(selt_ref[0, 0, pl.ds(kj * per_tile + c, 1), :], (NSA_SEL_BLOCK, tq))
                for c in range(per_tile)]
        bias = (jnp.concatenate(rows, axis=0) - 1.0) * (-NEG)
        return st + jnp.concatenate([bias] * NSA_GROUP, axis=1)

    def meta(kj):
        return slane * (kj * tk - q0).astype(F32), _vtile(vaug_ref, kj, tk)

    def own_tile():
        key = lax.broadcasted_iota(jnp.int32, (tk, tq), 0)
        qry = lax.broadcasted_iota(jnp.int32, (tk, tq), 1)
        causal = jnp.concatenate([key <= qry] * NSA_GROUP, axis=1)
        return jnp.where(causal, scores(diag), NEG)

    _init_state_t(m_ref, acc_ref)
    _flash_pipeline([(own_tile, lambda: meta(diag))], n_tiles,
                    lambda i: scores(list_ref[i]), lambda i: meta(list_ref[i]),
                    ((sa_ref, ma_ref), (sb_ref, mb_ref)), m_ref, acc_ref, n_even=False)
    out = _finish_t(acc_ref)
    for hh in range(NSA_GROUP):
        o_ref[0, hh] = out[:, hh * tq:(hh + 1) * tq]


def _nsa_tables(tq):
    slopes2 = _alibi_slopes2(NSA_HEADS)
    srow = _slope_rows(slopes2, tq)
    slane = np.repeat(slopes2.reshape(NSA_KV_HEADS, NSA_GROUP), tq, axis=1)
    slane8 = np.repeat(slane[:, None, :], 8, axis=1)
    return srow, jnp.asarray(slane8, F32)


def _active_tile_bits(cnt, tq, tk):
    b, g, nq128, _, nsb = cnt.shape
    qper, bper = tq // NSA_TQ, tk // NSA_SEL_BLOCK
    nq, nkv = nq128 // qper, nsb // bper
    act = cnt[:, :, :, 0, :].reshape(b, g, nq, qper, nkv, bper).sum(axis=(3, 5)) > 0.0
    words = -(-nkv // 32)
    act = jnp.pad(act, ((0, 0), (0, 0), (0, 0), (0, words * 32 - nkv))).reshape(b, g, nq, words, 32)
    bits = jnp.sum(act.astype(jnp.uint32) << jnp.arange(32, dtype=jnp.uint32), axis=-1, dtype=jnp.uint32)
    return lax.bitcast_convert_type(bits, jnp.int32).reshape(-1), nq, words


def nsa_sel_pallas(qvt, kaug, selt, cnt):
    b, _, dh, s = qvt.shape
    tq = NSA_SEL_TQ
    assert tq == NSA_TK
    lanes = NSA_GROUP * tq
    nsb = s // NSA_SEL_BLOCK
    srow, slane = _nsa_tables(tq)
    bits, nq, words = _active_tile_bits(cnt, tq, NSA_TK)
    grid_spec = pltpu.PrefetchScalarGridSpec(
        num_scalar_prefetch=1,
        grid=(b, NSA_KV_HEADS, nq),
        in_specs=[pl.BlockSpec((1, NSA_GROUP, dh, tq), lambda i, g, j, bt: (i, TQ_NSA // NSA_GROUP + g, 0, j)),
                  pl.BlockSpec((1, 1, s, KAUG), lambda i, g, j, bt: (i, KA_NSA + g, 0, 0)),
                  pl.BlockSpec((1, 1, dh, s), lambda i, g, j, bt: (i, TV_NSA + g, 0, 0)),
                  pl.BlockSpec((1, 1, nsb, tq), lambda i, g, j, bt: (i, g, 0, j)),
                  pl.BlockSpec((NSA_GROUP, dh, tq), lambda i, g, j, bt: (g, 0, 0)),
                  pl.BlockSpec((None, 8, lanes), lambda i, g, j, bt: (g, 0, 0))],
        out_specs=pl.BlockSpec((1, NSA_GROUP, dh, tq), lambda i, g, j, bt: (i, g, 0, j)),
        scratch_shapes=[pltpu.VMEM((1, lanes), F32), pltpu.VMEM((VROWS, lanes), F32),
                        pltpu.VMEM((VROWS, s), BF16)] + _score_buffers(NSA_TK, lanes)
        + [pltpu.SMEM((max(nq, 8),), jnp.int32)])
    return pl.pallas_call(
        functools.partial(_nsa_sel_kernel, nq=nq, words=words),
        grid_spec=grid_spec,
        out_shape=jax.ShapeDtypeStruct((b, NSA_HEADS, dh, s), F32),
        compiler_params=_cparams(("parallel", "parallel", "arbitrary")),
        name="nsa_selected",
    )(bits, qvt, kaug, qvt, selt, srow, slane)


NSA_WT = 256


def _nsa_win_kernel(qt_ref, k_ref, vt_ref, srow_ref, slane_ref, oc_ref, os_ref, gate_ref, o_ref,
                    m_ref, acc_ref, vaug_ref, sa_ref, ma_ref, sb_ref, mb_ref, gt_ref):
    tq = NSA_WT
    wt = NSA_WT
    qi = pl.program_id(2)

    @pl.when(qi == 0)
    def _():
        _fill_vaug(vaug_ref, vt_ref)

    qaug = _nsa_qaug(qt_ref, srow_ref)
    slane = slane_ref[0:1, :]
    key = lax.broadcasted_iota(jnp.int32, (wt, tq), 0)
    qry = lax.broadcasted_iota(jnp.int32, (wt, tq), 1)
    span = NSA_WINDOW // wt

    def tile(d, keep):
        kj = jnp.maximum(qi - d, 0)

        def scores():
            st = _dot(_ktile(k_ref, kj, wt), qaug)
            if keep is not None:
                st = jnp.where(jnp.concatenate([keep] * NSA_GROUP, axis=1), st, NEG)
            return st

        def meta():
            base = (kj * wt) // POS_PERIOD * POS_PERIOD - qi * tq
            delta = jnp.where(qi - d >= 0, slane * base.astype(F32), SKIP)
            return delta, _vtile(vaug_ref, kj, wt)

        return scores, meta

    _init_state_t(m_ref, acc_ref)
    tiles = [tile(0, key <= qry)] + [tile(d, None) for d in range(1, span)] + [tile(span, key > qry)]
    _flash_pipeline(tiles, None, None, None, ((sa_ref, ma_ref), (sb_ref, mb_ref)), m_ref, acc_ref, True)

    o_w = _finish_t(acc_ref)
    gt_ref[...] = gate_ref[0].T
    for hh in range(NSA_GROUP):
        c0 = 3 * (pl.program_id(1) * NSA_GROUP + hh)
        mix = (gt_ref[pl.ds(c0, 1), :] * oc_ref[0, hh] + gt_ref[pl.ds(c0 + 1, 1), :] * os_ref[0, hh]
               + gt_ref[pl.ds(c0 + 2, 1), :] * o_w[:, hh * tq:(hh + 1) * tq])
        o_ref[0, hh] = mix.astype(o_ref.dtype)


def nsa_win_pallas(qvt, kaug, o_c, o_s, gates):
    b, _, dh, s = qvt.shape
    tq = wt = NSA_WT
    assert NSA_WINDOW % wt == 0 and POS_PERIOD % wt == 0
    lanes = NSA_GROUP * tq
    srow, slane = _nsa_tables(tq)
    head_blk = pl.BlockSpec((1, NSA_GROUP, dh, tq), lambda i, g, j: (i, g, 0, j))
    out = pl.pallas_call(
        _nsa_win_kernel,
        grid=(b, NSA_KV_HEADS, s // tq),
        in_specs=[pl.BlockSpec((1, NSA_GROUP, dh, tq), lambda i, g, j: (i, TQ_NSA // NSA_GROUP + g, 0, j)),
                  pl.BlockSpec((1, 1, s, KAUG), lambda i, g, j: (i, KA_NSA + 2 + g, 0, 0)),
                  pl.BlockSpec((1, 1, dh, s), lambda i, g, j: (i, TV_NSA + 2 + g, 0, 0)),
                  pl.BlockSpec((NSA_GROUP, dh, tq), lambda i, g, j: (g, 0, 0)),
                  pl.BlockSpec((None, 8, lanes), lambda i, g, j: (g, 0, 0)),
                  head_blk, head_blk,
                  pl.BlockSpec((1, tq, LANES), lambda i, g, j: (i, j, 0))],
        out_specs=head_blk,
        out_shape=jax.ShapeDtypeStruct((b, NSA_HEADS, dh, s), BF16),
        scratch_shapes=[pltpu.VMEM((1, lanes), F32), pltpu.VMEM((VROWS, lanes), F32),
                        pltpu.VMEM((VROWS, s), BF16)] + _score_buffers(wt, lanes)
        + [pltpu.VMEM((LANES, tq), F32)],
        compiler_params=_cparams(("parallel", "parallel", "arbitrary")),
        name="nsa_window_mix",
    )(qvt, kaug, qvt, srow, slane, o_c, o_s, gates)
    return out.reshape(b, NSA_HEADS * dh, s)


def _mem_attn_kernel(q_ref, k_ref, vt_ref, o_ref):
    for hh in range(MEM_HEADS):
        s = _dot_nt(q_ref[0, hh], k_ref[0, hh])
        m = jnp.max(s, axis=-1, keepdims=True)
        e = jnp.exp2(s - m)
        p = e / jnp.sum(e, axis=-1, keepdims=True)
        o_ref[0, hh] = _dot_nt(vt_ref[0, hh], p.astype(BF16)).astype(o_ref.dtype)


def mem_attn_pallas(nat, mem_k, mem_vt, tq=512):
    b, _, s, dh = nat.shape
    n_mem = mem_k.shape[2]
    out = pl.pallas_call(
        _mem_attn_kernel,
        grid=(b, s // tq),
        in_specs=[pl.BlockSpec((1, MEM_HEADS, tq, dh), lambda i, j: (i, N_MEMQ // MEM_HEADS, j, 0)),
                  pl.BlockSpec((1, MEM_HEADS, n_mem, dh), lambda i, j: (i, 0, 0, 0)),
                  pl.BlockSpec((1, MEM_HEADS, dh, n_mem), lambda i, j: (i, 0, 0, 0))],
        out_specs=pl.BlockSpec((1, MEM_HEADS, dh, tq), lambda i, j: (i, 0, 0, j)),
        out_shape=jax.ShapeDtypeStruct((b, MEM_HEADS, dh, s), BF16),
        compiler_params=_cparams(("parallel", "parallel")),
        name="memory_attention",
    )(nat, mem_k, mem_vt)
    return out.reshape(b, MEM_HEADS * dh, s)


def _in_proj_weights(w_in):
    hd = HEAD_DIM
    sizes = (3 * MOBA_HEADS * hd, NSA_HEADS * hd, 6 * NSA_KV_HEADS * hd, 3 * NSA_HEADS,
             3 * FOX_HEADS * hd, FOX_HEADS, MEM_HEADS * hd)
    offs = np.concatenate([[0], np.cumsum(sizes)])
    moba, nsa_q, nsa_kv, nsa_g, fox, fox_f, mem_q = (w_in[:, offs[i]:offs[i + 1]] for i in range(7))
    mh, fh, g2 = MOBA_HEADS * hd, FOX_HEADS * hd, NSA_KV_HEADS * hd
    moba_q, moba_k, moba_v = moba[:, :mh], moba[:, mh:2 * mh], moba[:, 2 * mh:]
    fox_q, fox_k, fox_v = fox[:, :fh], fox[:, fh:2 * fh], fox[:, 2 * fh:]
    k_cmp, v_cmp, k_slc, v_slc, k_win, v_win = (nsa_kv[:, i * g2:(i + 1) * g2] for i in range(6))
    w_t = jnp.concatenate([moba_q, nsa_q, fox_q, moba_v, v_slc, v_win, fox_v], axis=1).T.astype(BF16)
    t_scale = np.ones((T_SLOTS * hd,), np.float32)
    t_scale[:TV_MOBA * hd] = Q_SCALE
    w_ka = jnp.concatenate([moba_k, k_slc, k_win], axis=1).astype(BF16)
    w_kf = fox_k.astype(BF16)
    w_nat = jnp.concatenate([k_cmp, v_cmp, mem_q], axis=1).astype(BF16)
    n_scale = np.ones((N_SLOTS * hd,), np.float32)
    n_scale[N_MEMQ * hd:] = Q_SCALE
    return w_t, jnp.asarray(t_scale), w_ka, w_kf, w_nat, jnp.asarray(n_scale), nsa_g, fox_f


def _mixer(h32, h16, mem16, w_in, b_forget, w_mem_kv, cmp_pe, cmp_w1, cmp_w2):
    b, s, d = h16.shape
    tm = min(1024, s)
    w_t, t_scale, w_ka, w_kf, w_nat, n_scale, w_gate, w_forget = _in_proj_weights(w_in)
    gates, caug = gates_pallas(h32, w_gate, w_forget, b_forget)
    qvt = proj_t_pallas(h16, w_t, t_scale, tm=tm, heads_per_step=13)
    k_alibi = proj_kaug_pallas(h16, w_ka, None, tm=tm, heads_per_step=KA_SLOTS)
    k_fox = proj_kaug_pallas(h16, w_kf, caug, tm=tm, heads_per_step=FOX_HEADS)
    nat = proj_heads_pallas(h16, w_nat, n_scale, tm=tm, heads_per_step=8)
    n_mem = mem16.shape[1]
    mk = MEM_HEADS * HEAD_DIM
    mem_k = proj_heads_pallas(mem16, w_mem_kv[:, :mk].astype(BF16), jnp.ones((mk,), F32),
                              tm=n_mem, heads_per_step=MEM_HEADS)
    mem_vt = proj_t_pallas(mem16, w_mem_kv[:, mk:].T.astype(BF16), jnp.ones((mk,), F32),
                           tm=n_mem, heads_per_step=MEM_HEADS)
    o_moba = moba_pallas(qvt, k_alibi)
    o_fox = fox_pallas(qvt, k_fox)
    cmp_kv, cmp_kvt = nsa_compress_pallas(nat, cmp_pe, cmp_w1, cmp_w2)
    o_c, selt, cnt = nsa_cmp_pallas(qvt, cmp_kv, cmp_kvt)
    o_s = nsa_sel_pallas(qvt, k_alibi, selt, cnt)
    o_nsa = nsa_win_pallas(qvt, k_alibi, o_c, o_s, gates)
    o_mem = mem_attn_pallas(nat, mem_k, mem_vt)
    return [o_moba, o_nsa, o_fox, o_mem]


def kernel(x, mem, emb_ln_g, emb_ln_b, w_in, b_forget, w_mem_kv, nsa_cmp_pe, nsa_cmp_w1, nsa_cmp_w2,
           w_out, ln1_g, ln1_b, ffn_w_up, ffn_conv_w, ffn_conv_b, ffn_w_down, ln2_g, ln2_b):
    b, s, d = x.shape
    depth = w_in.shape[0]
    dff = ffn_w_down.shape[1]
    mem16 = mem.astype(BF16)
    h32, h16 = layer_norm_pallas(x.reshape(b * s, d), emb_ln_g, emb_ln_b)
    for l in range(depth):
        heads = _mixer(h32.reshape(b, s, d), h16.reshape(b, s, d), mem16, w_in[l], b_forget[l], w_mem_kv[l],
                       nsa_cmp_pe[l], nsa_cmp_w1[l], nsa_cmp_w2[l])
        h32, h16 = out_proj_ln_pallas(heads, w_out[l].astype(BF16), h32.reshape(b, s, d),
                                      ln1_g[l], ln1_b[l], tm=512)
        a = ffn_up_pallas(h16, ffn_w_up[l].astype(BF16), ffn_conv_w[l], ffn_conv_b[l],
                          tm=min(1024, s), tn=512)
        h32, h16 = matmul_ln_resident_pallas(a.reshape(b * s, dff), ffn_w_down[l].astype(BF16),
                                             h32.reshape(b * s, d), ln2_g[l], ln2_b[l], tm=256)
    return h32.reshape(b, s, d)
```
